```python
import math
import jax
import jax.numpy as jnp
from jax import lax
import numpy as np

D_MODEL = 1024
BATCH = 2
SEQ = 8192
DEPTH = 1

GRID_W = 64
CTX_LEN = 256
HY_WIDTH = 512
ML_HEADS = 4
ML_DK = 64
ML_DV = 128
ML_WIDTH = ML_HEADS * ML_DV
MIX_WIDTH = HY_WIDTH + ML_WIDTH
HY_COLS = 3 * HY_WIDTH
QK_COLS = ML_HEADS * ML_DK
N_GATES = 4 * ML_HEADS
ML_COLS = 2 * QK_COLS + 2 * ML_WIDTH + N_GATES
IN_COLS = HY_COLS + ML_COLS
SHORT_W = 3
FILTER_EMB = 33
FILTER_HIDDEN = 64
DECAY_TARGET = 1e-2
FAST_DECAY_PCT = 0.3
SLOW_DECAY_PCT = 1.5
CHUNK = 128
GATE_CAP = 15.0
N_EXPERTS = 16
EXPERT_FF = 2048
CAP_FACTOR = 2
EPS = 1e-6

kernel_name = 'hyena_mlstm_ec_moe_diffusion_block'


def rmsnorm(x, g):
    xf = x.astype(jnp.float32)
    y = xf * lax.rsqrt(jnp.mean(xf * xf, axis=-1, keepdims=True) + EPS)
    return (y * g.astype(jnp.float32)).astype(x.dtype)


def modulated_norm(x, g, shift, scale):
    return rmsnorm(x, g) * (1 + scale) + shift


def pos_embed_2d(rows, cols, dim):
    quarter = dim // 4
    omega = 1.0 / (10000.0 ** (jnp.arange(quarter, dtype=jnp.float32) / quarter))

    def emb1d(n):
        ang = jnp.arange(n, dtype=jnp.float32)[:, None] * omega[None, :]
        return jnp.concatenate([jnp.sin(ang), jnp.cos(ang)], axis=-1)

    er = emb1d(rows)
    ec = emb1d(cols)
    pe = jnp.concatenate([jnp.broadcast_to(er[:, None, :], (rows, cols, dim // 2)),
                          jnp.broadcast_to(ec[None, :, :], (rows, cols, dim // 2))], axis=-1)
    return pe.reshape(rows * cols, dim)


def short_conv(z, w, b):
    L = z.shape[1]
    zp = jnp.pad(z, ((0, 0), (1, 1), (0, 0)))
    return zp[:, :L] * w[0] + zp[:, 1:L + 1] * w[1] + zp[:, 2:] * w[2] + b


def hyena_filter(L, w1, b1, w2, b2, w3, b3, w4, freq):
    f32 = jnp.float32
    t = jnp.linspace(0.0, 1.0, L, dtype=f32)[:, None]
    bands = (FILTER_EMB - 1) // 2
    w = 2.0 * math.pi * jnp.arange(L, dtype=f32) / L
    f = jnp.linspace(1e-4, bands - 1, bands, dtype=f32)
    ang = w[:, None] * f[None, :]
    z = jnp.concatenate([t, jnp.cos(ang), -jnp.sin(ang)], axis=-1)
    h = jnp.sin(freq * (z @ w1 + b1))
    h = jnp.sin(freq * (h @ w2 + b2))
    h = jnp.sin(freq * (h @ w3 + b3))
    h = (h @ w4).astype(f32)
    min_decay = math.log(DECAY_TARGET) / SLOW_DECAY_PCT
    max_decay = math.log(DECAY_TARGET) / FAST_DECAY_PCT
    deltas = jnp.linspace(min_decay, max_decay, HY_WIDTH, dtype=f32)
    decay = jnp.exp(-t * jnp.abs(deltas)[None, :])
    h_fwd = h[:, :HY_WIDTH] * decay
    h_bwd = h[:, HY_WIDTH:] * decay
    return jnp.concatenate([h_fwd, jnp.zeros((1, HY_WIDTH), f32), h_bwd[:0:-1]], axis=0)


def long_conv(v, k_circ, bias):
    L = v.shape[1]
    vf = v.astype(jnp.float32)
    spec = jnp.fft.rfft(vf, n=2 * L, axis=1) * jnp.fft.rfft(k_circ, axis=0)[None]
    y = jnp.fft.irfft(spec, n=2 * L, axis=1)[:, :L]
    return (y + vf * bias.astype(jnp.float32)).astype(v.dtype)


def hyena_group(z_hy, conv_w, conv_b, fw1, fb1, fw2, fb2, fw3, fb3, fw4, ffreq, hy_bias):
    z = short_conv(z_hy, conv_w, conv_b)
    x0, x1, v = jnp.split(z, 3, axis=-1)
    k_circ = hyena_filter(z.shape[1], fw1, fb1, fw2, fb2, fw3, fb3, fw4, ffreq)
    return x0 * long_conv(v * x1, k_circ, hy_bias)


def mlstm_chunkwise(q, k, v, i_pre, logf, state):
    f32 = jnp.float32
    q, k, v = q.astype(f32), k.astype(f32), v.astype(f32)
    B, H, L, _ = q.shape
    nc = L // CHUNK

    def to_chunks(a):
        return jnp.moveaxis(a.reshape(a.shape[:2] + (nc, CHUNK) + a.shape[3:]), 2, 0)

    causal = jnp.tril(jnp.ones((CHUNK, CHUNK), dtype=bool))

    def step(carry, inp):
        C, n, m = carry
        qt, kt, vt, it, ft = inp
        b = jnp.cumsum(ft, axis=-1)
        dlog = jnp.where(causal, b[..., :, None] - b[..., None, :] + it[..., None, :], -jnp.inf)
        m_inter = b + m[..., None]
        m_t = jnp.maximum(m_inter, jnp.max(dlog, axis=-1))
        w_intra = jnp.exp(dlog - m_t[..., None])
        w_inter = jnp.exp(m_inter - m_t)
        s = jnp.einsum('bhtk,bhsk->bhts', qt, kt) * w_intra
        num = jnp.einsum('bhts,bhsv->bhtv', s, vt) + w_inter[..., None] * jnp.einsum('bhtk,bhkv->bhtv', qt, C)
        den = jnp.sum(s, axis=-1) + w_inter * jnp.einsum('bhtk,bhk->bht', qt, n)
        h = num / jnp.maximum(jnp.abs(den), jnp.exp(-m_t))[..., None]
        b_end = b[..., -1]
        g = b_end[..., None] - b + it
        m_new = jnp.maximum(b_end + m, jnp.max(g, axis=-1))
        w_s = jnp.exp(g - m_new[..., None])
        w_c = jnp.exp(b_end + m - m_new)
        C_new = w_c[..., None, None] * C + jnp.einsum('bhs,bhsk,bhsv->bhkv', w_s, kt, vt)
        n_new = w_c[..., None] * n + jnp.einsum('bhs,bhsk->bhk', w_s, kt)
        return (C_new, n_new, m_new), h

    state, hc = lax.scan(step, state, (to_chunks(q), to_chunks(k), to_chunks(v), to_chunks(i_pre), to_chunks(logf)))
    h = jnp.moveaxis(hc, 0, 2).reshape(B, H, L, v.shape[-1])
    return h, state


def mlstm_inputs(zm, gate_b):
    B, L, _ = zm.shape
    q, k, v, o, gates = jnp.split(
        zm, [QK_COLS, 2 * QK_COLS, 2 * QK_COLS + ML_WIDTH, 2 * QK_COLS + 2 * ML_WIDTH], axis=-1)

    def heads(a, d):
        return a.reshape(B, L, ML_HEADS, d).transpose(0, 2, 1, 3)

    q = heads(q, ML_DK) * (ML_DK ** -0.5)
    k = heads(k, ML_DK)
    v = heads(v, ML_DV)
    g = gates.astype(jnp.float32) + gate_b.astype(jnp.float32)
    g = GATE_CAP * jnp.tanh(g / GATE_CAP)
    g = g.reshape(B, L, 4, ML_HEADS).transpose(2, 0, 3, 1)
    return q, k, v, o, g[0], jax.nn.log_sigmoid(g[1]), g[2], jax.nn.log_sigmoid(g[3])


def mlstm_output(h, o, norm_g):
    B, H, L, dv = h.shape
    hn = h * lax.rsqrt(jnp.mean(h * h, axis=-1, keepdims=True) + EPS)
    hn = hn.transpose(0, 2, 1, 3).reshape(B, L, H * dv) * norm_g.astype(jnp.float32)
    return (hn * jax.nn.sigmoid(o.astype(jnp.float32))).astype(o.dtype)


def mlstm_group(z_lat, z_ctx, gate_b, norm_g):
    q, k, v, o, i_f, lf_f, i_b, lf_b = mlstm_inputs(z_lat, gate_b)
    qc, kc, vc, oc, ic_f, lfc_f, ic_b, lfc_b = mlstm_inputs(z_ctx, gate_b)
    B = z_lat.shape[0]
    zero = (jnp.zeros((B, ML_HEADS, ML_DK, ML_DV), jnp.float32),
            jnp.zeros((B, ML_HEADS, ML_DK), jnp.float32),
            jnp.zeros((B, ML_HEADS), jnp.float32))

    def rev(a):
        return jnp.flip(a, axis=2)

    hc_f, st_f = mlstm_chunkwise(qc, kc, vc, ic_f, lfc_f, zero)
    hc_b, st_b = mlstm_chunkwise(rev(qc), rev(kc), rev(vc), rev(ic_b), rev(lfc_b), zero)
    h_f, _ = mlstm_chunkwise(q, k, v, i_f, lf_f, st_f)
    h_b, _ = mlstm_chunkwise(rev(q), rev(k), rev(v), rev(i_b), rev(lf_b), st_b)
    y_lat = mlstm_output(h_f + rev(h_b), o, norm_g)
    return y_lat, hc_f + rev(hc_b), oc


def ec_moe(h, w_router, w_gate, w_up, w_down):
    B, L, D = h.shape
    cap = CAP_FACTOR * L // N_EXPERTS
    aff = jax.nn.softmax((h @ w_router).astype(jnp.float32), axis=-1)
    g, idx = lax.top_k(jnp.swapaxes(aff, 1, 2), cap)
    xs = jax.vmap(lambda hb, ib: hb[ib])(h, idx)
    a = jnp.einsum('becd,edf->becf', xs, w_gate)
    u = jnp.einsum('becd,edf->becf', xs, w_up)
    ye = jnp.einsum('becf,efd->becd', jax.nn.silu(a) * u, w_down) * g[..., None].astype(h.dtype)
    return jax.vmap(lambda ib, yb: jnp.zeros((L, D), yb.dtype).at[ib.reshape(-1)].add(yb.reshape(-1, D)))(idx, ye)


def setup_inputs(seed: int = 0) -> dict:
    key = jax.random.key(seed)
    ks = iter(jax.random.split(key, 40))

    def nrm(shape, scale):
        return scale * jax.random.normal(next(ks), shape, jnp.float32)

    D, E, F, H = D_MODEL, N_EXPERTS, EXPERT_FF, ML_HEADS
    f_bias = jnp.linspace(3.0, 6.0, H, dtype=jnp.float32)[None, :]
    ml_gate_b = jnp.concatenate([nrm((DEPTH, H), 0.1), f_bias + nrm((DEPTH, H), 0.1),
                                 nrm((DEPTH, H), 0.1), f_bias + nrm((DEPTH, H), 0.1)], axis=-1)
    return {
        'x': nrm((BATCH, SEQ, D), 1.0),
        'c': nrm((BATCH, D), 1.0),
        'ctx': nrm((BATCH, CTX_LEN, D), 1.0),
        'c_ctx': nrm((D,), 1.0),
        'w_mod': nrm((DEPTH, D, 6 * D), D ** -0.5),
        'b_mod': nrm((DEPTH, 6 * D), 0.01),
        'pre_norm1': 1.0 + nrm((DEPTH, D), 0.05),
        'post_norm1': 1.0 + nrm((DEPTH, D), 0.05),
        'pre_norm2': 1.0 + nrm((DEPTH, D), 0.05),
        'post_norm2': 1.0 + nrm((DEPTH, D), 0.05),
        'w_in': nrm((DEPTH, D, IN_COLS), D ** -0.5),
        'conv_w': nrm((DEPTH, SHORT_W, HY_COLS), SHORT_W ** -0.5),
        'conv_b': nrm((DEPTH, HY_COLS), 0.01),
        'filt_w1': nrm((DEPTH, FILTER_EMB, FILTER_HIDDEN), FILTER_EMB ** -0.5),
        'filt_b1': nrm((DEPTH, FILTER_HIDDEN), 0.1),
        'filt_w2': nrm((DEPTH, FILTER_HIDDEN, FILTER_HIDDEN), FILTER_HIDDEN ** -0.5),
        'filt_b2': nrm((DEPTH, FILTER_HIDDEN), 0.1),
        'filt_w3': nrm((DEPTH, FILTER_HIDDEN, FILTER_HIDDEN), FILTER_HIDDEN ** -0.5),
        'filt_b3': nrm((DEPTH, FILTER_HIDDEN), 0.1),
        'filt_w4': nrm((DEPTH, FILTER_HIDDEN, 2 * HY_WIDTH), 0.02),
        'filt_freq': 1.0 + nrm((DEPTH, FILTER_HIDDEN), 0.05),
        'hyena_bias': nrm((DEPTH, HY_WIDTH), 0.5),
        'ml_gate_b': ml_gate_b,
        'ml_norm': 1.0 + nrm((DEPTH, ML_WIDTH), 0.05),
        'w_out': nrm((DEPTH, MIX_WIDTH, D), MIX_WIDTH ** -0.5),
        'w_router': nrm((DEPTH, D, E), D ** -0.5),
        'w_exp_gate': nrm((DEPTH, E, D, F), D ** -0.5),
        'w_exp_up': nrm((DEPTH, E, D, F), D ** -0.5),
        'w_exp_down': nrm((DEPTH, E, F, D), F ** -0.5),
    }


def reference(x, c, ctx, c_ctx, w_mod, b_mod, pre_norm1, post_norm1, pre_norm2, post_norm2,
              w_in, conv_w, conv_b, filt_w1, filt_b1, filt_w2, filt_b2, filt_w3, filt_b3, filt_w4,
              filt_freq, hyena_bias, ml_gate_b, ml_norm, w_out, w_router, w_exp_gate, w_exp_up,
              w_exp_down):
    B, L, D = x.shape
    rows = L // GRID_W
    x = x + pos_embed_2d(rows, GRID_W, D).astype(x.dtype)[None]
    xc = ctx
    for li in range(DEPTH):
        update_ctx = li < DEPTH - 1
        mod = (jax.nn.silu(c) @ w_mod[li] + b_mod[li])[:, None, :]
        mod_c = (jax.nn.silu(c_ctx)[None] @ w_mod[li] + b_mod[li])[:, None, :]
        sh1, sc1, g1, sh2, sc2, g2 = jnp.split(mod, 6, axis=-1)
        csh1, csc1, cg1, csh2, csc2, cg2 = jnp.split(mod_c, 6, axis=-1)
        hyena_w = (conv_w[li], conv_b[li], filt_w1[li], filt_b1[li], filt_w2[li], filt_b2[li],
                   filt_w3[li], filt_b3[li], filt_w4[li], filt_freq[li], hyena_bias[li])

        h = modulated_norm(x, pre_norm1[li], sh1, sc1)
        hc = modulated_norm(xc, pre_norm1[li], csh1, csc1)
        z = h @ w_in[li]
        zc_ml = hc @ w_in[li][:, HY_COLS:]
        y_hy = hyena_group(z[..., :HY_COLS], *hyena_w)
        y_ml, hc_ml, oc = mlstm_group(z[..., HY_COLS:], zc_ml, ml_gate_b[li], ml_norm[li])
        y = jnp.concatenate([y_hy, y_ml], axis=-1) @ w_out[li]
        x = x + g1 * rmsnorm(y, post_norm1[li])

        if update_ctx:
            yc_hy = hyena_group(hc @ w_in[li][:, :HY_COLS], *hyena_w)
            yc_ml = mlstm_output(hc_ml, oc, ml_norm[li])
            yc = jnp.concatenate([yc_hy, yc_ml], axis=-1) @ w_out[li]
            xc = xc + cg1 * rmsnorm(yc, post_norm1[li])
            hc2 = modulated_norm(xc, pre_norm2[li], csh2, csc2)
            xc = xc + cg2 * rmsnorm(ec_moe(hc2, w_router[li], w_exp_gate[li], w_exp_up[li], w_exp_down[li]), post_norm2[li])

        h2 = modulated_norm(x, pre_norm2[li], sh2, sc2)
        y2 = ec_moe(h2, w_router[li], w_exp_gate[li], w_exp_up[li], w_exp_down[li])
        x = x + g2 * rmsnorm(y2, post_norm2[li])
    return x
```

```python
import functools
import math

import jax
import jax.numpy as jnp
from jax import lax
from jax.experimental import pallas as pl
from jax.experimental.pallas import tpu as pltpu

D_MODEL = 1024
GRID_W = 64
HY_WIDTH = 512
ML_HEADS = 4
ML_DK = 64
ML_DV = 128
ML_WIDTH = ML_HEADS * ML_DV
HY_COLS = 3 * HY_WIDTH
QK_COLS = ML_HEADS * ML_DK
N_GATES = 4 * ML_HEADS
FILTER_EMB = 33
DECAY_TARGET = 1e-2
FAST_DECAY_PCT = 0.3
SLOW_DECAY_PCT = 1.5
CHUNK = 128
GATE_CAP = 15.0
N_EXPERTS = 16
CAP_FACTOR = 2
EPS = 1e-6

F32 = jnp.float32
BF16 = jnp.bfloat16

TOKEN_TILE = 512
VMEM_LIMIT = 56 * 1024 * 1024


def _cparams(sem):
    return pltpu.CompilerParams(dimension_semantics=sem, vmem_limit_bytes=VMEM_LIMIT)


def _rms(xf, g):
    return xf * lax.rsqrt(jnp.mean(xf * xf, axis=-1, keepdims=True) + EPS) * g


def _bdot(a, b):
    return jnp.dot(a.astype(BF16), b.astype(BF16), preferred_element_type=F32)


def _bdot_nt(a, b):
    return lax.dot_general(a.astype(BF16), b.astype(BF16), (((1,), (1,)), ((), ())),
                           preferred_element_type=F32)


def _pe_tables_kernel(omega_ref, er_ref, ec_ref):
    quarter = omega_ref.shape[1]
    om = omega_ref[...]
    for ref in (er_ref, ec_ref):
        n = ref.shape[0]
        pos = lax.broadcasted_iota(jnp.int32, (n, quarter), 0).astype(F32)
        ang = pos * om
        ref[:, :quarter] = jnp.sin(ang)
        ref[:, quarter:] = jnp.cos(ang)


def _pe_tables(rows, cols, dim):
    quarter = dim // 4
    omega = (1.0 / (10000.0 ** (jnp.arange(quarter, dtype=F32) / quarter)))[None, :]
    return pl.pallas_call(
        _pe_tables_kernel,
        out_shape=(jax.ShapeDtypeStruct((rows, dim // 2), F32),
                   jax.ShapeDtypeStruct((cols, dim // 2), F32)),
        name="pe_tables",
    )(omega)


def _pe_tile(er_blk, ec):
    nr, half = er_blk.shape
    row_part = jnp.broadcast_to(er_blk[:, None, :], (nr, GRID_W, half)).reshape(nr * GRID_W, half)
    col_part = jnp.broadcast_to(ec[None, :, :], (nr, GRID_W, half)).reshape(nr * GRID_W, half)
    return jnp.concatenate([row_part, col_part], axis=-1)


def _mod_kernel(c_ref, w_ref, b_ref, o_ref):
    c = c_ref[...]
    s = c * jax.nn.sigmoid(c)
    o_ref[...] = _bdot(s, w_ref[...]) + b_ref[...]


def _modulation(cc, w_mod, b_mod):
    rows, d = cc.shape
    n = w_mod.shape[1]
    tn = 512
    return pl.pallas_call(
        _mod_kernel,
        grid=(n // tn,),
        in_specs=[pl.BlockSpec((rows, d), lambda j: (0, 0)),
                  pl.BlockSpec((d, tn), lambda j: (0, j)),
                  pl.BlockSpec((1, tn), lambda j: (0, j))],
        out_specs=pl.BlockSpec((rows, tn), lambda j: (0, j)),
        out_shape=jax.ShapeDtypeStruct((rows, n), F32),
        compiler_params=_cparams(("arbitrary",)),
        name="modulation",
    )(cc, w_mod, b_mod[None, :])


def _in_proj_kernel(*refs, with_hyena):
    if with_hyena:
        (x_ref, er_ref, ec_ref, sh_ref, sc_ref, g_ref, whq_ref, wkt_ref, wg_ref, wgt_ref,
         zhy_ref, q_ref, kt_ref, v_ref, o_ref, gate_ref, gatet_ref) = refs
        xf = x_ref[0] + _pe_tile(er_ref[...], ec_ref[...])
    else:
        (x_ref, sh_ref, sc_ref, g_ref, whq_ref, wkt_ref, wg_ref, wgt_ref,
         q_ref, kt_ref, v_ref, o_ref, gate_ref, gatet_ref) = refs
        xf = x_ref[0]
    h = _rms(xf, g_ref[...]) * (1.0 + sc_ref[0]) + sh_ref[0]
    hb = h.astype(BF16)
    z = jnp.dot(hb, whq_ref[...], preferred_element_type=F32)
    off = 0
    if with_hyena:
        zhy_ref[0] = z[:, :HY_COLS]
        off = HY_COLS
    qs = z[:, off:off + QK_COLS] * (ML_DK ** -0.5)
    for hd in range(ML_HEADS):
        q_ref[0, hd] = qs[:, hd * ML_DK:(hd + 1) * ML_DK].astype(BF16)
    off += QK_COLS
    v_ref[0] = z[:, off:off + ML_WIDTH].astype(BF16)
    off += ML_WIDTH
    o_ref[0] = z[:, off:off + ML_WIDTH]
    kt = lax.dot_general(wkt_ref[...], hb, (((1,), (1,)), ((), ())), preferred_element_type=F32)
    for hd in range(ML_HEADS):
        kt_ref[0, hd] = kt[hd * ML_DK:(hd + 1) * ML_DK, :].astype(BF16)
    gate_ref[0] = jnp.dot(hb, wg_ref[...], preferred_element_type=F32)
    gatet_ref[0] = lax.dot_general(wgt_ref[...], hb, (((1,), (1,)), ((), ())),
                                   preferred_element_type=F32)


def _in_proj(x, tabs, sh, sc, g, w_in, tm, with_hyena):
    B, L, D = x.shape
    w_hy = w_in[:, :HY_COLS]
    w_q = w_in[:, HY_COLS:HY_COLS + QK_COLS]
    w_k = w_in[:, HY_COLS + QK_COLS:HY_COLS + 2 * QK_COLS]
    w_vo = w_in[:, HY_COLS + 2 * QK_COLS:HY_COLS + 2 * QK_COLS + 2 * ML_WIDTH]
    w_g = w_in[:, HY_COLS + 2 * QK_COLS + 2 * ML_WIDTH:]
    parts = ([w_hy] if with_hyena else []) + [w_q, w_vo]
    whq = jnp.concatenate(parts, axis=1).astype(BF16)
    wkt = w_k.T.astype(BF16)
    wg = w_g.astype(BF16)
    wgt = w_g.T.astype(BF16)
    nb = sh.shape[0]
    mod_map = (lambda b, i: (b, 0, 0)) if nb > 1 else (lambda b, i: (0, 0, 0))
    full = lambda a: pl.BlockSpec(a.shape, lambda b, i: (0,) * a.ndim)
    in_specs = [pl.BlockSpec((1, tm, D), lambda b, i: (b, i, 0))]
    args = [x]
    if with_hyena:
        er, ec = tabs
        in_specs += [pl.BlockSpec((tm // GRID_W, D // 2), lambda b, i: (i, 0)), full(ec)]
        args += [er, ec]
    in_specs += [pl.BlockSpec((1, 1, D), mod_map), pl.BlockSpec((1, 1, D), mod_map),
                 full(g), full(whq), full(wkt), full(wg), full(wgt)]
    args += [sh, sc, g, whq, wkt, wg, wgt]
    out_shape, out_specs = [], []
    if with_hyena:
        out_shape.append(jax.ShapeDtypeStruct((B, L, HY_COLS), F32))
        out_specs.append(pl.BlockSpec((1, tm, HY_COLS), lambda b, i: (b, i, 0)))
    out_shape += [jax.ShapeDtypeStruct((B, ML_HEADS, L, ML_DK), BF16),
                  jax.ShapeDtypeStruct((B, ML_HEADS, ML_DK, L), BF16),
                  jax.ShapeDtypeStruct((B, L, ML_WIDTH), BF16),
                  jax.ShapeDtypeStruct((B, L, ML_WIDTH), F32),
                  jax.ShapeDtypeStruct((B, L, N_GATES), F32),
                  jax.ShapeDtypeStruct((B, N_GATES, L), F32)]
    out_specs += [pl.BlockSpec((1, ML_HEADS, tm, ML_DK), lambda b, i: (b, 0, i, 0)),
                  pl.BlockSpec((1, ML_HEADS, ML_DK, tm), lambda b, i: (b, 0, 0, i)),
                  pl.BlockSpec((1, tm, ML_WIDTH), lambda b, i: (b, i, 0)),
                  pl.BlockSpec((1, tm, ML_WIDTH), lambda b, i: (b, i, 0)),
                  pl.BlockSpec((1, tm, N_GATES), lambda b, i: (b, i, 0)),
                  pl.BlockSpec((1, N_GATES, tm), lambda b, i: (b, 0, i))]
    return pl.pallas_call(
        functools.partial(_in_proj_kernel, with_hyena=with_hyena),
        grid=(B, L // tm),
        in_specs=in_specs,
        out_specs=out_specs,
        out_shape=out_shape,
        compiler_params=_cparams(("parallel", "parallel")),
        name="in_proj_hy" if with_hyena else "in_proj_ctx",
    )(*args)


def _short_conv(z, w, b):
    L = z.shape[1]
    zp = jnp.pad(z, ((0, 0), (1, 1), (0, 0)))
    return zp[:, :L] * w[0] + zp[:, 1:L + 1] * w[1] + zp[:, 2:] * w[2] + b


def _hyena_filter(L, w1, b1, w2, b2, w3, b3, w4, freq):
    t = jnp.linspace(0.0, 1.0, L, dtype=F32)[:, None]
    bands = (FILTER_EMB - 1) // 2
    w = 2.0 * math.pi * jnp.arange(L, dtype=F32) / L
    f = jnp.linspace(1e-4, bands - 1, bands, dtype=F32)
    ang = w[:, None] * f[None, :]
    z = jnp.concatenate([t, jnp.cos(ang), -jnp.sin(ang)], axis=-1)
    h = jnp.sin(freq * (z @ w1 + b1))
    h = jnp.sin(freq * (h @ w2 + b2))
    h = jnp.sin(freq * (h @ w3 + b3))
    h = (h @ w4).astype(F32)
    min_decay = math.log(DECAY_TARGET) / SLOW_DECAY_PCT
    max_decay = math.log(DECAY_TARGET) / FAST_DECAY_PCT
    deltas = jnp.linspace(min_decay, max_decay, HY_WIDTH, dtype=F32)
    decay = jnp.exp(-t * jnp.abs(deltas)[None, :])
    h_fwd = h[:, :HY_WIDTH] * decay
    h_bwd = h[:, HY_WIDTH:] * decay
    return jnp.concatenate([h_fwd, jnp.zeros((1, HY_WIDTH), F32), h_bwd[:0:-1]], axis=0)


def _long_conv(v, k_circ, bias):
    L = v.shape[1]
    spec = jnp.fft.rfft(v, n=2 * L, axis=1) * jnp.fft.rfft(k_circ, axis=0)[None]
    y = jnp.fft.irfft(spec, n=2 * L, axis=1)[:, :L]
    return y + v * bias


def _hyena_group(z_hy, conv_w, conv_b, fw1, fb1, fw2, fb2, fw3, fb3, fw4, ffreq, hy_bias):
    z = _short_conv(z_hy, conv_w, conv_b)
    x0, x1, v = jnp.split(z, 3, axis=-1)
    k_circ = _hyena_filter(z.shape[1], fw1, fb1, fw2, fb2, fw3, fb3, fw4, ffreq)
    return x0 * _long_conv(v * x1, k_circ, hy_bias)


def _split3(a):
    hi = a.astype(BF16)
    r1 = a - hi.astype(F32)
    mid = r1.astype(BF16)
    lo = (r1 - mid.astype(F32)).astype(BF16)
    return hi, mid, lo


def _exact_dot_left(tri_bf, a):
    hi, mid, lo = _split3(a)
    d = lambda p: jnp.dot(tri_bf, p, preferred_element_type=F32)
    return (d(lo) + d(mid)) + d(hi)


def _exact_dot_right(a, tri_bf):
    hi, mid, lo = _split3(a)
    d = lambda p: jnp.dot(p, tri_bf, preferred_element_type=F32)
    return (d(lo) + d(mid)) + d(hi)


def _soft_gates(g):
    g = GATE_CAP * jnp.tanh(g * (1.0 / GATE_CAP))
    logsig = jnp.minimum(g, 0.0) - jnp.log1p(jnp.exp(-jnp.abs(g)))
    return g, logsig


def _mlstm_kernel(q_ref, kt_ref, v_ref, gate_ref, gatet_ref, qc_ref, ktc_ref, vc_ref, gatec_ref,
                  gatetc_ref, gb_ref, gbt_ref, h_ref, cf_ref, cb_ref, mf_ref, mb_ref):
    hd = pl.program_id(1)
    T = CHUNK
    L = q_ref.shape[2]
    Lc = qc_ref.shape[2]
    nc, ncc = L // T, Lc // T
    row = lax.broadcasted_iota(jnp.int32, (T, T), 0)
    col = lax.broadcasted_iota(jnp.int32, (T, T), 1)
    lo_mask = col <= row
    up_mask = col >= row
    lo_bf = lo_mask.astype(BF16)
    up_bf = up_mask.astype(BF16)
    lane = lax.broadcasted_iota(jnp.int32, (T, ML_DV), 1)
    ones_col = (lane == 0).astype(BF16)
    gsel_c = lax.broadcasted_iota(jnp.int32, (1, N_GATES), 1)
    gsel_r = lax.broadcasted_iota(jnp.int32, (N_GATES, 1), 0)

    cf_ref[...] = jnp.zeros_like(cf_ref)
    cb_ref[...] = jnp.zeros_like(cb_ref)
    mf_ref[...] = jnp.zeros_like(mf_ref)
    mb_ref[...] = jnp.zeros_like(mb_ref)

    def pick_col(a, k):
        return jnp.sum(jnp.where(gsel_c == k, a, 0.0), axis=1, keepdims=True)

    def pick_row(a, k):
        return jnp.sum(jnp.where(gsel_r == k, a, 0.0), axis=0, keepdims=True)

    def step(q, kt, v, g_c, g_r, c_ref, m_ref, backward):
        gi, gf = (2, 3) if backward else (0, 1)
        cap_c, ls_c = _soft_gates(g_c + gb_ref[...])
        cap_r, ls_r = _soft_gates(g_r + gbt_ref[...])
        i_c = pick_col(cap_c, gi * ML_HEADS + hd)
        i_r = pick_row(cap_r, gi * ML_HEADS + hd)
        tri_l, tri_r, mask = (up_bf, lo_bf, up_mask) if backward else (lo_bf, up_bf, lo_mask)
        b_c = pick_col(_exact_dot_left(tri_l, ls_c), gf * ML_HEADS + hd)
        b_r = pick_row(_exact_dot_right(ls_r, tri_r), gf * ML_HEADS + hd)
        b_end = b_r[:, 0:1] if backward else b_r[:, T - 1:T]
        m = m_ref[...]
        dlog = jnp.where(mask, b_c - b_r + i_r, -jnp.inf)
        m_inter = b_c + m
        m_t = jnp.maximum(m_inter, jnp.max(dlog, axis=-1, keepdims=True))
        w_intra = jnp.exp(dlog - m_t)
        w_inter = jnp.exp(m_inter - m_t)
        s = jnp.dot(q, kt, preferred_element_type=F32) * w_intra
        v_aug = jnp.concatenate([v, ones_col], axis=1)
        c_aug = c_ref[...]
        res = (jnp.dot(s.astype(BF16), v_aug, preferred_element_type=F32)
               + w_inter * jnp.dot(q, c_aug.astype(BF16), preferred_element_type=F32))
        num = res[:, :ML_DV]
        den = res[:, ML_DV:ML_DV + 1]
        h = num / jnp.maximum(jnp.abs(den), jnp.exp(-m_t))
        g_row = b_end - b_r + i_r
        m_new = jnp.maximum(b_end + m, jnp.max(g_row, axis=-1, keepdims=True))
        w_s = jnp.exp(g_row - m_new)
        w_c = jnp.exp(b_end + m - m_new)
        kw = (kt.astype(F32) * w_s).astype(BF16)
        c_ref[...] = w_c * c_aug + jnp.dot(kw, v_aug, preferred_element_type=F32)
        m_ref[...] = m_new
        return h

    for j in range(ncc):
        jb = ncc - 1 - j
        step(qc_ref[0, 0, j * T:(j + 1) * T, :], ktc_ref[0, 0, :, j * T:(j + 1) * T],
             vc_ref[0, j * T:(j + 1) * T, :], gatec_ref[0, j * T:(j + 1) * T, :],
             gatetc_ref[0, :, j * T:(j + 1) * T], cf_ref, mf_ref, False)
        step(qc_ref[0, 0, jb * T:(jb + 1) * T, :], ktc_ref[0, 0, :, jb * T:(jb + 1) * T],
             vc_ref[0, jb * T:(jb + 1) * T, :], gatec_ref[0, jb * T:(jb + 1) * T, :],
             gatetc_ref[0, :, jb * T:(jb + 1) * T], cb_ref, mb_ref, True)

    def latent_pair(j, accumulate):
        for backward in (False, True):
            jj = (nc - 1 - j) if backward else j
            r0 = pl.multiple_of(jj * T, T)
            rs = pl.ds(r0, T)
            h = step(q_ref[0, 0, rs, :], kt_ref[0, 0, :, rs], v_ref[0, rs, :], gate_ref[0, rs, :],
                     gatet_ref[0, :, rs], cb_ref if backward else cf_ref,
                     mb_ref if backward else mf_ref, backward)
            if accumulate:
                h_ref[0, rs, :] = h_ref[0, rs, :] + h
            else:
                h_ref[0, rs, :] = h

    def first_half(j, carry):
        latent_pair(j, False)
        return carry

    def second_half(j, carry):
        latent_pair(j, True)
        return carry

    lax.fori_loop(0, nc // 2, first_half, 0)
    lax.fori_loop(nc // 2, nc, second_half, 0)


def _mlstm(q, kt, v, gate, gatet, qc, ktc, vc, gatec, gatetc, gate_b):
    B, H, L, dk = q.shape
    Lc = qc.shape[2]
    gb = gate_b[None, :]
    gbt = gate_b[:, None]
    return pl.pallas_call(
        _mlstm_kernel,
        grid=(B, H),
        in_specs=[pl.BlockSpec((1, 1, L, dk), lambda b, h: (b, h, 0, 0)),
                  pl.BlockSpec((1, 1, dk, L), lambda b, h: (b, h, 0, 0)),
                  pl.BlockSpec((1, L, ML_DV), lambda b, h: (b, 0, h)),
                  pl.BlockSpec((1, L, N_GATES), lambda b, h: (b, 0, 0)),
                  pl.BlockSpec((1, N_GATES, L), lambda b, h: (b, 0, 0)),
                  pl.BlockSpec((1, 1, Lc, dk), lambda b, h: (b, h, 0, 0)),
                  pl.BlockSpec((1, 1, dk, Lc), lambda b, h: (b, h, 0, 0)),
                  pl.BlockSpec((1, Lc, ML_DV), lambda b, h: (b, 0, h)),
                  pl.BlockSpec((1, Lc, N_GATES), lambda b, h: (b, 0, 0)),
                  pl.BlockSpec((1, N_GATES, Lc), lambda b, h: (b, 0, 0)),
                  pl.BlockSpec((1, N_GATES), lambda b, h: (0, 0)),
                  pl.BlockSpec((N_GATES, 1), lambda b, h: (0, 0))],
        out_specs=pl.BlockSpec((1, L, ML_DV), lambda b, h: (b, 0, h)),
        out_shape=jax.ShapeDtypeStruct((B, L, ML_WIDTH), F32),
        scratch_shapes=[pltpu.VMEM((dk, 2 * ML_DV), F32), pltpu.VMEM((dk, 2 * ML_DV), F32),
                        pltpu.VMEM((1, 1), F32), pltpu.VMEM((1, 1), F32)],
        compiler_params=_cparams(("parallel", "parallel")),
        name="mlstm_scan",
    )(q, kt, v, gate, gatet, qc, ktc, vc, gatec, gatetc, gb, gbt)


def _out_proj_kernel(yhy_ref, hs_ref, o_ref, x_ref, er_ref, ec_ref, g1_ref, sh2_ref, sc2_ref,
                     mln_ref, post1_ref, pre2_ref, wout_ref, wr_ref,
                     x1_ref, h2_ref, aff_ref):
    hs = hs_ref[0]
    parts = [yhy_ref[0]]
    for hd in range(ML_HEADS):
        hh = hs[:, hd * ML_DV:(hd + 1) * ML_DV]
        parts.append(hh * lax.rsqrt(jnp.mean(hh * hh, axis=-1, keepdims=True) + EPS))
    hn = jnp.concatenate(parts[1:], axis=-1) * mln_ref[...]
    y_ml = hn * jax.nn.sigmoid(o_ref[0])
    y = jnp.concatenate([parts[0], y_ml], axis=-1)
    yo = jnp.dot(y.astype(BF16), wout_ref[...], preferred_element_type=F32)
    xf = x_ref[0] + _pe_tile(er_ref[...], ec_ref[...])
    x1 = xf + g1_ref[0] * _rms(yo, post1_ref[...])
    x1_ref[0] = x1
    h2 = (_rms(x1, pre2_ref[...]) * (1.0 + sc2_ref[0]) + sh2_ref[0]).astype(BF16)
    h2_ref[0] = h2
    logits = jnp.dot(h2, wr_ref[...], preferred_element_type=F32)
    mx = jnp.max(logits, axis=-1, keepdims=True)
    ex = jnp.exp(logits - mx)
    aff_ref[0] = ex / jnp.sum(ex, axis=-1, keepdims=True)


def _out_proj(y_hy, hsum, o, x, tabs, g1, sh2, sc2, ml_norm, post1, pre2, w_out, w_router, tm):
    B, L, D = x.shape
    er, ec = tabs
    E = w_router.shape[1]
    full = lambda a: pl.BlockSpec(a.shape, lambda b, i: (0,) * a.ndim)
    tok = lambda w: pl.BlockSpec((1, tm, w), lambda b, i: (b, i, 0))
    modspec = pl.BlockSpec((1, 1, D), lambda b, i: (b, 0, 0))
    wout = w_out.astype(BF16)
    wr = w_router.astype(BF16)
    return pl.pallas_call(
        _out_proj_kernel,
        grid=(B, L // tm),
        in_specs=[tok(HY_WIDTH), tok(ML_WIDTH), tok(ML_WIDTH), tok(D),
                  pl.BlockSpec((tm // GRID_W, D // 2), lambda b, i: (i, 0)), full(ec),
                  modspec, modspec, modspec, full(ml_norm), full(post1), full(pre2),
                  full(wout), full(wr)],
        out_specs=[tok(D), tok(D), tok(E)],
        out_shape=[jax.ShapeDtypeStruct((B, L, D), F32), jax.ShapeDtypeStruct((B, L, D), BF16),
                   jax.ShapeDtypeStruct((B, L, E), F32)],
        compiler_params=_cparams(("parallel", "parallel")),
        name="out_proj_router",
    )(y_hy, hsum, o, x, er, ec, g1, sh2, sc2, ml_norm, post1, pre2, wout, wr)


def _expert_kernel(xs_ref, g_ref, wg_ref, wu_ref, wd_ref, ye_ref, *, m_split):
    f = pl.program_id(1)
    nf = pl.num_programs(1)
    wg = wg_ref[0].astype(BF16)
    wu = wu_ref[0].astype(BF16)
    wd = wd_ref[0].astype(BF16)
    M = xs_ref.shape[1]
    mt = M // m_split
    for mi in range(m_split):
        rs = slice(mi * mt, (mi + 1) * mt)
        xs = xs_ref[0, rs, :]
        a = jnp.dot(xs, wg, preferred_element_type=F32)
        u = jnp.dot(xs, wu, preferred_element_type=F32)
        hmid = (a * jax.nn.sigmoid(a) * u).astype(BF16)
        part = jnp.dot(hmid, wd, preferred_element_type=F32)

        @pl.when(f == 0)
        def _():
            ye_ref[0, rs, :] = part

        @pl.when(f > 0)
        def _():
            ye_ref[0, rs, :] = ye_ref[0, rs, :] + part

    @pl.when(f == nf - 1)
    def _():
        ye_ref[0] = ye_ref[0] * g_ref[0]


def _expert_ffn(xs, g, w_gate, w_up, w_down, tf):
    E, M, D = xs.shape
    F = w_gate.shape[2]
    return pl.pallas_call(
        functools.partial(_expert_kernel, m_split=2),
        grid=(E, F // tf),
        in_specs=[pl.BlockSpec((1, M, D), lambda e, f: (e, 0, 0)),
                  pl.BlockSpec((1, M, 1), lambda e, f: (e, 0, 0)),
                  pl.BlockSpec((1, D, tf), lambda e, f: (e, 0, f)),
                  pl.BlockSpec((1, D, tf), lambda e, f: (e, 0, f)),
                  pl.BlockSpec((1, tf, D), lambda e, f: (e, f, 0))],
        out_specs=pl.BlockSpec((1, M, D), lambda e, f: (e, 0, 0)),
        out_shape=jax.ShapeDtypeStruct((E, M, D), F32),
        compiler_params=_cparams(("parallel", "arbitrary")),
        name="expert_ffn",
    )(xs, g, w_gate, w_up, w_down)


def _final_kernel(x1_ref, y2_ref, g2_ref, post2_ref, o_ref):
    o_ref[0] = x1_ref[0] + g2_ref[0] * _rms(y2_ref[0], post2_ref[...])


def _final(x1, y2, g2, post2, tm):
    B, L, D = x1.shape
    tok = pl.BlockSpec((1, tm, D), lambda b, i: (b, i, 0))
    return pl.pallas_call(
        _final_kernel,
        grid=(B, L // tm),
        in_specs=[tok, tok, pl.BlockSpec((1, 1, D), lambda b, i: (b, 0, 0)),
                  pl.BlockSpec((1, D), lambda b, i: (0, 0))],
        out_specs=tok,
        out_shape=jax.ShapeDtypeStruct((B, L, D), F32),
        compiler_params=_cparams(("parallel", "parallel")),
        name="final_residual",
    )(x1, y2, g2, post2)


def kernel(x, c, ctx, c_ctx, w_mod, b_mod, pre_norm1, post_norm1, pre_norm2, post_norm2, w_in, conv_w, conv_b, filt_w1, filt_b1, filt_w2, filt_b2, filt_w3, filt_b3, filt_w4, filt_freq, hyena_bias, ml_gate_b, ml_norm, w_out, w_router, w_exp_gate, w_exp_up, w_exp_down):
    B, L, D = x.shape
    depth = w_mod.shape[0]
    assert depth == 1, "single-layer block"
    li = 0
    tabs = _pe_tables(L // GRID_W, GRID_W, D)

    cc = jnp.concatenate([c, c_ctx[None], jnp.zeros((8 - B - 1, D), F32)], axis=0)
    mod = _modulation(cc, w_mod[li], b_mod[li])
    chunks = [mod[:, k * D:(k + 1) * D] for k in range(6)]
    sh1, sc1, g1, sh2, sc2, g2 = [m[:B, None, :] for m in chunks]
    csh1, csc1 = chunks[0][B:B + 1, None, :], chunks[1][B:B + 1, None, :]

    pre1 = pre_norm1[li][None, :]
    zhy, q, kt, v, o, gate, gatet = _in_proj(x, tabs, sh1, sc1, pre1, w_in[li], TOKEN_TILE, True)
    qc, ktc, vc, _, gatec, gatetc = _in_proj(ctx, None, csh1, csc1, pre1, w_in[li], ctx.shape[1], False)

    y_hy = _hyena_group(zhy, conv_w[li], conv_b[li], filt_w1[li], filt_b1[li], filt_w2[li],
                        filt_b2[li], filt_w3[li], filt_b3[li], filt_w4[li], filt_freq[li],
                        hyena_bias[li])
    hsum = _mlstm(q, kt, v, gate, gatet, qc, ktc, vc, gatec, gatetc, ml_gate_b[li])

    x1, h2, aff = _out_proj(y_hy, hsum, o, x, tabs, g1, sh2, sc2, ml_norm[li][None, :],
                            post_norm1[li][None, :], pre_norm2[li][None, :], w_out[li],
                            w_router[li], TOKEN_TILE)

    cap = CAP_FACTOR * L // N_EXPERTS
    gsel, idx = lax.top_k(jnp.swapaxes(aff, 1, 2), cap)
    xs = jax.vmap(lambda hb, ib: hb[ib])(h2, idx)
    xs = jnp.swapaxes(xs, 0, 1).reshape(N_EXPERTS, B * cap, D)
    gs = jnp.swapaxes(gsel, 0, 1).reshape(N_EXPERTS, B * cap, 1)
    ye = _expert_ffn(xs, gs, w_exp_gate[li], w_exp_up[li], w_exp_down[li], 512)
    ye = jnp.swapaxes(ye.reshape(N_EXPERTS, B, cap, D), 0, 1)
    y2 = jax.vmap(lambda ib, yb: jnp.zeros((L, D), yb.dtype).at[ib.reshape(-1)].add(yb.reshape(-1, D)))(idx, ye)

    return _final(x1, y2, g2, post_norm2[li][None, :], TOKEN_TILE)
```

```python
import functools
import math

import jax
import jax.numpy as jnp
from jax import lax
from jax.experimental import pallas as pl
from jax.experimental.pallas import tpu as pltpu

D_MODEL = 1024
GRID_W = 64
HY_WIDTH = 512
ML_HEADS = 4
ML_DK = 64
ML_DV = 128
ML_WIDTH = ML_HEADS * ML_DV
HY_COLS = 3 * HY_WIDTH
QK_COLS = ML_HEADS * ML_DK
N_GATES = 4 * ML_HEADS
FILTER_EMB = 33
DECAY_TARGET = 1e-2
FAST_DECAY_PCT = 0.3
SLOW_DECAY_PCT = 1.5
CHUNK = 128
GATE_CAP = 15.0
N_EXPERTS = 16
CAP_FACTOR = 2
EPS = 1e-6

F32 = jnp.float32
BF16 = jnp.bfloat16

TOKEN_TILE = 512
VMEM_LIMIT = 56 * 1024 * 1024


def _cparams(sem):
    return pltpu.CompilerParams(dimension_semantics=sem, vmem_limit_bytes=VMEM_LIMIT)


def _rms(xf, g):
    return xf * lax.rsqrt(jnp.mean(xf * xf, axis=-1, keepdims=True) + EPS) * g


def _bdot(a, b):
    return jnp.dot(a.astype(BF16), b.astype(BF16), preferred_element_type=F32)


def _bdot_nt(a, b):
    return lax.dot_general(a.astype(BF16), b.astype(BF16), (((1,), (1,)), ((), ())),
                           preferred_element_type=F32)


def _pe_tables_kernel(omega_ref, er_ref, ec_ref):
    quarter = omega_ref.shape[1]
    om = omega_ref[...]
    for ref in (er_ref, ec_ref):
        n = ref.shape[0]
        pos = lax.broadcasted_iota(jnp.int32, (n, quarter), 0).astype(F32)
        ang = pos * om
        ref[:, :quarter] = jnp.sin(ang)
        ref[:, quarter:] = jnp.cos(ang)


def _pe_tables(rows, cols, dim):
    quarter = dim // 4
    omega = (1.0 / (10000.0 ** (jnp.arange(quarter, dtype=F32) / quarter)))[None, :]
    return pl.pallas_call(
        _pe_tables_kernel,
        out_shape=(jax.ShapeDtypeStruct((rows, dim // 2), F32),
                   jax.ShapeDtypeStruct((cols, dim // 2), F32)),
        name="pe_tables",
    )(omega)


def _pe_tile(er_blk, ec):
    nr, half = er_blk.shape
    row_part = jnp.broadcast_to(er_blk[:, None, :], (nr, GRID_W, half)).reshape(nr * GRID_W, half)
    col_part = jnp.broadcast_to(ec[None, :, :], (nr, GRID_W, half)).reshape(nr * GRID_W, half)
    return jnp.concatenate([row_part, col_part], axis=-1)


def _mod_kernel(c_ref, w_ref, b_ref, o_ref):
    c = c_ref[...]
    s = c * jax.nn.sigmoid(c)
    o_ref[...] = _bdot(s, w_ref[...]) + b_ref[...]


def _modulation(cc, w_mod, b_mod):
    rows, d = cc.shape
    n = w_mod.shape[1]
    tn = 512
    return pl.pallas_call(
        _mod_kernel,
        grid=(n // tn,),
        in_specs=[pl.BlockSpec((rows, d), lambda j: (0, 0)),
                  pl.BlockSpec((d, tn), lambda j: (0, j)),
                  pl.BlockSpec((1, tn), lambda j: (0, j))],
        out_specs=pl.BlockSpec((rows, tn), lambda j: (0, j)),
        out_shape=jax.ShapeDtypeStruct((rows, n), F32),
        compiler_params=_cparams(("arbitrary",)),
        name="modulation",
    )(cc, w_mod, b_mod[None, :])


def _in_proj_kernel(*refs, with_hyena):
    if with_hyena:
        (x_ref, er_ref, ec_ref, sh_ref, sc_ref, g_ref, wn_ref, wg_ref, wt_ref,
         zt_ref, q_ref, kt_ref, v_ref, o_ref, gate_ref, gatet_ref) = refs
        xf = x_ref[0] + _pe_tile(er_ref[...], ec_ref[...])
    else:
        (x_ref, sh_ref, sc_ref, g_ref, wn_ref, wg_ref, wt_ref,
         q_ref, kt_ref, v_ref, o_ref, gate_ref, gatet_ref) = refs
        xf = x_ref[0]
    h = _rms(xf, g_ref[...]) * (1.0 + sc_ref[0]) + sh_ref[0]
    hb = h.astype(BF16)
    z = jnp.dot(hb, wn_ref[...], preferred_element_type=F32)
    qs = z[:, :QK_COLS] * (ML_DK ** -0.5)
    for hd in range(ML_HEADS):
        q_ref[0, hd] = qs[:, hd * ML_DK:(hd + 1) * ML_DK].astype(BF16)
    v_ref[0] = z[:, QK_COLS:QK_COLS + ML_WIDTH].astype(BF16)
    o_ref[0] = z[:, QK_COLS + ML_WIDTH:]
    gate_ref[0] = jnp.dot(hb, wg_ref[...], preferred_element_type=F32)
    zt = lax.dot_general(wt_ref[...], hb, (((1,), (1,)), ((), ())), preferred_element_type=F32)
    off = 0
    if with_hyena:
        zt_ref[0] = zt[:HY_COLS, :]
        off = HY_COLS
    for hd in range(ML_HEADS):
        kt_ref[0, hd] = zt[off + hd * ML_DK:off + (hd + 1) * ML_DK, :].astype(BF16)
    gatet_ref[0] = zt[off + QK_COLS:, :]


def _in_proj(x, tabs, sh, sc, g, w_in, tm, with_hyena):
    B, L, D = x.shape
    w_hy = w_in[:, :HY_COLS]
    w_q = w_in[:, HY_COLS:HY_COLS + QK_COLS]
    w_k = w_in[:, HY_COLS + QK_COLS:HY_COLS + 2 * QK_COLS]
    w_vo = w_in[:, HY_COLS + 2 * QK_COLS:HY_COLS + 2 * QK_COLS + 2 * ML_WIDTH]
    w_g = w_in[:, HY_COLS + 2 * QK_COLS + 2 * ML_WIDTH:]
    wn = jnp.concatenate([w_q, w_vo], axis=1).astype(BF16)
    wg = w_g.astype(BF16)
    wt = jnp.concatenate(([w_hy] if with_hyena else []) + [w_k, w_g], axis=1).T.astype(BF16)
    nb = sh.shape[0]
    mod_map = (lambda b, i: (b, 0, 0)) if nb > 1 else (lambda b, i: (0, 0, 0))
    full = lambda a: pl.BlockSpec(a.shape, lambda b, i: (0,) * a.ndim)
    in_specs = [pl.BlockSpec((1, tm, D), lambda b, i: (b, i, 0))]
    args = [x]
    if with_hyena:
        er, ec = tabs
        in_specs += [pl.BlockSpec((tm // GRID_W, D // 2), lambda b, i: (i, 0)), full(ec)]
        args += [er, ec]
    in_specs += [pl.BlockSpec((1, 1, D), mod_map), pl.BlockSpec((1, 1, D), mod_map),
                 full(g), full(wn), full(wg), full(wt)]
    args += [sh, sc, g, wn, wg, wt]
    out_shape, out_specs = [], []
    if with_hyena:
        out_shape.append(jax.ShapeDtypeStruct((B, HY_COLS, L), F32))
        out_specs.append(pl.BlockSpec((1, HY_COLS, tm), lambda b, i: (b, 0, i)))
    out_shape += [jax.ShapeDtypeStruct((B, ML_HEADS, L, ML_DK), BF16),
                  jax.ShapeDtypeStruct((B, ML_HEADS, ML_DK, L), BF16),
                  jax.ShapeDtypeStruct((B, L, ML_WIDTH), BF16),
                  jax.ShapeDtypeStruct((B, L, ML_WIDTH), F32),
                  jax.ShapeDtypeStruct((B, L, N_GATES), F32),
                  jax.ShapeDtypeStruct((B, N_GATES, L), F32)]
    out_specs += [pl.BlockSpec((1, ML_HEADS, tm, ML_DK), lambda b, i: (b, 0, i, 0)),
                  pl.BlockSpec((1, ML_HEADS, ML_DK, tm), lambda b, i: (b, 0, 0, i)),
                  pl.BlockSpec((1, tm, ML_WIDTH), lambda b, i: (b, i, 0)),
                  pl.BlockSpec((1, tm, ML_WIDTH), lambda b, i: (b, i, 0)),
                  pl.BlockSpec((1, tm, N_GATES), lambda b, i: (b, i, 0)),
                  pl.BlockSpec((1, N_GATES, tm), lambda b, i: (b, 0, i))]
    return pl.pallas_call(
        functools.partial(_in_proj_kernel, with_hyena=with_hyena),
        grid=(B, L // tm),
        in_specs=in_specs,
        out_specs=out_specs,
        out_shape=out_shape,
        compiler_params=_cparams(("parallel", "parallel")),
        name="in_proj_hy" if with_hyena else "in_proj_ctx",
    )(*args)


FILT_TILE = 1024
FILT_CBLK = 128


def _filter_kernel(w1a_ref, w1b_ref, w1c_ref, b1_ref, w2_ref, b2_ref, w3_ref, b3_ref, fr_ref,
                   w4f_ref, w4b_ref, dl_ref, k_ref, hf_ref, hb_ref, *, L):
    bands = (FILTER_EMB - 1) // 2

    @pl.when(pl.program_id(0) == 0)
    def _():
        fk = (1e-4 + lax.broadcasted_iota(jnp.int32, (bands, 1), 0).astype(F32)
              * ((bands - 1 - 1e-4) / (bands - 1)))
        fr = fr_ref[...]
        for rev, dst in ((False, hf_ref), (True, hb_ref)):
            for j in range(L // FILT_TILE):
                pos = (lax.broadcasted_iota(jnp.int32, (1, FILT_TILE), 1) + j * FILT_TILE).astype(F32)
                if rev:
                    pos = float(L) - pos
                tl = pos * (1.0 / (L - 1))
                ang = fk * (pos * (2.0 * math.pi / L))
                pre = (w1a_ref[...].astype(F32) * tl.astype(BF16).astype(F32)
                       + _bdot(w1b_ref[...], jnp.cos(ang)) + _bdot(w1c_ref[...], -jnp.sin(ang)))
                h = jnp.sin(fr * (pre + b1_ref[...]))
                h = jnp.sin(fr * (_bdot(w2_ref[...], h) + b2_ref[...]))
                h = jnp.sin(fr * (_bdot(w3_ref[...], h) + b3_ref[...]))
                dst[:, j * FILT_TILE:(j + 1) * FILT_TILE] = h.astype(BF16)

    pos = lax.broadcasted_iota(jnp.int32, (1, L), 1).astype(F32)
    dl = dl_ref[...]
    tf = pos * (1.0 / (L - 1))
    k_ref[:, :L] = jnp.dot(w4f_ref[...], hf_ref[...], preferred_element_type=F32) * jnp.exp(-tf * dl)
    tb = (float(L) - pos) * (1.0 / (L - 1))
    kb = jnp.dot(w4b_ref[...], hb_ref[...], preferred_element_type=F32) * jnp.exp(-tb * dl)
    k_ref[:, L:] = jnp.where(pos == 0.0, 0.0, kb)


def _hyena_filter(L, w1, b1, w2, b2, w3, b3, w4, freq):
    hid = w2.shape[0]
    bands = (FILTER_EMB - 1) // 2
    col = lambda a: a[:, None]
    w1t = w1.T.astype(BF16)
    min_decay = math.log(DECAY_TARGET) / SLOW_DECAY_PCT
    max_decay = math.log(DECAY_TARGET) / FAST_DECAY_PCT
    dl = jnp.abs(jnp.linspace(min_decay, max_decay, HY_WIDTH, dtype=F32))[:, None]
    w4t = w4.T.astype(BF16)
    full = lambda a: pl.BlockSpec(a.shape, lambda i: (0,) * a.ndim)
    args = [w1t[:, 0:1], w1t[:, 1:1 + bands], w1t[:, 1 + bands:], col(b1), w2.T.astype(BF16), col(b2),
            w3.T.astype(BF16), col(b3), col(freq)]
    return pl.pallas_call(
        functools.partial(_filter_kernel, L=L),
        grid=(HY_WIDTH // FILT_CBLK,),
        in_specs=[full(a) for a in args] + [
            pl.BlockSpec((FILT_CBLK, hid), lambda i: (i, 0)),
            pl.BlockSpec((FILT_CBLK, hid), lambda i: (HY_WIDTH // FILT_CBLK + i, 0)),
            pl.BlockSpec((FILT_CBLK, 1), lambda i: (i, 0))],
        out_specs=pl.BlockSpec((FILT_CBLK, 2 * L), lambda i: (i, 0)),
        out_shape=jax.ShapeDtypeStruct((HY_WIDTH, 2 * L), F32),
        scratch_shapes=[pltpu.VMEM((hid, L), BF16), pltpu.VMEM((hid, L), BF16)],
        compiler_params=_cparams(("arbitrary",)),
        name="hyena_filter",
    )(*args, w4t, w4t, dl)


FFT_N = 128
HY_CBLK = 32
HY_GROUP = 8
DFT_PASSES = 3


def _dft_constants(n1_data):
    import numpy as np
    n = FFT_N
    k = np.arange(n)
    ang = -2.0 * np.pi * ((k[:, None] * k[None, :]) % n) / n
    fre, fim = np.cos(ang), np.sin(ang)
    m = n1_data
    fa_d = np.block([[fre[:, :m], -fim[:, :m]], [fim[:, :m], fre[:, :m]]])
    fa_f = np.concatenate([fre, fim], axis=0)
    fb = np.block([[fre, fim], [-fim, fre]])
    fbi = np.block([[fre, -fim], [fim, fre]])
    fc = np.block([[fre[:m, :], fim[:m, :]], [-fim[:m, :], fre[:m, :]]]) / (n * n)
    tang = -2.0 * np.pi * (k[:, None] * k[None, :]) / (n * n)
    tw = np.stack([np.cos(tang), np.sin(tang)])

    def hilo(a):
        a32 = jnp.asarray(a, F32)
        hi = a32.astype(BF16)
        lo = (a32 - hi.astype(F32)).astype(BF16)
        return jnp.stack([hi, lo])

    return hilo(fa_d), hilo(fa_f), hilo(fb), hilo(fbi), hilo(fc), jnp.asarray(tw, F32)


def _mm_const_lhs(c_ref, d):
    dh = d.astype(BF16)
    acc = jnp.dot(c_ref[0], dh, preferred_element_type=F32)
    if DFT_PASSES == 3:
        dl = (d - dh.astype(F32)).astype(BF16)
        acc = acc + (jnp.dot(c_ref[0], dl, preferred_element_type=F32)
                     + jnp.dot(c_ref[1], dh, preferred_element_type=F32))
    return acc


def _mm_const_rhs(d, c_ref):
    dh = d.astype(BF16)
    acc = jnp.dot(dh, c_ref[0], preferred_element_type=F32)
    if DFT_PASSES == 3:
        dl = (d - dh.astype(F32)).astype(BF16)
        acc = acc + (jnp.dot(dl, c_ref[0], preferred_element_type=F32)
                     + jnp.dot(dh, c_ref[1], preferred_element_type=F32))
    return acc


def _cmul(are, aim, bre, bim):
    return are * bre - aim * bim, are * bim + aim * bre


def _hyena_conv_kernel(x0_ref, x1_ref, v_ref, kc_ref, w0_ref, w1_ref, wv_ref, b0_ref, b1_ref, bv_ref,
                       hb_ref, fad_ref, faf_ref, fb_ref, fbi_ref, fc_ref, tw_ref,
                       o_ref, u_ref, s_ref, ks_ref):
    n = FFT_N
    cb = kc_ref.shape[0]
    m = x0_ref.shape[2]
    nb = x0_ref.shape[0]
    sub = lax.broadcasted_iota(jnp.int32, (m, n), 0)
    lane = lax.broadcasted_iota(jnp.int32, (m, n), 1)
    tre, tim = tw_ref[0], tw_ref[1]

    def sconv(z, w_ref, b_ref, c):
        a = pltpu.roll(z, 1, axis=1)
        prev = jnp.where(lane == 0, jnp.where(sub == 0, 0.0, pltpu.roll(a, 1, axis=0)), a)
        a2 = pltpu.roll(z, n - 1, axis=1)
        nxt = jnp.where(lane == n - 1, jnp.where(sub == m - 1, 0.0, pltpu.roll(a2, m - 1, axis=0)), a2)
        return prev * w_ref[0, c] + z * w_ref[1, c] + nxt * w_ref[2, c] + b_ref[c]

    def spectrum_rows(res):
        outs = []
        for h in range(2):
            are, aim = _cmul(res[:n, h * n:(h + 1) * n], res[n:, h * n:(h + 1) * n], tre, tim)
            outs.append(jnp.concatenate([are, aim], axis=1))
        return outs

    def fwd_pair(p, carry):
        c0 = 2 * p
        us = []
        for c in (c0, c0 + 1):
            ub = []
            for b in range(nb):
                x1c = sconv(x1_ref[b, c], w1_ref, b1_ref, c)
                vc = sconv(v_ref[b, c], wv_ref, bv_ref, c)
                u = x1c * vc
                u_ref[b, c] = u
                ub.append(u)
            us.append(ub)
        wd = jnp.concatenate([jnp.concatenate([us[0][b], us[1][b]], axis=1) for b in range(nb)], axis=0)
        sa, sb = spectrum_rows(_mm_const_lhs(fad_ref, wd))
        s_ref[c0] = sa
        s_ref[c0 + 1] = sb
        wk = jnp.concatenate([kc_ref[c0], kc_ref[c0 + 1]], axis=1)
        ka, kb = spectrum_rows(_mm_const_lhs(faf_ref, wk))
        ks_ref[c0] = ka
        ks_ref[c0 + 1] = kb
        return carry

    lax.fori_loop(0, cb // 2, fwd_pair, 0)

    def mid_group(g, carry):
        gs = pl.ds(pl.multiple_of(g * HY_GROUP, HY_GROUP), HY_GROUP)
        x = _mm_const_rhs(s_ref[gs].reshape(HY_GROUP * n, 2 * n), fb_ref)
        k = _mm_const_rhs(ks_ref[gs].reshape(HY_GROUP * n, 2 * n), fb_ref)
        yre, yim = _cmul(x[:, :n], x[:, n:], k[:, :n], k[:, n:])
        vv = _mm_const_rhs(jnp.concatenate([yre, yim], axis=1), fbi_ref).reshape(HY_GROUP, n, 2 * n)
        vre, vim = _cmul(vv[:, :, :n], vv[:, :, n:], tre[None], -tim[None])
        s_ref[gs] = jnp.concatenate([vre, vim], axis=2)
        return carry

    lax.fori_loop(0, cb // HY_GROUP, mid_group, 0)

    def inv_pair(p, carry):
        c0 = 2 * p
        sa, sb = s_ref[c0], s_ref[c0 + 1]
        wd = jnp.concatenate([jnp.concatenate([sa[:, :n], sb[:, :n]], axis=1),
                              jnp.concatenate([sa[:, n:], sb[:, n:]], axis=1)], axis=0)
        res = _mm_const_lhs(fc_ref, wd)
        for h, c in enumerate((c0, c0 + 1)):
            for b in range(nb):
                y = res[b * m:(b + 1) * m, h * n:(h + 1) * n]
                x0c = sconv(x0_ref[b, c], w0_ref, b0_ref, c)
                o_ref[b, c] = x0c * (y + hb_ref[c] * u_ref[b, c])
        return carry

    lax.fori_loop(0, cb // 2, inv_pair, 0)


def _hyena_conv(zt, kcirc, conv_w, conv_b, hy_bias):
    B, _, L = zt.shape
    n = FFT_N
    m = L // n
    C = HY_WIDTH
    assert B == 2 and 2 * L == n * n, "complex packing of two samples over a 128 x 128 point transform"
    z4 = zt.reshape(B, 3 * C, m, n)
    k3 = kcirc.reshape(C, n, n)
    cw = conv_w.reshape(3, 3 * C, 1, 1)
    cbias = conv_b.reshape(3 * C, 1, 1)
    hb = hy_bias.reshape(C, 1, 1)
    consts = _dft_constants(m)
    nblk = C // HY_CBLK
    zspec = lambda part: pl.BlockSpec((B, HY_CBLK, m, n), lambda i: (0, part * nblk + i, 0, 0))
    wspec = lambda part: pl.BlockSpec((3, HY_CBLK, 1, 1), lambda i: (0, part * nblk + i, 0, 0))
    bspec = lambda part: pl.BlockSpec((HY_CBLK, 1, 1), lambda i: (part * nblk + i, 0, 0))
    full = lambda a: pl.BlockSpec(a.shape, lambda i: (0,) * a.ndim)
    y = pl.pallas_call(
        _hyena_conv_kernel,
        grid=(nblk,),
        in_specs=[zspec(0), zspec(1), zspec(2), pl.BlockSpec((HY_CBLK, n, n), lambda i: (i, 0, 0)),
                  wspec(0), wspec(1), wspec(2), bspec(0), bspec(1), bspec(2), bspec(0)]
                 + [full(a) for a in consts],
        out_specs=pl.BlockSpec((B, HY_CBLK, m, n), lambda i: (0, i, 0, 0)),
        out_shape=jax.ShapeDtypeStruct((B, C, m, n), F32),
        scratch_shapes=[pltpu.VMEM((B, HY_CBLK, m, n), F32), pltpu.VMEM((HY_CBLK, n, 2 * n), F32),
                        pltpu.VMEM((HY_CBLK, n, 2 * n), F32)],
        compiler_params=_cparams(("parallel",)),
        name="hyena_conv",
    )(z4, z4, z4, k3, cw, cw, cw, cbias, cbias, cbias, hb, *consts)
    return y.reshape(B, C, L)


def _split3(a):
    hi = a.astype(BF16)
    r1 = a - hi.astype(F32)
    mid = r1.astype(BF16)
    lo = (r1 - mid.astype(F32)).astype(BF16)
    return hi, mid, lo


def _exact_dot_left(tri_bf, a):
    hi, mid, lo = _split3(a)
    d = lambda p: jnp.dot(tri_bf, p, preferred_element_type=F32)
    return (d(lo) + d(mid)) + d(hi)


def _exact_dot_right(a, tri_bf):
    hi, mid, lo = _split3(a)
    d = lambda p: jnp.dot(p, tri_bf, preferred_element_type=F32)
    return (d(lo) + d(mid)) + d(hi)


def _soft_gates(g):
    g = GATE_CAP * jnp.tanh(g * (1.0 / GATE_CAP))
    logsig = jnp.minimum(g, 0.0) - jnp.log1p(jnp.exp(-jnp.abs(g)))
    return g, logsig


def _mlstm_kernel(q_ref, kt_ref, v_ref, gate_ref, gatet_ref, qc_ref, ktc_ref, vc_ref, gatec_ref,
                  gatetc_ref, gb_ref, gbt_ref, h_ref, cf_ref, cb_ref, mf_ref, mb_ref):
    hd = pl.program_id(1)
    T = CHUNK
    L = q_ref.shape[2]
    Lc = qc_ref.shape[2]
    nc, ncc = L // T, Lc // T
    row = lax.broadcasted_iota(jnp.int32, (T, T), 0)
    col = lax.broadcasted_iota(jnp.int32, (T, T), 1)
    lo_mask = col <= row
    up_mask = col >= row
    lo_bf = lo_mask.astype(BF16)
    up_bf = up_mask.astype(BF16)
    lane = lax.broadcasted_iota(jnp.int32, (T, ML_DV), 1)
    ones_col = (lane == 0).astype(BF16)
    gsel_c = lax.broadcasted_iota(jnp.int32, (1, N_GATES), 1)
    gsel_r = lax.broadcasted_iota(jnp.int32, (N_GATES, 1), 0)

    cf_ref[...] = jnp.zeros_like(cf_ref)
    cb_ref[...] = jnp.zeros_like(cb_ref)
    mf_ref[...] = jnp.zeros_like(mf_ref)
    mb_ref[...] = jnp.zeros_like(mb_ref)

    def pick_col(a, k):
        return jnp.sum(jnp.where(gsel_c == k, a, 0.0), axis=1, keepdims=True)

    def pick_row(a, k):
        return jnp.sum(jnp.where(gsel_r == k, a, 0.0), axis=0, keepdims=True)

    def step(q, kt, v, g_c, g_r, c_ref, m_ref, backward):
        gi, gf = (2, 3) if backward else (0, 1)
        cap_c, ls_c = _soft_gates(g_c + gb_ref[...])
        cap_r, ls_r = _soft_gates(g_r + gbt_ref[...])
        i_c = pick_col(cap_c, gi * ML_HEADS + hd)
        i_r = pick_row(cap_r, gi * ML_HEADS + hd)
        tri_l, tri_r, mask = (up_bf, lo_bf, up_mask) if backward else (lo_bf, up_bf, lo_mask)
        b_c = pick_col(_exact_dot_left(tri_l, ls_c), gf * ML_HEADS + hd)
        b_r = pick_row(_exact_dot_right(ls_r, tri_r), gf * ML_HEADS + hd)
        b_end = b_r[:, 0:1] if backward else b_r[:, T - 1:T]
        m = m_ref[...]
        dlog = jnp.where(mask, b_c - b_r + i_r, -jnp.inf)
        m_inter = b_c + m
        m_t = jnp.maximum(m_inter, jnp.max(dlog, axis=-1, keepdims=True))
        w_intra = jnp.exp(dlog - m_t)
        w_inter = jnp.exp(m_inter - m_t)
        s = jnp.dot(q, kt, preferred_element_type=F32) * w_intra
        v_aug = jnp.concatenate([v, ones_col], axis=1)
        c_aug = c_ref[...]
        res = (jnp.dot(s.astype(BF16), v_aug, preferred_element_type=F32)
               + w_inter * jnp.dot(q, c_aug.astype(BF16), preferred_element_type=F32))
        num = res[:, :ML_DV]
        den = res[:, ML_DV:ML_DV + 1]
        h = num / jnp.maximum(jnp.abs(den), jnp.exp(-m_t))
        g_row = b_end - b_r + i_r
        m_new = jnp.maximum(b_end + m, jnp.max(g_row, axis=-1, keepdims=True))
        w_s = jnp.exp(g_row - m_new)
        w_c = jnp.exp(b_end + m - m_new)
        kw = (kt.astype(F32) * w_s).astype(BF16)
        c_ref[...] = w_c * c_aug + jnp.dot(kw, v_aug, preferred_element_type=F32)
        m_ref[...] = m_new
        return h

    for j in range(ncc):
        jb = ncc - 1 - j
        step(qc_ref[0, 0, j * T:(j + 1) * T, :], ktc_ref[0, 0, :, j * T:(j + 1) * T],
             vc_ref[0, j * T:(j + 1) * T, :], gatec_ref[0, j * T:(j + 1) * T, :],
             gatetc_ref[0, :, j * T:(j + 1) * T], cf_ref, mf_ref, False)
        step(qc_ref[0, 0, jb * T:(jb + 1) * T, :], ktc_ref[0, 0, :, jb * T:(jb + 1) * T],
             vc_ref[0, jb * T:(jb + 1) * T, :], gatec_ref[0, jb * T:(jb + 1) * T, :],
             gatetc_ref[0, :, jb * T:(jb + 1) * T], cb_ref, mb_ref, True)

    def latent_pair(j, accumulate):
        for backward in (False, True):
            jj = (nc - 1 - j) if backward else j
            r0 = pl.multiple_of(jj * T, T)
            rs = pl.ds(r0, T)
            h = step(q_ref[0, 0, rs, :], kt_ref[0, 0, :, rs], v_ref[0, rs, :], gate_ref[0, rs, :],
                     gatet_ref[0, :, rs], cb_ref if backward else cf_ref,
                     mb_ref if backward else mf_ref, backward)
            if accumulate:
                h_ref[0, rs, :] = h_ref[0, rs, :] + h
            else:
                h_ref[0, rs, :] = h

    def first_half(j, carry):
        latent_pair(j, False)
        return carry

    def second_half(j, carry):
        latent_pair(j, True)
        return carry

    lax.fori_loop(0, nc // 2, first_half, 0)
    lax.fori_loop(nc // 2, nc, second_half, 0)


def _mlstm(q, kt, v, gate, gatet, qc, ktc, vc, gatec, gatetc, gate_b):
    B, H, L, dk = q.shape
    Lc = qc.shape[2]
    gb = gate_b[None, :]
    gbt = gate_b[:, None]
    return pl.pallas_call(
        _mlstm_kernel,
        grid=(B, H),
        in_specs=[pl.BlockSpec((1, 1, L, dk), lambda b, h: (b, h, 0, 0)),
                  pl.BlockSpec((1, 1, dk, L), lambda b, h: (b, h, 0, 0)),
                  pl.BlockSpec((1, L, ML_DV), lambda b, h: (b, 0, h)),
                  pl.BlockSpec((1, L, N_GATES), lambda b, h: (b, 0, 0)),
                  pl.BlockSpec((1, N_GATES, L), lambda b, h: (b, 0, 0)),
                  pl.BlockSpec((1, 1, Lc, dk), lambda b, h: (b, h, 0, 0)),
                  pl.BlockSpec((1, 1, dk, Lc), lambda b, h: (b, h, 0, 0)),
                  pl.BlockSpec((1, Lc, ML_DV), lambda b, h: (b, 0, h)),
                  pl.BlockSpec((1, Lc, N_GATES), lambda b, h: (b, 0, 0)),
                  pl.BlockSpec((1, N_GATES, Lc), lambda b, h: (b, 0, 0)),
                  pl.BlockSpec((1, N_GATES), lambda b, h: (0, 0)),
                  pl.BlockSpec((N_GATES, 1), lambda b, h: (0, 0))],
        out_specs=pl.BlockSpec((1, L, ML_DV), lambda b, h: (b, 0, h)),
        out_shape=jax.ShapeDtypeStruct((B, L, ML_WIDTH), F32),
        scratch_shapes=[pltpu.VMEM((dk, 2 * ML_DV), F32), pltpu.VMEM((dk, 2 * ML_DV), F32),
                        pltpu.VMEM((1, 1), F32), pltpu.VMEM((1, 1), F32)],
        compiler_params=_cparams(("parallel", "parallel")),
        name="mlstm_scan",
    )(q, kt, v, gate, gatet, qc, ktc, vc, gatec, gatetc, gb, gbt)


def _out_proj_kernel(yhy_ref, hs_ref, o_ref, x_ref, er_ref, ec_ref, g1_ref, sh2_ref, sc2_ref,
                     mln_ref, post1_ref, pre2_ref, wout_ref, wr_ref,
                     x1_ref, h2_ref, aff_ref):
    hs = hs_ref[0]
    parts = []
    for hd in range(ML_HEADS):
        hh = hs[:, hd * ML_DV:(hd + 1) * ML_DV]
        parts.append(hh * lax.rsqrt(jnp.mean(hh * hh, axis=-1, keepdims=True) + EPS))
    hn = jnp.concatenate(parts, axis=-1) * mln_ref[...]
    y_ml = hn * jax.nn.sigmoid(o_ref[0])
    yo = (lax.dot_general(yhy_ref[0].astype(BF16), wout_ref[:HY_WIDTH, :], (((0,), (0,)), ((), ())),
                          preferred_element_type=F32)
          + jnp.dot(y_ml.astype(BF16), wout_ref[HY_WIDTH:, :], preferred_element_type=F32))
    xf = x_ref[0] + _pe_tile(er_ref[...], ec_ref[...])
    x1 = xf + g1_ref[0] * _rms(yo, post1_ref[...])
    x1_ref[0] = x1
    h2 = (_rms(x1, pre2_ref[...]) * (1.0 + sc2_ref[0]) + sh2_ref[0]).astype(BF16)
    h2_ref[0] = h2
    logits = jnp.dot(h2, wr_ref[...], preferred_element_type=F32)
    mx = jnp.max(logits, axis=-1, keepdims=True)
    ex = jnp.exp(logits - mx)
    aff_ref[0] = ex / jnp.sum(ex, axis=-1, keepdims=True)


def _out_proj(y_hy, hsum, o, x, tabs, g1, sh2, sc2, ml_norm, post1, pre2, w_out, w_router, tm):
    B, L, D = x.shape
    er, ec = tabs
    E = w_router.shape[1]
    full = lambda a: pl.BlockSpec(a.shape, lambda b, i: (0,) * a.ndim)
    tok = lambda w: pl.BlockSpec((1, tm, w), lambda b, i: (b, i, 0))
    modspec = pl.BlockSpec((1, 1, D), lambda b, i: (b, 0, 0))
    wout = w_out.astype(BF16)
    wr = w_router.astype(BF16)
    return pl.pallas_call(
        _out_proj_kernel,
        grid=(B, L // tm),
        in_specs=[pl.BlockSpec((1, HY_WIDTH, tm), lambda b, i: (b, 0, i)), tok(ML_WIDTH), tok(ML_WIDTH), tok(D),
                  pl.BlockSpec((tm // GRID_W, D // 2), lambda b, i: (i, 0)), full(ec),
                  modspec, modspec, modspec, full(ml_norm), full(post1), full(pre2),
                  full(wout), full(wr)],
        out_specs=[tok(D), tok(D), tok(E)],
        out_shape=[jax.ShapeDtypeStruct((B, L, D), F32), jax.ShapeDtypeStruct((B, L, D), BF16),
                   jax.ShapeDtypeStruct((B, L, E), F32)],
        compiler_params=_cparams(("parallel", "parallel")),
        name="out_proj_router",
    )(y_hy, hsum, o, x, er, ec, g1, sh2, sc2, ml_norm, post1, pre2, wout, wr)


def _expert_kernel(xs_ref, g_ref, wg_ref, wu_ref, wd_ref, ye_ref, *, m_split):
    f = pl.program_id(1)
    nf = pl.num_programs(1)
    wg = wg_ref[0].astype(BF16)
    wu = wu_ref[0].astype(BF16)
    wd = wd_ref[0].astype(BF16)
    M = xs_ref.shape[1]
    mt = M // m_split
    for mi in range(m_split):
        rs = slice(mi * mt, (mi + 1) * mt)
        xs = xs_ref[0, rs, :]
        a = jnp.dot(xs, wg, preferred_element_type=F32)
        u = jnp.dot(xs, wu, preferred_element_type=F32)
        hmid = (a * jax.nn.sigmoid(a) * u).astype(BF16)
        part = jnp.dot(hmid, wd, preferred_element_type=F32)

        @pl.when(f == 0)
        def _():
            ye_ref[0, rs, :] = part

        @pl.when(f > 0)
        def _():
            ye_ref[0, rs, :] = ye_ref[0, rs, :] + part

    @pl.when(f == nf - 1)
    def _():
        ye_ref[0] = ye_ref[0] * g_ref[0]


def _expert_ffn(xs, g, w_gate, w_up, w_down, tf):
    E, M, D = xs.shape
    F = w_gate.shape[2]
    return pl.pallas_call(
        functools.partial(_expert_kernel, m_split=2),
        grid=(E, F // tf),
        in_specs=[pl.BlockSpec((1, M, D), lambda e, f: (e, 0, 0)),
                  pl.BlockSpec((1, M, 1), lambda e, f: (e, 0, 0)),
                  pl.BlockSpec((1, D, tf), lambda e, f: (e, 0, f)),
                  pl.BlockSpec((1, D, tf), lambda e, f: (e, 0, f)),
                  pl.BlockSpec((1, tf, D), lambda e, f: (e, f, 0))],
        out_specs=pl.BlockSpec((1, M, D), lambda e, f: (e, 0, 0)),
        out_shape=jax.ShapeDtypeStruct((E, M, D), F32),
        compiler_params=_cparams(("parallel", "arbitrary")),
        name="expert_ffn",
    )(xs, g, w_gate, w_up, w_down)


def _final_kernel(x1_ref, y2_ref, g2_ref, post2_ref, o_ref):
    o_ref[0] = x1_ref[0] + g2_ref[0] * _rms(y2_ref[0], post2_ref[...])


def _final(x1, y2, g2, post2, tm):
    B, L, D = x1.shape
    tok = pl.BlockSpec((1, tm, D), lambda b, i: (b, i, 0))
    return pl.pallas_call(
        _final_kernel,
        grid=(B, L // tm),
        in_specs=[tok, tok, pl.BlockSpec((1, 1, D), lambda b, i: (b, 0, 0)),
                  pl.BlockSpec((1, D), lambda b, i: (0, 0))],
        out_specs=tok,
        out_shape=jax.ShapeDtypeStruct((B, L, D), F32),
        compiler_params=_cparams(("parallel", "parallel")),
        name="final_residual",
    )(x1, y2, g2, post2)


def kernel(x, c, ctx, c_ctx, w_mod, b_mod, pre_norm1, post_norm1, pre_norm2, post_norm2, w_in, conv_w, conv_b, filt_w1, filt_b1, filt_w2, filt_b2, filt_w3, filt_b3, filt_w4, filt_freq, hyena_bias, ml_gate_b, ml_norm, w_out, w_router, w_exp_gate, w_exp_up, w_exp_down):
    B, L, D = x.shape
    depth = w_mod.shape[0]
    assert depth == 1, "single-layer block"
    li = 0
    tabs = _pe_tables(L // GRID_W, GRID_W, D)

    cc = jnp.concatenate([c, c_ctx[None], jnp.zeros((8 - B - 1, D), F32)], axis=0)
    mod = _modulation(cc, w_mod[li], b_mod[li])
    chunks = [mod[:, k * D:(k + 1) * D] for k in range(6)]
    sh1, sc1, g1, sh2, sc2, g2 = [m[:B, None, :] for m in chunks]
    csh1, csc1 = chunks[0][B:B + 1, None, :], chunks[1][B:B + 1, None, :]

    pre1 = pre_norm1[li][None, :]
    zhy, q, kt, v, o, gate, gatet = _in_proj(x, tabs, sh1, sc1, pre1, w_in[li], TOKEN_TILE, True)
    qc, ktc, vc, _, gatec, gatetc = _in_proj(ctx, None, csh1, csc1, pre1, w_in[li], ctx.shape[1], False)

    kcirc = _hyena_filter(L, filt_w1[li], filt_b1[li], filt_w2[li], filt_b2[li], filt_w3[li],
                          filt_b3[li], filt_w4[li], filt_freq[li])
    y_hy = _hyena_conv(zhy, kcirc, conv_w[li], conv_b[li], hyena_bias[li])
    hsum = _mlstm(q, kt, v, gate, gatet, qc, ktc, vc, gatec, gatetc, ml_gate_b[li])

    x1, h2, aff = _out_proj(y_hy, hsum, o, x, tabs, g1, sh2, sc2, ml_norm[li][None, :],
                            post_norm1[li][None, :], pre_norm2[li][None, :], w_out[li],
                            w_router[li], TOKEN_TILE)

    cap = CAP_FACTOR * L // N_EXPERTS
    gsel, idx = lax.top_k(jnp.swapaxes(aff, 1, 2), cap)
    xs = jax.vmap(lambda hb, ib: hb[ib])(h2, idx)
    xs = jnp.swapaxes(xs, 0, 1).reshape(N_EXPERTS, B * cap, D)
    gs = jnp.swapaxes(gsel, 0, 1).reshape(N_EXPERTS, B * cap, 1)
    ye = _expert_ffn(xs, gs, w_exp_gate[li], w_exp_up[li], w_exp_down[li], 512)
    ye = jnp.swapaxes(ye.reshape(N_EXPERTS, B, cap, D), 0, 1)
    y2 = jax.vmap(lambda ib, yb: jnp.zeros((L, D), yb.dtype).at[ib.reshape(-1)].add(yb.reshape(-1, D)))(idx, ye)

    return _final(x1, y2, g2, post_norm2[li][None, :], TOKEN_TILE)
```

```python
import functools
import math

import jax
import jax.numpy as jnp
from jax import lax
from jax.experimental import pallas as pl
from jax.experimental.pallas import tpu as pltpu

D_MODEL = 1024
GRID_W = 64
HY_WIDTH = 512
ML_HEADS = 4
ML_DK = 64
ML_DV = 128
ML_WIDTH = ML_HEADS * ML_DV
HY_COLS = 3 * HY_WIDTH
QK_COLS = ML_HEADS * ML_DK
N_GATES = 4 * ML_HEADS
FILTER_EMB = 33
DECAY_TARGET = 1e-2
FAST_DECAY_PCT = 0.3
SLOW_DECAY_PCT = 1.5
CHUNK = 128
GATE_CAP = 15.0
N_EXPERTS = 16
CAP_FACTOR = 2
EPS = 1e-6

F32 = jnp.float32
BF16 = jnp.bfloat16

TOKEN_TILE = 512
IN_PROJ_TILE = 1024
VMEM_LIMIT = 56 * 1024 * 1024


def _cparams(sem):
    return pltpu.CompilerParams(dimension_semantics=sem, vmem_limit_bytes=VMEM_LIMIT)


def _rms(xf, g):
    return xf * lax.rsqrt(jnp.mean(xf * xf, axis=-1, keepdims=True) + EPS) * g


def _bdot(a, b):
    return jnp.dot(a.astype(BF16), b.astype(BF16), preferred_element_type=F32)


def _bdot_nt(a, b):
    return lax.dot_general(a.astype(BF16), b.astype(BF16), (((1,), (1,)), ((), ())),
                           preferred_element_type=F32)


def _pe_tables_kernel(omega_ref, er_ref, ec_ref):
    quarter = omega_ref.shape[1]
    om = omega_ref[...]
    for ref in (er_ref, ec_ref):
        n = ref.shape[0]
        pos = lax.broadcasted_iota(jnp.int32, (n, quarter), 0).astype(F32)
        ang = pos * om
        ref[:, :quarter] = jnp.sin(ang)
        ref[:, quarter:] = jnp.cos(ang)


def _pe_tables(rows, cols, dim):
    quarter = dim // 4
    omega = (1.0 / (10000.0 ** (jnp.arange(quarter, dtype=F32) / quarter)))[None, :]
    return pl.pallas_call(
        _pe_tables_kernel,
        out_shape=(jax.ShapeDtypeStruct((rows, dim // 2), F32),
                   jax.ShapeDtypeStruct((cols, dim // 2), F32)),
        name="pe_tables",
    )(omega)


def _pe_tile(er_blk, ec):
    nr, half = er_blk.shape
    row_part = jnp.broadcast_to(er_blk[:, None, :], (nr, GRID_W, half)).reshape(nr * GRID_W, half)
    col_part = jnp.broadcast_to(ec[None, :, :], (nr, GRID_W, half)).reshape(nr * GRID_W, half)
    return jnp.concatenate([row_part, col_part], axis=-1)


def _mod_kernel(c_ref, w_ref, b_ref, o_ref):
    c = c_ref[...]
    s = c * jax.nn.sigmoid(c)
    o_ref[...] = _bdot(s, w_ref[...]) + b_ref[...]


def _modulation(cc, w_mod, b_mod):
    rows, d = cc.shape
    n = w_mod.shape[1]
    tn = 512
    return pl.pallas_call(
        _mod_kernel,
        grid=(n // tn,),
        in_specs=[pl.BlockSpec((rows, d), lambda j: (0, 0)),
                  pl.BlockSpec((d, tn), lambda j: (0, j)),
                  pl.BlockSpec((1, tn), lambda j: (0, j))],
        out_specs=pl.BlockSpec((rows, tn), lambda j: (0, j)),
        out_shape=jax.ShapeDtypeStruct((rows, n), F32),
        compiler_params=_cparams(("arbitrary",)),
        name="modulation",
    )(cc, w_mod, b_mod[None, :])


def _in_proj_kernel(*refs, with_hyena):
    if with_hyena:
        (x_ref, er_ref, ec_ref, sh_ref, sc_ref, g_ref, wn_ref, wt_ref,
         zt_ref, q_ref, kt_ref, v_ref, o_ref, gatet_ref) = refs
        xf = x_ref[0] + _pe_tile(er_ref[...], ec_ref[...])
    else:
        (x_ref, sh_ref, sc_ref, g_ref, wn_ref, wt_ref,
         q_ref, kt_ref, v_ref, o_ref, gatet_ref) = refs
        xf = x_ref[0]
    h = _rms(xf, g_ref[...]) * (1.0 + sc_ref[0]) + sh_ref[0]
    hb = h.astype(BF16)
    z = jnp.dot(hb, wn_ref[...], preferred_element_type=F32)
    qs = z[:, :QK_COLS] * (ML_DK ** -0.5)
    for hd in range(ML_HEADS):
        q_ref[0, hd] = qs[:, hd * ML_DK:(hd + 1) * ML_DK].astype(BF16)
    v_ref[0] = z[:, QK_COLS:QK_COLS + ML_WIDTH].astype(BF16)
    o_ref[0] = z[:, QK_COLS + ML_WIDTH:]
    zt = lax.dot_general(wt_ref[...], hb, (((1,), (1,)), ((), ())), preferred_element_type=F32)
    off = 0
    if with_hyena:
        for j in range(zt.shape[1] // LANES):
            zt_ref[0, :, j, :] = zt[:HY_COLS, j * LANES:(j + 1) * LANES]
        off = HY_COLS
    for hd in range(ML_HEADS):
        kt_ref[0, hd] = zt[off + hd * ML_DK:off + (hd + 1) * ML_DK, :].astype(BF16)
    gatet_ref[0] = zt[off + QK_COLS:, :]


def _in_proj(x, tabs, sh, sc, g, w_in, tm, with_hyena):
    B, L, D = x.shape
    w_hy = w_in[:, :HY_COLS]
    w_q = w_in[:, HY_COLS:HY_COLS + QK_COLS]
    w_k = w_in[:, HY_COLS + QK_COLS:HY_COLS + 2 * QK_COLS]
    w_vo = w_in[:, HY_COLS + 2 * QK_COLS:HY_COLS + 2 * QK_COLS + 2 * ML_WIDTH]
    w_g = w_in[:, HY_COLS + 2 * QK_COLS + 2 * ML_WIDTH:]
    wn = jnp.concatenate([w_q, w_vo], axis=1).astype(BF16)
    wt = jnp.concatenate(([w_hy] if with_hyena else []) + [w_k, w_g], axis=1).T.astype(BF16)
    nb = sh.shape[0]
    mod_map = (lambda b, i: (b, 0, 0)) if nb > 1 else (lambda b, i: (0, 0, 0))
    full = lambda a: pl.BlockSpec(a.shape, lambda b, i: (0,) * a.ndim)
    in_specs = [pl.BlockSpec((1, tm, D), lambda b, i: (b, i, 0))]
    args = [x]
    if with_hyena:
        er, ec = tabs
        in_specs += [pl.BlockSpec((tm // GRID_W, D // 2), lambda b, i: (i, 0)), full(ec)]
        args += [er, ec]
    in_specs += [pl.BlockSpec((1, 1, D), mod_map), pl.BlockSpec((1, 1, D), mod_map),
                 full(g), full(wn), full(wt)]
    args += [sh, sc, g, wn, wt]
    out_shape, out_specs = [], []
    if with_hyena:
        out_shape.append(jax.ShapeDtypeStruct((B, HY_COLS, L // LANES, LANES), F32))
        out_specs.append(pl.BlockSpec((1, HY_COLS, tm // LANES, LANES), lambda b, i: (b, 0, i, 0)))
    out_shape += [jax.ShapeDtypeStruct((B, ML_HEADS, L, ML_DK), BF16),
                  jax.ShapeDtypeStruct((B, ML_HEADS, ML_DK, L), BF16),
                  jax.ShapeDtypeStruct((B, L, ML_WIDTH), BF16),
                  jax.ShapeDtypeStruct((B, L, ML_WIDTH), F32),
                  jax.ShapeDtypeStruct((B, N_GATES, L), F32)]
    out_specs += [pl.BlockSpec((1, ML_HEADS, tm, ML_DK), lambda b, i: (b, 0, i, 0)),
                  pl.BlockSpec((1, ML_HEADS, ML_DK, tm), lambda b, i: (b, 0, 0, i)),
                  pl.BlockSpec((1, tm, ML_WIDTH), lambda b, i: (b, i, 0)),
                  pl.BlockSpec((1, tm, ML_WIDTH), lambda b, i: (b, i, 0)),
                  pl.BlockSpec((1, N_GATES, tm), lambda b, i: (b, 0, i))]
    return pl.pallas_call(
        functools.partial(_in_proj_kernel, with_hyena=with_hyena),
        grid=(B, L // tm),
        in_specs=in_specs,
        out_specs=out_specs,
        out_shape=out_shape,
        compiler_params=_cparams(("parallel", "parallel")),
        name="in_proj_hy" if with_hyena else "in_proj_ctx",
    )(*args)


FILT_TILE = 1024
FILT_CBLK = 128


def _filter_kernel(w1a_ref, w1b_ref, w1c_ref, b1_ref, w2_ref, b2_ref, w3_ref, b3_ref, fr_ref,
                   w4f_ref, w4b_ref, dl_ref, k_ref, hf_ref, hb_ref, *, L):
    bands = (FILTER_EMB - 1) // 2

    @pl.when(pl.program_id(0) == 0)
    def _():
        fk = (1e-4 + lax.broadcasted_iota(jnp.int32, (bands, 1), 0).astype(F32)
              * ((bands - 1 - 1e-4) / (bands - 1)))
        fr = fr_ref[...]
        for rev, dst in ((False, hf_ref), (True, hb_ref)):
            for j in range(L // FILT_TILE):
                pos = (lax.broadcasted_iota(jnp.int32, (1, FILT_TILE), 1) + j * FILT_TILE).astype(F32)
                if rev:
                    pos = float(L) - pos
                tl = pos * (1.0 / (L - 1))
                ang = fk * (pos * (2.0 * math.pi / L))
                pre = (w1a_ref[...].astype(F32) * tl.astype(BF16).astype(F32)
                       + _bdot(w1b_ref[...], jnp.cos(ang)) + _bdot(w1c_ref[...], -jnp.sin(ang)))
                h = jnp.sin(fr * (pre + b1_ref[...]))
                h = jnp.sin(fr * (_bdot(w2_ref[...], h) + b2_ref[...]))
                h = jnp.sin(fr * (_bdot(w3_ref[...], h) + b3_ref[...]))
                dst[:, j * FILT_TILE:(j + 1) * FILT_TILE] = h.astype(BF16)

    pos = lax.broadcasted_iota(jnp.int32, (1, L), 1).astype(F32)
    dl = dl_ref[...]
    tf = pos * (1.0 / (L - 1))
    k_ref[:, :L] = jnp.dot(w4f_ref[...], hf_ref[...], preferred_element_type=F32) * jnp.exp(-tf * dl)
    tb = (float(L) - pos) * (1.0 / (L - 1))
    kb = jnp.dot(w4b_ref[...], hb_ref[...], preferred_element_type=F32) * jnp.exp(-tb * dl)
    k_ref[:, L:] = jnp.where(pos == 0.0, 0.0, kb)


def _hyena_filter(L, w1, b1, w2, b2, w3, b3, w4, freq):
    hid = w2.shape[0]
    bands = (FILTER_EMB - 1) // 2
    col = lambda a: a[:, None]
    w1t = w1.T.astype(BF16)
    min_decay = math.log(DECAY_TARGET) / SLOW_DECAY_PCT
    max_decay = math.log(DECAY_TARGET) / FAST_DECAY_PCT
    dl = jnp.abs(jnp.linspace(min_decay, max_decay, HY_WIDTH, dtype=F32))[:, None]
    w4t = w4.T.astype(BF16)
    full = lambda a: pl.BlockSpec(a.shape, lambda i: (0,) * a.ndim)
    args = [w1t[:, 0:1], w1t[:, 1:1 + bands], w1t[:, 1 + bands:], col(b1), w2.T.astype(BF16), col(b2),
            w3.T.astype(BF16), col(b3), col(freq)]
    return pl.pallas_call(
        functools.partial(_filter_kernel, L=L),
        grid=(HY_WIDTH // FILT_CBLK,),
        in_specs=[full(a) for a in args] + [
            pl.BlockSpec((FILT_CBLK, hid), lambda i: (i, 0)),
            pl.BlockSpec((FILT_CBLK, hid), lambda i: (HY_WIDTH // FILT_CBLK + i, 0)),
            pl.BlockSpec((FILT_CBLK, 1), lambda i: (i, 0))],
        out_specs=pl.BlockSpec((FILT_CBLK, 2 * L), lambda i: (i, 0)),
        out_shape=jax.ShapeDtypeStruct((HY_WIDTH, 2 * L), F32),
        scratch_shapes=[pltpu.VMEM((hid, L), BF16), pltpu.VMEM((hid, L), BF16)],
        compiler_params=_cparams(("arbitrary",)),
        name="hyena_filter",
    )(*args, w4t, w4t, dl)


FFT_N = 128
HY_CBLK = 32
HY_GROUP = 8
DFT_PASSES = 1


def _dft_constants(n1_data):
    import numpy as np
    n = FFT_N
    k = np.arange(n)
    ang = -2.0 * np.pi * ((k[:, None] * k[None, :]) % n) / n
    fre, fim = np.cos(ang), np.sin(ang)
    m = n1_data
    fa_d = np.block([[fre[:, :m], -fim[:, :m]], [fim[:, :m], fre[:, :m]]])
    fa_f = np.concatenate([fre, fim], axis=0)
    fb = np.block([[fre, fim], [-fim, fre]])
    fbi = np.block([[fre, -fim], [fim, fre]])
    fc = np.block([[fre[:m, :], fim[:m, :]], [-fim[:m, :], fre[:m, :]]]) / (n * n)
    tang = -2.0 * np.pi * (k[:, None] * k[None, :]) / (n * n)
    tw = np.stack([np.cos(tang), np.sin(tang)])

    def hilo(a):
        a32 = jnp.asarray(a, F32)
        hi = a32.astype(BF16)
        lo = (a32 - hi.astype(F32)).astype(BF16)
        return jnp.stack([hi, lo])

    return hilo(fa_d), hilo(fa_f), hilo(fb), hilo(fbi), hilo(fc), jnp.asarray(tw, F32)


def _mm_const_lhs(c_ref, d):
    dh = d.astype(BF16)
    acc = jnp.dot(c_ref[0], dh, preferred_element_type=F32)
    if DFT_PASSES == 3:
        dl = (d - dh.astype(F32)).astype(BF16)
        acc = acc + (jnp.dot(c_ref[0], dl, preferred_element_type=F32)
                     + jnp.dot(c_ref[1], dh, preferred_element_type=F32))
    return acc


def _mm_const_rhs(d, c_ref):
    dh = d.astype(BF16)
    acc = jnp.dot(dh, c_ref[0], preferred_element_type=F32)
    if DFT_PASSES == 3:
        dl = (d - dh.astype(F32)).astype(BF16)
        acc = acc + (jnp.dot(dl, c_ref[0], preferred_element_type=F32)
                     + jnp.dot(dh, c_ref[1], preferred_element_type=F32))
    return acc


def _cmul(are, aim, bre, bim):
    return are * bre - aim * bim, are * bim + aim * bre


def _hyena_conv_kernel(x0_ref, x1_ref, v_ref, kc_ref, w0_ref, w1_ref, wv_ref, b0_ref, b1_ref, bv_ref,
                       hb_ref, fad_ref, faf_ref, fb_ref, fbi_ref, fc_ref, tw_ref,
                       o_ref, u_ref, s_ref, ks_ref):
    n = FFT_N
    cb = kc_ref.shape[0]
    m = x0_ref.shape[2]
    nb = x0_ref.shape[0]
    sub = lax.broadcasted_iota(jnp.int32, (m, n), 0)
    lane = lax.broadcasted_iota(jnp.int32, (m, n), 1)
    tre, tim = tw_ref[0], tw_ref[1]

    def sconv(z, w_ref, b_ref, c):
        a = pltpu.roll(z, 1, axis=1)
        prev = jnp.where(lane == 0, jnp.where(sub == 0, 0.0, pltpu.roll(a, 1, axis=0)), a)
        a2 = pltpu.roll(z, n - 1, axis=1)
        nxt = jnp.where(lane == n - 1, jnp.where(sub == m - 1, 0.0, pltpu.roll(a2, m - 1, axis=0)), a2)
        return prev * w_ref[0, c] + z * w_ref[1, c] + nxt * w_ref[2, c] + b_ref[c]

    def spectrum_rows(res):
        outs = []
        for h in range(2):
            are, aim = _cmul(res[:n, h * n:(h + 1) * n], res[n:, h * n:(h + 1) * n], tre, tim)
            outs.append(jnp.concatenate([are, aim], axis=1))
        return outs

    def fwd_pair(p, carry):
        c0 = 2 * p
        us = []
        for c in (c0, c0 + 1):
            ub = []
            for b in range(nb):
                x1c = sconv(x1_ref[b, c], w1_ref, b1_ref, c)
                vc = sconv(v_ref[b, c], wv_ref, bv_ref, c)
                u = x1c * vc
                u_ref[b, c] = u
                ub.append(u)
            us.append(ub)
        wd = jnp.concatenate([jnp.concatenate([us[0][b], us[1][b]], axis=1) for b in range(nb)], axis=0)
        sa, sb = spectrum_rows(_mm_const_lhs(fad_ref, wd))
        s_ref[c0] = sa
        s_ref[c0 + 1] = sb
        wk = jnp.concatenate([kc_ref[c0], kc_ref[c0 + 1]], axis=1)
        ka, kb = spectrum_rows(_mm_const_lhs(faf_ref, wk))
        ks_ref[c0] = ka
        ks_ref[c0 + 1] = kb
        return carry

    lax.fori_loop(0, cb // 2, fwd_pair, 0)

    def mid_group(g, carry):
        gs = pl.ds(pl.multiple_of(g * HY_GROUP, HY_GROUP), HY_GROUP)
        x = _mm_const_rhs(s_ref[gs].reshape(HY_GROUP * n, 2 * n), fb_ref)
        k = _mm_const_rhs(ks_ref[gs].reshape(HY_GROUP * n, 2 * n), fb_ref)
        yre, yim = _cmul(x[:, :n], x[:, n:], k[:, :n], k[:, n:])
        vv = _mm_const_rhs(jnp.concatenate([yre, yim], axis=1), fbi_ref).reshape(HY_GROUP, n, 2 * n)
        vre, vim = _cmul(vv[:, :, :n], vv[:, :, n:], tre[None], -tim[None])
        s_ref[gs] = jnp.concatenate([vre, vim], axis=2)
        return carry

    lax.fori_loop(0, cb // HY_GROUP, mid_group, 0)

    def inv_pair(p, carry):
        c0 = 2 * p
        sa, sb = s_ref[c0], s_ref[c0 + 1]
        wd = jnp.concatenate([jnp.concatenate([sa[:, :n], sb[:, :n]], axis=1),
                              jnp.concatenate([sa[:, n:], sb[:, n:]], axis=1)], axis=0)
        res = _mm_const_lhs(fc_ref, wd)
        for h, c in enumerate((c0, c0 + 1)):
            for b in range(nb):
                y = res[b * m:(b + 1) * m, h * n:(h + 1) * n]
                x0c = sconv(x0_ref[b, c], w0_ref, b0_ref, c)
                o_ref[b, c] = x0c * (y + hb_ref[c] * u_ref[b, c])
        return carry

    lax.fori_loop(0, cb // 2, inv_pair, 0)


def _hyena_conv(zt, kcirc, conv_w, conv_b, hy_bias):
    B, _, m, n = zt.shape
    L = m * n
    C = HY_WIDTH
    assert B == 2 and n == FFT_N and 2 * L == n * n, "complex packing of two samples over a 128 x 128 point transform"
    z4 = zt
    k3 = kcirc.reshape(C, n, n)
    cw = conv_w.reshape(3, 3 * C, 1, 1)
    cbias = conv_b.reshape(3 * C, 1, 1)
    hb = hy_bias.reshape(C, 1, 1)
    consts = _dft_constants(m)
    nblk = C // HY_CBLK
    zspec = lambda part: pl.BlockSpec((B, HY_CBLK, m, n), lambda i: (0, part * nblk + i, 0, 0))
    wspec = lambda part: pl.BlockSpec((3, HY_CBLK, 1, 1), lambda i: (0, part * nblk + i, 0, 0))
    bspec = lambda part: pl.BlockSpec((HY_CBLK, 1, 1), lambda i: (part * nblk + i, 0, 0))
    full = lambda a: pl.BlockSpec(a.shape, lambda i: (0,) * a.ndim)
    y = pl.pallas_call(
        _hyena_conv_kernel,
        grid=(nblk,),
        in_specs=[zspec(0), zspec(1), zspec(2), pl.BlockSpec((HY_CBLK, n, n), lambda i: (i, 0, 0)),
                  wspec(0), wspec(1), wspec(2), bspec(0), bspec(1), bspec(2), bspec(0)]
                 + [full(a) for a in consts],
        out_specs=pl.BlockSpec((B, HY_CBLK, m, n), lambda i: (0, i, 0, 0)),
        out_shape=jax.ShapeDtypeStruct((B, C, m, n), F32),
        scratch_shapes=[pltpu.VMEM((B, HY_CBLK, m, n), F32), pltpu.VMEM((HY_CBLK, n, 2 * n), F32),
                        pltpu.VMEM((HY_CBLK, n, 2 * n), F32)],
        compiler_params=_cparams(("parallel",)),
        name="hyena_conv",
    )(z4, z4, z4, k3, cw, cw, cw, cbias, cbias, cbias, hb, *consts)
    return y


def _split3(a):
    hi = a.astype(BF16)
    r1 = a - hi.astype(F32)
    mid = r1.astype(BF16)
    lo = (r1 - mid.astype(F32)).astype(BF16)
    return hi, mid, lo


def _exact_dot_right(a, tri_bf):
    hi, mid, lo = _split3(a)
    d = lambda p: jnp.dot(p, tri_bf, preferred_element_type=F32)
    return (d(lo) + d(mid)) + d(hi)


def _soft_gates(g):
    g = GATE_CAP * jnp.tanh(g * (1.0 / GATE_CAP))
    logsig = jnp.minimum(g, 0.0) - jnp.log1p(jnp.exp(-jnp.abs(g)))
    return g, logsig


def _mlstm_kernel(q_ref, kt_ref, v_ref, gatet_ref, qc_ref, ktc_ref, vc_ref, gatetc_ref, gbt_ref,
                  h_ref, cf_ref, cb_ref, mf_ref, mb_ref, r_ref, rc_ref):
    hd = pl.program_id(1)
    T = CHUNK
    L = q_ref.shape[2]
    Lc = qc_ref.shape[2]
    nc, ncc = L // T, Lc // T
    row = lax.broadcasted_iota(jnp.int32, (T, T), 0)
    col = lax.broadcasted_iota(jnp.int32, (T, T), 1)
    lo_mask = col <= row
    up_mask = col >= row
    lo_bf = lo_mask.astype(BF16)
    up_bf = up_mask.astype(BF16)
    lane = lax.broadcasted_iota(jnp.int32, (T, ML_DV), 1)
    ones_col = (lane == 0).astype(BF16)
    rsel = lax.broadcasted_iota(jnp.int32, (N_GATES, 1), 0)
    out8 = lax.broadcasted_iota(jnp.int32, (8, 1), 0)

    cf_ref[...] = jnp.zeros_like(cf_ref)
    cb_ref[...] = jnp.zeros_like(cb_ref)
    mf_ref[...] = jnp.zeros_like(mf_ref)
    mb_ref[...] = jnp.zeros_like(mb_ref)

    def gate_rows(gt_ref, r_ref, n_chunks):
        cap, ls = _soft_gates(gt_ref[0] + gbt_ref[...])
        pick = lambda a, k: jnp.sum(jnp.where(rsel == k * ML_HEADS + hd, a, 0.0), axis=0, keepdims=True)
        i_f, lf_f, i_b, lf_b = pick(cap, 0), pick(ls, 1), pick(cap, 2), pick(ls, 3)
        r_ref[...] = jnp.where(out8 == 0, i_f, jnp.where(out8 == 2, i_b, 0.0))
        lf = jnp.where(out8 == 1, lf_f, jnp.where(out8 == 3, lf_b, 0.0))
        for j in range(n_chunks):
            blk = lf[:, j * T:(j + 1) * T]
            run = _exact_dot_right(blk, up_bf)
            suf = _exact_dot_right(blk, lo_bf)
            r_ref[:, j * T:(j + 1) * T] = (r_ref[:, j * T:(j + 1) * T]
                                           + jnp.where(out8 == 1, run, jnp.where(out8 == 3, suf, 0.0)))

    gate_rows(gatetc_ref, rc_ref, ncc)
    gate_rows(gatet_ref, r_ref, nc)

    def local(q, kt, v, rows, backward):
        i_r = rows[2:3] if backward else rows[0:1]
        b_r = rows[3:4] if backward else rows[1:2]
        rt = rows.T
        b_c = rt[:, 3:4] if backward else rt[:, 1:2]
        mask = up_mask if backward else lo_mask
        b_end = b_r[:, 0:1] if backward else b_r[:, T - 1:T]
        dlog = jnp.where(mask, b_c - b_r + i_r, -jnp.inf)
        m_loc = jnp.max(dlog, axis=-1, keepdims=True)
        s = jnp.dot(q, kt, preferred_element_type=F32) * jnp.exp(dlog - m_loc)
        v_aug = jnp.concatenate([v, ones_col], axis=1)
        res_loc = jnp.dot(s.astype(BF16), v_aug, preferred_element_type=F32)
        g_row = b_end - b_r + i_r
        mg = jnp.max(g_row, axis=-1, keepdims=True)
        kw = (kt.astype(F32) * jnp.exp(g_row - mg)).astype(BF16)
        dc = jnp.dot(kw, v_aug, preferred_element_type=F32)
        return res_loc, m_loc, b_c, b_end, mg, dc

    def carried(q, loc, c_ref, m_ref):
        res_loc, m_loc, b_c, b_end, mg, dc = loc
        m = m_ref[...]
        c_aug = c_ref[...]
        m_inter = b_c + m
        m_t = jnp.maximum(m_inter, m_loc)
        res = (jnp.exp(m_loc - m_t) * res_loc
               + jnp.exp(m_inter - m_t) * jnp.dot(q, c_aug.astype(BF16), preferred_element_type=F32))
        h = res[:, :ML_DV] / jnp.maximum(jnp.abs(res[:, ML_DV:ML_DV + 1]), jnp.exp(-m_t))
        m_new = jnp.maximum(b_end + m, mg)
        c_ref[...] = jnp.exp(b_end + m - m_new) * c_aug + jnp.exp(mg - m_new) * dc
        m_ref[...] = m_new
        return h

    for j in range(ncc):
        for backward in (False, True):
            jj = (ncc - 1 - j) if backward else j
            cs = slice(jj * T, (jj + 1) * T)
            qq = qc_ref[0, 0, cs, :]
            loc = local(qq, ktc_ref[0, 0, :, cs], vc_ref[0, cs, :], rc_ref[:, cs], backward)
            carried(qq, loc, cb_ref if backward else cf_ref, mb_ref if backward else mf_ref)

    def latent_pair(j, accumulate):
        for backward in (False, True):
            jj = (nc - 1 - j) if backward else j
            rs = pl.ds(pl.multiple_of(jj * T, T), T)
            qq = q_ref[0, 0, rs, :]
            loc = local(qq, kt_ref[0, 0, :, rs], v_ref[0, rs, :], r_ref[:, rs], backward)
            h = carried(qq, loc, cb_ref if backward else cf_ref, mb_ref if backward else mf_ref)
            if accumulate:
                h_ref[0, rs, :] = h_ref[0, rs, :] + h
            else:
                h_ref[0, rs, :] = h

    def first_half(j, carry):
        latent_pair(j, False)
        return carry

    def second_half(j, carry):
        latent_pair(j, True)
        return carry

    lax.fori_loop(0, nc // 2, first_half, 0, unroll=MLSTM_UNROLL)
    lax.fori_loop(nc // 2, nc, second_half, 0, unroll=MLSTM_UNROLL)


MLSTM_UNROLL = 2


def _mlstm(q, kt, v, gatet, qc, ktc, vc, gatetc, gate_b):
    B, H, L, dk = q.shape
    Lc = qc.shape[2]
    gbt = gate_b[:, None]
    return pl.pallas_call(
        _mlstm_kernel,
        grid=(B, H),
        in_specs=[pl.BlockSpec((1, 1, L, dk), lambda b, h: (b, h, 0, 0)),
                  pl.BlockSpec((1, 1, dk, L), lambda b, h: (b, h, 0, 0)),
                  pl.BlockSpec((1, L, ML_DV), lambda b, h: (b, 0, h)),
                  pl.BlockSpec((1, N_GATES, L), lambda b, h: (b, 0, 0)),
                  pl.BlockSpec((1, 1, Lc, dk), lambda b, h: (b, h, 0, 0)),
                  pl.BlockSpec((1, 1, dk, Lc), lambda b, h: (b, h, 0, 0)),
                  pl.BlockSpec((1, Lc, ML_DV), lambda b, h: (b, 0, h)),
                  pl.BlockSpec((1, N_GATES, Lc), lambda b, h: (b, 0, 0)),
                  pl.BlockSpec((N_GATES, 1), lambda b, h: (0, 0))],
        out_specs=pl.BlockSpec((1, L, ML_DV), lambda b, h: (b, 0, h)),
        out_shape=jax.ShapeDtypeStruct((B, L, ML_WIDTH), F32),
        scratch_shapes=[pltpu.VMEM((dk, 2 * ML_DV), F32), pltpu.VMEM((dk, 2 * ML_DV), F32),
                        pltpu.VMEM((1, 1), F32), pltpu.VMEM((1, 1), F32),
                        pltpu.VMEM((8, L), F32), pltpu.VMEM((8, Lc), F32)],
        compiler_params=_cparams(("parallel", "parallel")),
        name="mlstm_scan",
    )(q, kt, v, gatet, qc, ktc, vc, gatetc, gbt)


def _out_proj_kernel(yhy_ref, hs_ref, o_ref, x_ref, er_ref, ec_ref, g1_ref, sh2_ref, sc2_ref,
                     mln_ref, post1_ref, pre2_ref, wout_ref, wr_ref, wrt_ref,
                     x1_ref, h2_ref, aff_ref, afft_ref):
    hs = hs_ref[0]
    parts = []
    for hd in range(ML_HEADS):
        hh = hs[:, hd * ML_DV:(hd + 1) * ML_DV]
        parts.append(hh * lax.rsqrt(jnp.mean(hh * hh, axis=-1, keepdims=True) + EPS))
    hn = jnp.concatenate(parts, axis=-1) * mln_ref[...]
    y_ml = hn * jax.nn.sigmoid(o_ref[0])
    yo_hy = [lax.dot_general(yhy_ref[0, :, j, :].astype(BF16), wout_ref[:HY_WIDTH, :], (((0,), (0,)), ((), ())),
                             preferred_element_type=F32) for j in range(yhy_ref.shape[2])]
    yo = (jnp.concatenate(yo_hy, axis=0)
          + jnp.dot(y_ml.astype(BF16), wout_ref[HY_WIDTH:, :], preferred_element_type=F32))
    xf = x_ref[0] + _pe_tile(er_ref[...], ec_ref[...])
    x1 = xf + g1_ref[0] * _rms(yo, post1_ref[...])
    x1_ref[0] = x1
    h2f = _rms(x1, pre2_ref[...]) * (1.0 + sc2_ref[0]) + sh2_ref[0]
    _store_row_tiles(h2_ref, h2f)
    h2 = h2f.astype(BF16)
    logits = jnp.dot(h2, wr_ref[...], preferred_element_type=F32)
    ex = jnp.exp(logits - jnp.max(logits, axis=-1, keepdims=True))
    aff_ref[0] = ex / jnp.sum(ex, axis=-1, keepdims=True)
    logits_t = lax.dot_general(wrt_ref[...], h2, (((1,), (1,)), ((), ())), preferred_element_type=F32)
    ext = jnp.exp(logits_t - jnp.max(logits_t, axis=0, keepdims=True))
    afft_ref[0] = ext / jnp.sum(ext, axis=0, keepdims=True)


def _out_proj(y_hy, hsum, o, x, tabs, g1, sh2, sc2, ml_norm, post1, pre2, w_out, w_router, tm):
    B, L, D = x.shape
    er, ec = tabs
    E = w_router.shape[1]
    full = lambda a: pl.BlockSpec(a.shape, lambda b, i: (0,) * a.ndim)
    tok = lambda w: pl.BlockSpec((1, tm, w), lambda b, i: (b, i, 0))
    modspec = pl.BlockSpec((1, 1, D), lambda b, i: (b, 0, 0))
    wout = w_out.astype(BF16)
    wr = w_router.astype(BF16)
    wrt = w_router.T.astype(BF16)
    return pl.pallas_call(
        _out_proj_kernel,
        grid=(B, L // tm),
        in_specs=[pl.BlockSpec((1, HY_WIDTH, tm // LANES, LANES), lambda b, i: (b, 0, i, 0)),
                  tok(ML_WIDTH), tok(ML_WIDTH), tok(D),
                  pl.BlockSpec((tm // GRID_W, D // 2), lambda b, i: (i, 0)), full(ec),
                  modspec, modspec, modspec, full(ml_norm), full(post1), full(pre2),
                  full(wout), full(wr), full(wrt)],
        out_specs=[tok(D), pl.BlockSpec((1, tm, D // LANES, LANES), lambda b, i: (b, i, 0, 0)), tok(E),
                   pl.BlockSpec((1, E, tm), lambda b, i: (b, 0, i))],
        out_shape=[jax.ShapeDtypeStruct((B, L, D), F32),
                   jax.ShapeDtypeStruct((B, L, D // LANES, LANES), F32),
                   jax.ShapeDtypeStruct((B, L, E), F32), jax.ShapeDtypeStruct((B, E, L), F32)],
        compiler_params=_cparams(("parallel", "parallel")),
        name="out_proj_router",
    )(y_hy, hsum, o, x, er, ec, g1, sh2, sc2, ml_norm, post1, pre2, wout, wr, wrt)


LANES = 128
ROW_GROUP = 16


def _store_row_tiles(ref, val):
    for c in range(val.shape[1] // LANES):
        ref[0, :, c, :] = val[:, c * LANES:(c + 1) * LANES]


def _select_kernel(afft_ref, idx_ref, sel_ref, *, cap):
    E, L = afft_ref.shape[1], afft_ref.shape[2]
    idx_ref[...] = jnp.zeros_like(idx_ref)
    aff = afft_ref[0]
    iota = lax.broadcasted_iota(jnp.int32, (E, L), 1)
    count = lambda ind: jnp.sum(ind, axis=1, keepdims=True)
    count_ge = lambda th: count(jnp.where(aff >= th, 1.0, 0.0))
    pow2 = lambda j: pltpu.bitcast((j - 24) << 23, F32)

    def estep(_, c):
        lo, hi = c
        mid = (lo + hi) >> 1
        ok = count_ge(pow2(mid)) >= cap
        return jnp.where(ok, mid, lo), jnp.where(ok, hi, mid)

    jlo, jhi = lax.fori_loop(0, 7, estep, (jnp.full((E, 1), 24, jnp.int32),
                                           jnp.full((E, 1), 152, jnp.int32)))

    def vstep(_, c):
        lo, hi = c
        mid = lo + (hi - lo) * 0.5
        ok = count_ge(mid) >= cap
        return jnp.where(ok, mid, lo), jnp.where(ok, hi, mid)

    lo, hi = lax.fori_loop(0, 40, vstep, (pow2(jlo), pow2(jhi)))
    gt = jnp.where(aff >= hi, 1.0, 0.0)
    eq = jnp.where(aff >= lo, 1.0, 0.0) - gt
    need = cap - count(gt)

    def istep(_, c):
        lo, hi = c
        mid = (lo + hi) >> 1
        ok = count(jnp.where(iota <= mid, eq, 0.0)) >= need
        return jnp.where(ok, lo, mid), jnp.where(ok, mid, hi)

    _, last = lax.fori_loop(0, L.bit_length() - 1, istep,
                            (jnp.full((E, 1), -1, jnp.int32), jnp.full((E, 1), L - 1, jnp.int32)))
    sel_ref[...] = gt + jnp.where(iota <= last, eq, 0.0)

    T = LANES
    r_i = lax.broadcasted_iota(jnp.int32, (T, T), 0)
    c_i = lax.broadcasted_iota(jnp.int32, (T, T), 1)
    before = (r_i < c_i).astype(BF16)
    slot_f = r_i.astype(F32)
    lane_f = lax.broadcasted_iota(jnp.int32, (1, T), 1).astype(F32)
    esel = lax.broadcasted_iota(jnp.int32, (E, 1), 0)

    def group(g, off):
        s = sel_ref[:, pl.ds(pl.multiple_of(g * T, T), T)]
        rank = jnp.dot(s.astype(BF16), before, preferred_element_type=F32)
        tok = (lane_f + jnp.asarray(g * T, F32)) * s
        for e in range(E):
            hit = rank[e:e + 1, :] == slot_f
            ids = jnp.sum(jnp.where(hit, tok[e:e + 1, :], 0.0), axis=1, keepdims=True)
            o = jnp.sum(jnp.where(esel == e, off, 0))
            idx_ref[0, e, pl.ds(o, T), :] = ids.astype(jnp.int32)
        return off + jnp.sum(s, axis=1, keepdims=True).astype(jnp.int32)

    lax.fori_loop(0, L // T, group, jnp.zeros((E, 1), jnp.int32))


def _route_select(afft, cap):
    B, E, L = afft.shape
    idx = pl.pallas_call(
        functools.partial(_select_kernel, cap=cap),
        grid=(B,),
        in_specs=[pl.BlockSpec((1, E, L), lambda b: (b, 0, 0))],
        out_specs=pl.BlockSpec((1, E, cap + LANES, 1), lambda b: (b, 0, 0, 0)),
        out_shape=jax.ShapeDtypeStruct((B, E, cap + LANES, 1), jnp.int32),
        scratch_shapes=[pltpu.VMEM((E, L), F32)],
        compiler_params=_cparams(("parallel",)),
        name="route_select",
    )(afft)
    return idx[:, :, :cap, 0]


def _gather_kernel(idx_ref, h_ref, aff_ref, xs_ref, gs_ref):
    cap = xs_ref.shape[1]

    def body(s, c):
        base = s * ROW_GROUP
        rows = [idx_ref[0, 0, base + k] for k in range(ROW_GROUP)]
        vals = [h_ref[0, r] for r in rows]
        gates = [aff_ref[0, pl.ds(r, 1), :] for r in rows]
        for k in range(ROW_GROUP):
            xs_ref[0, base + k] = vals[k]
            gs_ref[0, pl.ds(base + k, 1), :] = gates[k]
        return c

    lax.fori_loop(0, cap // ROW_GROUP, body, 0)


def _route_gather(idx, h2t, aff):
    B, E, cap = idx.shape
    _, L, nt, _ = h2t.shape
    return pl.pallas_call(
        _gather_kernel,
        grid=(B, E),
        in_specs=[pl.BlockSpec((1, 1, cap), lambda b, e: (b * E + e, 0, 0), memory_space=pltpu.SMEM),
                  pl.BlockSpec((1, L, nt, LANES), lambda b, e: (b, 0, 0, 0), pipeline_mode=pl.Buffered(1)),
                  pl.BlockSpec((1, L, E), lambda b, e: (b, 0, 0))],
        out_specs=[pl.BlockSpec((1, cap, nt, LANES), lambda b, e: (e, b, 0, 0)),
                   pl.BlockSpec((1, cap, E), lambda b, e: (e, b, 0))],
        out_shape=[jax.ShapeDtypeStruct((E, B * cap, nt, LANES), F32),
                   jax.ShapeDtypeStruct((E, B * cap, E), F32)],
        compiler_params=_cparams(("arbitrary", "arbitrary")),
        name="route_gather",
    )(idx.reshape(B * E, 1, cap), h2t, aff)


def _combine_kernel(idx_ref, ye_ref, y_ref):
    cap = ye_ref.shape[1]

    @pl.when(pl.program_id(1) == 0)
    def _():
        y_ref[...] = jnp.zeros_like(y_ref)

    def body(s, c):
        base = s * ROW_GROUP
        rows = [idx_ref[0, 0, base + k] for k in range(ROW_GROUP)]
        vals = [y_ref[0, rows[k]] + ye_ref[0, base + k] for k in range(ROW_GROUP)]
        for k in range(ROW_GROUP):
            y_ref[0, rows[k]] = vals[k]
        return c

    lax.fori_loop(0, cap // ROW_GROUP, body, 0)


def _route_combine(idx, ye, L):
    B, E, cap = idx.shape
    nt = ye.shape[2]
    return pl.pallas_call(
        _combine_kernel,
        grid=(B, E),
        in_specs=[pl.BlockSpec((1, 1, cap), lambda b, e: (b * E + e, 0, 0), memory_space=pltpu.SMEM),
                  pl.BlockSpec((1, cap, nt, LANES), lambda b, e: (e, b, 0, 0))],
        out_specs=pl.BlockSpec((1, L, nt, LANES), lambda b, e: (b, 0, 0, 0), pipeline_mode=pl.Buffered(1)),
        out_shape=jax.ShapeDtypeStruct((B, L, nt, LANES), F32),
        compiler_params=_cparams(("arbitrary", "arbitrary")),
        name="route_combine",
    )(idx.reshape(B * E, 1, cap), ye)


def _expert_kernel(xs_ref, g_ref, wg_ref, wu_ref, wd_ref, ye_ref, xb_ref, acc_ref):
    e = pl.program_id(0)
    f = pl.program_id(1)
    nf = pl.num_programs(1)
    nt = xs_ref.shape[2]

    @pl.when(f == 0)
    def _():
        for c in range(nt):
            xb_ref[:, c * LANES:(c + 1) * LANES] = xs_ref[0, :, c, :].astype(BF16)

    wg = wg_ref[0].astype(BF16)
    wu = wu_ref[0].astype(BF16)
    wd = wd_ref[0].astype(BF16)
    mt = xb_ref.shape[0] // FFN_M_SPLIT
    for mi in range(FFN_M_SPLIT):
        rs = slice(mi * mt, (mi + 1) * mt)
        xs = xb_ref[rs, :]
        a = jnp.dot(xs, wg, preferred_element_type=F32)
        u = jnp.dot(xs, wu, preferred_element_type=F32)
        hmid = (a * jax.nn.sigmoid(a) * u).astype(BF16)
        part = jnp.dot(hmid, wd, preferred_element_type=F32)

        @pl.when(f == 0)
        def _():
            acc_ref[rs, :] = part

        @pl.when(f > 0)
        def _():
            acc_ref[rs, :] = acc_ref[rs, :] + part

    @pl.when(f == nf - 1)
    def _():
        gs = g_ref[0]
        esel = lax.broadcasted_iota(jnp.int32, gs.shape, 1)
        gate = jnp.sum(jnp.where(esel == e, gs, 0.0), axis=1, keepdims=True)
        for c in range(nt):
            ye_ref[0, :, c, :] = acc_ref[:, c * LANES:(c + 1) * LANES] * gate


FFN_M_SPLIT = 2


def _expert_ffn(xs, gs, w_gate, w_up, w_down, tf):
    E, M, nt, _ = xs.shape
    D = nt * LANES
    F = w_gate.shape[2]
    once = pl.Buffered(1)
    return pl.pallas_call(
        _expert_kernel,
        grid=(E, F // tf),
        in_specs=[pl.BlockSpec((1, M, nt, LANES), lambda e, f: (e, 0, 0, 0), pipeline_mode=once),
                  pl.BlockSpec((1, M, E), lambda e, f: (e, 0, 0), pipeline_mode=once),
                  pl.BlockSpec((1, D, tf), lambda e, f: (e, 0, f)),
                  pl.BlockSpec((1, D, tf), lambda e, f: (e, 0, f)),
                  pl.BlockSpec((1, tf, D), lambda e, f: (e, f, 0))],
        out_specs=pl.BlockSpec((1, M, nt, LANES), lambda e, f: (e, 0, 0, 0), pipeline_mode=once),
        out_shape=jax.ShapeDtypeStruct((E, M, nt, LANES), F32),
        scratch_shapes=[pltpu.VMEM((M, D), BF16), pltpu.VMEM((M, D), F32)],
        compiler_params=_cparams(("parallel", "arbitrary")),
        name="expert_ffn",
    )(xs, gs, w_gate, w_up, w_down)


def _final_kernel(x1_ref, y2_ref, g2_ref, post2_ref, o_ref):
    nt = y2_ref.shape[2]
    cols = [y2_ref[0, :, c, :] for c in range(nt)]
    ssq = sum(jnp.sum(y * y, axis=-1, keepdims=True) for y in cols)
    rstd = lax.rsqrt(ssq * (1.0 / (nt * LANES)) + EPS)
    for c in range(nt):
        cs = slice(c * LANES, (c + 1) * LANES)
        o_ref[0, :, cs] = x1_ref[0, :, cs] + g2_ref[0, :, cs] * (cols[c] * rstd * post2_ref[:, cs])


def _final(x1, y2t, g2, post2, tm):
    B, L, D = x1.shape
    tok = pl.BlockSpec((1, tm, D), lambda b, i: (b, i, 0))
    return pl.pallas_call(
        _final_kernel,
        grid=(B, L // tm),
        in_specs=[tok, pl.BlockSpec((1, tm, D // LANES, LANES), lambda b, i: (b, i, 0, 0)),
                  pl.BlockSpec((1, 1, D), lambda b, i: (b, 0, 0)),
                  pl.BlockSpec((1, D), lambda b, i: (0, 0))],
        out_specs=tok,
        out_shape=jax.ShapeDtypeStruct((B, L, D), F32),
        compiler_params=_cparams(("parallel", "parallel")),
        name="final_residual",
    )(x1, y2t, g2, post2)


def kernel(x, c, ctx, c_ctx, w_mod, b_mod, pre_norm1, post_norm1, pre_norm2, post_norm2, w_in, conv_w, conv_b, filt_w1, filt_b1, filt_w2, filt_b2, filt_w3, filt_b3, filt_w4, filt_freq, hyena_bias, ml_gate_b, ml_norm, w_out, w_router, w_exp_gate, w_exp_up, w_exp_down):
    B, L, D = x.shape
    depth = w_mod.shape[0]
    assert depth == 1, "single-layer block"
    li = 0
    tabs = _pe_tables(L // GRID_W, GRID_W, D)

    cc = jnp.concatenate([c, c_ctx[None], jnp.zeros((8 - B - 1, D), F32)], axis=0)
    mod = _modulation(cc, w_mod[li], b_mod[li])
    chunks = [mod[:, k * D:(k + 1) * D] for k in range(6)]
    sh1, sc1, g1, sh2, sc2, g2 = [m[:B, None, :] for m in chunks]
    csh1, csc1 = chunks[0][B:B + 1, None, :], chunks[1][B:B + 1, None, :]

    pre1 = pre_norm1[li][None, :]
    zhy, q, kt, v, o, gatet = _in_proj(x, tabs, sh1, sc1, pre1, w_in[li], IN_PROJ_TILE, True)
    qc, ktc, vc, _, gatetc = _in_proj(ctx, None, csh1, csc1, pre1, w_in[li], ctx.shape[1], False)

    kcirc = _hyena_filter(L, filt_w1[li], filt_b1[li], filt_w2[li], filt_b2[li], filt_w3[li],
                          filt_b3[li], filt_w4[li], filt_freq[li])
    y_hy = _hyena_conv(zhy, kcirc, conv_w[li], conv_b[li], hyena_bias[li])
    hsum = _mlstm(q, kt, v, gatet, qc, ktc, vc, gatetc, ml_gate_b[li])

    x1, h2t, aff, afft = _out_proj(y_hy, hsum, o, x, tabs, g1, sh2, sc2, ml_norm[li][None, :],
                                   post_norm1[li][None, :], pre_norm2[li][None, :], w_out[li],
                                   w_router[li], IN_PROJ_TILE)

    cap = CAP_FACTOR * L // N_EXPERTS
    idx = _route_select(afft, cap)
    xs, gs = _route_gather(idx, h2t, aff)
    ye = _expert_ffn(xs, gs, w_exp_gate[li], w_exp_up[li], w_exp_down[li], 512)
    y2t = _route_combine(idx, ye, L)

    return _final(x1, y2t, g2, post_norm2[li][None, :], TOKEN_TILE)
```

```python
import functools
import math

import jax
import jax.numpy as jnp
from jax import lax
from jax.experimental import pallas as pl
from jax.experimental.pallas import tpu as pltpu

D_MODEL = 1024
GRID_W = 64
HY_WIDTH = 512
ML_HEADS = 4
ML_DK = 64
ML_DV = 128
ML_WIDTH = ML_HEADS * ML_DV
HY_COLS = 3 * HY_WIDTH
QK_COLS = ML_HEADS * ML_DK
N_GATES = 4 * ML_HEADS
FILTER_EMB = 33
DECAY_TARGET = 1e-2
FAST_DECAY_PCT = 0.3
SLOW_DECAY_PCT = 1.5
CHUNK = 128
GATE_CAP = 15.0
N_EXPERTS = 16
CAP_FACTOR = 2
EPS = 1e-6

F32 = jnp.float32
BF16 = jnp.bfloat16

TOKEN_TILE = 512
IN_PROJ_TILE = 1024
VMEM_LIMIT = 56 * 1024 * 1024


def _cparams(sem):
    return pltpu.CompilerParams(dimension_semantics=sem, vmem_limit_bytes=VMEM_LIMIT)


def _rms(xf, g):
    return xf * lax.rsqrt(jnp.mean(xf * xf, axis=-1, keepdims=True) + EPS) * g


def _bdot(a, b):
    return jnp.dot(a.astype(BF16), b.astype(BF16), preferred_element_type=F32)


def _bdot_nt(a, b):
    return lax.dot_general(a.astype(BF16), b.astype(BF16), (((1,), (1,)), ((), ())),
                           preferred_element_type=F32)


def _pe_tables_kernel(omega_ref, er_ref, ec_ref):
    quarter = omega_ref.shape[1]
    om = omega_ref[...]
    for ref in (er_ref, ec_ref):
        n = ref.shape[0]
        pos = lax.broadcasted_iota(jnp.int32, (n, quarter), 0).astype(F32)
        ang = pos * om
        ref[:, :quarter] = jnp.sin(ang)
        ref[:, quarter:] = jnp.cos(ang)


def _pe_tables(rows, cols, dim):
    quarter = dim // 4
    omega = (1.0 / (10000.0 ** (jnp.arange(quarter, dtype=F32) / quarter)))[None, :]
    return pl.pallas_call(
        _pe_tables_kernel,
        out_shape=(jax.ShapeDtypeStruct((rows, dim // 2), F32),
                   jax.ShapeDtypeStruct((cols, dim // 2), F32)),
        name="pe_tables",
    )(omega)


def _pe_tile(er_blk, ec):
    nr, half = er_blk.shape
    row_part = jnp.broadcast_to(er_blk[:, None, :], (nr, GRID_W, half)).reshape(nr * GRID_W, half)
    col_part = jnp.broadcast_to(ec[None, :, :], (nr, GRID_W, half)).reshape(nr * GRID_W, half)
    return jnp.concatenate([row_part, col_part], axis=-1)


def _mod_kernel(c_ref, w_ref, b_ref, o_ref):
    c = c_ref[...]
    s = c * jax.nn.sigmoid(c)
    o_ref[...] = _bdot(s, w_ref[...]) + b_ref[...]


def _modulation(cc, w_mod, b_mod):
    rows, d = cc.shape
    n = w_mod.shape[1]
    tn = 512
    return pl.pallas_call(
        _mod_kernel,
        grid=(n // tn,),
        in_specs=[pl.BlockSpec((rows, d), lambda j: (0, 0)),
                  pl.BlockSpec((d, tn), lambda j: (0, j)),
                  pl.BlockSpec((1, tn), lambda j: (0, j))],
        out_specs=pl.BlockSpec((rows, tn), lambda j: (0, j)),
        out_shape=jax.ShapeDtypeStruct((rows, n), F32),
        compiler_params=_cparams(("arbitrary",)),
        name="modulation",
    )(cc, w_mod, b_mod[None, :])


def _in_proj_kernel(*refs, with_hyena):
    if with_hyena:
        (x_ref, er_ref, ec_ref, sh_ref, sc_ref, g_ref, wn_ref, wt_ref,
         zt_ref, q_ref, kt_ref, v_ref, o_ref, gatet_ref) = refs
        xf = x_ref[0] + _pe_tile(er_ref[...], ec_ref[...])
    else:
        (x_ref, sh_ref, sc_ref, g_ref, wn_ref, wt_ref,
         q_ref, kt_ref, v_ref, o_ref, gatet_ref) = refs
        xf = x_ref[0]
    h = _rms(xf, g_ref[...]) * (1.0 + sc_ref[0]) + sh_ref[0]
    hb = h.astype(BF16)
    z = jnp.dot(hb, wn_ref[...], preferred_element_type=F32)
    qs = z[:, :QK_COLS] * (ML_DK ** -0.5)
    for hd in range(ML_HEADS):
        q_ref[0, hd] = qs[:, hd * ML_DK:(hd + 1) * ML_DK].astype(BF16)
    v_ref[0] = z[:, QK_COLS:QK_COLS + ML_WIDTH].astype(BF16)
    o_ref[0] = z[:, QK_COLS + ML_WIDTH:]
    zt = lax.dot_general(wt_ref[...], hb, (((1,), (1,)), ((), ())), preferred_element_type=F32)
    off = 0
    if with_hyena:
        for j in range(zt.shape[1] // LANES):
            zt_ref[0, :, j, :] = zt[:HY_COLS, j * LANES:(j + 1) * LANES]
        off = HY_COLS
    for hd in range(ML_HEADS):
        kt_ref[0, hd] = zt[off + hd * ML_DK:off + (hd + 1) * ML_DK, :].astype(BF16)
    gatet_ref[0] = zt[off + QK_COLS:, :]


def _in_proj(x, tabs, sh, sc, g, w_in, tm, with_hyena):
    B, L, D = x.shape
    w_hy = w_in[:, :HY_COLS]
    w_q = w_in[:, HY_COLS:HY_COLS + QK_COLS]
    w_k = w_in[:, HY_COLS + QK_COLS:HY_COLS + 2 * QK_COLS]
    w_vo = w_in[:, HY_COLS + 2 * QK_COLS:HY_COLS + 2 * QK_COLS + 2 * ML_WIDTH]
    w_g = w_in[:, HY_COLS + 2 * QK_COLS + 2 * ML_WIDTH:]
    wn = jnp.concatenate([w_q, w_vo], axis=1).astype(BF16)
    wt = jnp.concatenate(([w_hy] if with_hyena else []) + [w_k, w_g], axis=1).T.astype(BF16)
    nb = sh.shape[0]
    mod_map = (lambda b, i: (b, 0, 0)) if nb > 1 else (lambda b, i: (0, 0, 0))
    full = lambda a: pl.BlockSpec(a.shape, lambda b, i: (0,) * a.ndim)
    in_specs = [pl.BlockSpec((1, tm, D), lambda b, i: (b, i, 0))]
    args = [x]
    if with_hyena:
        er, ec = tabs
        in_specs += [pl.BlockSpec((tm // GRID_W, D // 2), lambda b, i: (i, 0)), full(ec)]
        args += [er, ec]
    in_specs += [pl.BlockSpec((1, 1, D), mod_map), pl.BlockSpec((1, 1, D), mod_map),
                 full(g), full(wn), full(wt)]
    args += [sh, sc, g, wn, wt]
    out_shape, out_specs = [], []
    if with_hyena:
        out_shape.append(jax.ShapeDtypeStruct((B, HY_COLS, L // LANES, LANES), F32))
        out_specs.append(pl.BlockSpec((1, HY_COLS, tm // LANES, LANES), lambda b, i: (b, 0, i, 0)))
    out_shape += [jax.ShapeDtypeStruct((B, ML_HEADS, L, ML_DK), BF16),
                  jax.ShapeDtypeStruct((B, ML_HEADS, ML_DK, L), BF16),
                  jax.ShapeDtypeStruct((B, L, ML_WIDTH), BF16),
                  jax.ShapeDtypeStruct((B, L, ML_WIDTH), F32),
                  jax.ShapeDtypeStruct((B, N_GATES, L), F32)]
    out_specs += [pl.BlockSpec((1, ML_HEADS, tm, ML_DK), lambda b, i: (b, 0, i, 0)),
                  pl.BlockSpec((1, ML_HEADS, ML_DK, tm), lambda b, i: (b, 0, 0, i)),
                  pl.BlockSpec((1, tm, ML_WIDTH), lambda b, i: (b, i, 0)),
                  pl.BlockSpec((1, tm, ML_WIDTH), lambda b, i: (b, i, 0)),
                  pl.BlockSpec((1, N_GATES, tm), lambda b, i: (b, 0, i))]
    return pl.pallas_call(
        functools.partial(_in_proj_kernel, with_hyena=with_hyena),
        grid=(B, L // tm),
        in_specs=in_specs,
        out_specs=out_specs,
        out_shape=out_shape,
        compiler_params=_cparams(("parallel", "parallel")),
        name="in_proj_hy" if with_hyena else "in_proj_ctx",
    )(*args)


FILT_TILE = 1024
FILT_CBLK = 128


def _filter_kernel(w1a_ref, w1b_ref, w1c_ref, b1_ref, w2_ref, b2_ref, w3_ref, b3_ref, fr_ref,
                   w4f_ref, w4b_ref, dl_ref, k_ref, hf_ref, hb_ref, *, L):
    bands = (FILTER_EMB - 1) // 2

    @pl.when(pl.program_id(0) == 0)
    def _():
        fk = (1e-4 + lax.broadcasted_iota(jnp.int32, (bands, 1), 0).astype(F32)
              * ((bands - 1 - 1e-4) / (bands - 1)))
        fr = fr_ref[...]
        for rev, dst in ((False, hf_ref), (True, hb_ref)):
            for j in range(L // FILT_TILE):
                pos = (lax.broadcasted_iota(jnp.int32, (1, FILT_TILE), 1) + j * FILT_TILE).astype(F32)
                if rev:
                    pos = float(L) - pos
                tl = pos * (1.0 / (L - 1))
                ang = fk * (pos * (2.0 * math.pi / L))
                pre = (w1a_ref[...].astype(F32) * tl.astype(BF16).astype(F32)
                       + _bdot(w1b_ref[...], jnp.cos(ang)) + _bdot(w1c_ref[...], -jnp.sin(ang)))
                h = jnp.sin(fr * (pre + b1_ref[...]))
                h = jnp.sin(fr * (_bdot(w2_ref[...], h) + b2_ref[...]))
                h = jnp.sin(fr * (_bdot(w3_ref[...], h) + b3_ref[...]))
                dst[:, j * FILT_TILE:(j + 1) * FILT_TILE] = h.astype(BF16)

    pos = lax.broadcasted_iota(jnp.int32, (1, L), 1).astype(F32)
    dl = dl_ref[...]
    tf = pos * (1.0 / (L - 1))
    k_ref[:, :L] = jnp.dot(w4f_ref[...], hf_ref[...], preferred_element_type=F32) * jnp.exp(-tf * dl)
    tb = (float(L) - pos) * (1.0 / (L - 1))
    kb = jnp.dot(w4b_ref[...], hb_ref[...], preferred_element_type=F32) * jnp.exp(-tb * dl)
    k_ref[:, L:] = jnp.where(pos == 0.0, 0.0, kb)


def _hyena_filter(L, w1, b1, w2, b2, w3, b3, w4, freq):
    hid = w2.shape[0]
    bands = (FILTER_EMB - 1) // 2
    col = lambda a: a[:, None]
    w1t = w1.T.astype(BF16)
    min_decay = math.log(DECAY_TARGET) / SLOW_DECAY_PCT
    max_decay = math.log(DECAY_TARGET) / FAST_DECAY_PCT
    dl = jnp.abs(jnp.linspace(min_decay, max_decay, HY_WIDTH, dtype=F32))[:, None]
    w4t = w4.T.astype(BF16)
    full = lambda a: pl.BlockSpec(a.shape, lambda i: (0,) * a.ndim)
    args = [w1t[:, 0:1], w1t[:, 1:1 + bands], w1t[:, 1 + bands:], col(b1), w2.T.astype(BF16), col(b2),
            w3.T.astype(BF16), col(b3), col(freq)]
    return pl.pallas_call(
        functools.partial(_filter_kernel, L=L),
        grid=(HY_WIDTH // FILT_CBLK,),
        in_specs=[full(a) for a in args] + [
            pl.BlockSpec((FILT_CBLK, hid), lambda i: (i, 0)),
            pl.BlockSpec((FILT_CBLK, hid), lambda i: (HY_WIDTH // FILT_CBLK + i, 0)),
            pl.BlockSpec((FILT_CBLK, 1), lambda i: (i, 0))],
        out_specs=pl.BlockSpec((FILT_CBLK, 2 * L), lambda i: (i, 0)),
        out_shape=jax.ShapeDtypeStruct((HY_WIDTH, 2 * L), F32),
        scratch_shapes=[pltpu.VMEM((hid, L), BF16), pltpu.VMEM((hid, L), BF16)],
        compiler_params=_cparams(("arbitrary",)),
        name="hyena_filter",
    )(*args, w4t, w4t, dl)


FFT_N = 128
HY_CBLK = 32
HY_GROUP = 8
DFT_PASSES = 1


def _dft_constants(n1_data):
    import numpy as np
    n = FFT_N
    k = np.arange(n)
    ang = -2.0 * np.pi * ((k[:, None] * k[None, :]) % n) / n
    fre, fim = np.cos(ang), np.sin(ang)
    m = n1_data
    fa_d = np.block([[fre[:, :m], -fim[:, :m]], [fim[:, :m], fre[:, :m]]])
    fa_f = np.concatenate([fre, fim], axis=0)
    fb = np.block([[fre, fim], [-fim, fre]])
    fbi = np.block([[fre, -fim], [fim, fre]])
    fc = np.block([[fre[:m, :], fim[:m, :]], [-fim[:m, :], fre[:m, :]]]) / (n * n)
    tang = -2.0 * np.pi * (k[:, None] * k[None, :]) / (n * n)
    tw = np.stack([np.cos(tang), np.sin(tang)])

    def hilo(a):
        a32 = jnp.asarray(a, F32)
        hi = a32.astype(BF16)
        lo = (a32 - hi.astype(F32)).astype(BF16)
        return jnp.stack([hi, lo])

    return hilo(fa_d), hilo(fa_f), hilo(fb), hilo(fbi), hilo(fc), jnp.asarray(tw, F32)


def _mm_const_lhs(c_ref, d):
    dh = d.astype(BF16)
    acc = jnp.dot(c_ref[0], dh, preferred_element_type=F32)
    if DFT_PASSES == 3:
        dl = (d - dh.astype(F32)).astype(BF16)
        acc = acc + (jnp.dot(c_ref[0], dl, preferred_element_type=F32)
                     + jnp.dot(c_ref[1], dh, preferred_element_type=F32))
    return acc


def _mm_const_rhs(d, c_ref):
    dh = d.astype(BF16)
    acc = jnp.dot(dh, c_ref[0], preferred_element_type=F32)
    if DFT_PASSES == 3:
        dl = (d - dh.astype(F32)).astype(BF16)
        acc = acc + (jnp.dot(dl, c_ref[0], preferred_element_type=F32)
                     + jnp.dot(dh, c_ref[1], preferred_element_type=F32))
    return acc


def _cmul(are, aim, bre, bim):
    return are * bre - aim * bim, are * bim + aim * bre


def _hyena_conv_kernel(x0_ref, x1_ref, v_ref, kc_ref, w0_ref, w1_ref, wv_ref, b0_ref, b1_ref, bv_ref,
                       hb_ref, fad_ref, faf_ref, fb_ref, fbi_ref, fc_ref, tw_ref,
                       o_ref, u_ref, s_ref, ks_ref):
    n = FFT_N
    cb = kc_ref.shape[0]
    m = x0_ref.shape[2]
    nb = x0_ref.shape[0]
    sub = lax.broadcasted_iota(jnp.int32, (m, n), 0)
    lane = lax.broadcasted_iota(jnp.int32, (m, n), 1)
    tre, tim = tw_ref[0], tw_ref[1]

    def sconv(z, w_ref, b_ref, c):
        a = pltpu.roll(z, 1, axis=1)
        prev = jnp.where(lane == 0, jnp.where(sub == 0, 0.0, pltpu.roll(a, 1, axis=0)), a)
        a2 = pltpu.roll(z, n - 1, axis=1)
        nxt = jnp.where(lane == n - 1, jnp.where(sub == m - 1, 0.0, pltpu.roll(a2, m - 1, axis=0)), a2)
        return prev * w_ref[0, c] + z * w_ref[1, c] + nxt * w_ref[2, c] + b_ref[c]

    def spectrum_rows(res):
        outs = []
        for h in range(2):
            are, aim = _cmul(res[:n, h * n:(h + 1) * n], res[n:, h * n:(h + 1) * n], tre, tim)
            outs.append(jnp.concatenate([are, aim], axis=1))
        return outs

    def fwd_pair(p, carry):
        c0 = 2 * p
        us = []
        for c in (c0, c0 + 1):
            ub = []
            for b in range(nb):
                x1c = sconv(x1_ref[b, c], w1_ref, b1_ref, c)
                vc = sconv(v_ref[b, c], wv_ref, bv_ref, c)
                u = x1c * vc
                u_ref[b, c] = u
                ub.append(u)
            us.append(ub)
        wd = jnp.concatenate([jnp.concatenate([us[0][b], us[1][b]], axis=1) for b in range(nb)], axis=0)
        sa, sb = spectrum_rows(_mm_const_lhs(fad_ref, wd))
        s_ref[c0] = sa
        s_ref[c0 + 1] = sb
        wk = jnp.concatenate([kc_ref[c0], kc_ref[c0 + 1]], axis=1)
        ka, kb = spectrum_rows(_mm_const_lhs(faf_ref, wk))
        ks_ref[c0] = ka
        ks_ref[c0 + 1] = kb
        return carry

    lax.fori_loop(0, cb // 2, fwd_pair, 0)

    def mid_group(g, carry):
        gs = pl.ds(pl.multiple_of(g * HY_GROUP, HY_GROUP), HY_GROUP)
        x = _mm_const_rhs(s_ref[gs].reshape(HY_GROUP * n, 2 * n), fb_ref)
        k = _mm_const_rhs(ks_ref[gs].reshape(HY_GROUP * n, 2 * n), fb_ref)
        yre, yim = _cmul(x[:, :n], x[:, n:], k[:, :n], k[:, n:])
        vv = _mm_const_rhs(jnp.concatenate([yre, yim], axis=1), fbi_ref).reshape(HY_GROUP, n, 2 * n)
        vre, vim = _cmul(vv[:, :, :n], vv[:, :, n:], tre[None], -tim[None])
        s_ref[gs] = jnp.concatenate([vre, vim], axis=2)
        return carry

    lax.fori_loop(0, cb // HY_GROUP, mid_group, 0)

    def inv_pair(p, carry):
        c0 = 2 * p
        sa, sb = s_ref[c0], s_ref[c0 + 1]
        wd = jnp.concatenate([jnp.concatenate([sa[:, :n], sb[:, :n]], axis=1),
                              jnp.concatenate([sa[:, n:], sb[:, n:]], axis=1)], axis=0)
        res = _mm_const_lhs(fc_ref, wd)
        for h, c in enumerate((c0, c0 + 1)):
            for b in range(nb):
                y = res[b * m:(b + 1) * m, h * n:(h + 1) * n]
                x0c = sconv(x0_ref[b, c], w0_ref, b0_ref, c)
                o_ref[b, c] = x0c * (y + hb_ref[c] * u_ref[b, c])
        return carry

    lax.fori_loop(0, cb // 2, inv_pair, 0)


def _hyena_conv(zt, kcirc, conv_w, conv_b, hy_bias):
    B, _, m, n = zt.shape
    L = m * n
    C = HY_WIDTH
    assert B == 2 and n == FFT_N and 2 * L == n * n, "complex packing of two samples over a 128 x 128 point transform"
    z4 = zt
    k3 = kcirc.reshape(C, n, n)
    cw = conv_w.reshape(3, 3 * C, 1, 1)
    cbias = conv_b.reshape(3 * C, 1, 1)
    hb = hy_bias.reshape(C, 1, 1)
    consts = _dft_constants(m)
    nblk = C // HY_CBLK
    zspec = lambda part: pl.BlockSpec((B, HY_CBLK, m, n), lambda i: (0, part * nblk + i, 0, 0))
    wspec = lambda part: pl.BlockSpec((3, HY_CBLK, 1, 1), lambda i: (0, part * nblk + i, 0, 0))
    bspec = lambda part: pl.BlockSpec((HY_CBLK, 1, 1), lambda i: (part * nblk + i, 0, 0))
    full = lambda a: pl.BlockSpec(a.shape, lambda i: (0,) * a.ndim)
    y = pl.pallas_call(
        _hyena_conv_kernel,
        grid=(nblk,),
        in_specs=[zspec(0), zspec(1), zspec(2), pl.BlockSpec((HY_CBLK, n, n), lambda i: (i, 0, 0)),
                  wspec(0), wspec(1), wspec(2), bspec(0), bspec(1), bspec(2), bspec(0)]
                 + [full(a) for a in consts],
        out_specs=pl.BlockSpec((B, HY_CBLK, m, n), lambda i: (0, i, 0, 0)),
        out_shape=jax.ShapeDtypeStruct((B, C, m, n), F32),
        scratch_shapes=[pltpu.VMEM((B, HY_CBLK, m, n), F32), pltpu.VMEM((HY_CBLK, n, 2 * n), F32),
                        pltpu.VMEM((HY_CBLK, n, 2 * n), F32)],
        compiler_params=_cparams(("parallel",)),
        name="hyena_conv",
    )(z4, z4, z4, k3, cw, cw, cw, cbias, cbias, cbias, hb, *consts)
    return y


def _split3(a):
    hi = a.astype(BF16)
    r1 = a - hi.astype(F32)
    mid = r1.astype(BF16)
    lo = (r1 - mid.astype(F32)).astype(BF16)
    return hi, mid, lo


def _exact_dot_right(a, tri_bf):
    hi, mid, lo = _split3(a)
    d = lambda p: jnp.dot(p, tri_bf, preferred_element_type=F32)
    return (d(lo) + d(mid)) + d(hi)


def _soft_gates(g):
    g = GATE_CAP * jnp.tanh(g * (1.0 / GATE_CAP))
    logsig = jnp.minimum(g, 0.0) - jnp.log1p(jnp.exp(-jnp.abs(g)))
    return g, logsig


def _gate_prep_kernel(gt_ref, gbt_ref, rows_ref, cols_ref):
    T = CHUNK
    n = gt_ref.shape[2]
    up_bf = (lax.broadcasted_iota(jnp.int32, (T, T), 1) >= lax.broadcasted_iota(jnp.int32, (T, T), 0)).astype(BF16)
    out8 = lax.broadcasted_iota(jnp.int32, (8, 1), 0)
    cap, ls = _soft_gates(gt_ref[0] + gbt_ref[...])
    H = ML_HEADS
    for hd in range(H):
        base = jnp.where(out8 == 0, cap[hd:hd + 1], jnp.where(out8 == 1, ls[H + hd:H + hd + 1],
                         jnp.where(out8 == 2, cap[2 * H + hd:2 * H + hd + 1],
                                   jnp.where(out8 == 3, ls[3 * H + hd:3 * H + hd + 1], 0.0))))
        for j in range(n // T):
            blk = base[:, j * T:(j + 1) * T]
            run = _exact_dot_right(blk, up_bf)
            suf = run[:, T - 1:T] - run + blk
            rows = jnp.where(out8 == 1, run, jnp.where(out8 == 3, suf, blk))
            rows_ref[0, hd, :, j * T:(j + 1) * T] = rows
            cols_ref[0, hd, j * T:(j + 1) * T, :] = rows.T


def _gate_prep(gatet, gate_b, tile):
    B, G, L = gatet.shape
    H = ML_HEADS
    return pl.pallas_call(
        _gate_prep_kernel,
        grid=(B, L // tile),
        in_specs=[pl.BlockSpec((1, G, tile), lambda b, i: (b, 0, i)),
                  pl.BlockSpec((G, 1), lambda b, i: (0, 0))],
        out_specs=[pl.BlockSpec((1, H, 8, tile), lambda b, i: (b, 0, 0, i)),
                   pl.BlockSpec((1, H, tile, 8), lambda b, i: (b, 0, i, 0))],
        out_shape=[jax.ShapeDtypeStruct((B, H, 8, L), F32), jax.ShapeDtypeStruct((B, H, L, 8), F32)],
        compiler_params=_cparams(("parallel", "parallel")),
        name="mlstm_gate_prep",
    )(gatet, gate_b[:, None])


def _mlstm_kernel(q_ref, kt_ref, v_ref, r_ref, bt_ref, qc_ref, ktc_ref, vc_ref, rc_ref, btc_ref,
                  h_ref, cf_ref, cb_ref):
    T = CHUNK
    L = q_ref.shape[2]
    Lc = qc_ref.shape[2]
    nc, ncc = L // T, Lc // T
    row = lax.broadcasted_iota(jnp.int32, (T, T), 0)
    col = lax.broadcasted_iota(jnp.int32, (T, T), 1)
    lo_mask = col <= row
    up_mask = col >= row
    ones_blk = jnp.ones((T, ML_DV), BF16)

    cf_ref[...] = jnp.zeros_like(cf_ref)
    cb_ref[...] = jnp.zeros_like(cb_ref)

    def chunk_step(q, kt, v, rows, cols, c_ref, backward):
        i_r = rows[2:3] if backward else rows[0:1]
        b_r = rows[3:4] if backward else rows[1:2]
        b_c = jnp.broadcast_to(cols[:, 3:4] if backward else cols[:, 1:2], (T, T))
        b_end = b_r[:, 0:1] if backward else b_r[:, T - 1:T]
        mask = up_mask if backward else lo_mask
        w_intra = jnp.exp(jnp.where(mask, b_c - b_r + i_r, -jnp.inf) - GATE_CAP)
        s = jnp.dot(q, kt, preferred_element_type=F32) * w_intra
        qe = (q.astype(F32) * jnp.exp(b_c[:, :ML_DK])).astype(BF16)
        v_aug = jnp.concatenate([v, ones_blk], axis=1)
        c_aug = c_ref[...]
        res = jnp.dot(jnp.concatenate([s.astype(BF16), qe], axis=1),
                      jnp.concatenate([v_aug, c_aug.astype(BF16)], axis=0),
                      preferred_element_type=F32)
        h = res[:, :ML_DV] / jnp.maximum(jnp.abs(res[:, ML_DV:]), math.exp(-GATE_CAP))
        kw = (kt.astype(F32) * jnp.exp(b_end - b_r + i_r - GATE_CAP)).astype(BF16)
        c_ref[...] = jnp.exp(b_end) * c_aug + jnp.dot(kw, v_aug, preferred_element_type=F32)
        return h

    for j in range(ncc):
        for backward in (False, True):
            jj = (ncc - 1 - j) if backward else j
            cs = slice(jj * T, (jj + 1) * T)
            chunk_step(qc_ref[0, 0, cs, :], ktc_ref[0, 0, :, cs], vc_ref[0, cs, :], rc_ref[0, 0, :, cs],
                       btc_ref[0, 0, cs, :], cb_ref if backward else cf_ref, backward)

    def latent_pair(j, accumulate):
        for backward in (False, True):
            jj = (nc - 1 - j) if backward else j
            rs = pl.ds(pl.multiple_of(jj * T, T), T)
            h = chunk_step(q_ref[0, 0, rs, :], kt_ref[0, 0, :, rs], v_ref[0, rs, :], r_ref[0, 0, :, rs],
                           bt_ref[0, 0, rs, :], cb_ref if backward else cf_ref, backward)
            if accumulate:
                h_ref[0, rs, :] = h_ref[0, rs, :] + h
            else:
                h_ref[0, rs, :] = h

    def first_half(j, carry):
        latent_pair(j, False)
        return carry

    def second_half(j, carry):
        latent_pair(j, True)
        return carry

    lax.fori_loop(0, nc // 2, first_half, 0, unroll=MLSTM_UNROLL)
    lax.fori_loop(nc // 2, nc, second_half, 0, unroll=MLSTM_UNROLL)


MLSTM_UNROLL = 2


def _mlstm(q, kt, v, gatet, qc, ktc, vc, gatetc, gate_b):
    B, H, L, dk = q.shape
    Lc = qc.shape[2]
    rows, cols = _gate_prep(gatet, gate_b, min(L, 1024))
    rows_c, cols_c = _gate_prep(gatetc, gate_b, Lc)
    seq = lambda n: [pl.BlockSpec((1, 1, n, dk), lambda b, h: (b, h, 0, 0)),
                     pl.BlockSpec((1, 1, dk, n), lambda b, h: (b, h, 0, 0)),
                     pl.BlockSpec((1, n, ML_DV), lambda b, h: (b, 0, h)),
                     pl.BlockSpec((1, 1, 8, n), lambda b, h: (b, h, 0, 0)),
                     pl.BlockSpec((1, 1, n, 8), lambda b, h: (b, h, 0, 0))]
    return pl.pallas_call(
        _mlstm_kernel,
        grid=(B, H),
        in_specs=seq(L) + seq(Lc),
        out_specs=pl.BlockSpec((1, L, ML_DV), lambda b, h: (b, 0, h)),
        out_shape=jax.ShapeDtypeStruct((B, L, ML_WIDTH), F32),
        scratch_shapes=[pltpu.VMEM((dk, 2 * ML_DV), F32), pltpu.VMEM((dk, 2 * ML_DV), F32)],
        compiler_params=_cparams(("parallel", "parallel")),
        name="mlstm_scan",
    )(q, kt, v, rows, cols, qc, ktc, vc, rows_c, cols_c)


def _out_proj_kernel(yhy_ref, hs_ref, o_ref, x_ref, er_ref, ec_ref, g1_ref, sh2_ref, sc2_ref,
                     mln_ref, post1_ref, pre2_ref, wout_ref, wr_ref, wrt_ref,
                     x1_ref, h2_ref, aff_ref, afft_ref):
    hs = hs_ref[0]
    parts = []
    for hd in range(ML_HEADS):
        hh = hs[:, hd * ML_DV:(hd + 1) * ML_DV]
        parts.append(hh * lax.rsqrt(jnp.mean(hh * hh, axis=-1, keepdims=True) + EPS))
    hn = jnp.concatenate(parts, axis=-1) * mln_ref[...]
    y_ml = hn * jax.nn.sigmoid(o_ref[0])
    yo_hy = [lax.dot_general(yhy_ref[0, :, j, :].astype(BF16), wout_ref[:HY_WIDTH, :], (((0,), (0,)), ((), ())),
                             preferred_element_type=F32) for j in range(yhy_ref.shape[2])]
    yo = (jnp.concatenate(yo_hy, axis=0)
          + jnp.dot(y_ml.astype(BF16), wout_ref[HY_WIDTH:, :], preferred_element_type=F32))
    xf = x_ref[0] + _pe_tile(er_ref[...], ec_ref[...])
    x1 = xf + g1_ref[0] * _rms(yo, post1_ref[...])
    x1_ref[0] = x1
    h2f = _rms(x1, pre2_ref[...]) * (1.0 + sc2_ref[0]) + sh2_ref[0]
    _store_row_tiles(h2_ref, h2f)
    h2 = h2f.astype(BF16)
    logits = jnp.dot(h2, wr_ref[...], preferred_element_type=F32)
    ex = jnp.exp(logits - jnp.max(logits, axis=-1, keepdims=True))
    aff_ref[0] = ex / jnp.sum(ex, axis=-1, keepdims=True)
    logits_t = lax.dot_general(wrt_ref[...], h2, (((1,), (1,)), ((), ())), preferred_element_type=F32)
    ext = jnp.exp(logits_t - jnp.max(logits_t, axis=0, keepdims=True))
    afft_ref[0] = ext / jnp.sum(ext, axis=0, keepdims=True)


def _out_proj(y_hy, hsum, o, x, tabs, g1, sh2, sc2, ml_norm, post1, pre2, w_out, w_router, tm):
    B, L, D = x.shape
    er, ec = tabs
    E = w_router.shape[1]
    full = lambda a: pl.BlockSpec(a.shape, lambda b, i: (0,) * a.ndim)
    tok = lambda w: pl.BlockSpec((1, tm, w), lambda b, i: (b, i, 0))
    modspec = pl.BlockSpec((1, 1, D), lambda b, i: (b, 0, 0))
    wout = w_out.astype(BF16)
    wr = w_router.astype(BF16)
    wrt = w_router.T.astype(BF16)
    return pl.pallas_call(
        _out_proj_kernel,
        grid=(B, L // tm),
        in_specs=[pl.BlockSpec((1, HY_WIDTH, tm // LANES, LANES), lambda b, i: (b, 0, i, 0)),
                  tok(ML_WIDTH), tok(ML_WIDTH), tok(D),
                  pl.BlockSpec((tm // GRID_W, D // 2), lambda b, i: (i, 0)), full(ec),
                  modspec, modspec, modspec, full(ml_norm), full(post1), full(pre2),
                  full(wout), full(wr), full(wrt)],
        out_specs=[tok(D), pl.BlockSpec((1, tm * (D // LANES), LANES), lambda b, i: (b, i, 0)), tok(E),
                   pl.BlockSpec((1, E, tm), lambda b, i: (b, 0, i))],
        out_shape=[jax.ShapeDtypeStruct((B, L, D), F32),
                   jax.ShapeDtypeStruct((B, L * (D // LANES), LANES), F32),
                   jax.ShapeDtypeStruct((B, L, E), F32), jax.ShapeDtypeStruct((B, E, L), F32)],
        compiler_params=_cparams(("parallel", "parallel")),
        name="out_proj_router",
    )(y_hy, hsum, o, x, er, ec, g1, sh2, sc2, ml_norm, post1, pre2, wout, wr, wrt)


LANES = 128
ROW_GROUP = 16


def _store_row_tiles(ref, val):
    n, nt = val.shape[0], val.shape[1] // LANES
    for c in range(nt):
        ref[0, pl.ds(c, n, stride=nt), :] = val[:, c * LANES:(c + 1) * LANES]


def _load_row_tiles(ref, c, n, nt):
    return ref[0, pl.ds(c, n, stride=nt), :]


def _select_kernel(afft_ref, idx_ref, sel_ref, *, cap):
    E, L = afft_ref.shape[1], afft_ref.shape[2]
    idx_ref[...] = jnp.zeros_like(idx_ref)
    aff = afft_ref[0]
    iota = lax.broadcasted_iota(jnp.int32, (E, L), 1)
    count = lambda ind: jnp.sum(ind, axis=1, keepdims=True)
    count_ge = lambda th: count(jnp.where(aff >= th, 1.0, 0.0))
    pow2 = lambda j: pltpu.bitcast((j - 24) << 23, F32)

    def estep(_, c):
        lo, hi = c
        mid = (lo + hi) >> 1
        ok = count_ge(pow2(mid)) >= cap
        return jnp.where(ok, mid, lo), jnp.where(ok, hi, mid)

    jlo, jhi = lax.fori_loop(0, 7, estep, (jnp.full((E, 1), 24, jnp.int32),
                                           jnp.full((E, 1), 152, jnp.int32)))

    def vstep(_, c):
        lo, hi = c
        mid = lo + (hi - lo) * 0.5
        ok = count_ge(mid) >= cap
        return jnp.where(ok, mid, lo), jnp.where(ok, hi, mid)

    lo, hi = lax.fori_loop(0, 40, vstep, (pow2(jlo), pow2(jhi)))
    gt = jnp.where(aff >= hi, 1.0, 0.0)
    eq = jnp.where(aff >= lo, 1.0, 0.0) - gt
    need = cap - count(gt)

    def istep(_, c):
        lo, hi = c
        mid = (lo + hi) >> 1
        ok = count(jnp.where(iota <= mid, eq, 0.0)) >= need
        return jnp.where(ok, lo, mid), jnp.where(ok, mid, hi)

    _, last = lax.fori_loop(0, L.bit_length() - 1, istep,
                            (jnp.full((E, 1), -1, jnp.int32), jnp.full((E, 1), L - 1, jnp.int32)))
    sel_ref[...] = gt + jnp.where(iota <= last, eq, 0.0)

    T = LANES
    r_i = lax.broadcasted_iota(jnp.int32, (T, T), 0)
    c_i = lax.broadcasted_iota(jnp.int32, (T, T), 1)
    before = (r_i < c_i).astype(BF16)
    slot_f = r_i.astype(F32)
    lane_f = lax.broadcasted_iota(jnp.int32, (1, T), 1).astype(F32)
    esel = lax.broadcasted_iota(jnp.int32, (E, 1), 0)

    def group(g, off):
        s = sel_ref[:, pl.ds(pl.multiple_of(g * T, T), T)]
        rank = jnp.dot(s.astype(BF16), before, preferred_element_type=F32)
        tok = (lane_f + jnp.asarray(g * T, F32)) * s
        for e in range(E):
            hit = rank[e:e + 1, :] == slot_f
            ids = jnp.sum(jnp.where(hit, tok[e:e + 1, :], 0.0), axis=1, keepdims=True)
            o = jnp.sum(jnp.where(esel == e, off, 0))
            idx_ref[0, e, pl.ds(o, T), :] = ids.astype(jnp.int32)
        return off + jnp.sum(s, axis=1, keepdims=True).astype(jnp.int32)

    lax.fori_loop(0, L // T, group, jnp.zeros((E, 1), jnp.int32))


def _route_select(afft, cap):
    B, E, L = afft.shape
    idx = pl.pallas_call(
        functools.partial(_select_kernel, cap=cap),
        grid=(B,),
        in_specs=[pl.BlockSpec((1, E, L), lambda b: (b, 0, 0))],
        out_specs=pl.BlockSpec((1, E, cap + LANES, 1), lambda b: (b, 0, 0, 0)),
        out_shape=jax.ShapeDtypeStruct((B, E, cap + LANES, 1), jnp.int32),
        scratch_shapes=[pltpu.VMEM((E, L), F32)],
        compiler_params=_cparams(("parallel",)),
        name="route_select",
    )(afft)
    return idx[:, :, :cap, 0]


def _tile_rows(i, nt):
    return pl.ds(pl.multiple_of(i * nt, nt), nt)


def _gather_kernel(idx_ref, h_ref, aff_ref, xs_ref, gs_ref, buf_ref):
    cap, D = xs_ref.shape[1], xs_ref.shape[2]
    nt = D // LANES

    def body(s, c):
        base = s * ROW_GROUP
        rows = [idx_ref[0, 0, base + k] for k in range(ROW_GROUP)]
        vals = [h_ref[0, _tile_rows(r, nt), :] for r in rows]
        gates = [aff_ref[0, pl.ds(r, 1), :] for r in rows]
        for k in range(ROW_GROUP):
            buf_ref[_tile_rows(base + k, nt), :] = vals[k]
            gs_ref[0, pl.ds(base + k, 1), :] = gates[k]
        return c

    lax.fori_loop(0, cap // ROW_GROUP, body, 0)
    for c in range(nt):
        xs_ref[0, :, c * LANES:(c + 1) * LANES] = buf_ref[pl.ds(c, cap, stride=nt), :].astype(BF16)


def _route_gather(idx, h2t, aff, D):
    B, E, cap = idx.shape
    nt = D // LANES
    L = h2t.shape[1] // nt
    return pl.pallas_call(
        _gather_kernel,
        grid=(B, E),
        in_specs=[pl.BlockSpec((1, 1, cap), lambda b, e: (b * E + e, 0, 0), memory_space=pltpu.SMEM),
                  pl.BlockSpec((1, L * nt, LANES), lambda b, e: (b, 0, 0), pipeline_mode=pl.Buffered(1)),
                  pl.BlockSpec((1, L, E), lambda b, e: (b, 0, 0))],
        out_specs=[pl.BlockSpec((1, cap, D), lambda b, e: (e, b, 0)),
                   pl.BlockSpec((1, cap, E), lambda b, e: (e, b, 0))],
        out_shape=[jax.ShapeDtypeStruct((E, B * cap, D), BF16),
                   jax.ShapeDtypeStruct((E, B * cap, E), F32)],
        scratch_shapes=[pltpu.VMEM((cap * nt, LANES), F32)],
        compiler_params=_cparams(("arbitrary", "arbitrary")),
        name="route_gather",
    )(idx.reshape(B * E, 1, cap), h2t, aff)


def _combine_kernel(idx_ref, ye_ref, y_ref, buf_ref):
    cap, D = ye_ref.shape[1], ye_ref.shape[2]
    nt = D // LANES

    @pl.when(pl.program_id(1) == 0)
    def _():
        y_ref[...] = jnp.zeros_like(y_ref)

    for c in range(nt):
        buf_ref[pl.ds(c, cap, stride=nt), :] = ye_ref[0, :, c * LANES:(c + 1) * LANES]

    def body(s, c):
        base = s * ROW_GROUP
        rows = [idx_ref[0, 0, base + k] for k in range(ROW_GROUP)]
        vals = [y_ref[0, _tile_rows(rows[k], nt), :] + buf_ref[_tile_rows(base + k, nt), :]
                for k in range(ROW_GROUP)]
        for k in range(ROW_GROUP):
            y_ref[0, _tile_rows(rows[k], nt), :] = vals[k]
        return c

    lax.fori_loop(0, cap // ROW_GROUP, body, 0)


def _route_combine(idx, ye, L):
    B, E, cap = idx.shape
    D = ye.shape[2]
    nt = D // LANES
    return pl.pallas_call(
        _combine_kernel,
        grid=(B, E),
        in_specs=[pl.BlockSpec((1, 1, cap), lambda b, e: (b * E + e, 0, 0), memory_space=pltpu.SMEM),
                  pl.BlockSpec((1, cap, D), lambda b, e: (e, b, 0))],
        out_specs=pl.BlockSpec((1, L * nt, LANES), lambda b, e: (b, 0, 0), pipeline_mode=pl.Buffered(1)),
        out_shape=jax.ShapeDtypeStruct((B, L * nt, LANES), F32),
        scratch_shapes=[pltpu.VMEM((cap * nt, LANES), F32)],
        compiler_params=_cparams(("arbitrary", "arbitrary")),
        name="route_combine",
    )(idx.reshape(B * E, 1, cap), ye)


def _expert_kernel(xs_ref, g_ref, wg_ref, wu_ref, wd_ref, ye_ref):
    e = pl.program_id(0)
    f = pl.program_id(1)
    nf = pl.num_programs(1)
    wg = wg_ref[0].astype(BF16)
    wu = wu_ref[0].astype(BF16)
    wd = wd_ref[0].astype(BF16)
    mt = xs_ref.shape[1] // FFN_M_SPLIT
    for mi in range(FFN_M_SPLIT):
        rs = slice(mi * mt, (mi + 1) * mt)
        xs = xs_ref[0, rs, :]
        a = jnp.dot(xs, wg, preferred_element_type=F32)
        u = jnp.dot(xs, wu, preferred_element_type=F32)
        hmid = (a * jax.nn.sigmoid(a) * u).astype(BF16)
        part = jnp.dot(hmid, wd, preferred_element_type=F32)

        @pl.when(f == 0)
        def _():
            ye_ref[0, rs, :] = part

        @pl.when(f > 0)
        def _():
            ye_ref[0, rs, :] = ye_ref[0, rs, :] + part

    @pl.when(f == nf - 1)
    def _():
        gs = g_ref[0]
        esel = lax.broadcasted_iota(jnp.int32, gs.shape, 1)
        ye_ref[0] = ye_ref[0] * jnp.sum(jnp.where(esel == e, gs, 0.0), axis=1, keepdims=True)


FFN_M_SPLIT = 2


def _expert_ffn(xs, gs, w_gate, w_up, w_down, tf):
    E, M, D = xs.shape
    F = w_gate.shape[2]
    return pl.pallas_call(
        _expert_kernel,
        grid=(E, F // tf),
        in_specs=[pl.BlockSpec((1, M, D), lambda e, f: (e, 0, 0)),
                  pl.BlockSpec((1, M, E), lambda e, f: (e, 0, 0)),
                  pl.BlockSpec((1, D, tf), lambda e, f: (e, 0, f)),
                  pl.BlockSpec((1, D, tf), lambda e, f: (e, 0, f)),
                  pl.BlockSpec((1, tf, D), lambda e, f: (e, f, 0))],
        out_specs=pl.BlockSpec((1, M, D), lambda e, f: (e, 0, 0)),
        out_shape=jax.ShapeDtypeStruct((E, M, D), F32),
        compiler_params=_cparams(("parallel", "arbitrary")),
        name="expert_ffn",
    )(xs, gs, w_gate, w_up, w_down)


def _final_kernel(x1_ref, y2_ref, g2_ref, post2_ref, o_ref):
    n = x1_ref.shape[1]
    nt = x1_ref.shape[2] // LANES
    cols = [_load_row_tiles(y2_ref, c, n, nt) for c in range(nt)]
    ssq = sum(jnp.sum(y * y, axis=-1, keepdims=True) for y in cols)
    rstd = lax.rsqrt(ssq * (1.0 / (nt * LANES)) + EPS)
    for c in range(nt):
        cs = slice(c * LANES, (c + 1) * LANES)
        o_ref[0, :, cs] = x1_ref[0, :, cs] + g2_ref[0, :, cs] * (cols[c] * rstd * post2_ref[:, cs])


def _final(x1, y2t, g2, post2, tm):
    B, L, D = x1.shape
    tok = pl.BlockSpec((1, tm, D), lambda b, i: (b, i, 0))
    return pl.pallas_call(
        _final_kernel,
        grid=(B, L // tm),
        in_specs=[tok, pl.BlockSpec((1, tm * (D // LANES), LANES), lambda b, i: (b, i, 0)),
                  pl.BlockSpec((1, 1, D), lambda b, i: (b, 0, 0)),
                  pl.BlockSpec((1, D), lambda b, i: (0, 0))],
        out_specs=tok,
        out_shape=jax.ShapeDtypeStruct((B, L, D), F32),
        compiler_params=_cparams(("parallel", "parallel")),
        name="final_residual",
    )(x1, y2t, g2, post2)


def kernel(x, c, ctx, c_ctx, w_mod, b_mod, pre_norm1, post_norm1, pre_norm2, post_norm2, w_in, conv_w, conv_b, filt_w1, filt_b1, filt_w2, filt_b2, filt_w3, filt_b3, filt_w4, filt_freq, hyena_bias, ml_gate_b, ml_norm, w_out, w_router, w_exp_gate, w_exp_up, w_exp_down):
    B, L, D = x.shape
    depth = w_mod.shape[0]
    assert depth == 1, "single-layer block"
    li = 0
    tabs = _pe_tables(L // GRID_W, GRID_W, D)

    cc = jnp.concatenate([c, c_ctx[None], jnp.zeros((8 - B - 1, D), F32)], axis=0)
    mod = _modulation(cc, w_mod[li], b_mod[li])
    chunks = [mod[:, k * D:(k + 1) * D] for k in range(6)]
    sh1, sc1, g1, sh2, sc2, g2 = [m[:B, None, :] for m in chunks]
    csh1, csc1 = chunks[0][B:B + 1, None, :], chunks[1][B:B + 1, None, :]

    pre1 = pre_norm1[li][None, :]
    zhy, q, kt, v, o, gatet = _in_proj(x, tabs, sh1, sc1, pre1, w_in[li], IN_PROJ_TILE, True)
    qc, ktc, vc, _, gatetc = _in_proj(ctx, None, csh1, csc1, pre1, w_in[li], ctx.shape[1], False)

    kcirc = _hyena_filter(L, filt_w1[li], filt_b1[li], filt_w2[li], filt_b2[li], filt_w3[li],
                          filt_b3[li], filt_w4[li], filt_freq[li])
    y_hy = _hyena_conv(zhy, kcirc, conv_w[li], conv_b[li], hyena_bias[li])
    hsum = _mlstm(q, kt, v, gatet, qc, ktc, vc, gatetc, ml_gate_b[li])

    x1, h2t, aff, afft = _out_proj(y_hy, hsum, o, x, tabs, g1, sh2, sc2, ml_norm[li][None, :],
                                   post_norm1[li][None, :], pre_norm2[li][None, :], w_out[li],
                                   w_router[li], IN_PROJ_TILE)

    cap = CAP_FACTOR * L // N_EXPERTS
    idx = _route_select(afft, cap)
    xs, gs = _route_gather(idx, h2t, aff, D)
    ye = _expert_ffn(xs, gs, w_exp_gate[li], w_exp_up[li], w_exp_down[li], 512)
    y2t = _route_combine(idx, ye, L)

    return _final(x1, y2t, g2, post_norm2[li][None, :], TOKEN_TILE)
```

```python
import functools
import math

import jax
import jax.numpy as jnp
from jax import lax
from jax.experimental import pallas as pl
from jax.experimental.pallas import tpu as pltpu

D_MODEL = 1024
GRID_W = 64
HY_WIDTH = 512
ML_HEADS = 4
ML_DK = 64
ML_DV = 128
ML_WIDTH = ML_HEADS * ML_DV
HY_COLS = 3 * HY_WIDTH
QK_COLS = ML_HEADS * ML_DK
N_GATES = 4 * ML_HEADS
FILTER_EMB = 33
DECAY_TARGET = 1e-2
FAST_DECAY_PCT = 0.3
SLOW_DECAY_PCT = 1.5
CHUNK = 128
GATE_CAP = 15.0
N_EXPERTS = 16
CAP_FACTOR = 2
EPS = 1e-6

F32 = jnp.float32
BF16 = jnp.bfloat16

TOKEN_TILE = 512
IN_PROJ_TILE = 1024
VMEM_LIMIT = 56 * 1024 * 1024


def _cparams(sem):
    return pltpu.CompilerParams(dimension_semantics=sem, vmem_limit_bytes=VMEM_LIMIT)


def _rms(xf, g):
    return xf * lax.rsqrt(jnp.mean(xf * xf, axis=-1, keepdims=True) + EPS) * g


def _bdot(a, b):
    return jnp.dot(a.astype(BF16), b.astype(BF16), preferred_element_type=F32)


def _bdot_nt(a, b):
    return lax.dot_general(a.astype(BF16), b.astype(BF16), (((1,), (1,)), ((), ())),
                           preferred_element_type=F32)


def _pe_tables_kernel(omega_ref, er_ref, ec_ref):
    quarter = omega_ref.shape[1]
    om = omega_ref[...]
    for ref in (er_ref, ec_ref):
        n = ref.shape[0]
        pos = lax.broadcasted_iota(jnp.int32, (n, quarter), 0).astype(F32)
        ang = pos * om
        ref[:, :quarter] = jnp.sin(ang)
        ref[:, quarter:] = jnp.cos(ang)


def _pe_tables(rows, cols, dim):
    quarter = dim // 4
    omega = (1.0 / (10000.0 ** (jnp.arange(quarter, dtype=F32) / quarter)))[None, :]
    return pl.pallas_call(
        _pe_tables_kernel,
        out_shape=(jax.ShapeDtypeStruct((rows, dim // 2), F32),
                   jax.ShapeDtypeStruct((cols, dim // 2), F32)),
        name="pe_tables",
    )(omega)


def _pe_tile(er_blk, ec):
    nr, half = er_blk.shape
    row_part = jnp.broadcast_to(er_blk[:, None, :], (nr, GRID_W, half)).reshape(nr * GRID_W, half)
    col_part = jnp.broadcast_to(ec[None, :, :], (nr, GRID_W, half)).reshape(nr * GRID_W, half)
    return jnp.concatenate([row_part, col_part], axis=-1)


def _mod_kernel(c_ref, w_ref, b_ref, o_ref):
    c = c_ref[...]
    s = c * jax.nn.sigmoid(c)
    o_ref[...] = _bdot(s, w_ref[...]) + b_ref[...]


def _modulation(cc, w_mod, b_mod):
    rows, d = cc.shape
    n = w_mod.shape[1]
    tn = 512
    return pl.pallas_call(
        _mod_kernel,
        grid=(n // tn,),
        in_specs=[pl.BlockSpec((rows, d), lambda j: (0, 0)),
                  pl.BlockSpec((d, tn), lambda j: (0, j)),
                  pl.BlockSpec((1, tn), lambda j: (0, j))],
        out_specs=pl.BlockSpec((rows, tn), lambda j: (0, j)),
        out_shape=jax.ShapeDtypeStruct((rows, n), F32),
        compiler_params=_cparams(("arbitrary",)),
        name="modulation",
    )(cc, w_mod, b_mod[None, :])


def _in_proj_kernel(*refs, with_hyena):
    if with_hyena:
        (x_ref, er_ref, ec_ref, sh_ref, sc_ref, g_ref, wn_ref, wt_ref,
         zt_ref, q_ref, kt_ref, v_ref, o_ref, gatet_ref) = refs
        xf = x_ref[0] + _pe_tile(er_ref[...], ec_ref[...])
    else:
        (x_ref, sh_ref, sc_ref, g_ref, wn_ref, wt_ref,
         q_ref, kt_ref, v_ref, o_ref, gatet_ref) = refs
        xf = x_ref[0]
    h = _rms(xf, g_ref[...]) * (1.0 + sc_ref[0]) + sh_ref[0]
    hb = h.astype(BF16)
    z = jnp.dot(hb, wn_ref[...], preferred_element_type=F32)
    qs = z[:, :QK_COLS] * (ML_DK ** -0.5)
    for hd in range(ML_HEADS):
        q_ref[0, hd] = qs[:, hd * ML_DK:(hd + 1) * ML_DK].astype(BF16)
    v_ref[0] = z[:, QK_COLS:QK_COLS + ML_WIDTH].astype(BF16)
    o_ref[0] = z[:, QK_COLS + ML_WIDTH:]
    zt = lax.dot_general(wt_ref[...], hb, (((1,), (1,)), ((), ())), preferred_element_type=F32)
    off = 0
    if with_hyena:
        for j in range(zt.shape[1] // LANES):
            zt_ref[0, :, j, :] = zt[:HY_COLS, j * LANES:(j + 1) * LANES]
        off = HY_COLS
    for hd in range(ML_HEADS):
        kt_ref[0, hd] = zt[off + hd * ML_DK:off + (hd + 1) * ML_DK, :].astype(BF16)
    gatet_ref[0] = zt[off + QK_COLS:, :]


def _in_proj(x, tabs, sh, sc, g, w_in, tm, with_hyena):
    B, L, D = x.shape
    w_hy = w_in[:, :HY_COLS]
    w_q = w_in[:, HY_COLS:HY_COLS + QK_COLS]
    w_k = w_in[:, HY_COLS + QK_COLS:HY_COLS + 2 * QK_COLS]
    w_vo = w_in[:, HY_COLS + 2 * QK_COLS:HY_COLS + 2 * QK_COLS + 2 * ML_WIDTH]
    w_g = w_in[:, HY_COLS + 2 * QK_COLS + 2 * ML_WIDTH:]
    wn = jnp.concatenate([w_q, w_vo], axis=1).astype(BF16)
    wt = jnp.concatenate(([w_hy] if with_hyena else []) + [w_k, w_g], axis=1).T.astype(BF16)
    nb = sh.shape[0]
    mod_map = (lambda b, i: (b, 0, 0)) if nb > 1 else (lambda b, i: (0, 0, 0))
    full = lambda a: pl.BlockSpec(a.shape, lambda b, i: (0,) * a.ndim)
    in_specs = [pl.BlockSpec((1, tm, D), lambda b, i: (b, i, 0))]
    args = [x]
    if with_hyena:
        er, ec = tabs
        in_specs += [pl.BlockSpec((tm // GRID_W, D // 2), lambda b, i: (i, 0)), full(ec)]
        args += [er, ec]
    in_specs += [pl.BlockSpec((1, 1, D), mod_map), pl.BlockSpec((1, 1, D), mod_map),
                 full(g), full(wn), full(wt)]
    args += [sh, sc, g, wn, wt]
    out_shape, out_specs = [], []
    if with_hyena:
        out_shape.append(jax.ShapeDtypeStruct((B, HY_COLS, L // LANES, LANES), F32))
        out_specs.append(pl.BlockSpec((1, HY_COLS, tm // LANES, LANES), lambda b, i: (b, 0, i, 0)))
    out_shape += [jax.ShapeDtypeStruct((B, ML_HEADS, L, ML_DK), BF16),
                  jax.ShapeDtypeStruct((B, ML_HEADS, ML_DK, L), BF16),
                  jax.ShapeDtypeStruct((B, L, ML_WIDTH), BF16),
                  jax.ShapeDtypeStruct((B, L, ML_WIDTH), F32),
                  jax.ShapeDtypeStruct((B, N_GATES, L), F32)]
    out_specs += [pl.BlockSpec((1, ML_HEADS, tm, ML_DK), lambda b, i: (b, 0, i, 0)),
                  pl.BlockSpec((1, ML_HEADS, ML_DK, tm), lambda b, i: (b, 0, 0, i)),
                  pl.BlockSpec((1, tm, ML_WIDTH), lambda b, i: (b, i, 0)),
                  pl.BlockSpec((1, tm, ML_WIDTH), lambda b, i: (b, i, 0)),
                  pl.BlockSpec((1, N_GATES, tm), lambda b, i: (b, 0, i))]
    return pl.pallas_call(
        functools.partial(_in_proj_kernel, with_hyena=with_hyena),
        grid=(B, L // tm),
        in_specs=in_specs,
        out_specs=out_specs,
        out_shape=out_shape,
        compiler_params=_cparams(("parallel", "parallel")),
        name="in_proj_hy" if with_hyena else "in_proj_ctx",
    )(*args)


FILT_TILE = 1024
FILT_CBLK = 128


def _filter_kernel(w1a_ref, w1b_ref, w1c_ref, b1_ref, w2_ref, b2_ref, w3_ref, b3_ref, fr_ref,
                   w4f_ref, w4b_ref, dl_ref, mir_ref, k_ref, hf_ref, *, L):
    bands = (FILTER_EMB - 1) // 2

    @pl.when(pl.program_id(0) == 0)
    def _():
        fk = (1e-4 + lax.broadcasted_iota(jnp.int32, (bands, 1), 0).astype(F32)
              * ((bands - 1 - 1e-4) / (bands - 1)))
        fr = fr_ref[...]
        for j in range(L // FILT_TILE):
            pos = (lax.broadcasted_iota(jnp.int32, (1, FILT_TILE), 1) + j * FILT_TILE).astype(F32)
            tl = pos * (1.0 / (L - 1))
            ang = fk * (pos * (2.0 * math.pi / L))
            pre = (w1a_ref[...].astype(F32) * tl.astype(BF16).astype(F32)
                   + _bdot(w1b_ref[...], jnp.cos(ang)) + _bdot(w1c_ref[...], -jnp.sin(ang)))
            h = jnp.sin(fr * (pre + b1_ref[...]))
            h = jnp.sin(fr * (_bdot(w2_ref[...], h) + b2_ref[...]))
            h = jnp.sin(fr * (_bdot(w3_ref[...], h) + b3_ref[...]))
            hf_ref[:, j * FILT_TILE:(j + 1) * FILT_TILE] = h.astype(BF16)

    pos = lax.broadcasted_iota(jnp.int32, (1, L), 1).astype(F32)
    decay = jnp.exp(-(pos * (1.0 / (L - 1))) * dl_ref[...])
    hf = hf_ref[...]
    k_ref[:, :L] = jnp.dot(w4f_ref[...], hf, preferred_element_type=F32) * decay
    gb = (jnp.dot(w4b_ref[...], hf, preferred_element_type=F32) * decay).astype(BF16)
    nblk = L // LANES
    for j in range(nblk):
        src = gb[:, (nblk - 1 - j) * LANES:(nblk - j) * LANES]
        nxt = gb[:, (nblk - j) * LANES:(nblk - j + 1) * LANES] if j > 0 else jnp.zeros_like(src)
        k_ref[:, L + j * LANES:L + (j + 1) * LANES] = jnp.dot(
            jnp.concatenate([src, nxt], axis=1), mir_ref[...], preferred_element_type=F32)


def _hyena_filter(L, w1, b1, w2, b2, w3, b3, w4, freq):
    hid = w2.shape[0]
    bands = (FILTER_EMB - 1) // 2
    col = lambda a: a[:, None]
    w1t = w1.T.astype(BF16)
    min_decay = math.log(DECAY_TARGET) / SLOW_DECAY_PCT
    max_decay = math.log(DECAY_TARGET) / FAST_DECAY_PCT
    dl = jnp.abs(jnp.linspace(min_decay, max_decay, HY_WIDTH, dtype=F32))[:, None]
    w4t = w4.T.astype(BF16)
    assert DFT_PASSES == 1, "the mirrored taps are kept at the bf16 precision a single-pass DFT reads"
    lane = jnp.arange(LANES)
    mir = jnp.concatenate([(lane[:, None] + lane[None, :] == LANES),
                           (lane[:, None] == 0) & (lane[None, :] == 0)], axis=0).astype(BF16)
    full = lambda a: pl.BlockSpec(a.shape, lambda i: (0,) * a.ndim)
    args = [w1t[:, 0:1], w1t[:, 1:1 + bands], w1t[:, 1 + bands:], col(b1), w2.T.astype(BF16), col(b2),
            w3.T.astype(BF16), col(b3), col(freq)]
    return pl.pallas_call(
        functools.partial(_filter_kernel, L=L),
        grid=(HY_WIDTH // FILT_CBLK,),
        in_specs=[full(a) for a in args] + [
            pl.BlockSpec((FILT_CBLK, hid), lambda i: (i, 0)),
            pl.BlockSpec((FILT_CBLK, hid), lambda i: (HY_WIDTH // FILT_CBLK + i, 0)),
            pl.BlockSpec((FILT_CBLK, 1), lambda i: (i, 0)), full(mir)],
        out_specs=pl.BlockSpec((FILT_CBLK, 2 * L), lambda i: (i, 0)),
        out_shape=jax.ShapeDtypeStruct((HY_WIDTH, 2 * L), F32),
        scratch_shapes=[pltpu.VMEM((hid, L), BF16)],
        compiler_params=_cparams(("arbitrary",)),
        name="hyena_filter",
    )(*args, w4t, w4t, dl, mir)


FFT_N = 128
HY_CBLK = 32
HY_GROUP = 8
HY_UNROLL = 4
DFT_PASSES = 1


def _dft_constants(n1_data):
    import numpy as np
    n = FFT_N
    k = np.arange(n)
    ang = -2.0 * np.pi * ((k[:, None] * k[None, :]) % n) / n
    fre, fim = np.cos(ang), np.sin(ang)
    m = n1_data
    fa_d = np.block([[fre[:, :m], -fim[:, :m]], [fim[:, :m], fre[:, :m]]])
    fa_f = np.concatenate([fre, fim], axis=0)
    fb = np.block([[fre, fim], [-fim, fre]])
    fbi = np.block([[fre, -fim], [fim, fre]])
    fc = np.block([[fre[:m, :], fim[:m, :]], [-fim[:m, :], fre[:m, :]]]) / (n * n)
    tang = -2.0 * np.pi * (k[:, None] * k[None, :]) / (n * n)
    tw = np.stack([np.cos(tang), np.sin(tang)])

    def hilo(a):
        a32 = jnp.asarray(a, F32)
        hi = a32.astype(BF16)
        lo = (a32 - hi.astype(F32)).astype(BF16)
        return jnp.stack([hi, lo])

    return hilo(fa_d), hilo(fa_f), hilo(fb), hilo(fbi), hilo(fc), jnp.asarray(tw, F32)


def _mm_const_lhs(c_ref, d):
    dh = d.astype(BF16)
    acc = jnp.dot(c_ref[0], dh, preferred_element_type=F32)
    if DFT_PASSES == 3:
        dl = (d - dh.astype(F32)).astype(BF16)
        acc = acc + (jnp.dot(c_ref[0], dl, preferred_element_type=F32)
                     + jnp.dot(c_ref[1], dh, preferred_element_type=F32))
    return acc


def _mm_const_rhs(d, c_ref):
    dh = d.astype(BF16)
    acc = jnp.dot(dh, c_ref[0], preferred_element_type=F32)
    if DFT_PASSES == 3:
        dl = (d - dh.astype(F32)).astype(BF16)
        acc = acc + (jnp.dot(dl, c_ref[0], preferred_element_type=F32)
                     + jnp.dot(dh, c_ref[1], preferred_element_type=F32))
    return acc


def _cmul(are, aim, bre, bim):
    return are * bre - aim * bim, are * bim + aim * bre


def _hyena_conv_kernel(x0_ref, x1_ref, v_ref, kc_ref, w0_ref, w1_ref, wv_ref, b0_ref, b1_ref, bv_ref,
                       hb_ref, fad_ref, faf_ref, fb_ref, fbi_ref, fc_ref, tw_ref,
                       o_ref, u_ref, s_ref, ks_ref):
    n = FFT_N
    cb = kc_ref.shape[0]
    m = x0_ref.shape[2]
    nb = x0_ref.shape[0]
    sub = lax.broadcasted_iota(jnp.int32, (m, n), 0)
    lane = lax.broadcasted_iota(jnp.int32, (m, n), 1)
    tre, tim = tw_ref[0], tw_ref[1]

    def sconv(z, w_ref, b_ref, c):
        a = pltpu.roll(z, 1, axis=1)
        prev = jnp.where(lane == 0, jnp.where(sub == 0, 0.0, pltpu.roll(a, 1, axis=0)), a)
        a2 = pltpu.roll(z, n - 1, axis=1)
        nxt = jnp.where(lane == n - 1, jnp.where(sub == m - 1, 0.0, pltpu.roll(a2, m - 1, axis=0)), a2)
        return prev * w_ref[0, c] + z * w_ref[1, c] + nxt * w_ref[2, c] + b_ref[c]

    def spectrum_rows(res):
        outs = []
        for h in range(2):
            are, aim = _cmul(res[:n, h * n:(h + 1) * n], res[n:, h * n:(h + 1) * n], tre, tim)
            outs.append(jnp.concatenate([are, aim], axis=1))
        return outs

    def fwd_pair(p, carry):
        c0 = 2 * p
        us = []
        for c in (c0, c0 + 1):
            ub = []
            for b in range(nb):
                x1c = sconv(x1_ref[b, c], w1_ref, b1_ref, c)
                vc = sconv(v_ref[b, c], wv_ref, bv_ref, c)
                u = x1c * vc
                u_ref[b, c] = u
                ub.append(u)
            us.append(ub)
        wd = jnp.concatenate([jnp.concatenate([us[0][b], us[1][b]], axis=1) for b in range(nb)], axis=0)
        sa, sb = spectrum_rows(_mm_const_lhs(fad_ref, wd))
        s_ref[c0] = sa
        s_ref[c0 + 1] = sb
        wk = jnp.concatenate([kc_ref[c0], kc_ref[c0 + 1]], axis=1)
        ka, kb = spectrum_rows(_mm_const_lhs(faf_ref, wk))
        ks_ref[c0] = ka
        ks_ref[c0 + 1] = kb
        return carry

    lax.fori_loop(0, cb // 2, fwd_pair, 0, unroll=HY_UNROLL)

    def mid_group(g, carry):
        gs = pl.ds(pl.multiple_of(g * HY_GROUP, HY_GROUP), HY_GROUP)
        x = _mm_const_rhs(s_ref[gs].reshape(HY_GROUP * n, 2 * n), fb_ref)
        k = _mm_const_rhs(ks_ref[gs].reshape(HY_GROUP * n, 2 * n), fb_ref)
        yre, yim = _cmul(x[:, :n], x[:, n:], k[:, :n], k[:, n:])
        vv = _mm_const_rhs(jnp.concatenate([yre, yim], axis=1), fbi_ref).reshape(HY_GROUP, n, 2 * n)
        vre, vim = _cmul(vv[:, :, :n], vv[:, :, n:], tre[None], -tim[None])
        s_ref[gs] = jnp.concatenate([vre, vim], axis=2)
        return carry

    lax.fori_loop(0, cb // HY_GROUP, mid_group, 0, unroll=HY_UNROLL)

    def inv_pair(p, carry):
        c0 = 2 * p
        sa, sb = s_ref[c0], s_ref[c0 + 1]
        wd = jnp.concatenate([jnp.concatenate([sa[:, :n], sb[:, :n]], axis=1),
                              jnp.concatenate([sa[:, n:], sb[:, n:]], axis=1)], axis=0)
        res = _mm_const_lhs(fc_ref, wd)
        for h, c in enumerate((c0, c0 + 1)):
            for b in range(nb):
                y = res[b * m:(b + 1) * m, h * n:(h + 1) * n]
                x0c = sconv(x0_ref[b, c], w0_ref, b0_ref, c)
                o_ref[b, c] = x0c * (y + hb_ref[c] * u_ref[b, c])
        return carry

    lax.fori_loop(0, cb // 2, inv_pair, 0, unroll=HY_UNROLL)


def _hyena_conv(zt, kcirc, conv_w, conv_b, hy_bias):
    B, _, m, n = zt.shape
    L = m * n
    C = HY_WIDTH
    assert B == 2 and n == FFT_N and 2 * L == n * n, "complex packing of two samples over a 128 x 128 point transform"
    z4 = zt
    k3 = kcirc.reshape(C, n, n)
    cw = conv_w.reshape(3, 3 * C, 1, 1)
    cbias = conv_b.reshape(3 * C, 1, 1)
    hb = hy_bias.reshape(C, 1, 1)
    consts = _dft_constants(m)
    nblk = C // HY_CBLK
    zspec = lambda part: pl.BlockSpec((B, HY_CBLK, m, n), lambda i: (0, part * nblk + i, 0, 0))
    wspec = lambda part: pl.BlockSpec((3, HY_CBLK, 1, 1), lambda i: (0, part * nblk + i, 0, 0))
    bspec = lambda part: pl.BlockSpec((HY_CBLK, 1, 1), lambda i: (part * nblk + i, 0, 0))
    full = lambda a: pl.BlockSpec(a.shape, lambda i: (0,) * a.ndim)
    y = pl.pallas_call(
        _hyena_conv_kernel,
        grid=(nblk,),
        in_specs=[zspec(0), zspec(1), zspec(2), pl.BlockSpec((HY_CBLK, n, n), lambda i: (i, 0, 0)),
                  wspec(0), wspec(1), wspec(2), bspec(0), bspec(1), bspec(2), bspec(0)]
                 + [full(a) for a in consts],
        out_specs=pl.BlockSpec((B, HY_CBLK, m, n), lambda i: (0, i, 0, 0)),
        out_shape=jax.ShapeDtypeStruct((B, C, m, n), F32),
        scratch_shapes=[pltpu.VMEM((B, HY_CBLK, m, n), F32), pltpu.VMEM((HY_CBLK, n, 2 * n), F32),
                        pltpu.VMEM((HY_CBLK, n, 2 * n), F32)],
        compiler_params=_cparams(("parallel",)),
        name="hyena_conv",
    )(z4, z4, z4, k3, cw, cw, cw, cbias, cbias, cbias, hb, *consts)
    return y


def _split3(a):
    hi = a.astype(BF16)
    r1 = a - hi.astype(F32)
    mid = r1.astype(BF16)
    lo = (r1 - mid.astype(F32)).astype(BF16)
    return hi, mid, lo


def _exact_dot_right(a, tri_bf):
    hi, mid, lo = _split3(a)
    d = lambda p: jnp.dot(p, tri_bf, preferred_element_type=F32)
    return (d(lo) + d(mid)) + d(hi)


def _soft_gates(g):
    g = GATE_CAP * jnp.tanh(g * (1.0 / GATE_CAP))
    logsig = jnp.minimum(g, 0.0) - jnp.log1p(jnp.exp(-jnp.abs(g)))
    return g, logsig


def _gate_prep_kernel(gt_ref, gbt_ref, rows_ref, cols_ref):
    T = CHUNK
    n = gt_ref.shape[2]
    up_bf = (lax.broadcasted_iota(jnp.int32, (T, T), 1) >= lax.broadcasted_iota(jnp.int32, (T, T), 0)).astype(BF16)
    out8 = lax.broadcasted_iota(jnp.int32, (8, 1), 0)
    cap, ls = _soft_gates(gt_ref[0] + gbt_ref[...])
    H = ML_HEADS
    for hd in range(H):
        base = jnp.where(out8 == 0, cap[hd:hd + 1], jnp.where(out8 == 1, ls[H + hd:H + hd + 1],
                         jnp.where(out8 == 2, cap[2 * H + hd:2 * H + hd + 1],
                                   jnp.where(out8 == 3, ls[3 * H + hd:3 * H + hd + 1], 0.0))))
        for j in range(n // T):
            blk = base[:, j * T:(j + 1) * T]
            run = _exact_dot_right(blk, up_bf)
            suf = run[:, T - 1:T] - run + blk
            rows = jnp.where(out8 == 1, run, jnp.where(out8 == 3, suf, blk))
            rows_ref[0, hd, :, j * T:(j + 1) * T] = rows
            cols_ref[0, hd, j * T:(j + 1) * T, :] = rows.T


def _gate_prep(gatet, gate_b, tile):
    B, G, L = gatet.shape
    H = ML_HEADS
    return pl.pallas_call(
        _gate_prep_kernel,
        grid=(B, L // tile),
        in_specs=[pl.BlockSpec((1, G, tile), lambda b, i: (b, 0, i)),
                  pl.BlockSpec((G, 1), lambda b, i: (0, 0))],
        out_specs=[pl.BlockSpec((1, H, 8, tile), lambda b, i: (b, 0, 0, i)),
                   pl.BlockSpec((1, H, tile, 8), lambda b, i: (b, 0, i, 0))],
        out_shape=[jax.ShapeDtypeStruct((B, H, 8, L), F32), jax.ShapeDtypeStruct((B, H, L, 8), F32)],
        compiler_params=_cparams(("parallel", "parallel")),
        name="mlstm_gate_prep",
    )(gatet, gate_b[:, None])


def _mlstm_kernel(q_ref, kt_ref, v_ref, r_ref, bt_ref, qc_ref, ktc_ref, vc_ref, rc_ref, btc_ref,
                  h_ref, cf_ref, cb_ref):
    T = CHUNK
    L = q_ref.shape[2]
    Lc = qc_ref.shape[2]
    nc, ncc = L // T, Lc // T
    row = lax.broadcasted_iota(jnp.int32, (T, T), 0)
    col = lax.broadcasted_iota(jnp.int32, (T, T), 1)
    lo_mask = col <= row
    up_mask = col >= row
    ones_blk = jnp.ones((T, ML_DV), BF16)

    cf_ref[...] = jnp.zeros_like(cf_ref)
    cb_ref[...] = jnp.zeros_like(cb_ref)

    def chunk_step(q, kt, v, rows, cols, c_ref, backward):
        i_r = rows[2:3] if backward else rows[0:1]
        b_r = rows[3:4] if backward else rows[1:2]
        b_c = jnp.broadcast_to(cols[:, 3:4] if backward else cols[:, 1:2], (T, T))
        b_end = b_r[:, 0:1] if backward else b_r[:, T - 1:T]
        mask = up_mask if backward else lo_mask
        w_intra = jnp.exp(jnp.where(mask, b_c - b_r + i_r, -jnp.inf) - GATE_CAP)
        s = jnp.dot(q, kt, preferred_element_type=F32) * w_intra
        qe = (q.astype(F32) * jnp.exp(b_c[:, :ML_DK])).astype(BF16)
        v_aug = jnp.concatenate([v, ones_blk], axis=1)
        c_aug = c_ref[...]
        res = jnp.dot(jnp.concatenate([s.astype(BF16), qe], axis=1),
                      jnp.concatenate([v_aug, c_aug.astype(BF16)], axis=0),
                      preferred_element_type=F32)
        h = res[:, :ML_DV] / jnp.maximum(jnp.abs(res[:, ML_DV:]), math.exp(-GATE_CAP))
        kw = (kt.astype(F32) * jnp.exp(b_end - b_r + i_r - GATE_CAP)).astype(BF16)
        c_ref[...] = jnp.exp(b_end) * c_aug + jnp.dot(kw, v_aug, preferred_element_type=F32)
        return h

    for j in range(ncc):
        for backward in (False, True):
            jj = (ncc - 1 - j) if backward else j
            cs = slice(jj * T, (jj + 1) * T)
            chunk_step(qc_ref[0, 0, cs, :], ktc_ref[0, 0, :, cs], vc_ref[0, cs, :], rc_ref[0, 0, :, cs],
                       btc_ref[0, 0, cs, :], cb_ref if backward else cf_ref, backward)

    def latent_pair(j, accumulate):
        for backward in (False, True):
            jj = (nc - 1 - j) if backward else j
            rs = pl.ds(pl.multiple_of(jj * T, T), T)
            h = chunk_step(q_ref[0, 0, rs, :], kt_ref[0, 0, :, rs], v_ref[0, rs, :], r_ref[0, 0, :, rs],
                           bt_ref[0, 0, rs, :], cb_ref if backward else cf_ref, backward)
            if accumulate:
                h_ref[0, rs, :] = h_ref[0, rs, :] + h
            else:
                h_ref[0, rs, :] = h

    def first_half(j, carry):
        latent_pair(j, False)
        return carry

    def second_half(j, carry):
        latent_pair(j, True)
        return carry

    lax.fori_loop(0, nc // 2, first_half, 0, unroll=MLSTM_UNROLL)
    lax.fori_loop(nc // 2, nc, second_half, 0, unroll=MLSTM_UNROLL)


MLSTM_UNROLL = 4


def _mlstm(q, kt, v, gatet, qc, ktc, vc, gatetc, gate_b):
    B, H, L, dk = q.shape
    Lc = qc.shape[2]
    rows, cols = _gate_prep(gatet, gate_b, min(L, 1024))
    rows_c, cols_c = _gate_prep(gatetc, gate_b, Lc)
    seq = lambda n: [pl.BlockSpec((1, 1, n, dk), lambda b, h: (b, h, 0, 0)),
                     pl.BlockSpec((1, 1, dk, n), lambda b, h: (b, h, 0, 0)),
                     pl.BlockSpec((1, n, ML_DV), lambda b, h: (b, 0, h)),
                     pl.BlockSpec((1, 1, 8, n), lambda b, h: (b, h, 0, 0)),
                     pl.BlockSpec((1, 1, n, 8), lambda b, h: (b, h, 0, 0))]
    return pl.pallas_call(
        _mlstm_kernel,
        grid=(B, H),
        in_specs=seq(L) + seq(Lc),
        out_specs=pl.BlockSpec((1, L, ML_DV), lambda b, h: (b, 0, h)),
        out_shape=jax.ShapeDtypeStruct((B, L, ML_WIDTH), F32),
        scratch_shapes=[pltpu.VMEM((dk, 2 * ML_DV), F32), pltpu.VMEM((dk, 2 * ML_DV), F32)],
        compiler_params=_cparams(("parallel", "parallel")),
        name="mlstm_scan",
    )(q, kt, v, rows, cols, qc, ktc, vc, rows_c, cols_c)


def _out_proj_kernel(yhy_ref, hs_ref, o_ref, x_ref, er_ref, ec_ref, g1_ref, sh2_ref, sc2_ref,
                     mln_ref, post1_ref, pre2_ref, wout_ref, wr_ref, wrt_ref,
                     x1_ref, h2_ref, aff_ref, afft_ref):
    hs = hs_ref[0]
    parts = []
    for hd in range(ML_HEADS):
        hh = hs[:, hd * ML_DV:(hd + 1) * ML_DV]
        parts.append(hh * lax.rsqrt(jnp.mean(hh * hh, axis=-1, keepdims=True) + EPS))
    hn = jnp.concatenate(parts, axis=-1) * mln_ref[...]
    y_ml = hn * jax.nn.sigmoid(o_ref[0])
    yo_hy = [lax.dot_general(yhy_ref[0, :, j, :].astype(BF16), wout_ref[:HY_WIDTH, :], (((0,), (0,)), ((), ())),
                             preferred_element_type=F32) for j in range(yhy_ref.shape[2])]
    yo = (jnp.concatenate(yo_hy, axis=0)
          + jnp.dot(y_ml.astype(BF16), wout_ref[HY_WIDTH:, :], preferred_element_type=F32))
    xf = x_ref[0] + _pe_tile(er_ref[...], ec_ref[...])
    x1 = xf + g1_ref[0] * _rms(yo, post1_ref[...])
    x1_ref[0] = x1
    h2f = _rms(x1, pre2_ref[...]) * (1.0 + sc2_ref[0]) + sh2_ref[0]
    _store_row_tiles(h2_ref, h2f)
    h2 = h2f.astype(BF16)
    logits = jnp.dot(h2, wr_ref[...], preferred_element_type=F32)
    ex = jnp.exp(logits - jnp.max(logits, axis=-1, keepdims=True))
    aff_ref[0] = ex / jnp.sum(ex, axis=-1, keepdims=True)
    logits_t = lax.dot_general(wrt_ref[...], h2, (((1,), (1,)), ((), ())), preferred_element_type=F32)
    ext = jnp.exp(logits_t - jnp.max(logits_t, axis=0, keepdims=True))
    afft_ref[0] = ext / jnp.sum(ext, axis=0, keepdims=True)


def _out_proj(y_hy, hsum, o, x, tabs, g1, sh2, sc2, ml_norm, post1, pre2, w_out, w_router, tm):
    B, L, D = x.shape
    er, ec = tabs
    E = w_router.shape[1]
    full = lambda a: pl.BlockSpec(a.shape, lambda b, i: (0,) * a.ndim)
    tok = lambda w: pl.BlockSpec((1, tm, w), lambda b, i: (b, i, 0))
    modspec = pl.BlockSpec((1, 1, D), lambda b, i: (b, 0, 0))
    wout = w_out.astype(BF16)
    wr = w_router.astype(BF16)
    wrt = w_router.T.astype(BF16)
    return pl.pallas_call(
        _out_proj_kernel,
        grid=(B, L // tm),
        in_specs=[pl.BlockSpec((1, HY_WIDTH, tm // LANES, LANES), lambda b, i: (b, 0, i, 0)),
                  tok(ML_WIDTH), tok(ML_WIDTH), tok(D),
                  pl.BlockSpec((tm // GRID_W, D // 2), lambda b, i: (i, 0)), full(ec),
                  modspec, modspec, modspec, full(ml_norm), full(post1), full(pre2),
                  full(wout), full(wr), full(wrt)],
        out_specs=[tok(D), pl.BlockSpec((1, tm * (D // LANES), LANES), lambda b, i: (b, i, 0)), tok(E),
                   pl.BlockSpec((1, E, tm), lambda b, i: (b, 0, i))],
        out_shape=[jax.ShapeDtypeStruct((B, L, D), F32),
                   jax.ShapeDtypeStruct((B, L * (D // LANES), LANES), F32),
                   jax.ShapeDtypeStruct((B, L, E), F32), jax.ShapeDtypeStruct((B, E, L), F32)],
        compiler_params=_cparams(("parallel", "parallel")),
        name="out_proj_router",
    )(y_hy, hsum, o, x, er, ec, g1, sh2, sc2, ml_norm, post1, pre2, wout, wr, wrt)


LANES = 128
ROW_GROUP = 16
SELECT_FAST_SLOTS = 32


def _store_row_tiles(ref, val):
    n, nt = val.shape[0], val.shape[1] // LANES
    for c in range(nt):
        ref[0, pl.ds(c, n, stride=nt), :] = val[:, c * LANES:(c + 1) * LANES]


def _load_row_tiles(ref, c, n, nt):
    return ref[0, pl.ds(c, n, stride=nt), :]


def _select_kernel(afft_ref, idx_ref, sel_ref, *, cap):
    E, L = afft_ref.shape[1], afft_ref.shape[2]
    idx_ref[...] = jnp.zeros_like(idx_ref)
    aff = afft_ref[0]
    iota = lax.broadcasted_iota(jnp.int32, (E, L), 1)
    count = lambda ind: jnp.sum(ind, axis=1, keepdims=True)
    count_ge = lambda th: count(jnp.where(aff >= th, 1.0, 0.0))
    pow2 = lambda j: pltpu.bitcast((j - 24) << 23, F32)

    def estep(_, c):
        lo, hi = c
        mid = (lo + hi) >> 1
        ok = count_ge(pow2(mid)) >= cap
        return jnp.where(ok, mid, lo), jnp.where(ok, hi, mid)

    jlo, jhi = lax.fori_loop(0, 7, estep, (jnp.full((E, 1), 24, jnp.int32),
                                           jnp.full((E, 1), 152, jnp.int32)))

    def vstep(_, c):
        lo, hi = c
        mid = lo + (hi - lo) * 0.5
        ok = count_ge(mid) >= cap
        return jnp.where(ok, mid, lo), jnp.where(ok, hi, mid)

    lo, hi = lax.fori_loop(0, 40, vstep, (pow2(jlo), pow2(jhi)))
    gt = jnp.where(aff >= hi, 1.0, 0.0)
    eq = jnp.where(aff >= lo, 1.0, 0.0) - gt
    need = cap - count(gt)

    def istep(_, c):
        lo, hi = c
        mid = (lo + hi) >> 1
        ok = count(jnp.where(iota <= mid, eq, 0.0)) >= need
        return jnp.where(ok, lo, mid), jnp.where(ok, mid, hi)

    _, last = lax.fori_loop(0, L.bit_length() - 1, istep,
                            (jnp.full((E, 1), -1, jnp.int32), jnp.full((E, 1), L - 1, jnp.int32)))
    sel_ref[...] = gt + jnp.where(iota <= last, eq, 0.0)

    T = LANES
    r_i = lax.broadcasted_iota(jnp.int32, (T, T), 0)
    c_i = lax.broadcasted_iota(jnp.int32, (T, T), 1)
    before = (r_i < c_i).astype(BF16)
    slot_f = r_i.astype(F32)
    lane_f = lax.broadcasted_iota(jnp.int32, (1, T), 1).astype(F32)
    esel = lax.broadcasted_iota(jnp.int32, (E, 1), 0)

    def group(g, off):
        s = sel_ref[:, pl.ds(pl.multiple_of(g * T, T), T)]
        rank = jnp.dot(s.astype(BF16), before, preferred_element_type=F32)
        tok = (lane_f + jnp.asarray(g * T, F32)) * s
        cnt = jnp.sum(s, axis=1, keepdims=True)

        def emit(n_slots):
            for e in range(E):
                hit = rank[e:e + 1, :] == slot_f[:n_slots]
                ids = jnp.sum(jnp.where(hit, tok[e:e + 1, :], 0.0), axis=1, keepdims=True)
                o = jnp.sum(jnp.where(esel == e, off, 0))
                idx_ref[0, e, pl.ds(o, n_slots), :] = ids.astype(jnp.int32)

        few = jnp.max(cnt) <= SELECT_FAST_SLOTS
        pl.when(few)(lambda: emit(SELECT_FAST_SLOTS))
        pl.when(jnp.logical_not(few))(lambda: emit(T))
        return off + cnt.astype(jnp.int32)

    lax.fori_loop(0, L // T, group, jnp.zeros((E, 1), jnp.int32))


def _route_select(afft, cap):
    B, E, L = afft.shape
    idx = pl.pallas_call(
        functools.partial(_select_kernel, cap=cap),
        grid=(B,),
        in_specs=[pl.BlockSpec((1, E, L), lambda b: (b, 0, 0))],
        out_specs=pl.BlockSpec((1, E, cap + LANES, 1), lambda b: (b, 0, 0, 0)),
        out_shape=jax.ShapeDtypeStruct((B, E, cap + LANES, 1), jnp.int32),
        scratch_shapes=[pltpu.VMEM((E, L), F32)],
        compiler_params=_cparams(("parallel",)),
        name="route_select",
    )(afft)
    return idx[:, :, :cap, 0]


def _tile_rows(i, nt):
    return pl.ds(pl.multiple_of(i * nt, nt), nt)


def _gather_kernel(idx_ref, h_ref, aff_ref, xs_ref, gs_ref, buf_ref):
    cap, D = xs_ref.shape[1], xs_ref.shape[2]
    nt = D // LANES

    def body(s, c):
        base = s * ROW_GROUP
        rows = [idx_ref[0, 0, base + k] for k in range(ROW_GROUP)]
        vals = [h_ref[0, _tile_rows(r, nt), :] for r in rows]
        gates = [aff_ref[0, pl.ds(r, 1), :] for r in rows]
        for k in range(ROW_GROUP):
            buf_ref[_tile_rows(base + k, nt), :] = vals[k]
            gs_ref[0, pl.ds(base + k, 1), :] = gates[k]
        return c

    lax.fori_loop(0, cap // ROW_GROUP, body, 0)
    for c in range(nt):
        xs_ref[0, :, c * LANES:(c + 1) * LANES] = buf_ref[pl.ds(c, cap, stride=nt), :].astype(BF16)


def _route_gather(idx, h2t, aff, D):
    B, E, cap = idx.shape
    nt = D // LANES
    L = h2t.shape[1] // nt
    return pl.pallas_call(
        _gather_kernel,
        grid=(B, E),
        in_specs=[pl.BlockSpec((1, 1, cap), lambda b, e: (b * E + e, 0, 0), memory_space=pltpu.SMEM),
                  pl.BlockSpec((1, L * nt, LANES), lambda b, e: (b, 0, 0), pipeline_mode=pl.Buffered(1)),
                  pl.BlockSpec((1, L, E), lambda b, e: (b, 0, 0))],
        out_specs=[pl.BlockSpec((1, cap, D), lambda b, e: (e, b, 0)),
                   pl.BlockSpec((1, cap, E), lambda b, e: (e, b, 0))],
        out_shape=[jax.ShapeDtypeStruct((E, B * cap, D), BF16),
                   jax.ShapeDtypeStruct((E, B * cap, E), F32)],
        scratch_shapes=[pltpu.VMEM((cap * nt, LANES), F32)],
        compiler_params=_cparams(("arbitrary", "arbitrary")),
        name="route_gather",
    )(idx.reshape(B * E, 1, cap), h2t, aff)


def _combine_kernel(idx_ref, ye_ref, y_ref, buf_ref):
    cap, D = ye_ref.shape[1], ye_ref.shape[2]
    nt = D // LANES

    @pl.when(pl.program_id(1) == 0)
    def _():
        y_ref[...] = jnp.zeros_like(y_ref)

    for c in range(nt):
        buf_ref[pl.ds(c, cap, stride=nt), :] = ye_ref[0, :, c * LANES:(c + 1) * LANES]

    def body(s, c):
        base = s * ROW_GROUP
        rows = [idx_ref[0, 0, base + k] for k in range(ROW_GROUP)]
        vals = [y_ref[0, _tile_rows(rows[k], nt), :] + buf_ref[_tile_rows(base + k, nt), :]
                for k in range(ROW_GROUP)]
        for k in range(ROW_GROUP):
            y_ref[0, _tile_rows(rows[k], nt), :] = vals[k]
        return c

    lax.fori_loop(0, cap // ROW_GROUP, body, 0)


def _route_combine(idx, ye, L):
    B, E, cap = idx.shape
    D = ye.shape[2]
    nt = D // LANES
    return pl.pallas_call(
        _combine_kernel,
        grid=(B, E),
        in_specs=[pl.BlockSpec((1, 1, cap), lambda b, e: (b * E + e, 0, 0), memory_space=pltpu.SMEM),
                  pl.BlockSpec((1, cap, D), lambda b, e: (e, b, 0))],
        out_specs=pl.BlockSpec((1, L * nt, LANES), lambda b, e: (b, 0, 0), pipeline_mode=pl.Buffered(1)),
        out_shape=jax.ShapeDtypeStruct((B, L * nt, LANES), F32),
        scratch_shapes=[pltpu.VMEM((cap * nt, LANES), F32)],
        compiler_params=_cparams(("arbitrary", "arbitrary")),
        name="route_combine",
    )(idx.reshape(B * E, 1, cap), ye)


def _expert_kernel(xs_ref, g_ref, wg_ref, wu_ref, wd_ref, ye_ref):
    e = pl.program_id(0)
    f = pl.program_id(1)
    nf = pl.num_programs(1)
    wg = wg_ref[0].astype(BF16)
    wu = wu_ref[0].astype(BF16)
    wd = wd_ref[0].astype(BF16)
    mt = xs_ref.shape[1] // FFN_M_SPLIT
    for mi in range(FFN_M_SPLIT):
        rs = slice(mi * mt, (mi + 1) * mt)
        xs = xs_ref[0, rs, :]
        a = jnp.dot(xs, wg, preferred_element_type=F32)
        u = jnp.dot(xs, wu, preferred_element_type=F32)
        hmid = (a * jax.nn.sigmoid(a) * u).astype(BF16)
        part = jnp.dot(hmid, wd, preferred_element_type=F32)

        @pl.when(f == 0)
        def _():
            ye_ref[0, rs, :] = part

        @pl.when(f > 0)
        def _():
            ye_ref[0, rs, :] = ye_ref[0, rs, :] + part

    @pl.when(f == nf - 1)
    def _():
        gs = g_ref[0]
        esel = lax.broadcasted_iota(jnp.int32, gs.shape, 1)
        ye_ref[0] = ye_ref[0] * jnp.sum(jnp.where(esel == e, gs, 0.0), axis=1, keepdims=True)


FFN_M_SPLIT = 2


def _expert_ffn(xs, gs, w_gate, w_up, w_down, tf):
    E, M, D = xs.shape
    F = w_gate.shape[2]
    return pl.pallas_call(
        _expert_kernel,
        grid=(E, F // tf),
        in_specs=[pl.BlockSpec((1, M, D), lambda e, f: (e, 0, 0)),
                  pl.BlockSpec((1, M, E), lambda e, f: (e, 0, 0)),
                  pl.BlockSpec((1, D, tf), lambda e, f: (e, 0, f)),
                  pl.BlockSpec((1, D, tf), lambda e, f: (e, 0, f)),
                  pl.BlockSpec((1, tf, D), lambda e, f: (e, f, 0))],
        out_specs=pl.BlockSpec((1, M, D), lambda e, f: (e, 0, 0)),
        out_shape=jax.ShapeDtypeStruct((E, M, D), F32),
        compiler_params=_cparams(("parallel", "arbitrary")),
        name="expert_ffn",
    )(xs, gs, w_gate, w_up, w_down)


def _final_kernel(x1_ref, y2_ref, g2_ref, post2_ref, o_ref):
    n = x1_ref.shape[1]
    nt = x1_ref.shape[2] // LANES
    cols = [_load_row_tiles(y2_ref, c, n, nt) for c in range(nt)]
    ssq = sum(jnp.sum(y * y, axis=-1, keepdims=True) for y in cols)
    rstd = lax.rsqrt(ssq * (1.0 / (nt * LANES)) + EPS)
    for c in range(nt):
        cs = slice(c * LANES, (c + 1) * LANES)
        o_ref[0, :, cs] = x1_ref[0, :, cs] + g2_ref[0, :, cs] * (cols[c] * rstd * post2_ref[:, cs])


def _final(x1, y2t, g2, post2, tm):
    B, L, D = x1.shape
    tok = pl.BlockSpec((1, tm, D), lambda b, i: (b, i, 0))
    return pl.pallas_call(
        _final_kernel,
        grid=(B, L // tm),
        in_specs=[tok, pl.BlockSpec((1, tm * (D // LANES), LANES), lambda b, i: (b, i, 0)),
                  pl.BlockSpec((1, 1, D), lambda b, i: (b, 0, 0)),
                  pl.BlockSpec((1, D), lambda b, i: (0, 0))],
        out_specs=tok,
        out_shape=jax.ShapeDtypeStruct((B, L, D), F32),
        compiler_params=_cparams(("parallel", "parallel")),
        name="final_residual",
    )(x1, y2t, g2, post2)


def kernel(x, c, ctx, c_ctx, w_mod, b_mod, pre_norm1, post_norm1, pre_norm2, post_norm2, w_in, conv_w, conv_b, filt_w1, filt_b1, filt_w2, filt_b2, filt_w3, filt_b3, filt_w4, filt_freq, hyena_bias, ml_gate_b, ml_norm, w_out, w_router, w_exp_gate, w_exp_up, w_exp_down):
    B, L, D = x.shape
    depth = w_mod.shape[0]
    assert depth == 1, "single-layer block"
    li = 0
    tabs = _pe_tables(L // GRID_W, GRID_W, D)

    cc = jnp.concatenate([c, c_ctx[None], jnp.zeros((8 - B - 1, D), F32)], axis=0)
    mod = _modulation(cc, w_mod[li], b_mod[li])
    chunks = [mod[:, k * D:(k + 1) * D] for k in range(6)]
    sh1, sc1, g1, sh2, sc2, g2 = [m[:B, None, :] for m in chunks]
    csh1, csc1 = chunks[0][B:B + 1, None, :], chunks[1][B:B + 1, None, :]

    pre1 = pre_norm1[li][None, :]
    zhy, q, kt, v, o, gatet = _in_proj(x, tabs, sh1, sc1, pre1, w_in[li], IN_PROJ_TILE, True)
    qc, ktc, vc, _, gatetc = _in_proj(ctx, None, csh1, csc1, pre1, w_in[li], ctx.shape[1], False)

    kcirc = _hyena_filter(L, filt_w1[li], filt_b1[li], filt_w2[li], filt_b2[li], filt_w3[li],
                          filt_b3[li], filt_w4[li], filt_freq[li])
    y_hy = _hyena_conv(zhy, kcirc, conv_w[li], conv_b[li], hyena_bias[li])
    hsum = _mlstm(q, kt, v, gatet, qc, ktc, vc, gatetc, ml_gate_b[li])

    x1, h2t, aff, afft = _out_proj(y_hy, hsum, o, x, tabs, g1, sh2, sc2, ml_norm[li][None, :],
                                   post_norm1[li][None, :], pre_norm2[li][None, :], w_out[li],
                                   w_router[li], IN_PROJ_TILE)

    cap = CAP_FACTOR * L // N_EXPERTS
    idx = _route_select(afft, cap)
    xs, gs = _route_gather(idx, h2t, aff, D)
    ye = _expert_ffn(xs, gs, w_exp_gate[li], w_exp_up[li], w_exp_down[li], 512)
    y2t = _route_combine(idx, ye, L)

    return _final(x1, y2t, g2, post_norm2[li][None, :], TOKEN_TILE)
```

```python
import functools
import math

import jax
import jax.numpy as jnp
from jax import lax
from jax.experimental import pallas as pl
from jax.experimental.pallas import tpu as pltpu

D_MODEL = 1024
GRID_W = 64
HY_WIDTH = 512
ML_HEADS = 4
ML_DK = 64
ML_DV = 128
ML_WIDTH = ML_HEADS * ML_DV
HY_COLS = 3 * HY_WIDTH
QK_COLS = ML_HEADS * ML_DK
N_GATES = 4 * ML_HEADS
FILTER_EMB = 33
DECAY_TARGET = 1e-2
FAST_DECAY_PCT = 0.3
SLOW_DECAY_PCT = 1.5
CHUNK = 128
GATE_CAP = 15.0
N_EXPERTS = 16
CAP_FACTOR = 2
EPS = 1e-6

F32 = jnp.float32
BF16 = jnp.bfloat16

TOKEN_TILE = 512
TIME_ROWS = 8
IN_PROJ_TILE = 1024
VMEM_LIMIT = 56 * 1024 * 1024


def _cparams(sem):
    return pltpu.CompilerParams(dimension_semantics=sem, vmem_limit_bytes=VMEM_LIMIT)


def _rms(xf, g):
    return xf * lax.rsqrt(jnp.mean(xf * xf, axis=-1, keepdims=True) + EPS) * g


def _bdot(a, b):
    return jnp.dot(a.astype(BF16), b.astype(BF16), preferred_element_type=F32)


def _bdot_nt(a, b):
    return lax.dot_general(a.astype(BF16), b.astype(BF16), (((1,), (1,)), ((), ())),
                           preferred_element_type=F32)


def _pe_tables_kernel(omega_ref, er_ref, ec_ref):
    quarter = omega_ref.shape[1]
    om = omega_ref[...]
    for ref in (er_ref, ec_ref):
        n = ref.shape[0]
        pos = lax.broadcasted_iota(jnp.int32, (n, quarter), 0).astype(F32)
        ang = pos * om
        ref[:, :quarter] = jnp.sin(ang)
        ref[:, quarter:] = jnp.cos(ang)


def _pe_tables(rows, cols, dim):
    quarter = dim // 4
    omega = (1.0 / (10000.0 ** (jnp.arange(quarter, dtype=F32) / quarter)))[None, :]
    return pl.pallas_call(
        _pe_tables_kernel,
        out_shape=(jax.ShapeDtypeStruct((rows, dim // 2), F32),
                   jax.ShapeDtypeStruct((cols, dim // 2), F32)),
        name="pe_tables",
    )(omega)


def _pe_tile(er_blk, ec):
    nr, half = er_blk.shape
    row_part = jnp.broadcast_to(er_blk[:, None, :], (nr, GRID_W, half)).reshape(nr * GRID_W, half)
    col_part = jnp.broadcast_to(ec[None, :, :], (nr, GRID_W, half)).reshape(nr * GRID_W, half)
    return jnp.concatenate([row_part, col_part], axis=-1)


def _mod_kernel(c_ref, w_ref, b_ref, o_ref):
    c = c_ref[...]
    s = c * jax.nn.sigmoid(c)
    o_ref[...] = _bdot(s, w_ref[...]) + b_ref[...]


def _modulation(cc, w_mod, b_mod):
    rows, d = cc.shape
    n = w_mod.shape[1]
    tn = 512
    return pl.pallas_call(
        _mod_kernel,
        grid=(n // tn,),
        in_specs=[pl.BlockSpec((rows, d), lambda j: (0, 0)),
                  pl.BlockSpec((d, tn), lambda j: (0, j)),
                  pl.BlockSpec((1, tn), lambda j: (0, j))],
        out_specs=pl.BlockSpec((rows, tn), lambda j: (0, j)),
        out_shape=jax.ShapeDtypeStruct((rows, n), F32),
        compiler_params=_cparams(("arbitrary",)),
        name="modulation",
    )(cc, w_mod, b_mod[None, :])


def _in_proj_kernel(*refs, with_hyena):
    if with_hyena:
        (x_ref, er_ref, ec_ref, sh_ref, sc_ref, g_ref, wn_ref, wt_ref,
         zt_ref, q_ref, kt_ref, v_ref, o_ref, gatet_ref) = refs
        xf = x_ref[0] + _pe_tile(er_ref[...], ec_ref[...])
    else:
        (x_ref, sh_ref, sc_ref, g_ref, wn_ref, wt_ref,
         q_ref, kt_ref, v_ref, o_ref, gatet_ref) = refs
        xf = x_ref[0]
    h = _rms(xf, g_ref[...]) * (1.0 + sc_ref[0]) + sh_ref[0]
    hb = h.astype(BF16)
    z = jnp.dot(hb, wn_ref[...], preferred_element_type=F32)
    qs = z[:, :QK_COLS] * (ML_DK ** -0.5)
    for hd in range(ML_HEADS):
        q_ref[0, hd] = qs[:, hd * ML_DK:(hd + 1) * ML_DK].astype(BF16)
    v_ref[0] = z[:, QK_COLS:QK_COLS + ML_WIDTH].astype(BF16)
    o_ref[0] = z[:, QK_COLS + ML_WIDTH:]
    zt = lax.dot_general(wt_ref[...], hb, (((1,), (1,)), ((), ())), preferred_element_type=F32)
    off = 0
    if with_hyena:
        for j in range(zt.shape[1] // LANES):
            zt_ref[0, 0, pl.ds(j, HY_COLS, stride=TIME_ROWS), :] = zt[:HY_COLS, j * LANES:(j + 1) * LANES]
        off = HY_COLS
    for hd in range(ML_HEADS):
        kt_ref[0, hd] = zt[off + hd * ML_DK:off + (hd + 1) * ML_DK, :].astype(BF16)
    gatet_ref[0] = zt[off + QK_COLS:, :]


def _in_proj(x, tabs, sh, sc, g, w_in, tm, with_hyena):
    B, L, D = x.shape
    w_hy = w_in[:, :HY_COLS]
    w_q = w_in[:, HY_COLS:HY_COLS + QK_COLS]
    w_k = w_in[:, HY_COLS + QK_COLS:HY_COLS + 2 * QK_COLS]
    w_vo = w_in[:, HY_COLS + 2 * QK_COLS:HY_COLS + 2 * QK_COLS + 2 * ML_WIDTH]
    w_g = w_in[:, HY_COLS + 2 * QK_COLS + 2 * ML_WIDTH:]
    wn = jnp.concatenate([w_q, w_vo], axis=1).astype(BF16)
    wt = jnp.concatenate(([w_hy] if with_hyena else []) + [w_k, w_g], axis=1).T.astype(BF16)
    nb = sh.shape[0]
    mod_map = (lambda b, i: (b, 0, 0)) if nb > 1 else (lambda b, i: (0, 0, 0))
    full = lambda a: pl.BlockSpec(a.shape, lambda b, i: (0,) * a.ndim)
    in_specs = [pl.BlockSpec((1, tm, D), lambda b, i: (b, i, 0))]
    args = [x]
    if with_hyena:
        er, ec = tabs
        in_specs += [pl.BlockSpec((tm // GRID_W, D // 2), lambda b, i: (i, 0)), full(ec)]
        args += [er, ec]
    in_specs += [pl.BlockSpec((1, 1, D), mod_map), pl.BlockSpec((1, 1, D), mod_map),
                 full(g), full(wn), full(wt)]
    args += [sh, sc, g, wn, wt]
    out_shape, out_specs = [], []
    if with_hyena:
        assert tm == TIME_ROWS * LANES
        out_shape.append(jax.ShapeDtypeStruct((B, L // tm, HY_COLS * TIME_ROWS, LANES), F32))
        out_specs.append(pl.BlockSpec((1, 1, HY_COLS * TIME_ROWS, LANES), lambda b, i: (b, i, 0, 0)))
    out_shape += [jax.ShapeDtypeStruct((B, ML_HEADS, L, ML_DK), BF16),
                  jax.ShapeDtypeStruct((B, ML_HEADS, ML_DK, L), BF16),
                  jax.ShapeDtypeStruct((B, L, ML_WIDTH), BF16),
                  jax.ShapeDtypeStruct((B, L, ML_WIDTH), F32),
                  jax.ShapeDtypeStruct((B, N_GATES, L), F32)]
    out_specs += [pl.BlockSpec((1, ML_HEADS, tm, ML_DK), lambda b, i: (b, 0, i, 0)),
                  pl.BlockSpec((1, ML_HEADS, ML_DK, tm), lambda b, i: (b, 0, 0, i)),
                  pl.BlockSpec((1, tm, ML_WIDTH), lambda b, i: (b, i, 0)),
                  pl.BlockSpec((1, tm, ML_WIDTH), lambda b, i: (b, i, 0)),
                  pl.BlockSpec((1, N_GATES, tm), lambda b, i: (b, 0, i))]
    return pl.pallas_call(
        functools.partial(_in_proj_kernel, with_hyena=with_hyena),
        grid=(B, L // tm),
        in_specs=in_specs,
        out_specs=out_specs,
        out_shape=out_shape,
        compiler_params=_cparams(("parallel", "parallel")),
        name="in_proj_hy" if with_hyena else "in_proj_ctx",
    )(*args)


FILT_TILE = 1024
FILT_CBLK = 128


def _filter_kernel(w1a_ref, w1b_ref, w1c_ref, b1_ref, w2_ref, b2_ref, w3_ref, b3_ref, fr_ref,
                   w4f_ref, w4b_ref, dl_ref, mir_ref, k_ref, hf_ref, *, L):
    bands = (FILTER_EMB - 1) // 2

    @pl.when(pl.program_id(0) == 0)
    def _():
        fk = (1e-4 + lax.broadcasted_iota(jnp.int32, (bands, 1), 0).astype(F32)
              * ((bands - 1 - 1e-4) / (bands - 1)))
        fr = fr_ref[...]
        for j in range(L // FILT_TILE):
            pos = (lax.broadcasted_iota(jnp.int32, (1, FILT_TILE), 1) + j * FILT_TILE).astype(F32)
            tl = pos * (1.0 / (L - 1))
            ang = fk * (pos * (2.0 * math.pi / L))
            pre = (w1a_ref[...].astype(F32) * tl.astype(BF16).astype(F32)
                   + _bdot(w1b_ref[...], jnp.cos(ang)) + _bdot(w1c_ref[...], -jnp.sin(ang)))
            h = jnp.sin(fr * (pre + b1_ref[...]))
            h = jnp.sin(fr * (_bdot(w2_ref[...], h) + b2_ref[...]))
            h = jnp.sin(fr * (_bdot(w3_ref[...], h) + b3_ref[...]))
            hf_ref[:, j * FILT_TILE:(j + 1) * FILT_TILE] = h.astype(BF16)

    pos = lax.broadcasted_iota(jnp.int32, (1, L), 1).astype(F32)
    decay = jnp.exp(-(pos * (1.0 / (L - 1))) * dl_ref[...])
    hf = hf_ref[...]
    k_ref[:, :L] = jnp.dot(w4f_ref[...], hf, preferred_element_type=F32) * decay
    gb = (jnp.dot(w4b_ref[...], hf, preferred_element_type=F32) * decay).astype(BF16)
    nblk = L // LANES
    for j in range(nblk):
        src = gb[:, (nblk - 1 - j) * LANES:(nblk - j) * LANES]
        nxt = gb[:, (nblk - j) * LANES:(nblk - j + 1) * LANES] if j > 0 else jnp.zeros_like(src)
        k_ref[:, L + j * LANES:L + (j + 1) * LANES] = jnp.dot(
            jnp.concatenate([src, nxt], axis=1), mir_ref[...], preferred_element_type=F32)


def _hyena_filter(L, w1, b1, w2, b2, w3, b3, w4, freq):
    hid = w2.shape[0]
    bands = (FILTER_EMB - 1) // 2
    col = lambda a: a[:, None]
    w1t = w1.T.astype(BF16)
    min_decay = math.log(DECAY_TARGET) / SLOW_DECAY_PCT
    max_decay = math.log(DECAY_TARGET) / FAST_DECAY_PCT
    dl = jnp.abs(jnp.linspace(min_decay, max_decay, HY_WIDTH, dtype=F32))[:, None]
    w4t = w4.T.astype(BF16)
    assert DFT_PASSES == 1, "the mirrored taps are kept at the bf16 precision a single-pass DFT reads"
    lane = jnp.arange(LANES)
    mir = jnp.concatenate([(lane[:, None] + lane[None, :] == LANES),
                           (lane[:, None] == 0) & (lane[None, :] == 0)], axis=0).astype(BF16)
    full = lambda a: pl.BlockSpec(a.shape, lambda i: (0,) * a.ndim)
    args = [w1t[:, 0:1], w1t[:, 1:1 + bands], w1t[:, 1 + bands:], col(b1), w2.T.astype(BF16), col(b2),
            w3.T.astype(BF16), col(b3), col(freq)]
    return pl.pallas_call(
        functools.partial(_filter_kernel, L=L),
        grid=(HY_WIDTH // FILT_CBLK,),
        in_specs=[full(a) for a in args] + [
            pl.BlockSpec((FILT_CBLK, hid), lambda i: (i, 0)),
            pl.BlockSpec((FILT_CBLK, hid), lambda i: (HY_WIDTH // FILT_CBLK + i, 0)),
            pl.BlockSpec((FILT_CBLK, 1), lambda i: (i, 0)), full(mir)],
        out_specs=pl.BlockSpec((FILT_CBLK, 2 * L), lambda i: (i, 0)),
        out_shape=jax.ShapeDtypeStruct((HY_WIDTH, 2 * L), F32),
        scratch_shapes=[pltpu.VMEM((hid, L), BF16)],
        compiler_params=_cparams(("arbitrary",)),
        name="hyena_filter",
    )(*args, w4t, w4t, dl, mir)


FFT_N = 128
HY_CBLK = 32
HY_GROUP = 8
HY_UNROLL = 4
DFT_PASSES = 1


def _dft_constants(n1_data):
    import numpy as np
    n = FFT_N
    k = np.arange(n)
    ang = -2.0 * np.pi * ((k[:, None] * k[None, :]) % n) / n
    fre, fim = np.cos(ang), np.sin(ang)
    m = n1_data
    fa_d = np.block([[fre[:, :m], -fim[:, :m]], [fim[:, :m], fre[:, :m]]])
    fa_f = np.concatenate([fre, fim], axis=0)
    fb = np.block([[fre, fim], [-fim, fre]])
    fbi = np.block([[fre, -fim], [fim, fre]])
    fc = np.block([[fre[:m, :], fim[:m, :]], [-fim[:m, :], fre[:m, :]]]) / (n * n)
    tang = -2.0 * np.pi * (k[:, None] * k[None, :]) / (n * n)
    tw = np.stack([np.cos(tang), np.sin(tang)])

    def hilo(a):
        a32 = jnp.asarray(a, F32)
        hi = a32.astype(BF16)
        lo = (a32 - hi.astype(F32)).astype(BF16)
        return jnp.stack([hi, lo])

    return hilo(fa_d), hilo(fa_f), hilo(fb), hilo(fbi), hilo(fc), jnp.asarray(tw, F32)


def _mm_const_lhs(c_ref, d):
    dh = d.astype(BF16)
    acc = jnp.dot(c_ref[0], dh, preferred_element_type=F32)
    if DFT_PASSES == 3:
        dl = (d - dh.astype(F32)).astype(BF16)
        acc = acc + (jnp.dot(c_ref[0], dl, preferred_element_type=F32)
                     + jnp.dot(c_ref[1], dh, preferred_element_type=F32))
    return acc


def _mm_const_rhs(d, c_ref):
    dh = d.astype(BF16)
    acc = jnp.dot(dh, c_ref[0], preferred_element_type=F32)
    if DFT_PASSES == 3:
        dl = (d - dh.astype(F32)).astype(BF16)
        acc = acc + (jnp.dot(dl, c_ref[0], preferred_element_type=F32)
                     + jnp.dot(dh, c_ref[1], preferred_element_type=F32))
    return acc


def _cmul(are, aim, bre, bim):
    return are * bre - aim * bim, are * bim + aim * bre


def _hyena_conv_kernel(x0_ref, x1_ref, v_ref, kc_ref, w0_ref, w1_ref, wv_ref, b0_ref, b1_ref, bv_ref,
                       hb_ref, fad_ref, faf_ref, fb_ref, fbi_ref, fc_ref, tw_ref,
                       o_ref, u_ref, s_ref, ks_ref):
    n = FFT_N
    cb = kc_ref.shape[0]
    nb, nt = x0_ref.shape[0], x0_ref.shape[1]
    m = nt * TIME_ROWS

    def chan(ref, b, c):
        return ref[b, :, pl.ds(pl.multiple_of(c * TIME_ROWS, TIME_ROWS), TIME_ROWS), :].reshape(m, n)

    sub = lax.broadcasted_iota(jnp.int32, (m, n), 0)
    lane = lax.broadcasted_iota(jnp.int32, (m, n), 1)
    tre, tim = tw_ref[0], tw_ref[1]

    def sconv(z, w_ref, b_ref, c):
        a = pltpu.roll(z, 1, axis=1)
        prev = jnp.where(lane == 0, jnp.where(sub == 0, 0.0, pltpu.roll(a, 1, axis=0)), a)
        a2 = pltpu.roll(z, n - 1, axis=1)
        nxt = jnp.where(lane == n - 1, jnp.where(sub == m - 1, 0.0, pltpu.roll(a2, m - 1, axis=0)), a2)
        return prev * w_ref[0, c] + z * w_ref[1, c] + nxt * w_ref[2, c] + b_ref[c]

    def spectrum_rows(res):
        outs = []
        for h in range(2):
            are, aim = _cmul(res[:n, h * n:(h + 1) * n], res[n:, h * n:(h + 1) * n], tre, tim)
            outs.append(jnp.concatenate([are, aim], axis=1))
        return outs

    def fwd_pair(p, carry):
        c0 = 2 * p
        us = []
        for c in (c0, c0 + 1):
            ub = []
            for b in range(nb):
                x1c = sconv(chan(x1_ref, b, c), w1_ref, b1_ref, c)
                vc = sconv(chan(v_ref, b, c), wv_ref, bv_ref, c)
                u = x1c * vc
                u_ref[b, c] = u
                ub.append(u)
            us.append(ub)
        wd = jnp.concatenate([jnp.concatenate([us[0][b], us[1][b]], axis=1) for b in range(nb)], axis=0)
        sa, sb = spectrum_rows(_mm_const_lhs(fad_ref, wd))
        s_ref[c0] = sa
        s_ref[c0 + 1] = sb
        wk = jnp.concatenate([kc_ref[c0], kc_ref[c0 + 1]], axis=1)
        ka, kb = spectrum_rows(_mm_const_lhs(faf_ref, wk))
        ks_ref[c0] = ka
        ks_ref[c0 + 1] = kb
        return carry

    lax.fori_loop(0, cb // 2, fwd_pair, 0, unroll=HY_UNROLL)

    def mid_group(g, carry):
        gs = pl.ds(pl.multiple_of(g * HY_GROUP, HY_GROUP), HY_GROUP)
        x = _mm_const_rhs(s_ref[gs].reshape(HY_GROUP * n, 2 * n), fb_ref)
        k = _mm_const_rhs(ks_ref[gs].reshape(HY_GROUP * n, 2 * n), fb_ref)
        yre, yim = _cmul(x[:, :n], x[:, n:], k[:, :n], k[:, n:])
        vv = _mm_const_rhs(jnp.concatenate([yre, yim], axis=1), fbi_ref).reshape(HY_GROUP, n, 2 * n)
        vre, vim = _cmul(vv[:, :, :n], vv[:, :, n:], tre[None], -tim[None])
        s_ref[gs] = jnp.concatenate([vre, vim], axis=2)
        return carry

    lax.fori_loop(0, cb // HY_GROUP, mid_group, 0, unroll=HY_UNROLL)

    def inv_pair(p, carry):
        c0 = 2 * p
        sa, sb = s_ref[c0], s_ref[c0 + 1]
        wd = jnp.concatenate([jnp.concatenate([sa[:, :n], sb[:, :n]], axis=1),
                              jnp.concatenate([sa[:, n:], sb[:, n:]], axis=1)], axis=0)
        res = _mm_const_lhs(fc_ref, wd)
        for h, c in enumerate((c0, c0 + 1)):
            for b in range(nb):
                y = res[b * m:(b + 1) * m, h * n:(h + 1) * n]
                x0c = sconv(chan(x0_ref, b, c), w0_ref, b0_ref, c)
                out = x0c * (y + hb_ref[c] * u_ref[b, c])
                o_ref[b, :, pl.ds(pl.multiple_of(c * TIME_ROWS, TIME_ROWS), TIME_ROWS), :] = out.reshape(
                    nt, TIME_ROWS, n)
        return carry

    lax.fori_loop(0, cb // 2, inv_pair, 0, unroll=HY_UNROLL)


def _hyena_conv(zt, kcirc, conv_w, conv_b, hy_bias):
    B, nt, _, n = zt.shape
    m = nt * TIME_ROWS
    L = m * n
    C = HY_WIDTH
    assert B == 2 and n == FFT_N and 2 * L == n * n, "complex packing of two samples over a 128 x 128 point transform"
    z4 = zt
    rows = HY_CBLK * TIME_ROWS
    k3 = kcirc.reshape(C, n, n)
    cw = conv_w.reshape(3, 3 * C, 1, 1)
    cbias = conv_b.reshape(3 * C, 1, 1)
    hb = hy_bias.reshape(C, 1, 1)
    consts = _dft_constants(m)
    nblk = C // HY_CBLK
    zspec = lambda part: pl.BlockSpec((B, nt, rows, n), lambda i: (0, 0, part * nblk + i, 0))
    wspec = lambda part: pl.BlockSpec((3, HY_CBLK, 1, 1), lambda i: (0, part * nblk + i, 0, 0))
    bspec = lambda part: pl.BlockSpec((HY_CBLK, 1, 1), lambda i: (part * nblk + i, 0, 0))
    full = lambda a: pl.BlockSpec(a.shape, lambda i: (0,) * a.ndim)
    y = pl.pallas_call(
        _hyena_conv_kernel,
        grid=(nblk,),
        in_specs=[zspec(0), zspec(1), zspec(2), pl.BlockSpec((HY_CBLK, n, n), lambda i: (i, 0, 0)),
                  wspec(0), wspec(1), wspec(2), bspec(0), bspec(1), bspec(2), bspec(0)]
                 + [full(a) for a in consts],
        out_specs=pl.BlockSpec((B, nt, rows, n), lambda i: (0, 0, i, 0)),
        out_shape=jax.ShapeDtypeStruct((B, nt, C * TIME_ROWS, n), F32),
        scratch_shapes=[pltpu.VMEM((B, HY_CBLK, m, n), F32), pltpu.VMEM((HY_CBLK, n, 2 * n), F32),
                        pltpu.VMEM((HY_CBLK, n, 2 * n), F32)],
        compiler_params=_cparams(("parallel",)),
        name="hyena_conv",
    )(z4, z4, z4, k3, cw, cw, cw, cbias, cbias, cbias, hb, *consts)
    return y


def _split3(a):
    hi = a.astype(BF16)
    r1 = a - hi.astype(F32)
    mid = r1.astype(BF16)
    lo = (r1 - mid.astype(F32)).astype(BF16)
    return hi, mid, lo


def _exact_dot_right(a, tri_bf):
    hi, mid, lo = _split3(a)
    d = lambda p: jnp.dot(p, tri_bf, preferred_element_type=F32)
    return (d(lo) + d(mid)) + d(hi)


def _soft_gates(g):
    g = GATE_CAP * jnp.tanh(g * (1.0 / GATE_CAP))
    logsig = jnp.minimum(g, 0.0) - jnp.log1p(jnp.exp(-jnp.abs(g)))
    return g, logsig


def _gate_prep_kernel(gt_ref, gbt_ref, rows_ref, cols_ref):
    T = CHUNK
    n = gt_ref.shape[2]
    up_bf = (lax.broadcasted_iota(jnp.int32, (T, T), 1) >= lax.broadcasted_iota(jnp.int32, (T, T), 0)).astype(BF16)
    out8 = lax.broadcasted_iota(jnp.int32, (8, 1), 0)
    cap, ls = _soft_gates(gt_ref[0] + gbt_ref[...])
    H = ML_HEADS
    for hd in range(H):
        base = jnp.where(out8 == 0, cap[hd:hd + 1], jnp.where(out8 == 1, ls[H + hd:H + hd + 1],
                         jnp.where(out8 == 2, cap[2 * H + hd:2 * H + hd + 1],
                                   jnp.where(out8 == 3, ls[3 * H + hd:3 * H + hd + 1], 0.0))))
        for j in range(n // T):
            blk = base[:, j * T:(j + 1) * T]
            run = _exact_dot_right(blk, up_bf)
            suf = run[:, T - 1:T] - run + blk
            rows = jnp.where(out8 == 1, run, jnp.where(out8 == 3, suf, blk))
            rows_ref[0, hd, :, j * T:(j + 1) * T] = rows
            cols_ref[0, hd, j * T:(j + 1) * T, :] = rows.T


def _gate_prep(gatet, gate_b, tile):
    B, G, L = gatet.shape
    H = ML_HEADS
    return pl.pallas_call(
        _gate_prep_kernel,
        grid=(B, L // tile),
        in_specs=[pl.BlockSpec((1, G, tile), lambda b, i: (b, 0, i)),
                  pl.BlockSpec((G, 1), lambda b, i: (0, 0))],
        out_specs=[pl.BlockSpec((1, H, 8, tile), lambda b, i: (b, 0, 0, i)),
                   pl.BlockSpec((1, H, tile, 8), lambda b, i: (b, 0, i, 0))],
        out_shape=[jax.ShapeDtypeStruct((B, H, 8, L), F32), jax.ShapeDtypeStruct((B, H, L, 8), F32)],
        compiler_params=_cparams(("parallel", "parallel")),
        name="mlstm_gate_prep",
    )(gatet, gate_b[:, None])


def _mlstm_kernel(q_ref, kt_ref, v_ref, r_ref, bt_ref, qc_ref, ktc_ref, vc_ref, rc_ref, btc_ref,
                  h_ref, cf_ref, cb_ref):
    T = CHUNK
    L = q_ref.shape[2]
    Lc = qc_ref.shape[2]
    nc, ncc = L // T, Lc // T
    row = lax.broadcasted_iota(jnp.int32, (T, T), 0)
    col = lax.broadcasted_iota(jnp.int32, (T, T), 1)
    lo_mask = col <= row
    up_mask = col >= row
    ones_blk = jnp.ones((T, ML_DV), BF16)

    cf_ref[...] = jnp.zeros_like(cf_ref)
    cb_ref[...] = jnp.zeros_like(cb_ref)

    def chunk_step(q, kt, v, rows, cols, c_ref, backward):
        i_r = rows[2:3] if backward else rows[0:1]
        b_r = rows[3:4] if backward else rows[1:2]
        b_c = jnp.broadcast_to(cols[:, 3:4] if backward else cols[:, 1:2], (T, T))
        b_end = b_r[:, 0:1] if backward else b_r[:, T - 1:T]
        mask = up_mask if backward else lo_mask
        w_intra = jnp.exp(jnp.where(mask, b_c - b_r + i_r, -jnp.inf) - GATE_CAP)
        s = jnp.dot(q, kt, preferred_element_type=F32) * w_intra
        qe = (q.astype(F32) * jnp.exp(b_c[:, :ML_DK])).astype(BF16)
        v_aug = jnp.concatenate([v, ones_blk], axis=1)
        c_aug = c_ref[...]
        res = jnp.dot(jnp.concatenate([s.astype(BF16), qe], axis=1),
                      jnp.concatenate([v_aug, c_aug.astype(BF16)], axis=0),
                      preferred_element_type=F32)
        h = res[:, :ML_DV] / jnp.maximum(jnp.abs(res[:, ML_DV:]), math.exp(-GATE_CAP))
        kw = (kt.astype(F32) * jnp.exp(b_end - b_r + i_r - GATE_CAP)).astype(BF16)
        c_ref[...] = jnp.exp(b_end) * c_aug + jnp.dot(kw, v_aug, preferred_element_type=F32)
        return h

    for j in range(ncc):
        for backward in (False, True):
            jj = (ncc - 1 - j) if backward else j
            cs = slice(jj * T, (jj + 1) * T)
            chunk_step(qc_ref[0, 0, cs, :], ktc_ref[0, 0, :, cs], vc_ref[0, cs, :], rc_ref[0, 0, :, cs],
                       btc_ref[0, 0, cs, :], cb_ref if backward else cf_ref, backward)

    def latent_pair(j, accumulate):
        for backward in (False, True):
            jj = (nc - 1 - j) if backward else j
            rs = pl.ds(pl.multiple_of(jj * T, T), T)
            h = chunk_step(q_ref[0, 0, rs, :], kt_ref[0, 0, :, rs], v_ref[0, rs, :], r_ref[0, 0, :, rs],
                           bt_ref[0, 0, rs, :], cb_ref if backward else cf_ref, backward)
            if accumulate:
                h_ref[0, rs, :] = h_ref[0, rs, :] + h
            else:
                h_ref[0, rs, :] = h

    def first_half(j, carry):
        latent_pair(j, False)
        return carry

    def second_half(j, carry):
        latent_pair(j, True)
        return carry

    lax.fori_loop(0, nc // 2, first_half, 0, unroll=MLSTM_UNROLL)
    lax.fori_loop(nc // 2, nc, second_half, 0, unroll=MLSTM_UNROLL)


MLSTM_UNROLL = 4


def _mlstm(q, kt, v, gatet, qc, ktc, vc, gatetc, gate_b):
    B, H, L, dk = q.shape
    Lc = qc.shape[2]
    rows, cols = _gate_prep(gatet, gate_b, min(L, 1024))
    rows_c, cols_c = _gate_prep(gatetc, gate_b, Lc)
    seq = lambda n: [pl.BlockSpec((1, 1, n, dk), lambda b, h: (b, h, 0, 0)),
                     pl.BlockSpec((1, 1, dk, n), lambda b, h: (b, h, 0, 0)),
                     pl.BlockSpec((1, n, ML_DV), lambda b, h: (b, 0, h)),
                     pl.BlockSpec((1, 1, 8, n), lambda b, h: (b, h, 0, 0)),
                     pl.BlockSpec((1, 1, n, 8), lambda b, h: (b, h, 0, 0))]
    return pl.pallas_call(
        _mlstm_kernel,
        grid=(B, H),
        in_specs=seq(L) + seq(Lc),
        out_specs=pl.BlockSpec((1, L, ML_DV), lambda b, h: (b, 0, h)),
        out_shape=jax.ShapeDtypeStruct((B, L, ML_WIDTH), F32),
        scratch_shapes=[pltpu.VMEM((dk, 2 * ML_DV), F32), pltpu.VMEM((dk, 2 * ML_DV), F32)],
        compiler_params=_cparams(("parallel", "parallel")),
        name="mlstm_scan",
    )(q, kt, v, rows, cols, qc, ktc, vc, rows_c, cols_c)


def _out_proj_kernel(yhy_ref, hs_ref, o_ref, x_ref, er_ref, ec_ref, g1_ref, sh2_ref, sc2_ref,
                     mln_ref, post1_ref, pre2_ref, wout_ref, wr_ref, wrt_ref,
                     x1_ref, h2_ref, aff_ref, afft_ref):
    hs = hs_ref[0]
    parts = []
    for hd in range(ML_HEADS):
        hh = hs[:, hd * ML_DV:(hd + 1) * ML_DV]
        parts.append(hh * lax.rsqrt(jnp.mean(hh * hh, axis=-1, keepdims=True) + EPS))
    hn = jnp.concatenate(parts, axis=-1) * mln_ref[...]
    y_ml = hn * jax.nn.sigmoid(o_ref[0])
    yo_hy = [lax.dot_general(yhy_ref[0, 0, pl.ds(j, HY_WIDTH, stride=TIME_ROWS), :].astype(BF16),
                             wout_ref[:HY_WIDTH, :], (((0,), (0,)), ((), ())), preferred_element_type=F32)
             for j in range(TIME_ROWS)]
    yo = (jnp.concatenate(yo_hy, axis=0)
          + jnp.dot(y_ml.astype(BF16), wout_ref[HY_WIDTH:, :], preferred_element_type=F32))
    xf = x_ref[0] + _pe_tile(er_ref[...], ec_ref[...])
    x1 = xf + g1_ref[0] * _rms(yo, post1_ref[...])
    x1_ref[0] = x1
    h2f = _rms(x1, pre2_ref[...]) * (1.0 + sc2_ref[0]) + sh2_ref[0]
    _store_row_tiles(h2_ref, h2f)
    h2 = h2f.astype(BF16)
    logits = jnp.dot(h2, wr_ref[...], preferred_element_type=F32)
    ex = jnp.exp(logits - jnp.max(logits, axis=-1, keepdims=True))
    aff_ref[0] = ex / jnp.sum(ex, axis=-1, keepdims=True)
    logits_t = lax.dot_general(wrt_ref[...], h2, (((1,), (1,)), ((), ())), preferred_element_type=F32)
    ext = jnp.exp(logits_t - jnp.max(logits_t, axis=0, keepdims=True))
    afft_ref[0] = ext / jnp.sum(ext, axis=0, keepdims=True)


def _out_proj(y_hy, hsum, o, x, tabs, g1, sh2, sc2, ml_norm, post1, pre2, w_out, w_router, tm):
    B, L, D = x.shape
    er, ec = tabs
    E = w_router.shape[1]
    full = lambda a: pl.BlockSpec(a.shape, lambda b, i: (0,) * a.ndim)
    tok = lambda w: pl.BlockSpec((1, tm, w), lambda b, i: (b, i, 0))
    modspec = pl.BlockSpec((1, 1, D), lambda b, i: (b, 0, 0))
    wout = w_out.astype(BF16)
    wr = w_router.astype(BF16)
    wrt = w_router.T.astype(BF16)
    return pl.pallas_call(
        _out_proj_kernel,
        grid=(B, L // tm),
        in_specs=[pl.BlockSpec((1, 1, HY_WIDTH * TIME_ROWS, LANES), lambda b, i: (b, i, 0, 0)),
                  tok(ML_WIDTH), tok(ML_WIDTH), tok(D),
                  pl.BlockSpec((tm // GRID_W, D // 2), lambda b, i: (i, 0)), full(ec),
                  modspec, modspec, modspec, full(ml_norm), full(post1), full(pre2),
                  full(wout), full(wr), full(wrt)],
        out_specs=[tok(D), pl.BlockSpec((1, tm * (D // LANES), LANES), lambda b, i: (b, i, 0)), tok(E),
                   pl.BlockSpec((1, E, tm), lambda b, i: (b, 0, i))],
        out_shape=[jax.ShapeDtypeStruct((B, L, D), F32),
                   jax.ShapeDtypeStruct((B, L * (D // LANES), LANES), F32),
                   jax.ShapeDtypeStruct((B, L, E), F32), jax.ShapeDtypeStruct((B, E, L), F32)],
        compiler_params=_cparams(("parallel", "parallel")),
        name="out_proj_router",
    )(y_hy, hsum, o, x, er, ec, g1, sh2, sc2, ml_norm, post1, pre2, wout, wr, wrt)


LANES = 128
ROW_GROUP = 16
SELECT_FAST_SLOTS = 32


def _store_row_tiles(ref, val):
    n, nt = val.shape[0], val.shape[1] // LANES
    for c in range(nt):
        ref[0, pl.ds(c, n, stride=nt), :] = val[:, c * LANES:(c + 1) * LANES]


def _load_row_tiles(ref, c, n, nt):
    return ref[0, pl.ds(c, n, stride=nt), :]


def _select_kernel(afft_ref, idx_ref, sel_ref, *, cap):
    E, L = afft_ref.shape[1], afft_ref.shape[2]
    idx_ref[...] = jnp.zeros_like(idx_ref)
    aff = afft_ref[0]
    iota = lax.broadcasted_iota(jnp.int32, (E, L), 1)
    count = lambda ind: jnp.sum(ind, axis=1, keepdims=True)
    count_ge = lambda th: count(jnp.where(aff >= th, 1.0, 0.0))
    pow2 = lambda j: pltpu.bitcast((j - 24) << 23, F32)

    def estep(_, c):
        lo, hi = c
        mid = (lo + hi) >> 1
        ok = count_ge(pow2(mid)) >= cap
        return jnp.where(ok, mid, lo), jnp.where(ok, hi, mid)

    jlo, jhi = lax.fori_loop(0, 7, estep, (jnp.full((E, 1), 24, jnp.int32),
                                           jnp.full((E, 1), 152, jnp.int32)))

    def vstep(_, c):
        lo, hi = c
        mid = lo + (hi - lo) * 0.5
        ok = count_ge(mid) >= cap
        return jnp.where(ok, mid, lo), jnp.where(ok, hi, mid)

    lo, hi = lax.fori_loop(0, 40, vstep, (pow2(jlo), pow2(jhi)))
    gt = jnp.where(aff >= hi, 1.0, 0.0)
    eq = jnp.where(aff >= lo, 1.0, 0.0) - gt
    need = cap - count(gt)

    def istep(_, c):
        lo, hi = c
        mid = (lo + hi) >> 1
        ok = count(jnp.where(iota <= mid, eq, 0.0)) >= need
        return jnp.where(ok, lo, mid), jnp.where(ok, mid, hi)

    _, last = lax.fori_loop(0, L.bit_length() - 1, istep,
                            (jnp.full((E, 1), -1, jnp.int32), jnp.full((E, 1), L - 1, jnp.int32)))
    sel_ref[...] = gt + jnp.where(iota <= last, eq, 0.0)

    T = LANES
    r_i = lax.broadcasted_iota(jnp.int32, (T, T), 0)
    c_i = lax.broadcasted_iota(jnp.int32, (T, T), 1)
    before = (r_i < c_i).astype(BF16)
    slot_f = r_i.astype(F32)
    lane_f = lax.broadcasted_iota(jnp.int32, (1, T), 1).astype(F32)
    esel = lax.broadcasted_iota(jnp.int32, (E, 1), 0)

    def group(g, off):
        s = sel_ref[:, pl.ds(pl.multiple_of(g * T, T), T)]
        rank = jnp.dot(s.astype(BF16), before, preferred_element_type=F32)
        tok = (lane_f + jnp.asarray(g * T, F32)) * s
        cnt = jnp.sum(s, axis=1, keepdims=True)

        def emit(n_slots):
            for e in range(E):
                hit = rank[e:e + 1, :] == slot_f[:n_slots]
                ids = jnp.sum(jnp.where(hit, tok[e:e + 1, :], 0.0), axis=1, keepdims=True)
                o = jnp.sum(jnp.where(esel == e, off, 0))
                idx_ref[0, e, pl.ds(o, n_slots), :] = ids.astype(jnp.int32)

        few = jnp.max(cnt) <= SELECT_FAST_SLOTS
        pl.when(few)(lambda: emit(SELECT_FAST_SLOTS))
        pl.when(jnp.logical_not(few))(lambda: emit(T))
        return off + cnt.astype(jnp.int32)

    lax.fori_loop(0, L // T, group, jnp.zeros((E, 1), jnp.int32))


def _route_select(afft, cap):
    B, E, L = afft.shape
    idx = pl.pallas_call(
        functools.partial(_select_kernel, cap=cap),
        grid=(B,),
        in_specs=[pl.BlockSpec((1, E, L), lambda b: (b, 0, 0))],
        out_specs=pl.BlockSpec((1, E, cap + LANES, 1), lambda b: (b, 0, 0, 0)),
        out_shape=jax.ShapeDtypeStruct((B, E, cap + LANES, 1), jnp.int32),
        scratch_shapes=[pltpu.VMEM((E, L), F32)],
        compiler_params=_cparams(("parallel",)),
        name="route_select",
    )(afft)
    return idx[:, :, :cap, 0]


def _tile_rows(i, nt):
    return pl.ds(pl.multiple_of(i * nt, nt), nt)


def _gather_kernel(idx_ref, h_ref, aff_ref, xs_ref, gs_ref, buf_ref):
    cap, D = xs_ref.shape[1], xs_ref.shape[2]
    nt = D // LANES

    def body(s, c):
        base = s * ROW_GROUP
        rows = [idx_ref[0, 0, base + k] for k in range(ROW_GROUP)]
        vals = [h_ref[0, _tile_rows(r, nt), :] for r in rows]
        gates = [aff_ref[0, pl.ds(r, 1), :] for r in rows]
        for k in range(ROW_GROUP):
            buf_ref[_tile_rows(base + k, nt), :] = vals[k]
            gs_ref[0, pl.ds(base + k, 1), :] = gates[k]
        return c

    lax.fori_loop(0, cap // ROW_GROUP, body, 0)
    for c in range(nt):
        xs_ref[0, :, c * LANES:(c + 1) * LANES] = buf_ref[pl.ds(c, cap, stride=nt), :].astype(BF16)


def _route_gather(idx, h2t, aff, D):
    B, E, cap = idx.shape
    nt = D // LANES
    L = h2t.shape[1] // nt
    return pl.pallas_call(
        _gather_kernel,
        grid=(B, E),
        in_specs=[pl.BlockSpec((1, 1, cap), lambda b, e: (b * E + e, 0, 0), memory_space=pltpu.SMEM),
                  pl.BlockSpec((1, L * nt, LANES), lambda b, e: (b, 0, 0), pipeline_mode=pl.Buffered(1)),
                  pl.BlockSpec((1, L, E), lambda b, e: (b, 0, 0))],
        out_specs=[pl.BlockSpec((1, cap, D), lambda b, e: (e, b, 0)),
                   pl.BlockSpec((1, cap, E), lambda b, e: (e, b, 0))],
        out_shape=[jax.ShapeDtypeStruct((E, B * cap, D), BF16),
                   jax.ShapeDtypeStruct((E, B * cap, E), F32)],
        scratch_shapes=[pltpu.VMEM((cap * nt, LANES), F32)],
        compiler_params=_cparams(("arbitrary", "arbitrary")),
        name="route_gather",
    )(idx.reshape(B * E, 1, cap), h2t, aff)


def _combine_kernel(idx_ref, ye_ref, y_ref, buf_ref):
    cap, D = ye_ref.shape[1], ye_ref.shape[2]
    nt = D // LANES

    @pl.when(pl.program_id(1) == 0)
    def _():
        y_ref[...] = jnp.zeros_like(y_ref)

    for c in range(nt):
        buf_ref[pl.ds(c, cap, stride=nt), :] = ye_ref[0, :, c * LANES:(c + 1) * LANES]

    def body(s, c):
        base = s * ROW_GROUP
        rows = [idx_ref[0, 0, base + k] for k in range(ROW_GROUP)]
        vals = [y_ref[0, _tile_rows(rows[k], nt), :] + buf_ref[_tile_rows(base + k, nt), :]
                for k in range(ROW_GROUP)]
        for k in range(ROW_GROUP):
            y_ref[0, _tile_rows(rows[k], nt), :] = vals[k]
        return c

    lax.fori_loop(0, cap // ROW_GROUP, body, 0)


def _route_combine(idx, ye, L):
    B, E, cap = idx.shape
    D = ye.shape[2]
    nt = D // LANES
    return pl.pallas_call(
        _combine_kernel,
        grid=(B, E),
        in_specs=[pl.BlockSpec((1, 1, cap), lambda b, e: (b * E + e, 0, 0), memory_space=pltpu.SMEM),
                  pl.BlockSpec((1, cap, D), lambda b, e: (e, b, 0))],
        out_specs=pl.BlockSpec((1, L * nt, LANES), lambda b, e: (b, 0, 0), pipeline_mode=pl.Buffered(1)),
        out_shape=jax.ShapeDtypeStruct((B, L * nt, LANES), F32),
        scratch_shapes=[pltpu.VMEM((cap * nt, LANES), F32)],
        compiler_params=_cparams(("arbitrary", "arbitrary")),
        name="route_combine",
    )(idx.reshape(B * E, 1, cap), ye)


def _expert_kernel(xs_ref, g_ref, wg_ref, wu_ref, wd_ref, ye_ref):
    e = pl.program_id(0)
    f = pl.program_id(1)
    nf = pl.num_programs(1)
    wg = wg_ref[0].astype(BF16)
    wu = wu_ref[0].astype(BF16)
    wd = wd_ref[0].astype(BF16)
    mt = xs_ref.shape[1] // FFN_M_SPLIT
    for mi in range(FFN_M_SPLIT):
        rs = slice(mi * mt, (mi + 1) * mt)
        xs = xs_ref[0, rs, :]
        a = jnp.dot(xs, wg, preferred_element_type=F32)
        u = jnp.dot(xs, wu, preferred_element_type=F32)
        hmid = (a * jax.nn.sigmoid(a) * u).astype(BF16)
        part = jnp.dot(hmid, wd, preferred_element_type=F32)

        @pl.when(f == 0)
        def _():
            ye_ref[0, rs, :] = part

        @pl.when(f > 0)
        def _():
            ye_ref[0, rs, :] = ye_ref[0, rs, :] + part

    @pl.when(f == nf - 1)
    def _():
        gs = g_ref[0]
        esel = lax.broadcasted_iota(jnp.int32, gs.shape, 1)
        ye_ref[0] = ye_ref[0] * jnp.sum(jnp.where(esel == e, gs, 0.0), axis=1, keepdims=True)


FFN_M_SPLIT = 2


def _expert_ffn(xs, gs, w_gate, w_up, w_down, tf):
    E, M, D = xs.shape
    F = w_gate.shape[2]
    return pl.pallas_call(
        _expert_kernel,
        grid=(E, F // tf),
        in_specs=[pl.BlockSpec((1, M, D), lambda e, f: (e, 0, 0)),
                  pl.BlockSpec((1, M, E), lambda e, f: (e, 0, 0)),
                  pl.BlockSpec((1, D, tf), lambda e, f: (e, 0, f)),
                  pl.BlockSpec((1, D, tf), lambda e, f: (e, 0, f)),
                  pl.BlockSpec((1, tf, D), lambda e, f: (e, f, 0))],
        out_specs=pl.BlockSpec((1, M, D), lambda e, f: (e, 0, 0)),
        out_shape=jax.ShapeDtypeStruct((E, M, D), F32),
        compiler_params=_cparams(("parallel", "arbitrary")),
        name="expert_ffn",
    )(xs, gs, w_gate, w_up, w_down)


def _final_kernel(x1_ref, y2_ref, g2_ref, post2_ref, o_ref):
    n = x1_ref.shape[1]
    nt = x1_ref.shape[2] // LANES
    cols = [_load_row_tiles(y2_ref, c, n, nt) for c in range(nt)]
    ssq = sum(jnp.sum(y * y, axis=-1, keepdims=True) for y in cols)
    rstd = lax.rsqrt(ssq * (1.0 / (nt * LANES)) + EPS)
    for c in range(nt):
        cs = slice(c * LANES, (c + 1) * LANES)
        o_ref[0, :, cs] = x1_ref[0, :, cs] + g2_ref[0, :, cs] * (cols[c] * rstd * post2_ref[:, cs])


def _final(x1, y2t, g2, post2, tm):
    B, L, D = x1.shape
    tok = pl.BlockSpec((1, tm, D), lambda b, i: (b, i, 0))
    return pl.pallas_call(
        _final_kernel,
        grid=(B, L // tm),
        in_specs=[tok, pl.BlockSpec((1, tm * (D // LANES), LANES), lambda b, i: (b, i, 0)),
                  pl.BlockSpec((1, 1, D), lambda b, i: (b, 0, 0)),
                  pl.BlockSpec((1, D), lambda b, i: (0, 0))],
        out_specs=tok,
        out_shape=jax.ShapeDtypeStruct((B, L, D), F32),
        compiler_params=_cparams(("parallel", "parallel")),
        name="final_residual",
    )(x1, y2t, g2, post2)


def kernel(x, c, ctx, c_ctx, w_mod, b_mod, pre_norm1, post_norm1, pre_norm2, post_norm2, w_in, conv_w, conv_b, filt_w1, filt_b1, filt_w2, filt_b2, filt_w3, filt_b3, filt_w4, filt_freq, hyena_bias, ml_gate_b, ml_norm, w_out, w_router, w_exp_gate, w_exp_up, w_exp_down):
    B, L, D = x.shape
    depth = w_mod.shape[0]
    assert depth == 1, "single-layer block"
    li = 0
    tabs = _pe_tables(L // GRID_W, GRID_W, D)

    cc = jnp.concatenate([c, c_ctx[None], jnp.zeros((8 - B - 1, D), F32)], axis=0)
    mod = _modulation(cc, w_mod[li], b_mod[li])
    chunks = [mod[:, k * D:(k + 1) * D] for k in range(6)]
    sh1, sc1, g1, sh2, sc2, g2 = [m[:B, None, :] for m in chunks]
    csh1, csc1 = chunks[0][B:B + 1, None, :], chunks[1][B:B + 1, None, :]

    pre1 = pre_norm1[li][None, :]
    zhy, q, kt, v, o, gatet = _in_proj(x, tabs, sh1, sc1, pre1, w_in[li], IN_PROJ_TILE, True)
    qc, ktc, vc, _, gatetc = _in_proj(ctx, None, csh1, csc1, pre1, w_in[li], ctx.shape[1], False)

    kcirc = _hyena_filter(L, filt_w1[li], filt_b1[li], filt_w2[li], filt_b2[li], filt_w3[li],
                          filt_b3[li], filt_w4[li], filt_freq[li])
    y_hy = _hyena_conv(zhy, kcirc, conv_w[li], conv_b[li], hyena_bias[li])
    hsum = _mlstm(q, kt, v, gatet, qc, ktc, vc, gatetc, ml_gate_b[li])

    x1, h2t, aff, afft = _out_proj(y_hy, hsum, o, x, tabs, g1, sh2, sc2, ml_norm[li][None, :],
                                   post_norm1[li][None, :], pre_norm2[li][None, :], w_out[li],
                                   w_router[li], IN_PROJ_TILE)

    cap = CAP_FACTOR * L // N_EXPERTS
    idx = _route_select(afft, cap)
    xs, gs = _route_gather(idx, h2t, aff, D)
    ye = _expert_ffn(xs, gs, w_exp_gate[li], w_exp_up[li], w_exp_down[li], 512)
    y2t = _route_combine(idx, ye, L)

    return _final(x1, y2t, g2, post_norm2[li][None, :], TOKEN_TILE)
```

```python
import functools
import math

import jax
import jax.numpy as jnp
from jax import lax
from jax.experimental import pallas as pl
from jax.experimental.pallas import tpu as pltpu

D_MODEL = 1024
GRID_W = 64
HY_WIDTH = 512
ML_HEADS = 4
ML_DK = 64
ML_DV = 128
ML_WIDTH = ML_HEADS * ML_DV
HY_COLS = 3 * HY_WIDTH
QK_COLS = ML_HEADS * ML_DK
N_GATES = 4 * ML_HEADS
FILTER_EMB = 33
DECAY_TARGET = 1e-2
FAST_DECAY_PCT = 0.3
SLOW_DECAY_PCT = 1.5
CHUNK = 128
GATE_CAP = 15.0
N_EXPERTS = 16
CAP_FACTOR = 2
EPS = 1e-6

F32 = jnp.float32
BF16 = jnp.bfloat16

TOKEN_TILE = 512
TIME_ROWS = 8
IN_PROJ_TILE = 1024
VMEM_LIMIT = 56 * 1024 * 1024


def _cparams(sem):
    return pltpu.CompilerParams(dimension_semantics=sem, vmem_limit_bytes=VMEM_LIMIT)


def _rms(xf, g):
    return xf * lax.rsqrt(jnp.mean(xf * xf, axis=-1, keepdims=True) + EPS) * g


def _bdot(a, b):
    return jnp.dot(a.astype(BF16), b.astype(BF16), preferred_element_type=F32)


def _bdot_nt(a, b):
    return lax.dot_general(a.astype(BF16), b.astype(BF16), (((1,), (1,)), ((), ())),
                           preferred_element_type=F32)


def _pe_tables_kernel(omega_ref, er_ref, ec_ref):
    quarter = omega_ref.shape[1]
    om = omega_ref[...]
    for ref in (er_ref, ec_ref):
        n = ref.shape[0]
        pos = lax.broadcasted_iota(jnp.int32, (n, quarter), 0).astype(F32)
        ang = pos * om
        ref[:, :quarter] = jnp.sin(ang)
        ref[:, quarter:] = jnp.cos(ang)


def _pe_tables(rows, cols, dim):
    quarter = dim // 4
    omega = (1.0 / (10000.0 ** (jnp.arange(quarter, dtype=F32) / quarter)))[None, :]
    return pl.pallas_call(
        _pe_tables_kernel,
        out_shape=(jax.ShapeDtypeStruct((rows, dim // 2), F32),
                   jax.ShapeDtypeStruct((cols, dim // 2), F32)),
        name="pe_tables",
    )(omega)


def _pe_tile(er_blk, ec):
    nr, half = er_blk.shape
    row_part = jnp.broadcast_to(er_blk[:, None, :], (nr, GRID_W, half)).reshape(nr * GRID_W, half)
    col_part = jnp.broadcast_to(ec[None, :, :], (nr, GRID_W, half)).reshape(nr * GRID_W, half)
    return jnp.concatenate([row_part, col_part], axis=-1)


def _mod_kernel(c_ref, w_ref, b_ref, o_ref):
    c = c_ref[...]
    s = c * jax.nn.sigmoid(c)
    o_ref[...] = _bdot(s, w_ref[...]) + b_ref[...]


def _modulation(cc, w_mod, b_mod):
    rows, d = cc.shape
    n = w_mod.shape[1]
    tn = 512
    return pl.pallas_call(
        _mod_kernel,
        grid=(n // tn,),
        in_specs=[pl.BlockSpec((rows, d), lambda j: (0, 0)),
                  pl.BlockSpec((d, tn), lambda j: (0, j)),
                  pl.BlockSpec((1, tn), lambda j: (0, j))],
        out_specs=pl.BlockSpec((rows, tn), lambda j: (0, j)),
        out_shape=jax.ShapeDtypeStruct((rows, n), F32),
        compiler_params=_cparams(("arbitrary",)),
        name="modulation",
    )(cc, w_mod, b_mod[None, :])


def _in_proj_kernel(*refs, with_hyena):
    if with_hyena:
        (x_ref, er_ref, ec_ref, sh_ref, sc_ref, g_ref, wn_ref, wt_ref,
         zt_ref, q_ref, kt_ref, v_ref, o_ref, gatet_ref) = refs
        xf = x_ref[0] + _pe_tile(er_ref[...], ec_ref[...])
    else:
        (x_ref, sh_ref, sc_ref, g_ref, wn_ref, wt_ref,
         q_ref, kt_ref, v_ref, o_ref, gatet_ref) = refs
        xf = x_ref[0]
    h = _rms(xf, g_ref[...]) * (1.0 + sc_ref[0]) + sh_ref[0]
    hb = h.astype(BF16)
    z = jnp.dot(hb, wn_ref[...], preferred_element_type=F32)
    qs = z[:, :QK_COLS] * (ML_DK ** -0.5)
    for hd in range(ML_HEADS):
        q_ref[0, hd] = qs[:, hd * ML_DK:(hd + 1) * ML_DK].astype(BF16)
    v_ref[0] = z[:, QK_COLS:QK_COLS + ML_WIDTH].astype(BF16)
    o_ref[0] = z[:, QK_COLS + ML_WIDTH:]
    zt = lax.dot_general(wt_ref[...], hb, (((1,), (1,)), ((), ())), preferred_element_type=F32)
    off = 0
    if with_hyena:
        for j in range(zt.shape[1] // LANES):
            zt_ref[0, 0, pl.ds(j, HY_COLS, stride=TIME_ROWS), :] = zt[:HY_COLS, j * LANES:(j + 1) * LANES]
        off = HY_COLS
    for hd in range(ML_HEADS):
        kt_ref[0, hd] = zt[off + hd * ML_DK:off + (hd + 1) * ML_DK, :].astype(BF16)
    gatet_ref[0] = zt[off + QK_COLS:, :]


def _in_proj(x, tabs, sh, sc, g, w_in, tm, with_hyena):
    B, L, D = x.shape
    w_hy = w_in[:, :HY_COLS]
    w_q = w_in[:, HY_COLS:HY_COLS + QK_COLS]
    w_k = w_in[:, HY_COLS + QK_COLS:HY_COLS + 2 * QK_COLS]
    w_vo = w_in[:, HY_COLS + 2 * QK_COLS:HY_COLS + 2 * QK_COLS + 2 * ML_WIDTH]
    w_g = w_in[:, HY_COLS + 2 * QK_COLS + 2 * ML_WIDTH:]
    wn = jnp.concatenate([w_q, w_vo], axis=1).astype(BF16)
    wt = jnp.concatenate(([w_hy] if with_hyena else []) + [w_k, w_g], axis=1).T.astype(BF16)
    nb = sh.shape[0]
    mod_map = (lambda b, i: (b, 0, 0)) if nb > 1 else (lambda b, i: (0, 0, 0))
    full = lambda a: pl.BlockSpec(a.shape, lambda b, i: (0,) * a.ndim)
    in_specs = [pl.BlockSpec((1, tm, D), lambda b, i: (b, i, 0))]
    args = [x]
    if with_hyena:
        er, ec = tabs
        in_specs += [pl.BlockSpec((tm // GRID_W, D // 2), lambda b, i: (i, 0)), full(ec)]
        args += [er, ec]
    in_specs += [pl.BlockSpec((1, 1, D), mod_map), pl.BlockSpec((1, 1, D), mod_map),
                 full(g), full(wn), full(wt)]
    args += [sh, sc, g, wn, wt]
    out_shape, out_specs = [], []
    if with_hyena:
        assert tm == TIME_ROWS * LANES
        out_shape.append(jax.ShapeDtypeStruct((B, L // tm, HY_COLS * TIME_ROWS, LANES), F32))
        out_specs.append(pl.BlockSpec((1, 1, HY_COLS * TIME_ROWS, LANES), lambda b, i: (b, i, 0, 0)))
    out_shape += [jax.ShapeDtypeStruct((B, ML_HEADS, L, ML_DK), BF16),
                  jax.ShapeDtypeStruct((B, ML_HEADS, ML_DK, L), BF16),
                  jax.ShapeDtypeStruct((B, L, ML_WIDTH), BF16),
                  jax.ShapeDtypeStruct((B, L, ML_WIDTH), F32),
                  jax.ShapeDtypeStruct((B, N_GATES, L), F32)]
    out_specs += [pl.BlockSpec((1, ML_HEADS, tm, ML_DK), lambda b, i: (b, 0, i, 0)),
                  pl.BlockSpec((1, ML_HEADS, ML_DK, tm), lambda b, i: (b, 0, 0, i)),
                  pl.BlockSpec((1, tm, ML_WIDTH), lambda b, i: (b, i, 0)),
                  pl.BlockSpec((1, tm, ML_WIDTH), lambda b, i: (b, i, 0)),
                  pl.BlockSpec((1, N_GATES, tm), lambda b, i: (b, 0, i))]
    return pl.pallas_call(
        functools.partial(_in_proj_kernel, with_hyena=with_hyena),
        grid=(B, L // tm),
        in_specs=in_specs,
        out_specs=out_specs,
        out_shape=out_shape,
        compiler_params=_cparams(("parallel", "parallel")),
        name="in_proj_hy" if with_hyena else "in_proj_ctx",
    )(*args)


FILT_TILE = 1024
FILT_CBLK = 128


def _filter_kernel(w1a_ref, w1b_ref, w1c_ref, b1_ref, w2_ref, b2_ref, w3_ref, b3_ref, fr_ref,
                   w4f_ref, w4b_ref, dl_ref, mir_ref, k_ref, hf_ref, *, L):
    bands = (FILTER_EMB - 1) // 2

    @pl.when(pl.program_id(0) == 0)
    def _():
        fk = (1e-4 + lax.broadcasted_iota(jnp.int32, (bands, 1), 0).astype(F32)
              * ((bands - 1 - 1e-4) / (bands - 1)))
        fr = fr_ref[...]
        for j in range(L // FILT_TILE):
            pos = (lax.broadcasted_iota(jnp.int32, (1, FILT_TILE), 1) + j * FILT_TILE).astype(F32)
            tl = pos * (1.0 / (L - 1))
            ang = fk * (pos * (2.0 * math.pi / L))
            pre = (w1a_ref[...].astype(F32) * tl.astype(BF16).astype(F32)
                   + _bdot(w1b_ref[...], jnp.cos(ang)) + _bdot(w1c_ref[...], -jnp.sin(ang)))
            h = jnp.sin(fr * (pre + b1_ref[...]))
            h = jnp.sin(fr * (_bdot(w2_ref[...], h) + b2_ref[...]))
            h = jnp.sin(fr * (_bdot(w3_ref[...], h) + b3_ref[...]))
            hf_ref[:, j * FILT_TILE:(j + 1) * FILT_TILE] = h.astype(BF16)

    pos = lax.broadcasted_iota(jnp.int32, (1, L), 1).astype(F32)
    decay = jnp.exp(-(pos * (1.0 / (L - 1))) * dl_ref[...])
    hf = hf_ref[...]
    k_ref[:, :L] = jnp.dot(w4f_ref[...], hf, preferred_element_type=F32) * decay
    gb = (jnp.dot(w4b_ref[...], hf, preferred_element_type=F32) * decay).astype(BF16)
    nblk = L // LANES
    for j in range(nblk):
        src = gb[:, (nblk - 1 - j) * LANES:(nblk - j) * LANES]
        nxt = gb[:, (nblk - j) * LANES:(nblk - j + 1) * LANES] if j > 0 else jnp.zeros_like(src)
        k_ref[:, L + j * LANES:L + (j + 1) * LANES] = jnp.dot(
            jnp.concatenate([src, nxt], axis=1), mir_ref[...], preferred_element_type=F32)


def _hyena_filter(L, w1, b1, w2, b2, w3, b3, w4, freq):
    hid = w2.shape[0]
    bands = (FILTER_EMB - 1) // 2
    col = lambda a: a[:, None]
    w1t = w1.T.astype(BF16)
    min_decay = math.log(DECAY_TARGET) / SLOW_DECAY_PCT
    max_decay = math.log(DECAY_TARGET) / FAST_DECAY_PCT
    dl = jnp.abs(jnp.linspace(min_decay, max_decay, HY_WIDTH, dtype=F32))[:, None]
    w4t = w4.T.astype(BF16)
    assert DFT_PASSES == 1, "the mirrored taps are kept at the bf16 precision a single-pass DFT reads"
    lane = jnp.arange(LANES)
    mir = jnp.concatenate([(lane[:, None] + lane[None, :] == LANES),
                           (lane[:, None] == 0) & (lane[None, :] == 0)], axis=0).astype(BF16)
    full = lambda a: pl.BlockSpec(a.shape, lambda i: (0,) * a.ndim)
    args = [w1t[:, 0:1], w1t[:, 1:1 + bands], w1t[:, 1 + bands:], col(b1), w2.T.astype(BF16), col(b2),
            w3.T.astype(BF16), col(b3), col(freq)]
    return pl.pallas_call(
        functools.partial(_filter_kernel, L=L),
        grid=(HY_WIDTH // FILT_CBLK,),
        in_specs=[full(a) for a in args] + [
            pl.BlockSpec((FILT_CBLK, hid), lambda i: (i, 0)),
            pl.BlockSpec((FILT_CBLK, hid), lambda i: (HY_WIDTH // FILT_CBLK + i, 0)),
            pl.BlockSpec((FILT_CBLK, 1), lambda i: (i, 0)), full(mir)],
        out_specs=pl.BlockSpec((FILT_CBLK, 2 * L), lambda i: (i, 0)),
        out_shape=jax.ShapeDtypeStruct((HY_WIDTH, 2 * L), F32),
        scratch_shapes=[pltpu.VMEM((hid, L), BF16)],
        compiler_params=_cparams(("arbitrary",)),
        name="hyena_filter",
    )(*args, w4t, w4t, dl, mir)


FFT_N = 128
HY_CBLK = 32
HY_GROUP = 8
HY_UNROLL = 4
DFT_PASSES = 1


def _dft_constants(n1_data):
    import numpy as np
    n = FFT_N
    k = np.arange(n)
    ang = -2.0 * np.pi * ((k[:, None] * k[None, :]) % n) / n
    fre, fim = np.cos(ang), np.sin(ang)
    m = n1_data
    fa_d = np.block([[fre[:, :m], -fim[:, :m]], [fim[:, :m], fre[:, :m]]])
    fa_f = np.concatenate([fre, fim], axis=0)
    fb = np.block([[fre, fim], [-fim, fre]])
    fbi = np.block([[fre, -fim], [fim, fre]])
    fc = np.block([[fre[:m, :], fim[:m, :]], [-fim[:m, :], fre[:m, :]]]) / (n * n)
    tang = -2.0 * np.pi * (k[:, None] * k[None, :]) / (n * n)
    tw = np.stack([np.cos(tang), np.sin(tang)])

    def hilo(a):
        a32 = jnp.asarray(a, F32)
        hi = a32.astype(BF16)
        lo = (a32 - hi.astype(F32)).astype(BF16)
        return jnp.stack([hi, lo])

    return hilo(fa_d), hilo(fa_f), hilo(fb), hilo(fbi), hilo(fc), jnp.asarray(tw, F32)


def _mm_const_lhs(c_ref, d):
    dh = d.astype(BF16)
    acc = jnp.dot(c_ref[0], dh, preferred_element_type=F32)
    if DFT_PASSES == 3:
        dl = (d - dh.astype(F32)).astype(BF16)
        acc = acc + (jnp.dot(c_ref[0], dl, preferred_element_type=F32)
                     + jnp.dot(c_ref[1], dh, preferred_element_type=F32))
    return acc


def _mm_const_rhs(d, c_ref):
    dh = d.astype(BF16)
    acc = jnp.dot(dh, c_ref[0], preferred_element_type=F32)
    if DFT_PASSES == 3:
        dl = (d - dh.astype(F32)).astype(BF16)
        acc = acc + (jnp.dot(dl, c_ref[0], preferred_element_type=F32)
                     + jnp.dot(dh, c_ref[1], preferred_element_type=F32))
    return acc


def _cmul(are, aim, bre, bim):
    return are * bre - aim * bim, are * bim + aim * bre


def _hyena_conv_kernel(x0_ref, x1_ref, v_ref, kc_ref, w0_ref, w1_ref, wv_ref, b0_ref, b1_ref, bv_ref,
                       hb_ref, fad_ref, faf_ref, fb_ref, fbi_ref, fc_ref, tw_ref,
                       o_ref, u_ref, s_ref, ks_ref):
    n = FFT_N
    cb = kc_ref.shape[0]
    nb, nt = x0_ref.shape[0], x0_ref.shape[1]
    m = nt * TIME_ROWS

    def chan(ref, b, c):
        return ref[b, :, pl.ds(pl.multiple_of(c * TIME_ROWS, TIME_ROWS), TIME_ROWS), :].reshape(m, n)

    sub = lax.broadcasted_iota(jnp.int32, (m, n), 0)
    lane = lax.broadcasted_iota(jnp.int32, (m, n), 1)
    tre, tim = tw_ref[0], tw_ref[1]

    def sconv(z, w_ref, b_ref, c):
        a = pltpu.roll(z, 1, axis=1)
        prev = jnp.where(lane == 0, jnp.where(sub == 0, 0.0, pltpu.roll(a, 1, axis=0)), a)
        a2 = pltpu.roll(z, n - 1, axis=1)
        nxt = jnp.where(lane == n - 1, jnp.where(sub == m - 1, 0.0, pltpu.roll(a2, m - 1, axis=0)), a2)
        return prev * w_ref[0, c] + z * w_ref[1, c] + nxt * w_ref[2, c] + b_ref[c]

    def spectrum_rows(res):
        outs = []
        for h in range(2):
            are, aim = _cmul(res[:n, h * n:(h + 1) * n], res[n:, h * n:(h + 1) * n], tre, tim)
            outs.append(jnp.concatenate([are, aim], axis=1))
        return outs

    def fwd_pair(p, carry):
        c0 = 2 * p
        us = []
        for c in (c0, c0 + 1):
            ub = []
            for b in range(nb):
                x1c = sconv(chan(x1_ref, b, c), w1_ref, b1_ref, c)
                vc = sconv(chan(v_ref, b, c), wv_ref, bv_ref, c)
                u = x1c * vc
                u_ref[b, c] = u
                ub.append(u)
            us.append(ub)
        wd = jnp.concatenate([jnp.concatenate([us[0][b], us[1][b]], axis=1) for b in range(nb)], axis=0)
        sa, sb = spectrum_rows(_mm_const_lhs(fad_ref, wd))
        s_ref[c0] = sa
        s_ref[c0 + 1] = sb
        wk = jnp.concatenate([kc_ref[c0], kc_ref[c0 + 1]], axis=1)
        ka, kb = spectrum_rows(_mm_const_lhs(faf_ref, wk))
        ks_ref[c0] = ka
        ks_ref[c0 + 1] = kb
        return carry

    lax.fori_loop(0, cb // 2, fwd_pair, 0, unroll=HY_UNROLL)

    def mid_group(g, carry):
        gs = pl.ds(pl.multiple_of(g * HY_GROUP, HY_GROUP), HY_GROUP)
        x = _mm_const_rhs(s_ref[gs].reshape(HY_GROUP * n, 2 * n), fb_ref)
        k = _mm_const_rhs(ks_ref[gs].reshape(HY_GROUP * n, 2 * n), fb_ref)
        yre, yim = _cmul(x[:, :n], x[:, n:], k[:, :n], k[:, n:])
        vv = _mm_const_rhs(jnp.concatenate([yre, yim], axis=1), fbi_ref).reshape(HY_GROUP, n, 2 * n)
        vre, vim = _cmul(vv[:, :, :n], vv[:, :, n:], tre[None], -tim[None])
        s_ref[gs] = jnp.concatenate([vre, vim], axis=2)
        return carry

    lax.fori_loop(0, cb // HY_GROUP, mid_group, 0, unroll=HY_UNROLL)

    def inv_pair(p, carry):
        c0 = 2 * p
        sa, sb = s_ref[c0], s_ref[c0 + 1]
        wd = jnp.concatenate([jnp.concatenate([sa[:, :n], sb[:, :n]], axis=1),
                              jnp.concatenate([sa[:, n:], sb[:, n:]], axis=1)], axis=0)
        res = _mm_const_lhs(fc_ref, wd)
        for h, c in enumerate((c0, c0 + 1)):
            for b in range(nb):
                y = res[b * m:(b + 1) * m, h * n:(h + 1) * n]
                x0c = sconv(chan(x0_ref, b, c), w0_ref, b0_ref, c)
                out = x0c * (y + hb_ref[c] * u_ref[b, c])
                o_ref[b, :, pl.ds(pl.multiple_of(c * TIME_ROWS, TIME_ROWS), TIME_ROWS), :] = out.reshape(
                    nt, TIME_ROWS, n)
        return carry

    lax.fori_loop(0, cb // 2, inv_pair, 0, unroll=HY_UNROLL)


def _hyena_conv(zt, kcirc, conv_w, conv_b, hy_bias):
    B, nt, _, n = zt.shape
    m = nt * TIME_ROWS
    L = m * n
    C = HY_WIDTH
    assert B == 2 and n == FFT_N and 2 * L == n * n, "complex packing of two samples over a 128 x 128 point transform"
    z4 = zt
    rows = HY_CBLK * TIME_ROWS
    k3 = kcirc.reshape(C, n, n)
    cw = conv_w.reshape(3, 3 * C, 1, 1)
    cbias = conv_b.reshape(3 * C, 1, 1)
    hb = hy_bias.reshape(C, 1, 1)
    consts = _dft_constants(m)
    nblk = C // HY_CBLK
    zspec = lambda part: pl.BlockSpec((B, nt, rows, n), lambda i: (0, 0, part * nblk + i, 0))
    wspec = lambda part: pl.BlockSpec((3, HY_CBLK, 1, 1), lambda i: (0, part * nblk + i, 0, 0))
    bspec = lambda part: pl.BlockSpec((HY_CBLK, 1, 1), lambda i: (part * nblk + i, 0, 0))
    full = lambda a: pl.BlockSpec(a.shape, lambda i: (0,) * a.ndim)
    y = pl.pallas_call(
        _hyena_conv_kernel,
        grid=(nblk,),
        in_specs=[zspec(0), zspec(1), zspec(2), pl.BlockSpec((HY_CBLK, n, n), lambda i: (i, 0, 0)),
                  wspec(0), wspec(1), wspec(2), bspec(0), bspec(1), bspec(2), bspec(0)]
                 + [full(a) for a in consts],
        out_specs=pl.BlockSpec((B, nt, rows, n), lambda i: (0, 0, i, 0)),
        out_shape=jax.ShapeDtypeStruct((B, nt, C * TIME_ROWS, n), F32),
        scratch_shapes=[pltpu.VMEM((B, HY_CBLK, m, n), F32), pltpu.VMEM((HY_CBLK, n, 2 * n), F32),
                        pltpu.VMEM((HY_CBLK, n, 2 * n), F32)],
        compiler_params=_cparams(("parallel",)),
        name="hyena_conv",
    )(z4, z4, z4, k3, cw, cw, cw, cbias, cbias, cbias, hb, *consts)
    return y


def _split3(a):
    hi = a.astype(BF16)
    r1 = a - hi.astype(F32)
    mid = r1.astype(BF16)
    lo = (r1 - mid.astype(F32)).astype(BF16)
    return hi, mid, lo


def _exact_dot_right(a, tri_bf):
    hi, mid, lo = _split3(a)
    d = lambda p: jnp.dot(p, tri_bf, preferred_element_type=F32)
    return (d(lo) + d(mid)) + d(hi)


def _soft_gates(g):
    g = GATE_CAP * jnp.tanh(g * (1.0 / GATE_CAP))
    logsig = jnp.minimum(g, 0.0) - jnp.log1p(jnp.exp(-jnp.abs(g)))
    return g, logsig


def _gate_prep_kernel(gt_ref, gbt_ref, rows_ref, cols_ref):
    T = CHUNK
    n = gt_ref.shape[2]
    up_bf = (lax.broadcasted_iota(jnp.int32, (T, T), 1) >= lax.broadcasted_iota(jnp.int32, (T, T), 0)).astype(BF16)
    out8 = lax.broadcasted_iota(jnp.int32, (8, 1), 0)
    cap, ls = _soft_gates(gt_ref[0] + gbt_ref[...])
    H = ML_HEADS
    for hd in range(H):
        base = jnp.where(out8 == 0, cap[hd:hd + 1], jnp.where(out8 == 1, ls[H + hd:H + hd + 1],
                         jnp.where(out8 == 2, cap[2 * H + hd:2 * H + hd + 1],
                                   jnp.where(out8 == 3, ls[3 * H + hd:3 * H + hd + 1], 0.0))))
        for j in range(n // T):
            blk = base[:, j * T:(j + 1) * T]
            run = _exact_dot_right(blk, up_bf)
            suf = run[:, T - 1:T] - run + blk
            rows = jnp.where(out8 == 1, run, jnp.where(out8 == 3, suf, blk))
            rows_ref[0, hd, :, j * T:(j + 1) * T] = rows
            cols_ref[0, hd, j * T:(j + 1) * T, :] = rows.T


def _gate_prep(gatet, gate_b, tile):
    B, G, L = gatet.shape
    H = ML_HEADS
    return pl.pallas_call(
        _gate_prep_kernel,
        grid=(B, L // tile),
        in_specs=[pl.BlockSpec((1, G, tile), lambda b, i: (b, 0, i)),
                  pl.BlockSpec((G, 1), lambda b, i: (0, 0))],
        out_specs=[pl.BlockSpec((1, H, 8, tile), lambda b, i: (b, 0, 0, i)),
                   pl.BlockSpec((1, H, tile, 8), lambda b, i: (b, 0, i, 0))],
        out_shape=[jax.ShapeDtypeStruct((B, H, 8, L), F32), jax.ShapeDtypeStruct((B, H, L, 8), F32)],
        compiler_params=_cparams(("parallel", "parallel")),
        name="mlstm_gate_prep",
    )(gatet, gate_b[:, None])


def _mlstm_kernel(q_ref, kt_ref, v_ref, r_ref, bt_ref, qc_ref, ktc_ref, vc_ref, rc_ref, btc_ref,
                  h_ref, cf_ref, cb_ref):
    T = CHUNK
    L = q_ref.shape[2]
    Lc = qc_ref.shape[2]
    nc, ncc = L // T, Lc // T
    row = lax.broadcasted_iota(jnp.int32, (T, T), 0)
    col = lax.broadcasted_iota(jnp.int32, (T, T), 1)
    lo_mask = col <= row
    up_mask = col >= row
    ones_blk = jnp.ones((T, ML_DV), BF16)

    cf_ref[...] = jnp.zeros_like(cf_ref)
    cb_ref[...] = jnp.zeros_like(cb_ref)

    def chunk_step(q, kt, v, rows, cols, c_ref, backward):
        i_r = rows[2:3] if backward else rows[0:1]
        b_r = rows[3:4] if backward else rows[1:2]
        b_c = jnp.broadcast_to(cols[:, 3:4] if backward else cols[:, 1:2], (T, T))
        b_end = b_r[:, 0:1] if backward else b_r[:, T - 1:T]
        mask = up_mask if backward else lo_mask
        w_intra = jnp.exp(jnp.where(mask, b_c - b_r + i_r, -jnp.inf) - GATE_CAP)
        s = jnp.dot(q, kt, preferred_element_type=F32) * w_intra
        qe = (q.astype(F32) * jnp.exp(b_c[:, :ML_DK])).astype(BF16)
        v_aug = jnp.concatenate([v, ones_blk], axis=1)
        c_aug = c_ref[...]
        res = jnp.dot(jnp.concatenate([s.astype(BF16), qe], axis=1),
                      jnp.concatenate([v_aug, c_aug.astype(BF16)], axis=0),
                      preferred_element_type=F32)
        h = res[:, :ML_DV] / jnp.maximum(jnp.abs(res[:, ML_DV:]), math.exp(-GATE_CAP))
        kw = (kt.astype(F32) * jnp.exp(b_end - b_r + i_r - GATE_CAP)).astype(BF16)
        c_ref[...] = jnp.exp(b_end) * c_aug + jnp.dot(kw, v_aug, preferred_element_type=F32)
        return h

    for j in range(ncc):
        for backward in (False, True):
            jj = (ncc - 1 - j) if backward else j
            cs = slice(jj * T, (jj + 1) * T)
            chunk_step(qc_ref[0, 0, cs, :], ktc_ref[0, 0, :, cs], vc_ref[0, cs, :], rc_ref[0, 0, :, cs],
                       btc_ref[0, 0, cs, :], cb_ref if backward else cf_ref, backward)

    def latent_pair(j, accumulate):
        for backward in (False, True):
            jj = (nc - 1 - j) if backward else j
            rs = pl.ds(pl.multiple_of(jj * T, T), T)
            h = chunk_step(q_ref[0, 0, rs, :], kt_ref[0, 0, :, rs], v_ref[0, rs, :], r_ref[0, 0, :, rs],
                           bt_ref[0, 0, rs, :], cb_ref if backward else cf_ref, backward)
            if accumulate:
                h_ref[0, rs, :] = h_ref[0, rs, :] + h
            else:
                h_ref[0, rs, :] = h

    def first_half(j, carry):
        latent_pair(j, False)
        return carry

    def second_half(j, carry):
        latent_pair(j, True)
        return carry

    lax.fori_loop(0, nc // 2, first_half, 0, unroll=MLSTM_UNROLL)
    lax.fori_loop(nc // 2, nc, second_half, 0, unroll=MLSTM_UNROLL)


MLSTM_UNROLL = 4


def _mlstm(q, kt, v, gatet, qc, ktc, vc, gatetc, gate_b):
    B, H, L, dk = q.shape
    Lc = qc.shape[2]
    rows, cols = _gate_prep(gatet, gate_b, min(L, 1024))
    rows_c, cols_c = _gate_prep(gatetc, gate_b, Lc)
    seq = lambda n: [pl.BlockSpec((1, 1, n, dk), lambda b, h: (b, h, 0, 0)),
                     pl.BlockSpec((1, 1, dk, n), lambda b, h: (b, h, 0, 0)),
                     pl.BlockSpec((1, n, ML_DV), lambda b, h: (b, 0, h)),
                     pl.BlockSpec((1, 1, 8, n), lambda b, h: (b, h, 0, 0)),
                     pl.BlockSpec((1, 1, n, 8), lambda b, h: (b, h, 0, 0))]
    return pl.pallas_call(
        _mlstm_kernel,
        grid=(B, H),
        in_specs=seq(L) + seq(Lc),
        out_specs=pl.BlockSpec((1, L, ML_DV), lambda b, h: (b, 0, h)),
        out_shape=jax.ShapeDtypeStruct((B, L, ML_WIDTH), F32),
        scratch_shapes=[pltpu.VMEM((dk, 2 * ML_DV), F32), pltpu.VMEM((dk, 2 * ML_DV), F32)],
        compiler_params=_cparams(("parallel", "parallel")),
        name="mlstm_scan",
    )(q, kt, v, rows, cols, qc, ktc, vc, rows_c, cols_c)


def _out_proj_kernel(yhy_ref, hs_ref, o_ref, x_ref, er_ref, ec_ref, g1_ref, sh2_ref, sc2_ref,
                     mln_ref, post1_ref, pre2_ref, wout_ref, wr_ref, wrt_ref,
                     x1_ref, h2_ref, aff_ref, afft_ref):
    hs = hs_ref[0]
    parts = []
    for hd in range(ML_HEADS):
        hh = hs[:, hd * ML_DV:(hd + 1) * ML_DV]
        parts.append(hh * lax.rsqrt(jnp.mean(hh * hh, axis=-1, keepdims=True) + EPS))
    hn = jnp.concatenate(parts, axis=-1) * mln_ref[...]
    y_ml = hn * jax.nn.sigmoid(o_ref[0])
    yo_hy = [lax.dot_general(yhy_ref[0, 0, pl.ds(j, HY_WIDTH, stride=TIME_ROWS), :].astype(BF16),
                             wout_ref[:HY_WIDTH, :], (((0,), (0,)), ((), ())), preferred_element_type=F32)
             for j in range(TIME_ROWS)]
    yo = (jnp.concatenate(yo_hy, axis=0)
          + jnp.dot(y_ml.astype(BF16), wout_ref[HY_WIDTH:, :], preferred_element_type=F32))
    xf = x_ref[0] + _pe_tile(er_ref[...], ec_ref[...])
    x1 = xf + g1_ref[0] * _rms(yo, post1_ref[...])
    x1_ref[0] = x1
    h2f = _rms(x1, pre2_ref[...]) * (1.0 + sc2_ref[0]) + sh2_ref[0]
    _store_row_tiles(h2_ref, h2f)
    h2 = h2f.astype(BF16)
    logits = jnp.dot(h2, wr_ref[...], preferred_element_type=F32)
    ex = jnp.exp(logits - jnp.max(logits, axis=-1, keepdims=True))
    aff_ref[0] = ex / jnp.sum(ex, axis=-1, keepdims=True)
    logits_t = lax.dot_general(wrt_ref[...], h2, (((1,), (1,)), ((), ())), preferred_element_type=F32)
    ext = jnp.exp(logits_t - jnp.max(logits_t, axis=0, keepdims=True))
    afft_ref[0] = ext / jnp.sum(ext, axis=0, keepdims=True)


def _out_proj(y_hy, hsum, o, x, tabs, g1, sh2, sc2, ml_norm, post1, pre2, w_out, w_router, tm):
    B, L, D = x.shape
    er, ec = tabs
    E = w_router.shape[1]
    full = lambda a: pl.BlockSpec(a.shape, lambda b, i: (0,) * a.ndim)
    tok = lambda w: pl.BlockSpec((1, tm, w), lambda b, i: (b, i, 0))
    modspec = pl.BlockSpec((1, 1, D), lambda b, i: (b, 0, 0))
    wout = w_out.astype(BF16)
    wr = w_router.astype(BF16)
    wrt = w_router.T.astype(BF16)
    return pl.pallas_call(
        _out_proj_kernel,
        grid=(B, L // tm),
        in_specs=[pl.BlockSpec((1, 1, HY_WIDTH * TIME_ROWS, LANES), lambda b, i: (b, i, 0, 0)),
                  tok(ML_WIDTH), tok(ML_WIDTH), tok(D),
                  pl.BlockSpec((tm // GRID_W, D // 2), lambda b, i: (i, 0)), full(ec),
                  modspec, modspec, modspec, full(ml_norm), full(post1), full(pre2),
                  full(wout), full(wr), full(wrt)],
        out_specs=[tok(D), pl.BlockSpec((1, tm * (D // LANES), LANES), lambda b, i: (b, i, 0)), tok(E),
                   pl.BlockSpec((1, E, tm), lambda b, i: (b, 0, i))],
        out_shape=[jax.ShapeDtypeStruct((B, L, D), F32),
                   jax.ShapeDtypeStruct((B, L * (D // LANES), LANES), F32),
                   jax.ShapeDtypeStruct((B, L, E), F32), jax.ShapeDtypeStruct((B, E, L), F32)],
        compiler_params=_cparams(("parallel", "parallel")),
        name="out_proj_router",
    )(y_hy, hsum, o, x, er, ec, g1, sh2, sc2, ml_norm, post1, pre2, wout, wr, wrt)


LANES = 128
ROW_GROUP = 16
SELECT_FAST_SLOTS = 32


def _store_row_tiles(ref, val):
    n, nt = val.shape[0], val.shape[1] // LANES
    for c in range(nt):
        ref[0, pl.ds(c, n, stride=nt), :] = val[:, c * LANES:(c + 1) * LANES]


def _load_row_tiles(ref, c, n, nt):
    return ref[0, pl.ds(c, n, stride=nt), :]


def _select_kernel(afft_ref, idx_ref, sel_ref, *, cap):
    E, L = afft_ref.shape[1], afft_ref.shape[2]
    idx_ref[...] = jnp.zeros_like(idx_ref)
    aff = afft_ref[0]
    iota = lax.broadcasted_iota(jnp.int32, (E, L), 1)
    count = lambda ind: jnp.sum(ind, axis=1, keepdims=True)
    count_ge = lambda th: count(jnp.where(aff >= th, 1.0, 0.0))
    pow2 = lambda j: pltpu.bitcast((j - 24) << 23, F32)

    def estep(_, c):
        lo, hi = c
        mid = (lo + hi) >> 1
        ok = count_ge(pow2(mid)) >= cap
        return jnp.where(ok, mid, lo), jnp.where(ok, hi, mid)

    jlo, jhi = lax.fori_loop(0, 7, estep, (jnp.full((E, 1), 24, jnp.int32),
                                           jnp.full((E, 1), 152, jnp.int32)))

    def vstep(_, c):
        lo, hi = c
        mid = lo + (hi - lo) * 0.5
        ok = count_ge(mid) >= cap
        return jnp.where(ok, mid, lo), jnp.where(ok, hi, mid)

    lo, hi = lax.fori_loop(0, 40, vstep, (pow2(jlo), pow2(jhi)))
    gt = jnp.where(aff >= hi, 1.0, 0.0)
    eq = jnp.where(aff >= lo, 1.0, 0.0) - gt
    need = cap - count(gt)

    def istep(_, c):
        lo, hi = c
        mid = (lo + hi) >> 1
        ok = count(jnp.where(iota <= mid, eq, 0.0)) >= need
        return jnp.where(ok, lo, mid), jnp.where(ok, mid, hi)

    _, last = lax.fori_loop(0, L.bit_length() - 1, istep,
                            (jnp.full((E, 1), -1, jnp.int32), jnp.full((E, 1), L - 1, jnp.int32)))
    sel_ref[...] = gt + jnp.where(iota <= last, eq, 0.0)

    T = LANES
    r_i = lax.broadcasted_iota(jnp.int32, (T, T), 0)
    c_i = lax.broadcasted_iota(jnp.int32, (T, T), 1)
    before = (r_i < c_i).astype(BF16)
    slot_f = r_i.astype(F32)
    lane_f = lax.broadcasted_iota(jnp.int32, (1, T), 1).astype(F32)
    esel = lax.broadcasted_iota(jnp.int32, (E, 1), 0)

    def group(g, off):
        s = sel_ref[:, pl.ds(pl.multiple_of(g * T, T), T)]
        rank = jnp.dot(s.astype(BF16), before, preferred_element_type=F32)
        tok = (lane_f + jnp.asarray(g * T, F32)) * s
        cnt = jnp.sum(s, axis=1, keepdims=True)

        def emit(n_slots):
            for e in range(E):
                hit = rank[e:e + 1, :] == slot_f[:n_slots]
                ids = jnp.sum(jnp.where(hit, tok[e:e + 1, :], 0.0), axis=1, keepdims=True)
                o = jnp.sum(jnp.where(esel == e, off, 0))
                idx_ref[0, e, pl.ds(o, n_slots), :] = ids.astype(jnp.int32)

        few = jnp.max(cnt) <= SELECT_FAST_SLOTS
        pl.when(few)(lambda: emit(SELECT_FAST_SLOTS))
        pl.when(jnp.logical_not(few))(lambda: emit(T))
        return off + cnt.astype(jnp.int32)

    lax.fori_loop(0, L // T, group, jnp.zeros((E, 1), jnp.int32))


def _route_select(afft, cap):
    B, E, L = afft.shape
    idx = pl.pallas_call(
        functools.partial(_select_kernel, cap=cap),
        grid=(B,),
        in_specs=[pl.BlockSpec((1, E, L), lambda b: (b, 0, 0))],
        out_specs=pl.BlockSpec((1, E, cap + LANES, 1), lambda b: (b, 0, 0, 0)),
        out_shape=jax.ShapeDtypeStruct((B, E, cap + LANES, 1), jnp.int32),
        scratch_shapes=[pltpu.VMEM((E, L), F32)],
        compiler_params=_cparams(("parallel",)),
        name="route_select",
    )(afft)
    return idx[:, :, :cap, 0]


def _tile_rows(i, nt):
    return pl.ds(pl.multiple_of(i * nt, nt), nt)


def _gather_kernel(idx_ref, h_ref, aff_ref, xs_ref, gs_ref, buf_ref):
    cap, D = xs_ref.shape[1], xs_ref.shape[2]
    nt = D // LANES

    def body(s, c):
        base = s * ROW_GROUP
        rows = [idx_ref[0, 0, base + k] for k in range(ROW_GROUP)]
        vals = [h_ref[0, _tile_rows(r, nt), :] for r in rows]
        gates = [aff_ref[0, pl.ds(r, 1), :] for r in rows]
        for k in range(ROW_GROUP):
            buf_ref[_tile_rows(base + k, nt), :] = vals[k]
            gs_ref[0, pl.ds(base + k, 1), :] = gates[k]
        return c

    lax.fori_loop(0, cap // ROW_GROUP, body, 0)
    for c in range(nt):
        xs_ref[0, :, c * LANES:(c + 1) * LANES] = buf_ref[pl.ds(c, cap, stride=nt), :].astype(BF16)


def _route_gather(idx, h2t, aff, D):
    B, E, cap = idx.shape
    nt = D // LANES
    L = h2t.shape[1] // nt
    return pl.pallas_call(
        _gather_kernel,
        grid=(B, E),
        in_specs=[pl.BlockSpec((1, 1, cap), lambda b, e: (b * E + e, 0, 0), memory_space=pltpu.SMEM),
                  pl.BlockSpec((1, L * nt, LANES), lambda b, e: (b, 0, 0), pipeline_mode=pl.Buffered(1)),
                  pl.BlockSpec((1, L, E), lambda b, e: (b, 0, 0))],
        out_specs=[pl.BlockSpec((1, cap, D), lambda b, e: (e, b, 0)),
                   pl.BlockSpec((1, cap, E), lambda b, e: (e, b, 0))],
        out_shape=[jax.ShapeDtypeStruct((E, B * cap, D), BF16),
                   jax.ShapeDtypeStruct((E, B * cap, E), F32)],
        scratch_shapes=[pltpu.VMEM((cap * nt, LANES), F32)],
        compiler_params=_cparams(("arbitrary", "arbitrary")),
        name="route_gather",
    )(idx.reshape(B * E, 1, cap), h2t, aff)


def _combine_kernel(idx_ref, ye_ref, y_ref, buf_ref):
    cap, D = ye_ref.shape[1], ye_ref.shape[2]
    nt = D // LANES

    @pl.when(pl.program_id(1) == 0)
    def _():
        y_ref[...] = jnp.zeros_like(y_ref)

    for c in range(nt):
        buf_ref[pl.ds(c, cap, stride=nt), :] = ye_ref[0, :, c * LANES:(c + 1) * LANES]

    def body(s, c):
        base = s * ROW_GROUP
        rows = [idx_ref[0, 0, base + k] for k in range(ROW_GROUP)]
        vals = [y_ref[0, _tile_rows(rows[k], nt), :] + buf_ref[_tile_rows(base + k, nt), :]
                for k in range(ROW_GROUP)]
        for k in range(ROW_GROUP):
            y_ref[0, _tile_rows(rows[k], nt), :] = vals[k]
        return c

    lax.fori_loop(0, cap // ROW_GROUP, body, 0)


def _route_combine(idx, ye, L):
    B, E, cap = idx.shape
    D = ye.shape[2]
    nt = D // LANES
    assert ye.shape[0] == E + 1
    return pl.pallas_call(
        _combine_kernel,
        grid=(B, E),
        in_specs=[pl.BlockSpec((1, 1, cap), lambda b, e: (b * E + e, 0, 0), memory_space=pltpu.SMEM),
                  pl.BlockSpec((1, cap, D), lambda b, e: (e + 1, b, 0))],
        out_specs=pl.BlockSpec((1, L * nt, LANES), lambda b, e: (b, 0, 0), pipeline_mode=pl.Buffered(1)),
        out_shape=jax.ShapeDtypeStruct((B, L * nt, LANES), F32),
        scratch_shapes=[pltpu.VMEM((cap * nt, LANES), F32)],
        compiler_params=_cparams(("arbitrary", "arbitrary")),
        name="route_combine",
    )(idx.reshape(B * E, 1, cap), ye)


def _expert_kernel(xs_ref, g_ref, wg_ref, wu_ref, wd_ref, ye_ref, wgb_ref, wub_ref, wdb_ref):
    s = pl.program_id(0)
    f = pl.program_id(1)
    ne = pl.num_programs(0) - 1
    tf = wg_ref.shape[2]
    slot = lax.rem(s, 2)

    @pl.when(s < ne)
    def _():
        cols = pl.ds(pl.multiple_of(f * tf, tf), tf)
        wgb_ref[slot, :, cols] = wg_ref[0].astype(BF16)
        wub_ref[slot, :, cols] = wu_ref[0].astype(BF16)
        wdb_ref[slot, cols, :] = wd_ref[0].astype(BF16)

    @pl.when(s == 0)
    def _():
        ye_ref[...] = jnp.zeros_like(ye_ref)

    @pl.when(s > 0)
    def _():
        e = s - 1
        prev = 1 - slot
        mt = xs_ref.shape[1] // FFN_M_SPLIT
        for mi in range(FFN_M_SPLIT):
            rs = slice(mi * mt, (mi + 1) * mt)
            xs = xs_ref[0, rs, :]
            a = jnp.dot(xs, wgb_ref[prev], preferred_element_type=F32)
            u = jnp.dot(xs, wub_ref[prev], preferred_element_type=F32)
            hmid = (a * jax.nn.sigmoid(a) * u).astype(BF16)
            out = jnp.dot(hmid, wdb_ref[prev], preferred_element_type=F32)
            gs = g_ref[0, rs, :]
            esel = lax.broadcasted_iota(jnp.int32, gs.shape, 1)
            ye_ref[0, rs, :] = out * jnp.sum(jnp.where(esel == e, gs, 0.0), axis=1, keepdims=True)


FFN_M_SPLIT = 2


def _expert_ffn(xs, gs, w_gate, w_up, w_down, tf):
    E, M, D = xs.shape
    F = w_gate.shape[2]
    nf = F // tf
    mt = M // nf
    behind = lambda s: jnp.maximum(s - 1, 0)
    ahead = lambda s: jnp.minimum(s, E - 1)
    return pl.pallas_call(
        _expert_kernel,
        grid=(E + 1, nf),
        in_specs=[pl.BlockSpec((1, mt, D), lambda s, f: (behind(s), f, 0)),
                  pl.BlockSpec((1, mt, E), lambda s, f: (behind(s), f, 0)),
                  pl.BlockSpec((1, D, tf), lambda s, f: (ahead(s), 0, f)),
                  pl.BlockSpec((1, D, tf), lambda s, f: (ahead(s), 0, f)),
                  pl.BlockSpec((1, tf, D), lambda s, f: (ahead(s), f, 0))],
        out_specs=pl.BlockSpec((1, mt, D), lambda s, f: (s, f, 0)),
        out_shape=jax.ShapeDtypeStruct((E + 1, M, D), F32),
        scratch_shapes=[pltpu.VMEM((2, D, F), BF16), pltpu.VMEM((2, D, F), BF16), pltpu.VMEM((2, F, D), BF16)],
        compiler_params=_cparams(("arbitrary", "arbitrary")),
        name="expert_ffn",
    )(xs, gs, w_gate, w_up, w_down)


def _final_kernel(x1_ref, y2_ref, g2_ref, post2_ref, o_ref):
    n = x1_ref.shape[1]
    nt = x1_ref.shape[2] // LANES
    cols = [_load_row_tiles(y2_ref, c, n, nt) for c in range(nt)]
    ssq = sum(jnp.sum(y * y, axis=-1, keepdims=True) for y in cols)
    rstd = lax.rsqrt(ssq * (1.0 / (nt * LANES)) + EPS)
    for c in range(nt):
        cs = slice(c * LANES, (c + 1) * LANES)
        o_ref[0, :, cs] = x1_ref[0, :, cs] + g2_ref[0, :, cs] * (cols[c] * rstd * post2_ref[:, cs])


def _final(x1, y2t, g2, post2, tm):
    B, L, D = x1.shape
    tok = pl.BlockSpec((1, tm, D), lambda b, i: (b, i, 0))
    return pl.pallas_call(
        _final_kernel,
        grid=(B, L // tm),
        in_specs=[tok, pl.BlockSpec((1, tm * (D // LANES), LANES), lambda b, i: (b, i, 0)),
                  pl.BlockSpec((1, 1, D), lambda b, i: (b, 0, 0)),
                  pl.BlockSpec((1, D), lambda b, i: (0, 0))],
        out_specs=tok,
        out_shape=jax.ShapeDtypeStruct((B, L, D), F32),
        compiler_params=_cparams(("parallel", "parallel")),
        name="final_residual",
    )(x1, y2t, g2, post2)


def kernel(x, c, ctx, c_ctx, w_mod, b_mod, pre_norm1, post_norm1, pre_norm2, post_norm2, w_in, conv_w, conv_b, filt_w1, filt_b1, filt_w2, filt_b2, filt_w3, filt_b3, filt_w4, filt_freq, hyena_bias, ml_gate_b, ml_norm, w_out, w_router, w_exp_gate, w_exp_up, w_exp_down):
    B, L, D = x.shape
    depth = w_mod.shape[0]
    assert depth == 1, "single-layer block"
    li = 0
    tabs = _pe_tables(L // GRID_W, GRID_W, D)

    cc = jnp.concatenate([c, c_ctx[None], jnp.zeros((8 - B - 1, D), F32)], axis=0)
    mod = _modulation(cc, w_mod[li], b_mod[li])
    chunks = [mod[:, k * D:(k + 1) * D] for k in range(6)]
    sh1, sc1, g1, sh2, sc2, g2 = [m[:B, None, :] for m in chunks]
    csh1, csc1 = chunks[0][B:B + 1, None, :], chunks[1][B:B + 1, None, :]

    pre1 = pre_norm1[li][None, :]
    zhy, q, kt, v, o, gatet = _in_proj(x, tabs, sh1, sc1, pre1, w_in[li], IN_PROJ_TILE, True)
    qc, ktc, vc, _, gatetc = _in_proj(ctx, None, csh1, csc1, pre1, w_in[li], ctx.shape[1], False)

    kcirc = _hyena_filter(L, filt_w1[li], filt_b1[li], filt_w2[li], filt_b2[li], filt_w3[li],
                          filt_b3[li], filt_w4[li], filt_freq[li])
    y_hy = _hyena_conv(zhy, kcirc, conv_w[li], conv_b[li], hyena_bias[li])
    hsum = _mlstm(q, kt, v, gatet, qc, ktc, vc, gatetc, ml_gate_b[li])

    x1, h2t, aff, afft = _out_proj(y_hy, hsum, o, x, tabs, g1, sh2, sc2, ml_norm[li][None, :],
                                   post_norm1[li][None, :], pre_norm2[li][None, :], w_out[li],
                                   w_router[li], IN_PROJ_TILE)

    cap = CAP_FACTOR * L // N_EXPERTS
    idx = _route_select(afft, cap)
    xs, gs = _route_gather(idx, h2t, aff, D)
    ye = _expert_ffn(xs, gs, w_exp_gate[li], w_exp_up[li], w_exp_down[li], 512)
    y2t = _route_combine(idx, ye, L)

    return _final(x1, y2t, g2, post_norm2[li][None, :], TOKEN_TILE)
```

```python
import functools
import math

import jax
import jax.numpy as jnp
from jax import lax
from jax.experimental import pallas as pl
from jax.experimental.pallas import tpu as pltpu

D_MODEL = 1024
GRID_W = 64
HY_WIDTH = 512
ML_HEADS = 4
ML_DK = 64
ML_DV = 128
ML_WIDTH = ML_HEADS * ML_DV
HY_COLS = 3 * HY_WIDTH
QK_COLS = ML_HEADS * ML_DK
N_GATES = 4 * ML_HEADS
FILTER_EMB = 33
DECAY_TARGET = 1e-2
FAST_DECAY_PCT = 0.3
SLOW_DECAY_PCT = 1.5
CHUNK = 128
GATE_CAP = 15.0
N_EXPERTS = 16
CAP_FACTOR = 2
EPS = 1e-6

F32 = jnp.float32
BF16 = jnp.bfloat16

TOKEN_TILE = 512
TIME_ROWS = 8
IN_PROJ_TILE = 1024
VMEM_LIMIT = 56 * 1024 * 1024


def _cparams(sem):
    return pltpu.CompilerParams(dimension_semantics=sem, vmem_limit_bytes=VMEM_LIMIT)


def _rms(xf, g):
    return xf * lax.rsqrt(jnp.mean(xf * xf, axis=-1, keepdims=True) + EPS) * g


def _bdot(a, b):
    return jnp.dot(a.astype(BF16), b.astype(BF16), preferred_element_type=F32)


def _bdot_nt(a, b):
    return lax.dot_general(a.astype(BF16), b.astype(BF16), (((1,), (1,)), ((), ())),
                           preferred_element_type=F32)


def _pe_tables_kernel(omega_ref, er_ref, ec_ref):
    quarter = omega_ref.shape[1]
    om = omega_ref[...]
    for ref in (er_ref, ec_ref):
        n = ref.shape[0]
        pos = lax.broadcasted_iota(jnp.int32, (n, quarter), 0).astype(F32)
        ang = pos * om
        ref[:, :quarter] = jnp.sin(ang)
        ref[:, quarter:] = jnp.cos(ang)


def _pe_tables(rows, cols, dim):
    quarter = dim // 4
    omega = (1.0 / (10000.0 ** (jnp.arange(quarter, dtype=F32) / quarter)))[None, :]
    return pl.pallas_call(
        _pe_tables_kernel,
        out_shape=(jax.ShapeDtypeStruct((rows, dim // 2), F32),
                   jax.ShapeDtypeStruct((cols, dim // 2), F32)),
        name="pe_tables",
    )(omega)


def _pe_tile(er_blk, ec):
    nr, half = er_blk.shape
    row_part = jnp.broadcast_to(er_blk[:, None, :], (nr, GRID_W, half)).reshape(nr * GRID_W, half)
    col_part = jnp.broadcast_to(ec[None, :, :], (nr, GRID_W, half)).reshape(nr * GRID_W, half)
    return jnp.concatenate([row_part, col_part], axis=-1)


def _mod_kernel(c_ref, w_ref, b_ref, o_ref):
    c = c_ref[...]
    s = c * jax.nn.sigmoid(c)
    o_ref[...] = _bdot(s, w_ref[...]) + b_ref[...]


def _modulation(cc, w_mod, b_mod):
    rows, d = cc.shape
    n = w_mod.shape[1]
    tn = 512
    return pl.pallas_call(
        _mod_kernel,
        grid=(n // tn,),
        in_specs=[pl.BlockSpec((rows, d), lambda j: (0, 0)),
                  pl.BlockSpec((d, tn), lambda j: (0, j)),
                  pl.BlockSpec((1, tn), lambda j: (0, j))],
        out_specs=pl.BlockSpec((rows, tn), lambda j: (0, j)),
        out_shape=jax.ShapeDtypeStruct((rows, n), F32),
        compiler_params=_cparams(("arbitrary",)),
        name="modulation",
    )(cc, w_mod, b_mod[None, :])


def _in_proj_kernel(*refs, with_hyena):
    if with_hyena:
        (x_ref, er_ref, ec_ref, sh_ref, sc_ref, g_ref, wn_ref, wt_ref,
         zt_ref, q_ref, kt_ref, v_ref, o_ref, gatet_ref) = refs
        xf = x_ref[0] + _pe_tile(er_ref[...], ec_ref[...])
    else:
        (x_ref, sh_ref, sc_ref, g_ref, wn_ref, wt_ref,
         q_ref, kt_ref, v_ref, o_ref, gatet_ref) = refs
        xf = x_ref[0]
    h = _rms(xf, g_ref[...]) * (1.0 + sc_ref[0]) + sh_ref[0]
    hb = h.astype(BF16)
    z = jnp.dot(hb, wn_ref[...], preferred_element_type=F32)
    qs = z[:, :QK_COLS] * (ML_DK ** -0.5)
    for hd in range(ML_HEADS):
        q_ref[0, hd] = qs[:, hd * ML_DK:(hd + 1) * ML_DK].astype(BF16)
    v_ref[0] = z[:, QK_COLS:QK_COLS + ML_WIDTH].astype(BF16)
    o_ref[0] = z[:, QK_COLS + ML_WIDTH:]
    zt = lax.dot_general(wt_ref[...], hb, (((1,), (1,)), ((), ())), preferred_element_type=F32)
    off = 0
    if with_hyena:
        for j in range(zt.shape[1] // LANES):
            zt_ref[0, 0, pl.ds(j, HY_COLS, stride=TIME_ROWS), :] = zt[:HY_COLS, j * LANES:(j + 1) * LANES]
        off = HY_COLS
    for hd in range(ML_HEADS):
        kt_ref[0, hd] = zt[off + hd * ML_DK:off + (hd + 1) * ML_DK, :].astype(BF16)
    gatet_ref[0] = zt[off + QK_COLS:, :]


def _in_proj(x, tabs, sh, sc, g, w_in, tm, with_hyena):
    B, L, D = x.shape
    w_hy = w_in[:, :HY_COLS]
    w_q = w_in[:, HY_COLS:HY_COLS + QK_COLS]
    w_k = w_in[:, HY_COLS + QK_COLS:HY_COLS + 2 * QK_COLS]
    w_vo = w_in[:, HY_COLS + 2 * QK_COLS:HY_COLS + 2 * QK_COLS + 2 * ML_WIDTH]
    w_g = w_in[:, HY_COLS + 2 * QK_COLS + 2 * ML_WIDTH:]
    wn = jnp.concatenate([w_q, w_vo], axis=1).astype(BF16)
    wt = jnp.concatenate(([w_hy] if with_hyena else []) + [w_k, w_g], axis=1).T.astype(BF16)
    nb = sh.shape[0]
    mod_map = (lambda b, i: (b, 0, 0)) if nb > 1 else (lambda b, i: (0, 0, 0))
    full = lambda a: pl.BlockSpec(a.shape, lambda b, i: (0,) * a.ndim)
    in_specs = [pl.BlockSpec((1, tm, D), lambda b, i: (b, i, 0))]
    args = [x]
    if with_hyena:
        er, ec = tabs
        in_specs += [pl.BlockSpec((tm // GRID_W, D // 2), lambda b, i: (i, 0)), full(ec)]
        args += [er, ec]
    in_specs += [pl.BlockSpec((1, 1, D), mod_map), pl.BlockSpec((1, 1, D), mod_map),
                 full(g), full(wn), full(wt)]
    args += [sh, sc, g, wn, wt]
    out_shape, out_specs = [], []
    if with_hyena:
        assert tm == TIME_ROWS * LANES
        out_shape.append(jax.ShapeDtypeStruct((B, L // tm, HY_COLS * TIME_ROWS, LANES), F32))
        out_specs.append(pl.BlockSpec((1, 1, HY_COLS * TIME_ROWS, LANES), lambda b, i: (b, i, 0, 0)))
    out_shape += [jax.ShapeDtypeStruct((B, ML_HEADS, L, ML_DK), BF16),
                  jax.ShapeDtypeStruct((B, ML_HEADS, ML_DK, L), BF16),
                  jax.ShapeDtypeStruct((B, L, ML_WIDTH), BF16),
                  jax.ShapeDtypeStruct((B, L, ML_WIDTH), F32),
                  jax.ShapeDtypeStruct((B, N_GATES, L), F32)]
    out_specs += [pl.BlockSpec((1, ML_HEADS, tm, ML_DK), lambda b, i: (b, 0, i, 0)),
                  pl.BlockSpec((1, ML_HEADS, ML_DK, tm), lambda b, i: (b, 0, 0, i)),
                  pl.BlockSpec((1, tm, ML_WIDTH), lambda b, i: (b, i, 0)),
                  pl.BlockSpec((1, tm, ML_WIDTH), lambda b, i: (b, i, 0)),
                  pl.BlockSpec((1, N_GATES, tm), lambda b, i: (b, 0, i))]
    return pl.pallas_call(
        functools.partial(_in_proj_kernel, with_hyena=with_hyena),
        grid=(B, L // tm),
        in_specs=in_specs,
        out_specs=out_specs,
        out_shape=out_shape,
        compiler_params=_cparams(("parallel", "parallel")),
        name="in_proj_hy" if with_hyena else "in_proj_ctx",
    )(*args)


FILT_TILE = 1024
FILT_CBLK = 128


def _filter_kernel(w1a_ref, w1b_ref, w1c_ref, b1_ref, w2_ref, b2_ref, w3_ref, b3_ref, fr_ref,
                   w4f_ref, w4b_ref, dl_ref, mir_ref, k_ref, hf_ref, *, L):
    bands = (FILTER_EMB - 1) // 2

    @pl.when(pl.program_id(0) == 0)
    def _():
        fk = (1e-4 + lax.broadcasted_iota(jnp.int32, (bands, 1), 0).astype(F32)
              * ((bands - 1 - 1e-4) / (bands - 1)))
        fr = fr_ref[...]
        for j in range(L // FILT_TILE):
            pos = (lax.broadcasted_iota(jnp.int32, (1, FILT_TILE), 1) + j * FILT_TILE).astype(F32)
            tl = pos * (1.0 / (L - 1))
            ang = fk * (pos * (2.0 * math.pi / L))
            pre = (w1a_ref[...].astype(F32) * tl.astype(BF16).astype(F32)
                   + _bdot(w1b_ref[...], jnp.cos(ang)) + _bdot(w1c_ref[...], -jnp.sin(ang)))
            h = jnp.sin(fr * (pre + b1_ref[...]))
            h = jnp.sin(fr * (_bdot(w2_ref[...], h) + b2_ref[...]))
            h = jnp.sin(fr * (_bdot(w3_ref[...], h) + b3_ref[...]))
            hf_ref[:, j * FILT_TILE:(j + 1) * FILT_TILE] = h.astype(BF16)

    pos = lax.broadcasted_iota(jnp.int32, (1, L), 1).astype(F32)
    decay = jnp.exp(-(pos * (1.0 / (L - 1))) * dl_ref[...])
    hf = hf_ref[...]
    k_ref[:, :L] = jnp.dot(w4f_ref[...], hf, preferred_element_type=F32) * decay
    gb = (jnp.dot(w4b_ref[...], hf, preferred_element_type=F32) * decay).astype(BF16)
    nblk = L // LANES
    for j in range(nblk):
        src = gb[:, (nblk - 1 - j) * LANES:(nblk - j) * LANES]
        nxt = gb[:, (nblk - j) * LANES:(nblk - j + 1) * LANES] if j > 0 else jnp.zeros_like(src)
        k_ref[:, L + j * LANES:L + (j + 1) * LANES] = jnp.dot(
            jnp.concatenate([src, nxt], axis=1), mir_ref[...], preferred_element_type=F32)


def _hyena_filter(L, w1, b1, w2, b2, w3, b3, w4, freq):
    hid = w2.shape[0]
    bands = (FILTER_EMB - 1) // 2
    col = lambda a: a[:, None]
    w1t = w1.T.astype(BF16)
    min_decay = math.log(DECAY_TARGET) / SLOW_DECAY_PCT
    max_decay = math.log(DECAY_TARGET) / FAST_DECAY_PCT
    dl = jnp.abs(jnp.linspace(min_decay, max_decay, HY_WIDTH, dtype=F32))[:, None]
    w4t = w4.T.astype(BF16)
    assert DFT_PASSES == 1, "the mirrored taps are kept at the bf16 precision a single-pass DFT reads"
    lane = jnp.arange(LANES)
    mir = jnp.concatenate([(lane[:, None] + lane[None, :] == LANES),
                           (lane[:, None] == 0) & (lane[None, :] == 0)], axis=0).astype(BF16)
    full = lambda a: pl.BlockSpec(a.shape, lambda i: (0,) * a.ndim)
    args = [w1t[:, 0:1], w1t[:, 1:1 + bands], w1t[:, 1 + bands:], col(b1), w2.T.astype(BF16), col(b2),
            w3.T.astype(BF16), col(b3), col(freq)]
    return pl.pallas_call(
        functools.partial(_filter_kernel, L=L),
        grid=(HY_WIDTH // FILT_CBLK,),
        in_specs=[full(a) for a in args] + [
            pl.BlockSpec((FILT_CBLK, hid), lambda i: (i, 0)),
            pl.BlockSpec((FILT_CBLK, hid), lambda i: (HY_WIDTH // FILT_CBLK + i, 0)),
            pl.BlockSpec((FILT_CBLK, 1), lambda i: (i, 0)), full(mir)],
        out_specs=pl.BlockSpec((FILT_CBLK, 2 * L), lambda i: (i, 0)),
        out_shape=jax.ShapeDtypeStruct((HY_WIDTH, 2 * L), F32),
        scratch_shapes=[pltpu.VMEM((hid, L), BF16)],
        compiler_params=_cparams(("arbitrary",)),
        name="hyena_filter",
    )(*args, w4t, w4t, dl, mir)


FFT_N = 128
HY_CBLK = 32
HY_GROUP = 8
HY_UNROLL = 4
DFT_PASSES = 1


def _dft_constants(n1_data):
    import numpy as np
    n = FFT_N
    k = np.arange(n)
    ang = -2.0 * np.pi * ((k[:, None] * k[None, :]) % n) / n
    fre, fim = np.cos(ang), np.sin(ang)
    m = n1_data
    fa_d = np.block([[fre[:, :m], -fim[:, :m]], [fim[:, :m], fre[:, :m]]])
    fa_f = np.concatenate([fre, fim], axis=0)
    fb = np.block([[fre, fim], [-fim, fre]])
    fbi = np.block([[fre, -fim], [fim, fre]])
    fc = np.block([[fre[:m, :], fim[:m, :]], [-fim[:m, :], fre[:m, :]]]) / (n * n)
    tang = -2.0 * np.pi * (k[:, None] * k[None, :]) / (n * n)
    tw = np.stack([np.cos(tang), np.sin(tang)])

    def hilo(a):
        a32 = jnp.asarray(a, F32)
        hi = a32.astype(BF16)
        lo = (a32 - hi.astype(F32)).astype(BF16)
        return jnp.stack([hi, lo])

    return hilo(fa_d), hilo(fa_f), hilo(fb), hilo(fbi), hilo(fc), jnp.asarray(tw, F32)


def _mm_const_lhs(c_ref, d):
    dh = d.astype(BF16)
    acc = jnp.dot(c_ref[0], dh, preferred_element_type=F32)
    if DFT_PASSES == 3:
        dl = (d - dh.astype(F32)).astype(BF16)
        acc = acc + (jnp.dot(c_ref[0], dl, preferred_element_type=F32)
                     + jnp.dot(c_ref[1], dh, preferred_element_type=F32))
    return acc


def _mm_const_rhs(d, c_ref):
    dh = d.astype(BF16)
    acc = jnp.dot(dh, c_ref[0], preferred_element_type=F32)
    if DFT_PASSES == 3:
        dl = (d - dh.astype(F32)).astype(BF16)
        acc = acc + (jnp.dot(dl, c_ref[0], preferred_element_type=F32)
                     + jnp.dot(dh, c_ref[1], preferred_element_type=F32))
    return acc


def _cmul(are, aim, bre, bim):
    return are * bre - aim * bim, are * bim + aim * bre


def _hyena_conv_kernel(x0_ref, x1_ref, v_ref, kc_ref, w0_ref, w1_ref, wv_ref, b0_ref, b1_ref, bv_ref,
                       hb_ref, fad_ref, faf_ref, fb_ref, fbi_ref, fc_ref, tw_ref,
                       o_ref, u_ref, s_ref, ks_ref):
    n = FFT_N
    cb = kc_ref.shape[0]
    nb, nt = x0_ref.shape[0], x0_ref.shape[1]
    m = nt * TIME_ROWS

    def chan(ref, b, c):
        return ref[b, :, pl.ds(pl.multiple_of(c * TIME_ROWS, TIME_ROWS), TIME_ROWS), :].reshape(m, n)

    sub = lax.broadcasted_iota(jnp.int32, (m, n), 0)
    lane = lax.broadcasted_iota(jnp.int32, (m, n), 1)
    tre, tim = tw_ref[0], tw_ref[1]

    def sconv(z, w_ref, b_ref, c):
        a = pltpu.roll(z, 1, axis=1)
        prev = jnp.where(lane == 0, jnp.where(sub == 0, 0.0, pltpu.roll(a, 1, axis=0)), a)
        a2 = pltpu.roll(z, n - 1, axis=1)
        nxt = jnp.where(lane == n - 1, jnp.where(sub == m - 1, 0.0, pltpu.roll(a2, m - 1, axis=0)), a2)
        return prev * w_ref[0, c] + z * w_ref[1, c] + nxt * w_ref[2, c] + b_ref[c]

    def spectrum_rows(res):
        outs = []
        for h in range(2):
            are, aim = _cmul(res[:n, h * n:(h + 1) * n], res[n:, h * n:(h + 1) * n], tre, tim)
            outs.append(jnp.concatenate([are, aim], axis=1))
        return outs

    def fwd_pair(p, carry):
        c0 = 2 * p
        us = []
        for c in (c0, c0 + 1):
            ub = []
            for b in range(nb):
                x1c = sconv(chan(x1_ref, b, c), w1_ref, b1_ref, c)
                vc = sconv(chan(v_ref, b, c), wv_ref, bv_ref, c)
                u = x1c * vc
                u_ref[b, c] = u
                ub.append(u)
            us.append(ub)
        wd = jnp.concatenate([jnp.concatenate([us[0][b], us[1][b]], axis=1) for b in range(nb)], axis=0)
        sa, sb = spectrum_rows(_mm_const_lhs(fad_ref, wd))
        s_ref[c0] = sa
        s_ref[c0 + 1] = sb
        wk = jnp.concatenate([kc_ref[c0], kc_ref[c0 + 1]], axis=1)
        ka, kb = spectrum_rows(_mm_const_lhs(faf_ref, wk))
        ks_ref[c0] = ka
        ks_ref[c0 + 1] = kb
        return carry

    lax.fori_loop(0, cb // 2, fwd_pair, 0, unroll=HY_UNROLL)

    def mid_group(g, carry):
        gs = pl.ds(pl.multiple_of(g * HY_GROUP, HY_GROUP), HY_GROUP)
        x = _mm_const_rhs(s_ref[gs].reshape(HY_GROUP * n, 2 * n), fb_ref)
        k = _mm_const_rhs(ks_ref[gs].reshape(HY_GROUP * n, 2 * n), fb_ref)
        yre, yim = _cmul(x[:, :n], x[:, n:], k[:, :n], k[:, n:])
        vv = _mm_const_rhs(jnp.concatenate([yre, yim], axis=1), fbi_ref).reshape(HY_GROUP, n, 2 * n)
        vre, vim = _cmul(vv[:, :, :n], vv[:, :, n:], tre[None], -tim[None])
        s_ref[gs] = jnp.concatenate([vre, vim], axis=2)
        return carry

    lax.fori_loop(0, cb // HY_GROUP, mid_group, 0, unroll=HY_UNROLL)

    def inv_pair(p, carry):
        c0 = 2 * p
        sa, sb = s_ref[c0], s_ref[c0 + 1]
        wd = jnp.concatenate([jnp.concatenate([sa[:, :n], sb[:, :n]], axis=1),
                              jnp.concatenate([sa[:, n:], sb[:, n:]], axis=1)], axis=0)
        res = _mm_const_lhs(fc_ref, wd)
        for h, c in enumerate((c0, c0 + 1)):
            for b in range(nb):
                y = res[b * m:(b + 1) * m, h * n:(h + 1) * n]
                x0c = sconv(chan(x0_ref, b, c), w0_ref, b0_ref, c)
                out = x0c * (y + hb_ref[c] * u_ref[b, c])
                o_ref[b, :, pl.ds(pl.multiple_of(c * TIME_ROWS, TIME_ROWS), TIME_ROWS), :] = out.reshape(
                    nt, TIME_ROWS, n)
        return carry

    lax.fori_loop(0, cb // 2, inv_pair, 0, unroll=HY_UNROLL)


def _hyena_conv(zt, kcirc, conv_w, conv_b, hy_bias):
    B, nt, _, n = zt.shape
    m = nt * TIME_ROWS
    L = m * n
    C = HY_WIDTH
    assert B == 2 and n == FFT_N and 2 * L == n * n, "complex packing of two samples over a 128 x 128 point transform"
    z4 = zt
    rows = HY_CBLK * TIME_ROWS
    k3 = kcirc.reshape(C, n, n)
    cw = conv_w.reshape(3, 3 * C, 1, 1)
    cbias = conv_b.reshape(3 * C, 1, 1)
    hb = hy_bias.reshape(C, 1, 1)
    consts = _dft_constants(m)
    nblk = C // HY_CBLK
    zspec = lambda part: pl.BlockSpec((B, nt, rows, n), lambda i: (0, 0, part * nblk + i, 0))
    wspec = lambda part: pl.BlockSpec((3, HY_CBLK, 1, 1), lambda i: (0, part * nblk + i, 0, 0))
    bspec = lambda part: pl.BlockSpec((HY_CBLK, 1, 1), lambda i: (part * nblk + i, 0, 0))
    full = lambda a: pl.BlockSpec(a.shape, lambda i: (0,) * a.ndim)
    y = pl.pallas_call(
        _hyena_conv_kernel,
        grid=(nblk,),
        in_specs=[zspec(0), zspec(1), zspec(2), pl.BlockSpec((HY_CBLK, n, n), lambda i: (i, 0, 0)),
                  wspec(0), wspec(1), wspec(2), bspec(0), bspec(1), bspec(2), bspec(0)]
                 + [full(a) for a in consts],
        out_specs=pl.BlockSpec((B, nt, rows, n), lambda i: (0, 0, i, 0)),
        out_shape=jax.ShapeDtypeStruct((B, nt, C * TIME_ROWS, n), F32),
        scratch_shapes=[pltpu.VMEM((B, HY_CBLK, m, n), F32), pltpu.VMEM((HY_CBLK, n, 2 * n), F32),
                        pltpu.VMEM((HY_CBLK, n, 2 * n), F32)],
        compiler_params=_cparams(("parallel",)),
        name="hyena_conv",
    )(z4, z4, z4, k3, cw, cw, cw, cbias, cbias, cbias, hb, *consts)
    return y


def _split3(a):
    hi = a.astype(BF16)
    r1 = a - hi.astype(F32)
    mid = r1.astype(BF16)
    lo = (r1 - mid.astype(F32)).astype(BF16)
    return hi, mid, lo


def _exact_dot_right(a, tri_bf):
    hi, mid, lo = _split3(a)
    d = lambda p: jnp.dot(p, tri_bf, preferred_element_type=F32)
    return (d(lo) + d(mid)) + d(hi)


def _soft_gates(g):
    g = GATE_CAP * jnp.tanh(g * (1.0 / GATE_CAP))
    logsig = jnp.minimum(g, 0.0) - jnp.log1p(jnp.exp(-jnp.abs(g)))
    return g, logsig


def _gate_prep_kernel(gt_ref, gbt_ref, rows_ref, cols_ref):
    T = CHUNK
    n = gt_ref.shape[2]
    up_bf = (lax.broadcasted_iota(jnp.int32, (T, T), 1) >= lax.broadcasted_iota(jnp.int32, (T, T), 0)).astype(BF16)
    out8 = lax.broadcasted_iota(jnp.int32, (8, 1), 0)
    cap, ls = _soft_gates(gt_ref[0] + gbt_ref[...])
    H = ML_HEADS
    head = lambda hd: jnp.where(out8 == 0, cap[hd:hd + 1], jnp.where(out8 == 1, ls[H + hd:H + hd + 1],
                                jnp.where(out8 == 2, cap[2 * H + hd:2 * H + hd + 1],
                                          jnp.where(out8 == 3, ls[3 * H + hd:3 * H + hd + 1], 0.0))))
    base = jnp.concatenate([head(hd) for hd in range(H)], axis=0)
    kind = jnp.concatenate([out8] * H, axis=0)
    for j in range(n // T):
        blk = base[:, j * T:(j + 1) * T]
        run = _exact_dot_right(blk, up_bf)
        suf = run[:, T - 1:T] - run + blk
        rows = jnp.where(kind == 1, run, jnp.where(kind == 3, suf, blk))
        cols = rows.T
        for hd in range(H):
            rows_ref[0, hd, :, j * T:(j + 1) * T] = rows[8 * hd:8 * (hd + 1)]
            cols_ref[0, hd, j * T:(j + 1) * T, :] = cols[:, 8 * hd:8 * (hd + 1)]


def _gate_prep(gatet, gate_b, tile):
    B, G, L = gatet.shape
    H = ML_HEADS
    return pl.pallas_call(
        _gate_prep_kernel,
        grid=(B, L // tile),
        in_specs=[pl.BlockSpec((1, G, tile), lambda b, i: (b, 0, i)),
                  pl.BlockSpec((G, 1), lambda b, i: (0, 0))],
        out_specs=[pl.BlockSpec((1, H, 8, tile), lambda b, i: (b, 0, 0, i)),
                   pl.BlockSpec((1, H, tile, 8), lambda b, i: (b, 0, i, 0))],
        out_shape=[jax.ShapeDtypeStruct((B, H, 8, L), F32), jax.ShapeDtypeStruct((B, H, L, 8), F32)],
        compiler_params=_cparams(("parallel", "parallel")),
        name="mlstm_gate_prep",
    )(gatet, gate_b[:, None])


def _mlstm_kernel(q_ref, kt_ref, v_ref, r_ref, bt_ref, qc_ref, ktc_ref, vc_ref, rc_ref, btc_ref,
                  h_ref, cf_ref, cb_ref):
    T = CHUNK
    L = q_ref.shape[2]
    Lc = qc_ref.shape[2]
    nc, ncc = L // T, Lc // T
    row = lax.broadcasted_iota(jnp.int32, (T, T), 0)
    col = lax.broadcasted_iota(jnp.int32, (T, T), 1)
    lo_mask = col <= row
    up_mask = col >= row
    ones_blk = jnp.ones((T, ML_DV), BF16)

    cf_ref[...] = jnp.zeros_like(cf_ref)
    cb_ref[...] = jnp.zeros_like(cb_ref)

    def chunk_step(q, kt, v, rows, cols, c_ref, backward):
        i_r = rows[2:3] if backward else rows[0:1]
        b_r = rows[3:4] if backward else rows[1:2]
        b_c = jnp.broadcast_to(cols[:, 3:4] if backward else cols[:, 1:2], (T, T))
        b_end = b_r[:, 0:1] if backward else b_r[:, T - 1:T]
        mask = up_mask if backward else lo_mask
        w_intra = jnp.exp(jnp.where(mask, b_c - b_r + i_r, -jnp.inf) - GATE_CAP)
        s = jnp.dot(q, kt, preferred_element_type=F32) * w_intra
        qe = (q.astype(F32) * jnp.exp(b_c[:, :ML_DK])).astype(BF16)
        v_aug = jnp.concatenate([v, ones_blk], axis=1)
        c_aug = c_ref[...]
        res = jnp.dot(jnp.concatenate([s.astype(BF16), qe], axis=1),
                      jnp.concatenate([v_aug, c_aug.astype(BF16)], axis=0),
                      preferred_element_type=F32)
        h = res[:, :ML_DV] / jnp.maximum(jnp.abs(res[:, ML_DV:]), math.exp(-GATE_CAP))
        kw = (kt.astype(F32) * jnp.exp(b_end - b_r + i_r - GATE_CAP)).astype(BF16)
        c_ref[...] = jnp.exp(b_end) * c_aug + jnp.dot(kw, v_aug, preferred_element_type=F32)
        return h

    for j in range(ncc):
        for backward in (False, True):
            jj = (ncc - 1 - j) if backward else j
            cs = slice(jj * T, (jj + 1) * T)
            chunk_step(qc_ref[0, 0, cs, :], ktc_ref[0, 0, :, cs], vc_ref[0, cs, :], rc_ref[0, 0, :, cs],
                       btc_ref[0, 0, cs, :], cb_ref if backward else cf_ref, backward)

    def latent_pair(j, accumulate):
        for backward in (False, True):
            jj = (nc - 1 - j) if backward else j
            rs = pl.ds(pl.multiple_of(jj * T, T), T)
            h = chunk_step(q_ref[0, 0, rs, :], kt_ref[0, 0, :, rs], v_ref[0, rs, :], r_ref[0, 0, :, rs],
                           bt_ref[0, 0, rs, :], cb_ref if backward else cf_ref, backward)
            if accumulate:
                h_ref[0, rs, :] = h_ref[0, rs, :] + h
            else:
                h_ref[0, rs, :] = h

    def first_half(j, carry):
        latent_pair(j, False)
        return carry

    def second_half(j, carry):
        latent_pair(j, True)
        return carry

    lax.fori_loop(0, nc // 2, first_half, 0, unroll=MLSTM_UNROLL)
    lax.fori_loop(nc // 2, nc, second_half, 0, unroll=MLSTM_UNROLL)


MLSTM_UNROLL = 4


def _mlstm(q, kt, v, gatet, qc, ktc, vc, gatetc, gate_b):
    B, H, L, dk = q.shape
    Lc = qc.shape[2]
    rows, cols = _gate_prep(gatet, gate_b, min(L, 1024))
    rows_c, cols_c = _gate_prep(gatetc, gate_b, Lc)
    seq = lambda n: [pl.BlockSpec((1, 1, n, dk), lambda b, h: (b, h, 0, 0)),
                     pl.BlockSpec((1, 1, dk, n), lambda b, h: (b, h, 0, 0)),
                     pl.BlockSpec((1, n, ML_DV), lambda b, h: (b, 0, h)),
                     pl.BlockSpec((1, 1, 8, n), lambda b, h: (b, h, 0, 0)),
                     pl.BlockSpec((1, 1, n, 8), lambda b, h: (b, h, 0, 0))]
    return pl.pallas_call(
        _mlstm_kernel,
        grid=(B, H),
        in_specs=seq(L) + seq(Lc),
        out_specs=pl.BlockSpec((1, L, ML_DV), lambda b, h: (b, 0, h)),
        out_shape=jax.ShapeDtypeStruct((B, L, ML_WIDTH), F32),
        scratch_shapes=[pltpu.VMEM((dk, 2 * ML_DV), F32), pltpu.VMEM((dk, 2 * ML_DV), F32)],
        compiler_params=_cparams(("parallel", "parallel")),
        name="mlstm_scan",
    )(q, kt, v, rows, cols, qc, ktc, vc, rows_c, cols_c)


def _out_proj_kernel(yhy_ref, hs_ref, o_ref, x_ref, er_ref, ec_ref, g1_ref, sh2_ref, sc2_ref,
                     mln_ref, post1_ref, pre2_ref, wout_ref, wr_ref, wrt_ref,
                     x1_ref, h2_ref, aff_ref, afft_ref):
    hs = hs_ref[0]
    parts = []
    for hd in range(ML_HEADS):
        hh = hs[:, hd * ML_DV:(hd + 1) * ML_DV]
        parts.append(hh * lax.rsqrt(jnp.mean(hh * hh, axis=-1, keepdims=True) + EPS))
    hn = jnp.concatenate(parts, axis=-1) * mln_ref[...]
    y_ml = hn * jax.nn.sigmoid(o_ref[0])
    yo_hy = [lax.dot_general(yhy_ref[0, 0, pl.ds(j, HY_WIDTH, stride=TIME_ROWS), :].astype(BF16),
                             wout_ref[:HY_WIDTH, :], (((0,), (0,)), ((), ())), preferred_element_type=F32)
             for j in range(TIME_ROWS)]
    yo = (jnp.concatenate(yo_hy, axis=0)
          + jnp.dot(y_ml.astype(BF16), wout_ref[HY_WIDTH:, :], preferred_element_type=F32))
    xf = x_ref[0] + _pe_tile(er_ref[...], ec_ref[...])
    x1 = xf + g1_ref[0] * _rms(yo, post1_ref[...])
    x1_ref[0] = x1
    h2f = _rms(x1, pre2_ref[...]) * (1.0 + sc2_ref[0]) + sh2_ref[0]
    _store_row_tiles(h2_ref, h2f)
    h2 = h2f.astype(BF16)
    logits = jnp.dot(h2, wr_ref[...], preferred_element_type=F32)
    ex = jnp.exp(logits - jnp.max(logits, axis=-1, keepdims=True))
    aff_ref[0] = ex / jnp.sum(ex, axis=-1, keepdims=True)
    logits_t = lax.dot_general(wrt_ref[...], h2, (((1,), (1,)), ((), ())), preferred_element_type=F32)
    ext = jnp.exp(logits_t - jnp.max(logits_t, axis=0, keepdims=True))
    afft_ref[0] = ext / jnp.sum(ext, axis=0, keepdims=True)


def _out_proj(y_hy, hsum, o, x, tabs, g1, sh2, sc2, ml_norm, post1, pre2, w_out, w_router, tm):
    B, L, D = x.shape
    er, ec = tabs
    E = w_router.shape[1]
    full = lambda a: pl.BlockSpec(a.shape, lambda b, i: (0,) * a.ndim)
    tok = lambda w: pl.BlockSpec((1, tm, w), lambda b, i: (b, i, 0))
    modspec = pl.BlockSpec((1, 1, D), lambda b, i: (b, 0, 0))
    wout = w_out.astype(BF16)
    wr = w_router.astype(BF16)
    wrt = w_router.T.astype(BF16)
    return pl.pallas_call(
        _out_proj_kernel,
        grid=(B, L // tm),
        in_specs=[pl.BlockSpec((1, 1, HY_WIDTH * TIME_ROWS, LANES), lambda b, i: (b, i, 0, 0)),
                  tok(ML_WIDTH), tok(ML_WIDTH), tok(D),
                  pl.BlockSpec((tm // GRID_W, D // 2), lambda b, i: (i, 0)), full(ec),
                  modspec, modspec, modspec, full(ml_norm), full(post1), full(pre2),
                  full(wout), full(wr), full(wrt)],
        out_specs=[tok(D), pl.BlockSpec((1, tm * (D // LANES), LANES), lambda b, i: (b, i, 0)), tok(E),
                   pl.BlockSpec((1, E, tm), lambda b, i: (b, 0, i))],
        out_shape=[jax.ShapeDtypeStruct((B, L, D), F32),
                   jax.ShapeDtypeStruct((B, L * (D // LANES), LANES), F32),
                   jax.ShapeDtypeStruct((B, L, E), F32), jax.ShapeDtypeStruct((B, E, L), F32)],
        compiler_params=_cparams(("parallel", "parallel")),
        name="out_proj_router",
    )(y_hy, hsum, o, x, er, ec, g1, sh2, sc2, ml_norm, post1, pre2, wout, wr, wrt)


LANES = 128
ROW_GROUP = 16
SELECT_FAST_SLOTS = 32


def _store_row_tiles(ref, val):
    n, nt = val.shape[0], val.shape[1] // LANES
    for c in range(nt):
        ref[0, pl.ds(c, n, stride=nt), :] = val[:, c * LANES:(c + 1) * LANES]


def _select_kernel(afft_ref, idx_ref, sel_ref, *, cap):
    E, L = afft_ref.shape[1], afft_ref.shape[2]
    idx_ref[...] = jnp.zeros_like(idx_ref)
    aff = afft_ref[0]
    iota = lax.broadcasted_iota(jnp.int32, (E, L), 1)
    count = lambda ind: jnp.sum(ind, axis=1, keepdims=True)
    count_ge = lambda th: count(jnp.where(aff >= th, 1.0, 0.0))
    pow2 = lambda j: pltpu.bitcast((j - 24) << 23, F32)

    def estep(_, c):
        lo, hi = c
        mid = (lo + hi) >> 1
        ok = count_ge(pow2(mid)) >= cap
        return jnp.where(ok, mid, lo), jnp.where(ok, hi, mid)

    jlo, jhi = lax.fori_loop(0, 7, estep, (jnp.full((E, 1), 24, jnp.int32),
                                           jnp.full((E, 1), 152, jnp.int32)))

    def vstep(_, c):
        lo, hi = c
        mid = lo + (hi - lo) * 0.5
        ok = count_ge(mid) >= cap
        return jnp.where(ok, mid, lo), jnp.where(ok, hi, mid)

    lo, hi = lax.fori_loop(0, 40, vstep, (pow2(jlo), pow2(jhi)))
    gt = jnp.where(aff >= hi, 1.0, 0.0)
    eq = jnp.where(aff >= lo, 1.0, 0.0) - gt
    need = cap - count(gt)

    def istep(_, c):
        lo, hi = c
        mid = (lo + hi) >> 1
        ok = count(jnp.where(iota <= mid, eq, 0.0)) >= need
        return jnp.where(ok, lo, mid), jnp.where(ok, mid, hi)

    _, last = lax.fori_loop(0, L.bit_length() - 1, istep,
                            (jnp.full((E, 1), -1, jnp.int32), jnp.full((E, 1), L - 1, jnp.int32)))
    sel_ref[...] = gt + jnp.where(iota <= last, eq, 0.0)

    T = LANES
    r_i = lax.broadcasted_iota(jnp.int32, (T, T), 0)
    c_i = lax.broadcasted_iota(jnp.int32, (T, T), 1)
    before = (r_i < c_i).astype(BF16)
    slot_f = r_i.astype(F32)
    lane_f = lax.broadcasted_iota(jnp.int32, (1, T), 1).astype(F32)
    esel = lax.broadcasted_iota(jnp.int32, (E, 1), 0)

    def group(g, off):
        s = sel_ref[:, pl.ds(pl.multiple_of(g * T, T), T)]
        rank = jnp.dot(s.astype(BF16), before, preferred_element_type=F32)
        tok = (lane_f + jnp.asarray(g * T, F32)) * s
        cnt = jnp.sum(s, axis=1, keepdims=True)

        def emit(n_slots):
            for e in range(E):
                hit = rank[e:e + 1, :] == slot_f[:n_slots]
                ids = jnp.sum(jnp.where(hit, tok[e:e + 1, :], 0.0), axis=1, keepdims=True)
                o = jnp.sum(jnp.where(esel == e, off, 0))
                idx_ref[0, e, pl.ds(o, n_slots), :] = ids.astype(jnp.int32)

        few = jnp.max(cnt) <= SELECT_FAST_SLOTS
        pl.when(few)(lambda: emit(SELECT_FAST_SLOTS))
        pl.when(jnp.logical_not(few))(lambda: emit(T))
        return off + cnt.astype(jnp.int32)

    lax.fori_loop(0, L // T, group, jnp.zeros((E, 1), jnp.int32))


def _route_select(afft, cap):
    B, E, L = afft.shape
    idx = pl.pallas_call(
        functools.partial(_select_kernel, cap=cap),
        grid=(B,),
        in_specs=[pl.BlockSpec((1, E, L), lambda b: (b, 0, 0))],
        out_specs=pl.BlockSpec((1, E, cap + LANES, 1), lambda b: (b, 0, 0, 0)),
        out_shape=jax.ShapeDtypeStruct((B, E, cap + LANES, 1), jnp.int32),
        scratch_shapes=[pltpu.VMEM((E, L), F32)],
        compiler_params=_cparams(("parallel",)),
        name="route_select",
    )(afft)
    return idx[:, :, :cap, 0]


def _tile_rows(i, nt):
    return pl.ds(pl.multiple_of(i * nt, nt), nt)


def _gather_kernel(idx_ref, h_ref, aff_ref, xs_ref, gs_ref, buf_ref):
    cap, D = xs_ref.shape[1], xs_ref.shape[2]
    nt = D // LANES

    def body(s, c):
        base = s * ROW_GROUP
        rows = [idx_ref[0, 0, base + k] for k in range(ROW_GROUP)]
        vals = [h_ref[0, _tile_rows(r, nt), :] for r in rows]
        gates = [aff_ref[0, pl.ds(r, 1), :] for r in rows]
        for k in range(ROW_GROUP):
            buf_ref[_tile_rows(base + k, nt), :] = vals[k]
            gs_ref[0, pl.ds(base + k, 1), :] = gates[k]
        return c

    lax.fori_loop(0, cap // ROW_GROUP, body, 0)
    for c in range(nt):
        xs_ref[0, :, c * LANES:(c + 1) * LANES] = buf_ref[pl.ds(c, cap, stride=nt), :].astype(BF16)


def _route_gather(idx, h2t, aff, D):
    B, E, cap = idx.shape
    nt = D // LANES
    L = h2t.shape[1] // nt
    return pl.pallas_call(
        _gather_kernel,
        grid=(B, E),
        in_specs=[pl.BlockSpec((1, 1, cap), lambda b, e: (b * E + e, 0, 0), memory_space=pltpu.SMEM),
                  pl.BlockSpec((1, L * nt, LANES), lambda b, e: (b, 0, 0), pipeline_mode=pl.Buffered(1)),
                  pl.BlockSpec((1, L, E), lambda b, e: (b, 0, 0))],
        out_specs=[pl.BlockSpec((1, cap, D), lambda b, e: (e, b, 0)),
                   pl.BlockSpec((1, cap, E), lambda b, e: (e, b, 0))],
        out_shape=[jax.ShapeDtypeStruct((E, B * cap, D), BF16),
                   jax.ShapeDtypeStruct((E, B * cap, E), F32)],
        scratch_shapes=[pltpu.VMEM((cap * nt, LANES), F32)],
        compiler_params=_cparams(("arbitrary", "arbitrary")),
        name="route_gather",
    )(idx.reshape(B * E, 1, cap), h2t, aff)


def _combine_kernel(idx_ref, ye_ref, x1_ref, g2_ref, post2_ref, o_ref, y_ref, buf_ref, *, n_experts):
    j = pl.program_id(1)
    cap, D = ye_ref.shape[1], ye_ref.shape[2]
    nt = D // LANES
    tm = x1_ref.shape[1]

    @pl.when(j == 0)
    def _():
        y_ref[...] = jnp.zeros_like(y_ref)

    @pl.when(j < n_experts)
    def _():
        for c in range(nt):
            buf_ref[pl.ds(c, cap, stride=nt), :] = ye_ref[0, :, c * LANES:(c + 1) * LANES]

        def body(s, c):
            base = s * ROW_GROUP
            rows = [idx_ref[0, 0, base + k] for k in range(ROW_GROUP)]
            vals = [y_ref[_tile_rows(rows[k], nt), :] + buf_ref[_tile_rows(base + k, nt), :]
                    for k in range(ROW_GROUP)]
            for k in range(ROW_GROUP):
                y_ref[_tile_rows(rows[k], nt), :] = vals[k]
            return c

        lax.fori_loop(0, cap // ROW_GROUP, body, 0)

    @pl.when(j >= n_experts)
    def _():
        base = pl.multiple_of((j - n_experts) * (tm * nt), tm * nt)
        cols = [y_ref[pl.ds(base + c, tm, stride=nt), :] for c in range(nt)]
        ssq = sum(jnp.sum(y * y, axis=-1, keepdims=True) for y in cols)
        rstd = lax.rsqrt(ssq * (1.0 / D) + EPS)
        for c in range(nt):
            cs = slice(c * LANES, (c + 1) * LANES)
            o_ref[0, :, cs] = x1_ref[0, :, cs] + g2_ref[0, :, cs] * (cols[c] * rstd * post2_ref[:, cs])


def _route_combine(idx, ye, x1, g2, post2, tm):
    B, E, cap = idx.shape
    _, L, D = x1.shape
    nt = D // LANES
    assert ye.shape[0] == E + 1
    tile = lambda j: jnp.maximum(j - E, 0)
    expert = lambda j: jnp.minimum(j, E - 1)
    return pl.pallas_call(
        functools.partial(_combine_kernel, n_experts=E),
        grid=(B, E + L // tm),
        in_specs=[pl.BlockSpec((1, 1, cap), lambda b, j: (b * E + expert(j), 0, 0), memory_space=pltpu.SMEM),
                  pl.BlockSpec((1, cap, D), lambda b, j: (expert(j) + 1, b, 0)),
                  pl.BlockSpec((1, tm, D), lambda b, j: (b, tile(j), 0)),
                  pl.BlockSpec((1, 1, D), lambda b, j: (b, 0, 0)),
                  pl.BlockSpec((1, D), lambda b, j: (0, 0))],
        out_specs=pl.BlockSpec((1, tm, D), lambda b, j: (b, tile(j), 0)),
        out_shape=jax.ShapeDtypeStruct((B, L, D), F32),
        scratch_shapes=[pltpu.VMEM((L * nt, LANES), F32), pltpu.VMEM((cap * nt, LANES), F32)],
        compiler_params=_cparams(("arbitrary", "arbitrary")),
        name="route_combine_final",
    )(idx.reshape(B * E, 1, cap), ye, x1, g2, post2)


def _expert_kernel(xs_ref, g_ref, wg_ref, wu_ref, wd_ref, ye_ref, wgb_ref, wub_ref, wdb_ref):
    s = pl.program_id(0)
    f = pl.program_id(1)
    ne = pl.num_programs(0) - 1
    tf = wg_ref.shape[2]
    slot = lax.rem(s, 2)

    @pl.when(s < ne)
    def _():
        cols = pl.ds(pl.multiple_of(f * tf, tf), tf)
        wgb_ref[slot, :, cols] = wg_ref[0].astype(BF16)
        wub_ref[slot, :, cols] = wu_ref[0].astype(BF16)
        wdb_ref[slot, cols, :] = wd_ref[0].astype(BF16)

    @pl.when(s == 0)
    def _():
        ye_ref[...] = jnp.zeros_like(ye_ref)

    @pl.when(s > 0)
    def _():
        e = s - 1
        prev = 1 - slot
        mt = xs_ref.shape[1] // FFN_M_SPLIT
        for mi in range(FFN_M_SPLIT):
            rs = slice(mi * mt, (mi + 1) * mt)
            xs = xs_ref[0, rs, :]
            a = jnp.dot(xs, wgb_ref[prev], preferred_element_type=F32)
            u = jnp.dot(xs, wub_ref[prev], preferred_element_type=F32)
            hmid = (a * jax.nn.sigmoid(a) * u).astype(BF16)
            out = jnp.dot(hmid, wdb_ref[prev], preferred_element_type=F32)
            gs = g_ref[0, rs, :]
            esel = lax.broadcasted_iota(jnp.int32, gs.shape, 1)
            ye_ref[0, rs, :] = out * jnp.sum(jnp.where(esel == e, gs, 0.0), axis=1, keepdims=True)


FFN_M_SPLIT = 2


def _expert_ffn(xs, gs, w_gate, w_up, w_down, tf):
    E, M, D = xs.shape
    F = w_gate.shape[2]
    nf = F // tf
    mt = M // nf
    behind = lambda s: jnp.maximum(s - 1, 0)
    ahead = lambda s: jnp.minimum(s, E - 1)
    return pl.pallas_call(
        _expert_kernel,
        grid=(E + 1, nf),
        in_specs=[pl.BlockSpec((1, mt, D), lambda s, f: (behind(s), f, 0)),
                  pl.BlockSpec((1, mt, E), lambda s, f: (behind(s), f, 0)),
                  pl.BlockSpec((1, D, tf), lambda s, f: (ahead(s), 0, f)),
                  pl.BlockSpec((1, D, tf), lambda s, f: (ahead(s), 0, f)),
                  pl.BlockSpec((1, tf, D), lambda s, f: (ahead(s), f, 0))],
        out_specs=pl.BlockSpec((1, mt, D), lambda s, f: (s, f, 0)),
        out_shape=jax.ShapeDtypeStruct((E + 1, M, D), F32),
        scratch_shapes=[pltpu.VMEM((2, D, F), BF16), pltpu.VMEM((2, D, F), BF16), pltpu.VMEM((2, F, D), BF16)],
        compiler_params=_cparams(("arbitrary", "arbitrary")),
        name="expert_ffn",
    )(xs, gs, w_gate, w_up, w_down)


def kernel(x, c, ctx, c_ctx, w_mod, b_mod, pre_norm1, post_norm1, pre_norm2, post_norm2, w_in, conv_w, conv_b, filt_w1, filt_b1, filt_w2, filt_b2, filt_w3, filt_b3, filt_w4, filt_freq, hyena_bias, ml_gate_b, ml_norm, w_out, w_router, w_exp_gate, w_exp_up, w_exp_down):
    B, L, D = x.shape
    depth = w_mod.shape[0]
    assert depth == 1, "single-layer block"
    li = 0
    tabs = _pe_tables(L // GRID_W, GRID_W, D)

    cc = jnp.concatenate([c, c_ctx[None], jnp.zeros((8 - B - 1, D), F32)], axis=0)
    mod = _modulation(cc, w_mod[li], b_mod[li])
    chunks = [mod[:, k * D:(k + 1) * D] for k in range(6)]
    sh1, sc1, g1, sh2, sc2, g2 = [m[:B, None, :] for m in chunks]
    csh1, csc1 = chunks[0][B:B + 1, None, :], chunks[1][B:B + 1, None, :]

    pre1 = pre_norm1[li][None, :]
    zhy, q, kt, v, o, gatet = _in_proj(x, tabs, sh1, sc1, pre1, w_in[li], IN_PROJ_TILE, True)
    qc, ktc, vc, _, gatetc = _in_proj(ctx, None, csh1, csc1, pre1, w_in[li], ctx.shape[1], False)

    kcirc = _hyena_filter(L, filt_w1[li], filt_b1[li], filt_w2[li], filt_b2[li], filt_w3[li],
                          filt_b3[li], filt_w4[li], filt_freq[li])
    y_hy = _hyena_conv(zhy, kcirc, conv_w[li], conv_b[li], hyena_bias[li])
    hsum = _mlstm(q, kt, v, gatet, qc, ktc, vc, gatetc, ml_gate_b[li])

    x1, h2t, aff, afft = _out_proj(y_hy, hsum, o, x, tabs, g1, sh2, sc2, ml_norm[li][None, :],
                                   post_norm1[li][None, :], pre_norm2[li][None, :], w_out[li],
                                   w_router[li], IN_PROJ_TILE)

    cap = CAP_FACTOR * L // N_EXPERTS
    idx = _route_select(afft, cap)
    xs, gs = _route_gather(idx, h2t, aff, D)
    ye = _expert_ffn(xs, gs, w_exp_gate[li], w_exp_up[li], w_exp_down[li], 512)
    return _route_combine(idx, ye, x1, g2, post_norm2[li][None, :], TOKEN_TILE)
```

```python
import functools
import math

import jax
import jax.numpy as jnp
from jax import lax
from jax.experimental import pallas as pl
from jax.experimental.pallas import tpu as pltpu

D_MODEL = 1024
GRID_W = 64
HY_WIDTH = 512
ML_HEADS = 4
ML_DK = 64
ML_DV = 128
ML_WIDTH = ML_HEADS * ML_DV
HY_COLS = 3 * HY_WIDTH
QK_COLS = ML_HEADS * ML_DK
N_GATES = 4 * ML_HEADS
FILTER_EMB = 33
DECAY_TARGET = 1e-2
FAST_DECAY_PCT = 0.3
SLOW_DECAY_PCT = 1.5
CHUNK = 128
GATE_CAP = 15.0
N_EXPERTS = 16
CAP_FACTOR = 2
EPS = 1e-6

F32 = jnp.float32
BF16 = jnp.bfloat16

TOKEN_TILE = 512
TIME_ROWS = 8
IN_PROJ_TILE = 1024
VMEM_LIMIT = 56 * 1024 * 1024


def _cparams(sem):
    return pltpu.CompilerParams(dimension_semantics=sem, vmem_limit_bytes=VMEM_LIMIT)


def _rms(xf, g):
    return xf * lax.rsqrt(jnp.mean(xf * xf, axis=-1, keepdims=True) + EPS) * g


def _bdot(a, b):
    return jnp.dot(a.astype(BF16), b.astype(BF16), preferred_element_type=F32)


def _bdot_nt(a, b):
    return lax.dot_general(a.astype(BF16), b.astype(BF16), (((1,), (1,)), ((), ())),
                           preferred_element_type=F32)


def _pe_tables_kernel(omega_ref, er_ref, ec_ref):
    quarter = omega_ref.shape[1]
    om = omega_ref[...]
    for ref in (er_ref, ec_ref):
        n = ref.shape[0]
        pos = lax.broadcasted_iota(jnp.int32, (n, quarter), 0).astype(F32)
        ang = pos * om
        ref[:, :quarter] = jnp.sin(ang)
        ref[:, quarter:] = jnp.cos(ang)


def _pe_tables(rows, cols, dim):
    quarter = dim // 4
    omega = (1.0 / (10000.0 ** (jnp.arange(quarter, dtype=F32) / quarter)))[None, :]
    return pl.pallas_call(
        _pe_tables_kernel,
        out_shape=(jax.ShapeDtypeStruct((rows, dim // 2), F32),
                   jax.ShapeDtypeStruct((cols, dim // 2), F32)),
        name="pe_tables",
    )(omega)


def _pe_tile(er_blk, ec):
    nr, half = er_blk.shape
    row_part = jnp.broadcast_to(er_blk[:, None, :], (nr, GRID_W, half)).reshape(nr * GRID_W, half)
    col_part = jnp.broadcast_to(ec[None, :, :], (nr, GRID_W, half)).reshape(nr * GRID_W, half)
    return jnp.concatenate([row_part, col_part], axis=-1)


def _mod_kernel(c_ref, w_ref, b_ref, o_ref):
    c = c_ref[...]
    s = c * jax.nn.sigmoid(c)
    o_ref[...] = _bdot(s, w_ref[...]) + b_ref[...]


def _modulation(cc, w_mod, b_mod):
    rows, d = cc.shape
    n = w_mod.shape[1]
    tn = 512
    return pl.pallas_call(
        _mod_kernel,
        grid=(n // tn,),
        in_specs=[pl.BlockSpec((rows, d), lambda j: (0, 0)),
                  pl.BlockSpec((d, tn), lambda j: (0, j)),
                  pl.BlockSpec((1, tn), lambda j: (0, j))],
        out_specs=pl.BlockSpec((rows, tn), lambda j: (0, j)),
        out_shape=jax.ShapeDtypeStruct((rows, n), F32),
        compiler_params=_cparams(("arbitrary",)),
        name="modulation",
    )(cc, w_mod, b_mod[None, :])


def _in_proj_kernel(*refs, with_hyena):
    if with_hyena:
        (x_ref, er_ref, ec_ref, sh_ref, sc_ref, g_ref, wn_ref, wt_ref,
         zt_ref, q_ref, kt_ref, v_ref, o_ref, gatet_ref) = refs
        xf = x_ref[0] + _pe_tile(er_ref[...], ec_ref[...])
    else:
        (x_ref, sh_ref, sc_ref, g_ref, wn_ref, wt_ref,
         q_ref, kt_ref, v_ref, o_ref, gatet_ref) = refs
        xf = x_ref[0]
    h = _rms(xf, g_ref[...]) * (1.0 + sc_ref[0]) + sh_ref[0]
    hb = h.astype(BF16)
    z = jnp.dot(hb, wn_ref[...], preferred_element_type=F32)
    qs = z[:, :QK_COLS] * (ML_DK ** -0.5)
    for hd in range(ML_HEADS):
        q_ref[0, hd] = qs[:, hd * ML_DK:(hd + 1) * ML_DK].astype(BF16)
    v_ref[0] = z[:, QK_COLS:QK_COLS + ML_WIDTH].astype(BF16)
    o_ref[0] = z[:, QK_COLS + ML_WIDTH:]
    zt = lax.dot_general(wt_ref[...], hb, (((1,), (1,)), ((), ())), preferred_element_type=F32)
    off = 0
    if with_hyena:
        for j in range(zt.shape[1] // LANES):
            zt_ref[0, 0, pl.ds(j, HY_COLS, stride=TIME_ROWS), :] = zt[:HY_COLS, j * LANES:(j + 1) * LANES]
        off = HY_COLS
    for hd in range(ML_HEADS):
        kt_ref[0, hd] = zt[off + hd * ML_DK:off + (hd + 1) * ML_DK, :].astype(BF16)
    gatet_ref[0] = zt[off + QK_COLS:, :]


def _mod_spec(D, chunk, row0, per_sample):
    return pl.BlockSpec((1, 1, D), lambda b, i: ((b if per_sample else 0) + row0, 0, chunk))


def _in_proj_weights(w_in):
    w_q = w_in[:, HY_COLS:HY_COLS + QK_COLS]
    w_k = w_in[:, HY_COLS + QK_COLS:HY_COLS + 2 * QK_COLS]
    w_vo = w_in[:, HY_COLS + 2 * QK_COLS:HY_COLS + 2 * QK_COLS + 2 * ML_WIDTH]
    w_g = w_in[:, HY_COLS + 2 * QK_COLS + 2 * ML_WIDTH:]
    wn = jnp.concatenate([w_q, w_vo], axis=1).astype(BF16)
    wt = jnp.concatenate([w_in[:, :HY_COLS], w_k, w_g], axis=1).T.astype(BF16)
    return wn, wt


def _in_proj(x, tabs, mod3, mod_row0, g, weights, tm, with_hyena):
    B, L, D = x.shape
    wn, wt = weights
    if not with_hyena:
        wt = wt[HY_COLS:]
    full = lambda a: pl.BlockSpec(a.shape, lambda b, i: (0,) * a.ndim)
    in_specs = [pl.BlockSpec((1, tm, D), lambda b, i: (b, i, 0))]
    args = [x]
    if with_hyena:
        er, ec = tabs
        in_specs += [pl.BlockSpec((tm // GRID_W, D // 2), lambda b, i: (i, 0)), full(ec)]
        args += [er, ec]
    in_specs += [_mod_spec(D, 0, mod_row0, with_hyena), _mod_spec(D, 1, mod_row0, with_hyena),
                 full(g), full(wn), full(wt)]
    args += [mod3, mod3, g, wn, wt]
    out_shape, out_specs = [], []
    if with_hyena:
        assert tm == TIME_ROWS * LANES
        out_shape.append(jax.ShapeDtypeStruct((B, L // tm, HY_COLS * TIME_ROWS, LANES), F32))
        out_specs.append(pl.BlockSpec((1, 1, HY_COLS * TIME_ROWS, LANES), lambda b, i: (b, i, 0, 0)))
    out_shape += [jax.ShapeDtypeStruct((B, ML_HEADS, L, ML_DK), BF16),
                  jax.ShapeDtypeStruct((B, ML_HEADS, ML_DK, L), BF16),
                  jax.ShapeDtypeStruct((B, L, ML_WIDTH), BF16),
                  jax.ShapeDtypeStruct((B, L, ML_WIDTH), F32),
                  jax.ShapeDtypeStruct((B, N_GATES, L), F32)]
    out_specs += [pl.BlockSpec((1, ML_HEADS, tm, ML_DK), lambda b, i: (b, 0, i, 0)),
                  pl.BlockSpec((1, ML_HEADS, ML_DK, tm), lambda b, i: (b, 0, 0, i)),
                  pl.BlockSpec((1, tm, ML_WIDTH), lambda b, i: (b, i, 0)),
                  pl.BlockSpec((1, tm, ML_WIDTH), lambda b, i: (b, i, 0)),
                  pl.BlockSpec((1, N_GATES, tm), lambda b, i: (b, 0, i))]
    return pl.pallas_call(
        functools.partial(_in_proj_kernel, with_hyena=with_hyena),
        grid=(B, L // tm),
        in_specs=in_specs,
        out_specs=out_specs,
        out_shape=out_shape,
        compiler_params=_cparams(("parallel", "parallel")),
        name="in_proj_hy" if with_hyena else "in_proj_ctx",
    )(*args)


FILT_TILE = 1024
FILT_CBLK = 128


def _filter_kernel(w1a_ref, w1b_ref, w1c_ref, b1_ref, w2_ref, b2_ref, w3_ref, b3_ref, fr_ref,
                   w4f_ref, w4b_ref, dl_ref, mir_ref, k_ref, hf_ref, *, L):
    bands = (FILTER_EMB - 1) // 2

    @pl.when(pl.program_id(0) == 0)
    def _():
        fk = (1e-4 + lax.broadcasted_iota(jnp.int32, (bands, 1), 0).astype(F32)
              * ((bands - 1 - 1e-4) / (bands - 1)))
        fr = fr_ref[...]
        for j in range(L // FILT_TILE):
            pos = (lax.broadcasted_iota(jnp.int32, (1, FILT_TILE), 1) + j * FILT_TILE).astype(F32)
            tl = pos * (1.0 / (L - 1))
            ang = fk * (pos * (2.0 * math.pi / L))
            pre = (w1a_ref[...].astype(F32) * tl.astype(BF16).astype(F32)
                   + _bdot(w1b_ref[...], jnp.cos(ang)) + _bdot(w1c_ref[...], -jnp.sin(ang)))
            h = jnp.sin(fr * (pre + b1_ref[...]))
            h = jnp.sin(fr * (_bdot(w2_ref[...], h) + b2_ref[...]))
            h = jnp.sin(fr * (_bdot(w3_ref[...], h) + b3_ref[...]))
            hf_ref[:, j * FILT_TILE:(j + 1) * FILT_TILE] = h.astype(BF16)

    pos = lax.broadcasted_iota(jnp.int32, (1, L), 1).astype(F32)
    decay = jnp.exp(-(pos * (1.0 / (L - 1))) * dl_ref[...])
    hf = hf_ref[...]
    k_ref[:, :L] = jnp.dot(w4f_ref[...], hf, preferred_element_type=F32) * decay
    gb = (jnp.dot(w4b_ref[...], hf, preferred_element_type=F32) * decay).astype(BF16)
    nblk = L // LANES
    for j in range(nblk):
        src = gb[:, (nblk - 1 - j) * LANES:(nblk - j) * LANES]
        nxt = gb[:, (nblk - j) * LANES:(nblk - j + 1) * LANES] if j > 0 else jnp.zeros_like(src)
        k_ref[:, L + j * LANES:L + (j + 1) * LANES] = jnp.dot(
            jnp.concatenate([src, nxt], axis=1), mir_ref[...], preferred_element_type=F32)


def _hyena_filter(L, w1, b1, w2, b2, w3, b3, w4, freq):
    hid = w2.shape[0]
    bands = (FILTER_EMB - 1) // 2
    col = lambda a: a[:, None]
    w1t = w1.T.astype(BF16)
    min_decay = math.log(DECAY_TARGET) / SLOW_DECAY_PCT
    max_decay = math.log(DECAY_TARGET) / FAST_DECAY_PCT
    dl = jnp.abs(jnp.linspace(min_decay, max_decay, HY_WIDTH, dtype=F32))[:, None]
    w4t = w4.T.astype(BF16)
    assert DFT_PASSES == 1, "the mirrored taps are kept at the bf16 precision a single-pass DFT reads"
    lane = jnp.arange(LANES)
    mir = jnp.concatenate([(lane[:, None] + lane[None, :] == LANES),
                           (lane[:, None] == 0) & (lane[None, :] == 0)], axis=0).astype(BF16)
    full = lambda a: pl.BlockSpec(a.shape, lambda i: (0,) * a.ndim)
    args = [w1t[:, 0:1], w1t[:, 1:1 + bands], w1t[:, 1 + bands:], col(b1), w2.T.astype(BF16), col(b2),
            w3.T.astype(BF16), col(b3), col(freq)]
    return pl.pallas_call(
        functools.partial(_filter_kernel, L=L),
        grid=(HY_WIDTH // FILT_CBLK,),
        in_specs=[full(a) for a in args] + [
            pl.BlockSpec((FILT_CBLK, hid), lambda i: (i, 0)),
            pl.BlockSpec((FILT_CBLK, hid), lambda i: (HY_WIDTH // FILT_CBLK + i, 0)),
            pl.BlockSpec((FILT_CBLK, 1), lambda i: (i, 0)), full(mir)],
        out_specs=pl.BlockSpec((FILT_CBLK, 2 * L), lambda i: (i, 0)),
        out_shape=jax.ShapeDtypeStruct((HY_WIDTH, 2 * L), F32),
        scratch_shapes=[pltpu.VMEM((hid, L), BF16)],
        compiler_params=_cparams(("arbitrary",)),
        name="hyena_filter",
    )(*args, w4t, w4t, dl, mir)


FFT_N = 128
HY_CBLK = 32
HY_GROUP = 8
HY_UNROLL = 4
DFT_PASSES = 1


def _dft_constants(n1_data):
    import numpy as np
    n = FFT_N
    k = np.arange(n)
    ang = -2.0 * np.pi * ((k[:, None] * k[None, :]) % n) / n
    fre, fim = np.cos(ang), np.sin(ang)
    m = n1_data
    fa_d = np.block([[fre[:, :m], -fim[:, :m]], [fim[:, :m], fre[:, :m]]])
    fa_f = np.concatenate([fre, fim], axis=0)
    fb = np.block([[fre, fim], [-fim, fre]])
    fbi = np.block([[fre, -fim], [fim, fre]])
    fc = np.block([[fre[:m, :], fim[:m, :]], [-fim[:m, :], fre[:m, :]]]) / (n * n)
    tang = -2.0 * np.pi * (k[:, None] * k[None, :]) / (n * n)
    tw = np.stack([np.cos(tang), np.sin(tang)])

    def hilo(a):
        a32 = jnp.asarray(a, F32)
        hi = a32.astype(BF16)
        lo = (a32 - hi.astype(F32)).astype(BF16)
        return jnp.stack([hi, lo])

    return hilo(fa_d), hilo(fa_f), hilo(fb), hilo(fbi), hilo(fc), jnp.asarray(tw, F32)


def _mm_const_lhs(c_ref, d):
    dh = d.astype(BF16)
    acc = jnp.dot(c_ref[0], dh, preferred_element_type=F32)
    if DFT_PASSES == 3:
        dl = (d - dh.astype(F32)).astype(BF16)
        acc = acc + (jnp.dot(c_ref[0], dl, preferred_element_type=F32)
                     + jnp.dot(c_ref[1], dh, preferred_element_type=F32))
    return acc


def _mm_const_rhs(d, c_ref):
    dh = d.astype(BF16)
    acc = jnp.dot(dh, c_ref[0], preferred_element_type=F32)
    if DFT_PASSES == 3:
        dl = (d - dh.astype(F32)).astype(BF16)
        acc = acc + (jnp.dot(dl, c_ref[0], preferred_element_type=F32)
                     + jnp.dot(dh, c_ref[1], preferred_element_type=F32))
    return acc


def _cmul(are, aim, bre, bim):
    return are * bre - aim * bim, are * bim + aim * bre


def _hyena_conv_kernel(x0_ref, x1_ref, v_ref, kc_ref, w0_ref, w1_ref, wv_ref, b0_ref, b1_ref, bv_ref,
                       hb_ref, fad_ref, faf_ref, fb_ref, fbi_ref, fc_ref, tw_ref,
                       o_ref, u_ref, s_ref, ks_ref):
    n = FFT_N
    cb = kc_ref.shape[0]
    nb, nt = x0_ref.shape[0], x0_ref.shape[1]
    m = nt * TIME_ROWS

    def chan(ref, b, c):
        return ref[b, :, pl.ds(pl.multiple_of(c * TIME_ROWS, TIME_ROWS), TIME_ROWS), :].reshape(m, n)

    sub = lax.broadcasted_iota(jnp.int32, (m, n), 0)
    lane = lax.broadcasted_iota(jnp.int32, (m, n), 1)
    tre, tim = tw_ref[0], tw_ref[1]

    def sconv(z, w_ref, b_ref, c):
        a = pltpu.roll(z, 1, axis=1)
        prev = jnp.where(lane == 0, jnp.where(sub == 0, 0.0, pltpu.roll(a, 1, axis=0)), a)
        a2 = pltpu.roll(z, n - 1, axis=1)
        nxt = jnp.where(lane == n - 1, jnp.where(sub == m - 1, 0.0, pltpu.roll(a2, m - 1, axis=0)), a2)
        return prev * w_ref[0, c] + z * w_ref[1, c] + nxt * w_ref[2, c] + b_ref[c]

    def spectrum_rows(res):
        outs = []
        for h in range(2):
            are, aim = _cmul(res[:n, h * n:(h + 1) * n], res[n:, h * n:(h + 1) * n], tre, tim)
            outs.append(jnp.concatenate([are, aim], axis=1))
        return outs

    def fwd_pair(p, carry):
        c0 = 2 * p
        us = []
        for c in (c0, c0 + 1):
            ub = []
            for b in range(nb):
                x1c = sconv(chan(x1_ref, b, c), w1_ref, b1_ref, c)
                vc = sconv(chan(v_ref, b, c), wv_ref, bv_ref, c)
                u = x1c * vc
                u_ref[b, c] = u
                ub.append(u)
            us.append(ub)
        wd = jnp.concatenate([jnp.concatenate([us[0][b], us[1][b]], axis=1) for b in range(nb)], axis=0)
        sa, sb = spectrum_rows(_mm_const_lhs(fad_ref, wd))
        s_ref[c0] = sa.astype(s_ref.dtype)
        s_ref[c0 + 1] = sb.astype(s_ref.dtype)
        wk = jnp.concatenate([kc_ref[c0], kc_ref[c0 + 1]], axis=1)
        ka, kb = spectrum_rows(_mm_const_lhs(faf_ref, wk))
        ks_ref[c0] = ka.astype(ks_ref.dtype)
        ks_ref[c0 + 1] = kb.astype(ks_ref.dtype)
        return carry

    lax.fori_loop(0, cb // 2, fwd_pair, 0, unroll=HY_UNROLL)

    def mid_group(g, carry):
        gs = pl.ds(pl.multiple_of(g * HY_GROUP, HY_GROUP), HY_GROUP)
        x = _mm_const_rhs(s_ref[gs].reshape(HY_GROUP * n, 2 * n), fb_ref)
        k = _mm_const_rhs(ks_ref[gs].reshape(HY_GROUP * n, 2 * n), fb_ref)
        yre, yim = _cmul(x[:, :n], x[:, n:], k[:, :n], k[:, n:])
        vv = _mm_const_rhs(jnp.concatenate([yre, yim], axis=1), fbi_ref).reshape(HY_GROUP, n, 2 * n)
        vre, vim = _cmul(vv[:, :, :n], vv[:, :, n:], tre[None], -tim[None])
        s_ref[gs] = jnp.concatenate([vre, vim], axis=2).astype(s_ref.dtype)
        return carry

    lax.fori_loop(0, cb // HY_GROUP, mid_group, 0, unroll=HY_UNROLL)

    def inv_pair(p, carry):
        c0 = 2 * p
        sa, sb = s_ref[c0], s_ref[c0 + 1]
        wd = jnp.concatenate([jnp.concatenate([sa[:, :n], sb[:, :n]], axis=1),
                              jnp.concatenate([sa[:, n:], sb[:, n:]], axis=1)], axis=0)
        res = _mm_const_lhs(fc_ref, wd)
        for h, c in enumerate((c0, c0 + 1)):
            for b in range(nb):
                y = res[b * m:(b + 1) * m, h * n:(h + 1) * n]
                x0c = sconv(chan(x0_ref, b, c), w0_ref, b0_ref, c)
                out = x0c * (y + hb_ref[c] * u_ref[b, c])
                o_ref[b, :, pl.ds(pl.multiple_of(c * TIME_ROWS, TIME_ROWS), TIME_ROWS), :] = out.reshape(
                    nt, TIME_ROWS, n)
        return carry

    lax.fori_loop(0, cb // 2, inv_pair, 0, unroll=HY_UNROLL)


def _hyena_conv(zt, kcirc, conv_w, conv_b, hy_bias):
    B, nt, _, n = zt.shape
    m = nt * TIME_ROWS
    L = m * n
    C = HY_WIDTH
    assert B == 2 and n == FFT_N and 2 * L == n * n, "complex packing of two samples over a 128 x 128 point transform"
    z4 = zt
    rows = HY_CBLK * TIME_ROWS
    spec_dt = BF16 if DFT_PASSES == 1 else F32
    k3 = kcirc.reshape(C, n, n)
    cw = conv_w.reshape(3, 3 * C, 1, 1)
    cbias = conv_b.reshape(3 * C, 1, 1)
    hb = hy_bias.reshape(C, 1, 1)
    consts = _dft_constants(m)
    nblk = C // HY_CBLK
    zspec = lambda part: pl.BlockSpec((B, nt, rows, n), lambda i: (0, 0, part * nblk + i, 0))
    wspec = lambda part: pl.BlockSpec((3, HY_CBLK, 1, 1), lambda i: (0, part * nblk + i, 0, 0))
    bspec = lambda part: pl.BlockSpec((HY_CBLK, 1, 1), lambda i: (part * nblk + i, 0, 0))
    full = lambda a: pl.BlockSpec(a.shape, lambda i: (0,) * a.ndim)
    y = pl.pallas_call(
        _hyena_conv_kernel,
        grid=(nblk,),
        in_specs=[zspec(0), zspec(1), zspec(2), pl.BlockSpec((HY_CBLK, n, n), lambda i: (i, 0, 0)),
                  wspec(0), wspec(1), wspec(2), bspec(0), bspec(1), bspec(2), bspec(0)]
                 + [full(a) for a in consts],
        out_specs=pl.BlockSpec((B, nt, rows, n), lambda i: (0, 0, i, 0)),
        out_shape=jax.ShapeDtypeStruct((B, nt, C * TIME_ROWS, n), F32),
        scratch_shapes=[pltpu.VMEM((B, HY_CBLK, m, n), F32), pltpu.VMEM((HY_CBLK, n, 2 * n), spec_dt),
                        pltpu.VMEM((HY_CBLK, n, 2 * n), spec_dt)],
        compiler_params=_cparams(("parallel",)),
        name="hyena_conv",
    )(z4, z4, z4, k3, cw, cw, cw, cbias, cbias, cbias, hb, *consts)
    return y


def _split3(a):
    hi = a.astype(BF16)
    r1 = a - hi.astype(F32)
    mid = r1.astype(BF16)
    lo = (r1 - mid.astype(F32)).astype(BF16)
    return hi, mid, lo


def _exact_dot_right(a, tri_bf):
    hi, mid, lo = _split3(a)
    d = lambda p: jnp.dot(p, tri_bf, preferred_element_type=F32)
    return (d(lo) + d(mid)) + d(hi)


def _soft_gates(g):
    g = GATE_CAP * jnp.tanh(g * (1.0 / GATE_CAP))
    logsig = jnp.minimum(g, 0.0) - jnp.log1p(jnp.exp(-jnp.abs(g)))
    return g, logsig


def _gate_prep_kernel(gt_ref, gbt_ref, rows_ref, cols_ref):
    T = CHUNK
    n = gt_ref.shape[2]
    up_bf = (lax.broadcasted_iota(jnp.int32, (T, T), 1) >= lax.broadcasted_iota(jnp.int32, (T, T), 0)).astype(BF16)
    out8 = lax.broadcasted_iota(jnp.int32, (8, 1), 0)
    cap, ls = _soft_gates(gt_ref[0] + gbt_ref[...])
    H = ML_HEADS
    head = lambda hd: jnp.where(out8 == 0, cap[hd:hd + 1], jnp.where(out8 == 1, ls[H + hd:H + hd + 1],
                                jnp.where(out8 == 2, cap[2 * H + hd:2 * H + hd + 1],
                                          jnp.where(out8 == 3, ls[3 * H + hd:3 * H + hd + 1], 0.0))))
    base = jnp.concatenate([head(hd) for hd in range(H)], axis=0)
    kind = jnp.concatenate([out8] * H, axis=0)
    for j in range(n // T):
        blk = base[:, j * T:(j + 1) * T]
        run = _exact_dot_right(blk, up_bf)
        suf = run[:, T - 1:T] - run + blk
        rows = jnp.where(kind == 1, run, jnp.where(kind == 3, suf, blk))
        cols = rows.T
        for hd in range(H):
            rows_ref[0, hd, :, j * T:(j + 1) * T] = rows[8 * hd:8 * (hd + 1)]
            cols_ref[0, hd, j * T:(j + 1) * T, :] = cols[:, 8 * hd:8 * (hd + 1)]


def _gate_prep(gatet, gate_b, tile):
    B, G, L = gatet.shape
    H = ML_HEADS
    return pl.pallas_call(
        _gate_prep_kernel,
        grid=(B, L // tile),
        in_specs=[pl.BlockSpec((1, G, tile), lambda b, i: (b, 0, i)),
                  pl.BlockSpec((G, 1), lambda b, i: (0, 0))],
        out_specs=[pl.BlockSpec((1, H, 8, tile), lambda b, i: (b, 0, 0, i)),
                   pl.BlockSpec((1, H, tile, 8), lambda b, i: (b, 0, i, 0))],
        out_shape=[jax.ShapeDtypeStruct((B, H, 8, L), F32), jax.ShapeDtypeStruct((B, H, L, 8), F32)],
        compiler_params=_cparams(("parallel", "parallel")),
        name="mlstm_gate_prep",
    )(gatet, gate_b[:, None])


def _mlstm_kernel(q_ref, kt_ref, v_ref, r_ref, bt_ref, qc_ref, ktc_ref, vc_ref, rc_ref, btc_ref,
                  h_ref, cf_ref, cb_ref):
    T = CHUNK
    L = q_ref.shape[2]
    Lc = qc_ref.shape[2]
    nc, ncc = L // T, Lc // T
    row = lax.broadcasted_iota(jnp.int32, (T, T), 0)
    col = lax.broadcasted_iota(jnp.int32, (T, T), 1)
    lo_mask = col <= row
    up_mask = col >= row
    ones_blk = jnp.ones((T, ML_DV), BF16)

    cf_ref[...] = jnp.zeros_like(cf_ref)
    cb_ref[...] = jnp.zeros_like(cb_ref)

    def chunk_step(q, kt, v, rows, cols, c_ref, backward):
        i_r = rows[2:3] if backward else rows[0:1]
        b_r = rows[3:4] if backward else rows[1:2]
        b_c = jnp.broadcast_to(cols[:, 3:4] if backward else cols[:, 1:2], (T, T))
        b_end = b_r[:, 0:1] if backward else b_r[:, T - 1:T]
        mask = up_mask if backward else lo_mask
        w_intra = jnp.exp(jnp.where(mask, b_c - b_r + i_r, -jnp.inf) - GATE_CAP)
        s = jnp.dot(q, kt, preferred_element_type=F32) * w_intra
        qe = (q.astype(F32) * jnp.exp(b_c[:, :ML_DK])).astype(BF16)
        v_aug = jnp.concatenate([v, ones_blk], axis=1)
        c_aug = c_ref[...]
        res = jnp.dot(jnp.concatenate([s.astype(BF16), qe], axis=1),
                      jnp.concatenate([v_aug, c_aug.astype(BF16)], axis=0),
                      preferred_element_type=F32)
        h = res[:, :ML_DV] / jnp.maximum(jnp.abs(res[:, ML_DV:]), math.exp(-GATE_CAP))
        kw = (kt.astype(F32) * jnp.exp(b_end - b_r + i_r - GATE_CAP)).astype(BF16)
        c_ref[...] = jnp.exp(b_end) * c_aug + jnp.dot(kw, v_aug, preferred_element_type=F32)
        return h

    for j in range(ncc):
        for backward in (False, True):
            jj = (ncc - 1 - j) if backward else j
            cs = slice(jj * T, (jj + 1) * T)
            chunk_step(qc_ref[0, 0, cs, :], ktc_ref[0, 0, :, cs], vc_ref[0, cs, :], rc_ref[0, 0, :, cs],
                       btc_ref[0, 0, cs, :], cb_ref if backward else cf_ref, backward)

    def latent_pair(j, accumulate):
        for backward in (False, True):
            jj = (nc - 1 - j) if backward else j
            rs = pl.ds(pl.multiple_of(jj * T, T), T)
            h = chunk_step(q_ref[0, 0, rs, :], kt_ref[0, 0, :, rs], v_ref[0, rs, :], r_ref[0, 0, :, rs],
                           bt_ref[0, 0, rs, :], cb_ref if backward else cf_ref, backward)
            if accumulate:
                h_ref[0, rs, :] = h_ref[0, rs, :] + h
            else:
                h_ref[0, rs, :] = h

    def first_half(j, carry):
        latent_pair(j, False)
        return carry

    def second_half(j, carry):
        latent_pair(j, True)
        return carry

    lax.fori_loop(0, nc // 2, first_half, 0, unroll=MLSTM_UNROLL)
    lax.fori_loop(nc // 2, nc, second_half, 0, unroll=MLSTM_UNROLL)


MLSTM_UNROLL = 4


def _mlstm(q, kt, v, gatet, qc, ktc, vc, gatetc, gate_b):
    B, H, L, dk = q.shape
    Lc = qc.shape[2]
    rows, cols = _gate_prep(gatet, gate_b, min(L, 1024))
    rows_c, cols_c = _gate_prep(gatetc, gate_b, Lc)
    seq = lambda n: [pl.BlockSpec((1, 1, n, dk), lambda b, h: (b, h, 0, 0)),
                     pl.BlockSpec((1, 1, dk, n), lambda b, h: (b, h, 0, 0)),
                     pl.BlockSpec((1, n, ML_DV), lambda b, h: (b, 0, h)),
                     pl.BlockSpec((1, 1, 8, n), lambda b, h: (b, h, 0, 0)),
                     pl.BlockSpec((1, 1, n, 8), lambda b, h: (b, h, 0, 0))]
    return pl.pallas_call(
        _mlstm_kernel,
        grid=(B, H),
        in_specs=seq(L) + seq(Lc),
        out_specs=pl.BlockSpec((1, L, ML_DV), lambda b, h: (b, 0, h)),
        out_shape=jax.ShapeDtypeStruct((B, L, ML_WIDTH), F32),
        scratch_shapes=[pltpu.VMEM((dk, 2 * ML_DV), F32), pltpu.VMEM((dk, 2 * ML_DV), F32)],
        compiler_params=_cparams(("parallel", "parallel")),
        name="mlstm_scan",
    )(q, kt, v, rows, cols, qc, ktc, vc, rows_c, cols_c)


def _out_proj_kernel(yhy_ref, hs_ref, o_ref, x_ref, er_ref, ec_ref, g1_ref, sh2_ref, sc2_ref,
                     mln_ref, post1_ref, pre2_ref, wout_ref, wr_ref, wrt_ref,
                     x1_ref, h2_ref, aff_ref, afft_ref):
    hs = hs_ref[0]
    parts = []
    for hd in range(ML_HEADS):
        hh = hs[:, hd * ML_DV:(hd + 1) * ML_DV]
        parts.append(hh * lax.rsqrt(jnp.mean(hh * hh, axis=-1, keepdims=True) + EPS))
    hn = jnp.concatenate(parts, axis=-1) * mln_ref[...]
    y_ml = hn * jax.nn.sigmoid(o_ref[0])
    yo_hy = [lax.dot_general(yhy_ref[0, 0, pl.ds(j, HY_WIDTH, stride=TIME_ROWS), :].astype(BF16),
                             wout_ref[:HY_WIDTH, :], (((0,), (0,)), ((), ())), preferred_element_type=F32)
             for j in range(TIME_ROWS)]
    yo = (jnp.concatenate(yo_hy, axis=0)
          + jnp.dot(y_ml.astype(BF16), wout_ref[HY_WIDTH:, :], preferred_element_type=F32))
    xf = x_ref[0] + _pe_tile(er_ref[...], ec_ref[...])
    x1 = xf + g1_ref[0] * _rms(yo, post1_ref[...])
    x1_ref[0] = x1
    h2f = _rms(x1, pre2_ref[...]) * (1.0 + sc2_ref[0]) + sh2_ref[0]
    _store_row_tiles(h2_ref, h2f)
    h2 = h2f.astype(BF16)
    logits = jnp.dot(h2, wr_ref[...], preferred_element_type=F32)
    ex = jnp.exp(logits - jnp.max(logits, axis=-1, keepdims=True))
    aff_ref[0] = ex / jnp.sum(ex, axis=-1, keepdims=True)
    logits_t = lax.dot_general(wrt_ref[...], h2, (((1,), (1,)), ((), ())), preferred_element_type=F32)
    ext = jnp.exp(logits_t - jnp.max(logits_t, axis=0, keepdims=True))
    afft_ref[0] = ext / jnp.sum(ext, axis=0, keepdims=True)


def _out_proj(y_hy, hsum, o, x, tabs, mod3, ml_norm, post1, pre2, w_out, w_router, tm):
    B, L, D = x.shape
    er, ec = tabs
    E = w_router.shape[1]
    assert tm == TIME_ROWS * LANES
    full = lambda a: pl.BlockSpec(a.shape, lambda b, i: (0,) * a.ndim)
    tok = lambda w: pl.BlockSpec((1, tm, w), lambda b, i: (b, i, 0))
    wout = w_out.astype(BF16)
    wr = w_router.astype(BF16)
    wrt = w_router.T.astype(BF16)
    return pl.pallas_call(
        _out_proj_kernel,
        grid=(B, L // tm),
        in_specs=[pl.BlockSpec((1, 1, HY_WIDTH * TIME_ROWS, LANES), lambda b, i: (b, i, 0, 0)),
                  tok(ML_WIDTH), tok(ML_WIDTH), tok(D),
                  pl.BlockSpec((tm // GRID_W, D // 2), lambda b, i: (i, 0)), full(ec),
                  _mod_spec(D, 2, 0, True), _mod_spec(D, 3, 0, True), _mod_spec(D, 4, 0, True),
                  full(ml_norm), full(post1), full(pre2),
                  full(wout), full(wr), full(wrt)],
        out_specs=[tok(D), pl.BlockSpec((1, tm * (D // LANES), LANES), lambda b, i: (b, i, 0)), tok(E),
                   pl.BlockSpec((1, E, tm), lambda b, i: (b, 0, i))],
        out_shape=[jax.ShapeDtypeStruct((B, L, D), F32),
                   jax.ShapeDtypeStruct((B, L * (D // LANES), LANES), F32),
                   jax.ShapeDtypeStruct((B, L, E), F32), jax.ShapeDtypeStruct((B, E, L), F32)],
        compiler_params=_cparams(("parallel", "parallel")),
        name="out_proj_router",
    )(y_hy, hsum, o, x, er, ec, mod3, mod3, mod3, ml_norm, post1, pre2, wout, wr, wrt)


LANES = 128
ROW_GROUP = 16
SELECT_FAST_SLOTS = 32


def _store_row_tiles(ref, val):
    n, nt = val.shape[0], val.shape[1] // LANES
    for c in range(nt):
        ref[0, pl.ds(c, n, stride=nt), :] = val[:, c * LANES:(c + 1) * LANES]


def _select_kernel(afft_ref, idx_ref, sel_ref, *, cap):
    E, L = afft_ref.shape[1], afft_ref.shape[2]
    idx_ref[...] = jnp.zeros_like(idx_ref)
    aff = afft_ref[0]
    iota = lax.broadcasted_iota(jnp.int32, (E, L), 1)
    count = lambda ind: jnp.sum(ind, axis=1, keepdims=True)
    count_ge = lambda th: count(jnp.where(aff >= th, 1.0, 0.0))
    pow2 = lambda j: pltpu.bitcast((j - 24) << 23, F32)

    def estep(_, c):
        lo, hi = c
        mid = (lo + hi) >> 1
        ok = count_ge(pow2(mid)) >= cap
        return jnp.where(ok, mid, lo), jnp.where(ok, hi, mid)

    jlo, jhi = lax.fori_loop(0, 7, estep, (jnp.full((E, 1), 24, jnp.int32),
                                           jnp.full((E, 1), 152, jnp.int32)))

    def vstep(_, c):
        lo, hi = c
        mid = lo + (hi - lo) * 0.5
        ok = count_ge(mid) >= cap
        return jnp.where(ok, mid, lo), jnp.where(ok, hi, mid)

    lo, hi = lax.fori_loop(0, 40, vstep, (pow2(jlo), pow2(jhi)))
    gt = jnp.where(aff >= hi, 1.0, 0.0)
    eq = jnp.where(aff >= lo, 1.0, 0.0) - gt
    need = cap - count(gt)

    def istep(_, c):
        lo, hi = c
        mid = (lo + hi) >> 1
        ok = count(jnp.where(iota <= mid, eq, 0.0)) >= need
        return jnp.where(ok, lo, mid), jnp.where(ok, mid, hi)

    _, last = lax.fori_loop(0, L.bit_length() - 1, istep,
                            (jnp.full((E, 1), -1, jnp.int32), jnp.full((E, 1), L - 1, jnp.int32)))
    sel_ref[...] = gt + jnp.where(iota <= last, eq, 0.0)

    T = LANES
    r_i = lax.broadcasted_iota(jnp.int32, (T, T), 0)
    c_i = lax.broadcasted_iota(jnp.int32, (T, T), 1)
    before = (r_i < c_i).astype(BF16)
    slot_f = r_i.astype(F32)
    lane_f = lax.broadcasted_iota(jnp.int32, (1, T), 1).astype(F32)
    esel = lax.broadcasted_iota(jnp.int32, (E, 1), 0)

    def group(g, off):
        s = sel_ref[:, pl.ds(pl.multiple_of(g * T, T), T)]
        rank = jnp.dot(s.astype(BF16), before, preferred_element_type=F32)
        tok = (lane_f + jnp.asarray(g * T, F32)) * s
        cnt = jnp.sum(s, axis=1, keepdims=True)

        def emit(n_slots):
            for e in range(E):
                hit = rank[e:e + 1, :] == slot_f[:n_slots]
                ids = jnp.sum(jnp.where(hit, tok[e:e + 1, :], 0.0), axis=1, keepdims=True)
                o = jnp.sum(jnp.where(esel == e, off, 0))
                idx_ref[0, e, pl.ds(o, n_slots), :] = ids.astype(jnp.int32)

        few = jnp.max(cnt) <= SELECT_FAST_SLOTS
        pl.when(few)(lambda: emit(SELECT_FAST_SLOTS))
        pl.when(jnp.logical_not(few))(lambda: emit(T))
        return off + cnt.astype(jnp.int32)

    lax.fori_loop(0, L // T, group, jnp.zeros((E, 1), jnp.int32))


def _route_select(afft, cap):
    B, E, L = afft.shape
    idx = pl.pallas_call(
        functools.partial(_select_kernel, cap=cap),
        grid=(B,),
        in_specs=[pl.BlockSpec((1, E, L), lambda b: (b, 0, 0))],
        out_specs=pl.BlockSpec((1, E, cap + LANES, 1), lambda b: (b, 0, 0, 0)),
        out_shape=jax.ShapeDtypeStruct((B, E, cap + LANES, 1), jnp.int32),
        scratch_shapes=[pltpu.VMEM((E, L), F32)],
        compiler_params=_cparams(("parallel",)),
        name="route_select",
    )(afft)
    return idx[:, :, :cap, 0]


def _tile_rows(i, nt):
    return pl.ds(pl.multiple_of(i * nt, nt), nt)


def _gather_kernel(idx_ref, h_ref, aff_ref, xs_ref, gs_ref, buf_ref):
    cap, D = xs_ref.shape[1], xs_ref.shape[2]
    nt = D // LANES

    def body(s, c):
        base = s * ROW_GROUP
        rows = [idx_ref[0, 0, base + k] for k in range(ROW_GROUP)]
        vals = [h_ref[0, _tile_rows(r, nt), :] for r in rows]
        gates = [aff_ref[0, pl.ds(r, 1), :] for r in rows]
        for k in range(ROW_GROUP):
            buf_ref[_tile_rows(base + k, nt), :] = vals[k]
            gs_ref[0, pl.ds(base + k, 1), :] = gates[k]
        return c

    lax.fori_loop(0, cap // ROW_GROUP, body, 0)
    for c in range(nt):
        xs_ref[0, :, c * LANES:(c + 1) * LANES] = buf_ref[pl.ds(c, cap, stride=nt), :].astype(BF16)


def _route_gather(idx, h2t, aff, D):
    B, E, cap = idx.shape
    nt = D // LANES
    L = h2t.shape[1] // nt
    return pl.pallas_call(
        _gather_kernel,
        grid=(B, E),
        in_specs=[pl.BlockSpec((1, 1, cap), lambda b, e: (b * E + e, 0, 0), memory_space=pltpu.SMEM),
                  pl.BlockSpec((1, L * nt, LANES), lambda b, e: (b, 0, 0), pipeline_mode=pl.Buffered(1)),
                  pl.BlockSpec((1, L, E), lambda b, e: (b, 0, 0))],
        out_specs=[pl.BlockSpec((1, cap, D), lambda b, e: (e, b, 0)),
                   pl.BlockSpec((1, cap, E), lambda b, e: (e, b, 0))],
        out_shape=[jax.ShapeDtypeStruct((E, B * cap, D), BF16),
                   jax.ShapeDtypeStruct((E, B * cap, E), F32)],
        scratch_shapes=[pltpu.VMEM((cap * nt, LANES), F32)],
        compiler_params=_cparams(("arbitrary", "arbitrary")),
        name="route_gather",
    )(idx.reshape(B * E, 1, cap), h2t, aff)


def _combine_kernel(idx_ref, ye_ref, x1_ref, g2_ref, post2_ref, o_ref, y_ref, buf_ref, *, n_experts):
    j = pl.program_id(1)
    cap, D = ye_ref.shape[1], ye_ref.shape[2]
    nt = D // LANES
    tm = x1_ref.shape[1]

    @pl.when(j == 0)
    def _():
        y_ref[...] = jnp.zeros_like(y_ref)

    @pl.when(j < n_experts)
    def _():
        for c in range(nt):
            buf_ref[pl.ds(c, cap, stride=nt), :] = ye_ref[0, :, c * LANES:(c + 1) * LANES]

        def body(s, c):
            base = s * ROW_GROUP
            rows = [idx_ref[0, 0, base + k] for k in range(ROW_GROUP)]
            vals = [y_ref[_tile_rows(rows[k], nt), :] + buf_ref[_tile_rows(base + k, nt), :]
                    for k in range(ROW_GROUP)]
            for k in range(ROW_GROUP):
                y_ref[_tile_rows(rows[k], nt), :] = vals[k]
            return c

        lax.fori_loop(0, cap // ROW_GROUP, body, 0)

    @pl.when(j >= n_experts)
    def _():
        base = pl.multiple_of((j - n_experts) * (tm * nt), tm * nt)
        cols = [y_ref[pl.ds(base + c, tm, stride=nt), :] for c in range(nt)]
        ssq = sum(jnp.sum(y * y, axis=-1, keepdims=True) for y in cols)
        rstd = lax.rsqrt(ssq * (1.0 / D) + EPS)
        for c in range(nt):
            cs = slice(c * LANES, (c + 1) * LANES)
            o_ref[0, :, cs] = x1_ref[0, :, cs] + g2_ref[0, :, cs] * (cols[c] * rstd * post2_ref[:, cs])


def _route_combine(idx, ye, x1, mod3, post2, tm):
    B, E, cap = idx.shape
    _, L, D = x1.shape
    nt = D // LANES
    assert ye.shape[0] == E + 1
    tile = lambda j: jnp.maximum(j - E, 0)
    expert = lambda j: jnp.minimum(j, E - 1)
    return pl.pallas_call(
        functools.partial(_combine_kernel, n_experts=E),
        grid=(B, E + L // tm),
        in_specs=[pl.BlockSpec((1, 1, cap), lambda b, j: (b * E + expert(j), 0, 0), memory_space=pltpu.SMEM),
                  pl.BlockSpec((1, cap, D), lambda b, j: (expert(j) + 1, b, 0)),
                  pl.BlockSpec((1, tm, D), lambda b, j: (b, tile(j), 0)),
                  _mod_spec(D, 5, 0, True),
                  pl.BlockSpec((1, D), lambda b, j: (0, 0))],
        out_specs=pl.BlockSpec((1, tm, D), lambda b, j: (b, tile(j), 0)),
        out_shape=jax.ShapeDtypeStruct((B, L, D), F32),
        scratch_shapes=[pltpu.VMEM((L * nt, LANES), F32), pltpu.VMEM((cap * nt, LANES), F32)],
        compiler_params=_cparams(("arbitrary", "arbitrary")),
        name="route_combine_final",
    )(idx.reshape(B * E, 1, cap), ye, x1, mod3, post2)


def _expert_kernel(xs_ref, g_ref, wg_ref, wu_ref, wd_ref, ye_ref, wgb_ref, wub_ref, wdb_ref):
    s = pl.program_id(0)
    f = pl.program_id(1)
    ne = pl.num_programs(0) - 1
    tf = wg_ref.shape[2]
    slot = lax.rem(s, 2)

    @pl.when(s < ne)
    def _():
        cols = pl.ds(pl.multiple_of(f * tf, tf), tf)
        wgb_ref[slot, :, cols] = wg_ref[0].astype(BF16)
        wub_ref[slot, :, cols] = wu_ref[0].astype(BF16)
        wdb_ref[slot, cols, :] = wd_ref[0].astype(BF16)

    @pl.when(s == 0)
    def _():
        ye_ref[...] = jnp.zeros_like(ye_ref)

    @pl.when(s > 0)
    def _():
        e = s - 1
        prev = 1 - slot
        mt = xs_ref.shape[1] // FFN_M_SPLIT
        for mi in range(FFN_M_SPLIT):
            rs = slice(mi * mt, (mi + 1) * mt)
            xs = xs_ref[0, rs, :]
            a = jnp.dot(xs, wgb_ref[prev], preferred_element_type=F32)
            u = jnp.dot(xs, wub_ref[prev], preferred_element_type=F32)
            hmid = (a * jax.nn.sigmoid(a) * u).astype(BF16)
            out = jnp.dot(hmid, wdb_ref[prev], preferred_element_type=F32)
            gs = g_ref[0, rs, :]
            esel = lax.broadcasted_iota(jnp.int32, gs.shape, 1)
            ye_ref[0, rs, :] = out * jnp.sum(jnp.where(esel == e, gs, 0.0), axis=1, keepdims=True)


FFN_M_SPLIT = 2


def _expert_ffn(xs, gs, w_gate, w_up, w_down, tf):
    E, M, D = xs.shape
    F = w_gate.shape[2]
    nf = F // tf
    mt = M // nf
    behind = lambda s: jnp.maximum(s - 1, 0)
    ahead = lambda s: jnp.minimum(s, E - 1)
    return pl.pallas_call(
        _expert_kernel,
        grid=(E + 1, nf),
        in_specs=[pl.BlockSpec((1, mt, D), lambda s, f: (behind(s), f, 0)),
                  pl.BlockSpec((1, mt, E), lambda s, f: (behind(s), f, 0)),
                  pl.BlockSpec((1, D, tf), lambda s, f: (ahead(s), 0, f)),
                  pl.BlockSpec((1, D, tf), lambda s, f: (ahead(s), 0, f)),
                  pl.BlockSpec((1, tf, D), lambda s, f: (ahead(s), f, 0))],
        out_specs=pl.BlockSpec((1, mt, D), lambda s, f: (s, f, 0)),
        out_shape=jax.ShapeDtypeStruct((E + 1, M, D), F32),
        scratch_shapes=[pltpu.VMEM((2, D, F), BF16), pltpu.VMEM((2, D, F), BF16), pltpu.VMEM((2, F, D), BF16)],
        compiler_params=_cparams(("arbitrary", "arbitrary")),
        name="expert_ffn",
    )(xs, gs, w_gate, w_up, w_down)


def kernel(x, c, ctx, c_ctx, w_mod, b_mod, pre_norm1, post_norm1, pre_norm2, post_norm2, w_in, conv_w, conv_b, filt_w1, filt_b1, filt_w2, filt_b2, filt_w3, filt_b3, filt_w4, filt_freq, hyena_bias, ml_gate_b, ml_norm, w_out, w_router, w_exp_gate, w_exp_up, w_exp_down):
    B, L, D = x.shape
    depth = w_mod.shape[0]
    assert depth == 1, "single-layer block"
    li = 0
    tabs = _pe_tables(L // GRID_W, GRID_W, D)

    cc = jnp.concatenate([c, c_ctx[None], jnp.zeros((8 - B - 1, D), F32)], axis=0)
    mod3 = _modulation(cc, w_mod[li], b_mod[li])[:, None, :]

    pre1 = pre_norm1[li][None, :]
    w_proj = _in_proj_weights(w_in[li])
    zhy, q, kt, v, o, gatet = _in_proj(x, tabs, mod3, 0, pre1, w_proj, IN_PROJ_TILE, True)
    qc, ktc, vc, _, gatetc = _in_proj(ctx, None, mod3, B, pre1, w_proj, ctx.shape[1], False)

    kcirc = _hyena_filter(L, filt_w1[li], filt_b1[li], filt_w2[li], filt_b2[li], filt_w3[li],
                          filt_b3[li], filt_w4[li], filt_freq[li])
    y_hy = _hyena_conv(zhy, kcirc, conv_w[li], conv_b[li], hyena_bias[li])
    hsum = _mlstm(q, kt, v, gatet, qc, ktc, vc, gatetc, ml_gate_b[li])

    x1, h2t, aff, afft = _out_proj(y_hy, hsum, o, x, tabs, mod3, ml_norm[li][None, :],
                                   post_norm1[li][None, :], pre_norm2[li][None, :], w_out[li],
                                   w_router[li], IN_PROJ_TILE)

    cap = CAP_FACTOR * L // N_EXPERTS
    idx = _route_select(afft, cap)
    xs, gs = _route_gather(idx, h2t, aff, D)
    ye = _expert_ffn(xs, gs, w_exp_gate[li], w_exp_up[li], w_exp_down[li], 512)
    return _route_combine(idx, ye, x1, mod3, post_norm2[li][None, :], TOKEN_TILE)
```

```python
import functools
import math

import jax
import jax.numpy as jnp
from jax import lax
from jax.experimental import pallas as pl
from jax.experimental.pallas import tpu as pltpu

GRID_W = 64
HY_WIDTH = 512
ML_HEADS = 4
ML_DK = 64
ML_DV = 128
ML_WIDTH = ML_HEADS * ML_DV
HY_COLS = 3 * HY_WIDTH
QK_COLS = ML_HEADS * ML_DK
N_GATES = 4 * ML_HEADS
FILTER_EMB = 33
DECAY_TARGET = 1e-2
FAST_DECAY_PCT = 0.3
SLOW_DECAY_PCT = 1.5
CHUNK = 128
GATE_CAP = 15.0
N_EXPERTS = 16
CAP_FACTOR = 2
EPS = 1e-6

F32 = jnp.float32
BF16 = jnp.bfloat16

TOKEN_TILE = 512
TIME_ROWS = 8
IN_PROJ_TILE = 1024
VMEM_LIMIT = 56 * 1024 * 1024


def _cparams(sem):
    return pltpu.CompilerParams(dimension_semantics=sem, vmem_limit_bytes=VMEM_LIMIT)


def _rms(xf, g):
    return xf * lax.rsqrt(jnp.mean(xf * xf, axis=-1, keepdims=True) + EPS) * g


def _bdot(a, b):
    return jnp.dot(a.astype(BF16), b.astype(BF16), preferred_element_type=F32)


def _pe_tables_kernel(omega_ref, er_ref, ec_ref):
    quarter = omega_ref.shape[1]
    om = omega_ref[...]
    for ref in (er_ref, ec_ref):
        n = ref.shape[0]
        pos = lax.broadcasted_iota(jnp.int32, (n, quarter), 0).astype(F32)
        ang = pos * om
        ref[:, :quarter] = jnp.sin(ang)
        ref[:, quarter:] = jnp.cos(ang)


def _pe_tables(rows, cols, dim):
    quarter = dim // 4
    omega = (1.0 / (10000.0 ** (jnp.arange(quarter, dtype=F32) / quarter)))[None, :]
    return pl.pallas_call(
        _pe_tables_kernel,
        out_shape=(jax.ShapeDtypeStruct((rows, dim // 2), F32),
                   jax.ShapeDtypeStruct((cols, dim // 2), F32)),
        name="pe_tables",
    )(omega)


def _pe_tile(er_blk, ec):
    nr, half = er_blk.shape
    row_part = jnp.broadcast_to(er_blk[:, None, :], (nr, GRID_W, half)).reshape(nr * GRID_W, half)
    col_part = jnp.broadcast_to(ec[None, :, :], (nr, GRID_W, half)).reshape(nr * GRID_W, half)
    return jnp.concatenate([row_part, col_part], axis=-1)


def _mod_kernel(c_ref, w_ref, b_ref, o_ref):
    c = c_ref[...]
    s = c * jax.nn.sigmoid(c)
    o_ref[...] = _bdot(s, w_ref[...]) + b_ref[...]


def _modulation(cc, w_mod, b_mod):
    rows, d = cc.shape
    n = w_mod.shape[1]
    tn = 512
    return pl.pallas_call(
        _mod_kernel,
        grid=(n // tn,),
        in_specs=[pl.BlockSpec((rows, d), lambda j: (0, 0)),
                  pl.BlockSpec((d, tn), lambda j: (0, j)),
                  pl.BlockSpec((1, tn), lambda j: (0, j))],
        out_specs=pl.BlockSpec((rows, tn), lambda j: (0, j)),
        out_shape=jax.ShapeDtypeStruct((rows, n), F32),
        compiler_params=_cparams(("arbitrary",)),
        name="modulation",
    )(cc, w_mod, b_mod[None, :])


def _in_proj_kernel(*refs, with_hyena):
    if with_hyena:
        (x_ref, er_ref, ec_ref, sh_ref, sc_ref, g_ref, wn_ref, wt_ref,
         zt_ref, q_ref, kt_ref, v_ref, o_ref, gatet_ref) = refs
        xf = x_ref[0] + _pe_tile(er_ref[...], ec_ref[...])
    else:
        (x_ref, sh_ref, sc_ref, g_ref, wn_ref, wt_ref,
         q_ref, kt_ref, v_ref, o_ref, gatet_ref) = refs
        xf = x_ref[0]
    h = _rms(xf, g_ref[...]) * (1.0 + sc_ref[0]) + sh_ref[0]
    hb = h.astype(BF16)
    z = jnp.dot(hb, wn_ref[...], preferred_element_type=F32)
    qs = z[:, :QK_COLS] * (ML_DK ** -0.5)
    for hd in range(ML_HEADS):
        q_ref[0, hd] = qs[:, hd * ML_DK:(hd + 1) * ML_DK].astype(BF16)
    v_ref[0] = z[:, QK_COLS:QK_COLS + ML_WIDTH].astype(BF16)
    o_ref[0] = z[:, QK_COLS + ML_WIDTH:]
    zt = lax.dot_general(wt_ref[...], hb, (((1,), (1,)), ((), ())), preferred_element_type=F32)
    off = 0
    if with_hyena:
        for j in range(zt.shape[1] // LANES):
            zt_ref[0, 0, pl.ds(j, HY_COLS, stride=TIME_ROWS), :] = zt[:HY_COLS, j * LANES:(j + 1) * LANES]
        off = HY_COLS
    for hd in range(ML_HEADS):
        kt_ref[0, hd] = zt[off + hd * ML_DK:off + (hd + 1) * ML_DK, :].astype(BF16)
    gatet_ref[0] = zt[off + QK_COLS:, :]


def _mod_spec(D, chunk, row0, per_sample):
    return pl.BlockSpec((1, 1, D), lambda b, i: ((b if per_sample else 0) + row0, 0, chunk))


def _in_proj_weights(w_in):
    w_q = w_in[:, HY_COLS:HY_COLS + QK_COLS]
    w_k = w_in[:, HY_COLS + QK_COLS:HY_COLS + 2 * QK_COLS]
    w_vo = w_in[:, HY_COLS + 2 * QK_COLS:HY_COLS + 2 * QK_COLS + 2 * ML_WIDTH]
    w_g = w_in[:, HY_COLS + 2 * QK_COLS + 2 * ML_WIDTH:]
    wn = jnp.concatenate([w_q, w_vo], axis=1).astype(BF16)
    wt = jnp.concatenate([w_in[:, :HY_COLS], w_k, w_g], axis=1).T.astype(BF16)
    return wn, wt


def _in_proj(x, tabs, mod3, mod_row0, g, weights, tm, with_hyena):
    B, L, D = x.shape
    wn, wt = weights
    if not with_hyena:
        wt = wt[HY_COLS:]
    full = lambda a: pl.BlockSpec(a.shape, lambda b, i: (0,) * a.ndim)
    in_specs = [pl.BlockSpec((1, tm, D), lambda b, i: (b, i, 0))]
    args = [x]
    if with_hyena:
        er, ec = tabs
        in_specs += [pl.BlockSpec((tm // GRID_W, D // 2), lambda b, i: (i, 0)), full(ec)]
        args += [er, ec]
    in_specs += [_mod_spec(D, 0, mod_row0, with_hyena), _mod_spec(D, 1, mod_row0, with_hyena),
                 full(g), full(wn), full(wt)]
    args += [mod3, mod3, g, wn, wt]
    out_shape, out_specs = [], []
    if with_hyena:
        assert tm == TIME_ROWS * LANES
        out_shape.append(jax.ShapeDtypeStruct((B, L // tm, HY_COLS * TIME_ROWS, LANES), F32))
        out_specs.append(pl.BlockSpec((1, 1, HY_COLS * TIME_ROWS, LANES), lambda b, i: (b, i, 0, 0)))
    out_shape += [jax.ShapeDtypeStruct((B, ML_HEADS, L, ML_DK), BF16),
                  jax.ShapeDtypeStruct((B, ML_HEADS, ML_DK, L), BF16),
                  jax.ShapeDtypeStruct((B, L, ML_WIDTH), BF16),
                  jax.ShapeDtypeStruct((B, L, ML_WIDTH), F32),
                  jax.ShapeDtypeStruct((B, N_GATES, L), F32)]
    out_specs += [pl.BlockSpec((1, ML_HEADS, tm, ML_DK), lambda b, i: (b, 0, i, 0)),
                  pl.BlockSpec((1, ML_HEADS, ML_DK, tm), lambda b, i: (b, 0, 0, i)),
                  pl.BlockSpec((1, tm, ML_WIDTH), lambda b, i: (b, i, 0)),
                  pl.BlockSpec((1, tm, ML_WIDTH), lambda b, i: (b, i, 0)),
                  pl.BlockSpec((1, N_GATES, tm), lambda b, i: (b, 0, i))]
    return pl.pallas_call(
        functools.partial(_in_proj_kernel, with_hyena=with_hyena),
        grid=(B, L // tm),
        in_specs=in_specs,
        out_specs=out_specs,
        out_shape=out_shape,
        compiler_params=_cparams(("parallel", "parallel")),
        name="in_proj_hy" if with_hyena else "in_proj_ctx",
    )(*args)


FILT_TILE = 1024
FILT_CBLK = 128


def _filter_kernel(w1a_ref, w1b_ref, w1c_ref, b1_ref, w2_ref, b2_ref, w3_ref, b3_ref, fr_ref,
                   w4f_ref, w4b_ref, dl_ref, mir_ref, k_ref, hf_ref, *, L):
    bands = (FILTER_EMB - 1) // 2

    @pl.when(pl.program_id(0) == 0)
    def _():
        fk = (1e-4 + lax.broadcasted_iota(jnp.int32, (bands, 1), 0).astype(F32)
              * ((bands - 1 - 1e-4) / (bands - 1)))
        fr = fr_ref[...]
        for j in range(L // FILT_TILE):
            pos = (lax.broadcasted_iota(jnp.int32, (1, FILT_TILE), 1) + j * FILT_TILE).astype(F32)
            tl = pos * (1.0 / (L - 1))
            ang = fk * (pos * (2.0 * math.pi / L))
            pre = (w1a_ref[...].astype(F32) * tl.astype(BF16).astype(F32)
                   + _bdot(w1b_ref[...], jnp.cos(ang)) + _bdot(w1c_ref[...], -jnp.sin(ang)))
            h = jnp.sin(fr * (pre + b1_ref[...]))
            h = jnp.sin(fr * (_bdot(w2_ref[...], h) + b2_ref[...]))
            h = jnp.sin(fr * (_bdot(w3_ref[...], h) + b3_ref[...]))
            hf_ref[:, j * FILT_TILE:(j + 1) * FILT_TILE] = h.astype(BF16)

    pos = lax.broadcasted_iota(jnp.int32, (1, L), 1).astype(F32)
    decay = jnp.exp(-(pos * (1.0 / (L - 1))) * dl_ref[...])
    hf = hf_ref[...]
    k_ref[:, :L] = jnp.dot(w4f_ref[...], hf, preferred_element_type=F32) * decay
    gb = (jnp.dot(w4b_ref[...], hf, preferred_element_type=F32) * decay).astype(BF16)
    nblk = L // LANES
    for j in range(nblk):
        src = gb[:, (nblk - 1 - j) * LANES:(nblk - j) * LANES]
        nxt = gb[:, (nblk - j) * LANES:(nblk - j + 1) * LANES] if j > 0 else jnp.zeros_like(src)
        k_ref[:, L + j * LANES:L + (j + 1) * LANES] = jnp.dot(
            jnp.concatenate([src, nxt], axis=1), mir_ref[...], preferred_element_type=F32)


def _hyena_filter(L, w1, b1, w2, b2, w3, b3, w4, freq):
    hid = w2.shape[0]
    bands = (FILTER_EMB - 1) // 2
    col = lambda a: a[:, None]
    w1t = w1.T.astype(BF16)
    min_decay = math.log(DECAY_TARGET) / SLOW_DECAY_PCT
    max_decay = math.log(DECAY_TARGET) / FAST_DECAY_PCT
    dl = jnp.abs(jnp.linspace(min_decay, max_decay, HY_WIDTH, dtype=F32))[:, None]
    w4t = w4.T.astype(BF16)
    assert DFT_PASSES == 1, "the mirrored taps are kept at the bf16 precision a single-pass DFT reads"
    lane = jnp.arange(LANES)
    mir = jnp.concatenate([(lane[:, None] + lane[None, :] == LANES),
                           (lane[:, None] == 0) & (lane[None, :] == 0)], axis=0).astype(BF16)
    full = lambda a: pl.BlockSpec(a.shape, lambda i: (0,) * a.ndim)
    args = [w1t[:, 0:1], w1t[:, 1:1 + bands], w1t[:, 1 + bands:], col(b1), w2.T.astype(BF16), col(b2),
            w3.T.astype(BF16), col(b3), col(freq)]
    return pl.pallas_call(
        functools.partial(_filter_kernel, L=L),
        grid=(HY_WIDTH // FILT_CBLK,),
        in_specs=[full(a) for a in args] + [
            pl.BlockSpec((FILT_CBLK, hid), lambda i: (i, 0)),
            pl.BlockSpec((FILT_CBLK, hid), lambda i: (HY_WIDTH // FILT_CBLK + i, 0)),
            pl.BlockSpec((FILT_CBLK, 1), lambda i: (i, 0)), full(mir)],
        out_specs=pl.BlockSpec((FILT_CBLK, 2 * L), lambda i: (i, 0)),
        out_shape=jax.ShapeDtypeStruct((HY_WIDTH, 2 * L), F32),
        scratch_shapes=[pltpu.VMEM((hid, L), BF16)],
        compiler_params=_cparams(("arbitrary",)),
        name="hyena_filter",
    )(*args, w4t, w4t, dl, mir)


FFT_N = 128
HY_CBLK = 32
HY_GROUP = 8
HY_UNROLL = 4
DFT_PASSES = 1


def _dft_constants(n1_data):
    import numpy as np
    n = FFT_N
    k = np.arange(n)
    ang = -2.0 * np.pi * ((k[:, None] * k[None, :]) % n) / n
    fre, fim = np.cos(ang), np.sin(ang)
    m = n1_data
    fa_d = np.block([[fre[:, :m], -fim[:, :m]], [fim[:, :m], fre[:, :m]]])
    fa_f = np.concatenate([fre, fim], axis=0)
    fb = np.block([[fre, fim], [-fim, fre]])
    fbi = np.block([[fre, -fim], [fim, fre]])
    fc = np.block([[fre[:m, :], fim[:m, :]], [-fim[:m, :], fre[:m, :]]]) / (n * n)
    tang = -2.0 * np.pi * (k[:, None] * k[None, :]) / (n * n)
    tw = np.stack([np.cos(tang), np.sin(tang)])

    def hilo(a):
        a32 = jnp.asarray(a, F32)
        hi = a32.astype(BF16)
        lo = (a32 - hi.astype(F32)).astype(BF16)
        return jnp.stack([hi, lo])

    return hilo(fa_d), hilo(fa_f), hilo(fb), hilo(fbi), hilo(fc), jnp.asarray(tw, F32)


def _mm_const_lhs(c_ref, d):
    dh = d.astype(BF16)
    acc = jnp.dot(c_ref[0], dh, preferred_element_type=F32)
    if DFT_PASSES == 3:
        dl = (d - dh.astype(F32)).astype(BF16)
        acc = acc + (jnp.dot(c_ref[0], dl, preferred_element_type=F32)
                     + jnp.dot(c_ref[1], dh, preferred_element_type=F32))
    return acc


def _mm_const_rhs(d, c_ref):
    dh = d.astype(BF16)
    acc = jnp.dot(dh, c_ref[0], preferred_element_type=F32)
    if DFT_PASSES == 3:
        dl = (d - dh.astype(F32)).astype(BF16)
        acc = acc + (jnp.dot(dl, c_ref[0], preferred_element_type=F32)
                     + jnp.dot(dh, c_ref[1], preferred_element_type=F32))
    return acc


def _cmul(are, aim, bre, bim):
    return are * bre - aim * bim, are * bim + aim * bre


def _hyena_conv_kernel(x0_ref, x1_ref, v_ref, kc_ref, w0_ref, w1_ref, wv_ref, b0_ref, b1_ref, bv_ref,
                       hb_ref, fad_ref, faf_ref, fb_ref, fbi_ref, fc_ref, tw_ref,
                       o_ref, u_ref, s_ref, ks_ref):
    n = FFT_N
    cb = kc_ref.shape[0]
    nb, nt = x0_ref.shape[0], x0_ref.shape[1]
    m = nt * TIME_ROWS

    def chan(ref, b, c):
        return ref[b, :, pl.ds(pl.multiple_of(c * TIME_ROWS, TIME_ROWS), TIME_ROWS), :].reshape(m, n)

    sub = lax.broadcasted_iota(jnp.int32, (m, n), 0)
    lane = lax.broadcasted_iota(jnp.int32, (m, n), 1)
    tre, tim = tw_ref[0], tw_ref[1]

    def sconv(z, w_ref, b_ref, c):
        a = pltpu.roll(z, 1, axis=1)
        prev = jnp.where(lane == 0, jnp.where(sub == 0, 0.0, pltpu.roll(a, 1, axis=0)), a)
        a2 = pltpu.roll(z, n - 1, axis=1)
        nxt = jnp.where(lane == n - 1, jnp.where(sub == m - 1, 0.0, pltpu.roll(a2, m - 1, axis=0)), a2)
        return prev * w_ref[0, c] + z * w_ref[1, c] + nxt * w_ref[2, c] + b_ref[c]

    def spectrum_rows(res):
        outs = []
        for h in range(2):
            are, aim = _cmul(res[:n, h * n:(h + 1) * n], res[n:, h * n:(h + 1) * n], tre, tim)
            outs.append(jnp.concatenate([are, aim], axis=1))
        return outs

    def fwd_pair(p, carry):
        c0 = 2 * p
        us = []
        for c in (c0, c0 + 1):
            ub = []
            for b in range(nb):
                x1c = sconv(chan(x1_ref, b, c), w1_ref, b1_ref, c)
                vc = sconv(chan(v_ref, b, c), wv_ref, bv_ref, c)
                u = x1c * vc
                u_ref[b, c] = u
                ub.append(u)
            us.append(ub)
        wd = jnp.concatenate([jnp.concatenate([us[0][b], us[1][b]], axis=1) for b in range(nb)], axis=0)
        sa, sb = spectrum_rows(_mm_const_lhs(fad_ref, wd))
        s_ref[c0] = sa.astype(s_ref.dtype)
        s_ref[c0 + 1] = sb.astype(s_ref.dtype)
        wk = jnp.concatenate([kc_ref[c0], kc_ref[c0 + 1]], axis=1)
        ka, kb = spectrum_rows(_mm_const_lhs(faf_ref, wk))
        ks_ref[c0] = ka.astype(ks_ref.dtype)
        ks_ref[c0 + 1] = kb.astype(ks_ref.dtype)
        return carry

    lax.fori_loop(0, cb // 2, fwd_pair, 0, unroll=HY_UNROLL)

    def mid_group(g, carry):
        gs = pl.ds(pl.multiple_of(g * HY_GROUP, HY_GROUP), HY_GROUP)
        x = _mm_const_rhs(s_ref[gs].reshape(HY_GROUP * n, 2 * n), fb_ref)
        k = _mm_const_rhs(ks_ref[gs].reshape(HY_GROUP * n, 2 * n), fb_ref)
        yre, yim = _cmul(x[:, :n], x[:, n:], k[:, :n], k[:, n:])
        vv = _mm_const_rhs(jnp.concatenate([yre, yim], axis=1), fbi_ref).reshape(HY_GROUP, n, 2 * n)
        vre, vim = _cmul(vv[:, :, :n], vv[:, :, n:], tre[None], -tim[None])
        s_ref[gs] = jnp.concatenate([vre, vim], axis=2).astype(s_ref.dtype)
        return carry

    lax.fori_loop(0, cb // HY_GROUP, mid_group, 0, unroll=HY_UNROLL)

    def inv_pair(p, carry):
        c0 = 2 * p
        sa, sb = s_ref[c0], s_ref[c0 + 1]
        wd = jnp.concatenate([jnp.concatenate([sa[:, :n], sb[:, :n]], axis=1),
                              jnp.concatenate([sa[:, n:], sb[:, n:]], axis=1)], axis=0)
        res = _mm_const_lhs(fc_ref, wd)
        for h, c in enumerate((c0, c0 + 1)):
            for b in range(nb):
                y = res[b * m:(b + 1) * m, h * n:(h + 1) * n]
                x0c = sconv(chan(x0_ref, b, c), w0_ref, b0_ref, c)
                out = x0c * (y + hb_ref[c] * u_ref[b, c])
                o_ref[b, :, pl.ds(pl.multiple_of(c * TIME_ROWS, TIME_ROWS), TIME_ROWS), :] = out.reshape(
                    nt, TIME_ROWS, n)
        return carry

    lax.fori_loop(0, cb // 2, inv_pair, 0, unroll=HY_UNROLL)


def _hyena_conv(zt, kcirc, conv_w, conv_b, hy_bias):
    B, nt, _, n = zt.shape
    m = nt * TIME_ROWS
    L = m * n
    C = HY_WIDTH
    assert B == 2 and n == FFT_N and 2 * L == n * n, "complex packing of two samples over a 128 x 128 point transform"
    z4 = zt
    rows = HY_CBLK * TIME_ROWS
    spec_dt = BF16 if DFT_PASSES == 1 else F32
    k3 = kcirc.reshape(C, n, n)
    cw = conv_w.reshape(3, 3 * C, 1, 1)
    cbias = conv_b.reshape(3 * C, 1, 1)
    hb = hy_bias.reshape(C, 1, 1)
    consts = _dft_constants(m)
    nblk = C // HY_CBLK
    zspec = lambda part: pl.BlockSpec((B, nt, rows, n), lambda i: (0, 0, part * nblk + i, 0))
    wspec = lambda part: pl.BlockSpec((3, HY_CBLK, 1, 1), lambda i: (0, part * nblk + i, 0, 0))
    bspec = lambda part: pl.BlockSpec((HY_CBLK, 1, 1), lambda i: (part * nblk + i, 0, 0))
    full = lambda a: pl.BlockSpec(a.shape, lambda i: (0,) * a.ndim)
    y = pl.pallas_call(
        _hyena_conv_kernel,
        grid=(nblk,),
        in_specs=[zspec(0), zspec(1), zspec(2), pl.BlockSpec((HY_CBLK, n, n), lambda i: (i, 0, 0)),
                  wspec(0), wspec(1), wspec(2), bspec(0), bspec(1), bspec(2), bspec(0)]
                 + [full(a) for a in consts],
        out_specs=pl.BlockSpec((B, nt, rows, n), lambda i: (0, 0, i, 0)),
        out_shape=jax.ShapeDtypeStruct((B, nt, C * TIME_ROWS, n), F32),
        scratch_shapes=[pltpu.VMEM((B, HY_CBLK, m, n), F32), pltpu.VMEM((HY_CBLK, n, 2 * n), spec_dt),
                        pltpu.VMEM((HY_CBLK, n, 2 * n), spec_dt)],
        compiler_params=_cparams(("parallel",)),
        name="hyena_conv",
    )(z4, z4, z4, k3, cw, cw, cw, cbias, cbias, cbias, hb, *consts)
    return y


def _split3(a):
    hi = a.astype(BF16)
    r1 = a - hi.astype(F32)
    mid = r1.astype(BF16)
    lo = (r1 - mid.astype(F32)).astype(BF16)
    return hi, mid, lo


def _exact_dot_right(a, tri_bf):
    hi, mid, lo = _split3(a)
    d = lambda p: jnp.dot(p, tri_bf, preferred_element_type=F32)
    return (d(lo) + d(mid)) + d(hi)


def _soft_gates(g):
    g = GATE_CAP * jnp.tanh(g * (1.0 / GATE_CAP))
    logsig = jnp.minimum(g, 0.0) - jnp.log1p(jnp.exp(-jnp.abs(g)))
    return g, logsig


def _gate_prep_kernel(gt_ref, gbt_ref, rows_ref, cols_ref):
    T = CHUNK
    n = gt_ref.shape[2]
    up_bf = (lax.broadcasted_iota(jnp.int32, (T, T), 1) >= lax.broadcasted_iota(jnp.int32, (T, T), 0)).astype(BF16)
    out8 = lax.broadcasted_iota(jnp.int32, (8, 1), 0)
    cap, ls = _soft_gates(gt_ref[0] + gbt_ref[...])
    H = ML_HEADS
    head = lambda hd: jnp.where(out8 == 0, cap[hd:hd + 1], jnp.where(out8 == 1, ls[H + hd:H + hd + 1],
                                jnp.where(out8 == 2, cap[2 * H + hd:2 * H + hd + 1],
                                          jnp.where(out8 == 3, ls[3 * H + hd:3 * H + hd + 1], 0.0))))
    base = jnp.concatenate([head(hd) for hd in range(H)], axis=0)
    kind = jnp.concatenate([out8] * H, axis=0)
    for j in range(n // T):
        blk = base[:, j * T:(j + 1) * T]
        run = _exact_dot_right(blk, up_bf)
        suf = run[:, T - 1:T] - run + blk
        rows = jnp.where(kind == 1, run, jnp.where(kind == 3, suf, blk))
        cols = rows.T
        for hd in range(H):
            rows_ref[0, hd, :, j * T:(j + 1) * T] = rows[8 * hd:8 * (hd + 1)]
            cols_ref[0, hd, j * T:(j + 1) * T, :] = cols[:, 8 * hd:8 * (hd + 1)]


def _gate_prep(gatet, gate_b, tile):
    B, G, L = gatet.shape
    H = ML_HEADS
    return pl.pallas_call(
        _gate_prep_kernel,
        grid=(B, L // tile),
        in_specs=[pl.BlockSpec((1, G, tile), lambda b, i: (b, 0, i)),
                  pl.BlockSpec((G, 1), lambda b, i: (0, 0))],
        out_specs=[pl.BlockSpec((1, H, 8, tile), lambda b, i: (b, 0, 0, i)),
                   pl.BlockSpec((1, H, tile, 8), lambda b, i: (b, 0, i, 0))],
        out_shape=[jax.ShapeDtypeStruct((B, H, 8, L), F32), jax.ShapeDtypeStruct((B, H, L, 8), F32)],
        compiler_params=_cparams(("parallel", "parallel")),
        name="mlstm_gate_prep",
    )(gatet, gate_b[:, None])


def _mlstm_kernel(q_ref, kt_ref, v_ref, r_ref, bt_ref, qc_ref, ktc_ref, vc_ref, rc_ref, btc_ref,
                  h_ref, cf_ref, cb_ref):
    T = CHUNK
    L = q_ref.shape[2]
    Lc = qc_ref.shape[2]
    nc, ncc = L // T, Lc // T
    row = lax.broadcasted_iota(jnp.int32, (T, T), 0)
    col = lax.broadcasted_iota(jnp.int32, (T, T), 1)
    lo_mask = col <= row
    up_mask = col >= row
    ones_blk = jnp.ones((T, ML_DV), BF16)

    cf_ref[...] = jnp.zeros_like(cf_ref)
    cb_ref[...] = jnp.zeros_like(cb_ref)

    def chunk_step(q, kt, v, rows, cols, c_ref, backward):
        i_r = rows[2:3] if backward else rows[0:1]
        b_r = rows[3:4] if backward else rows[1:2]
        b_c = jnp.broadcast_to(cols[:, 3:4] if backward else cols[:, 1:2], (T, T))
        b_end = b_r[:, 0:1] if backward else b_r[:, T - 1:T]
        mask = up_mask if backward else lo_mask
        w_intra = jnp.exp(jnp.where(mask, b_c - b_r + i_r, -jnp.inf) - GATE_CAP)
        s = jnp.dot(q, kt, preferred_element_type=F32) * w_intra
        qe = (q.astype(F32) * jnp.exp(b_c[:, :ML_DK])).astype(BF16)
        v_aug = jnp.concatenate([v, ones_blk], axis=1)
        c_aug = c_ref[...]
        res = jnp.dot(jnp.concatenate([s.astype(BF16), qe], axis=1),
                      jnp.concatenate([v_aug, c_aug.astype(BF16)], axis=0),
                      preferred_element_type=F32)
        h = res[:, :ML_DV] / jnp.maximum(jnp.abs(res[:, ML_DV:]), math.exp(-GATE_CAP))
        kw = (kt.astype(F32) * jnp.exp(b_end - b_r + i_r - GATE_CAP)).astype(BF16)
        c_ref[...] = jnp.exp(b_end) * c_aug + jnp.dot(kw, v_aug, preferred_element_type=F32)
        return h

    for j in range(ncc):
        for backward in (False, True):
            jj = (ncc - 1 - j) if backward else j
            cs = slice(jj * T, (jj + 1) * T)
            chunk_step(qc_ref[0, 0, cs, :], ktc_ref[0, 0, :, cs], vc_ref[0, cs, :], rc_ref[0, 0, :, cs],
                       btc_ref[0, 0, cs, :], cb_ref if backward else cf_ref, backward)

    def latent_pair(j, accumulate):
        for backward in (False, True):
            jj = (nc - 1 - j) if backward else j
            rs = pl.ds(pl.multiple_of(jj * T, T), T)
            h = chunk_step(q_ref[0, 0, rs, :], kt_ref[0, 0, :, rs], v_ref[0, rs, :], r_ref[0, 0, :, rs],
                           bt_ref[0, 0, rs, :], cb_ref if backward else cf_ref, backward)
            if accumulate:
                h_ref[0, rs, :] = h_ref[0, rs, :] + h
            else:
                h_ref[0, rs, :] = h

    def first_half(j, carry):
        latent_pair(j, False)
        return carry

    def second_half(j, carry):
        latent_pair(j, True)
        return carry

    lax.fori_loop(0, nc // 2, first_half, 0, unroll=MLSTM_UNROLL)
    lax.fori_loop(nc // 2, nc, second_half, 0, unroll=MLSTM_UNROLL)


MLSTM_UNROLL = 4


def _mlstm(q, kt, v, gatet, qc, ktc, vc, gatetc, gate_b):
    B, H, L, dk = q.shape
    Lc = qc.shape[2]
    rows, cols = _gate_prep(gatet, gate_b, min(L, 1024))
    rows_c, cols_c = _gate_prep(gatetc, gate_b, Lc)
    seq = lambda n: [pl.BlockSpec((1, 1, n, dk), lambda b, h: (b, h, 0, 0)),
                     pl.BlockSpec((1, 1, dk, n), lambda b, h: (b, h, 0, 0)),
                     pl.BlockSpec((1, n, ML_DV), lambda b, h: (b, 0, h)),
                     pl.BlockSpec((1, 1, 8, n), lambda b, h: (b, h, 0, 0)),
                     pl.BlockSpec((1, 1, n, 8), lambda b, h: (b, h, 0, 0))]
    return pl.pallas_call(
        _mlstm_kernel,
        grid=(B, H),
        in_specs=seq(L) + seq(Lc),
        out_specs=pl.BlockSpec((1, L, ML_DV), lambda b, h: (b, 0, h)),
        out_shape=jax.ShapeDtypeStruct((B, L, ML_WIDTH), F32),
        scratch_shapes=[pltpu.VMEM((dk, 2 * ML_DV), F32), pltpu.VMEM((dk, 2 * ML_DV), F32)],
        compiler_params=_cparams(("parallel", "parallel")),
        name="mlstm_scan",
    )(q, kt, v, rows, cols, qc, ktc, vc, rows_c, cols_c)


def _out_proj_kernel(yhy_ref, hs_ref, o_ref, x_ref, er_ref, ec_ref, g1_ref, sh2_ref, sc2_ref,
                     mln_ref, post1_ref, pre2_ref, wout_ref, wr_ref, wrt_ref,
                     x1_ref, h2_ref, aff_ref, afft_ref):
    hs = hs_ref[0]
    parts = []
    for hd in range(ML_HEADS):
        hh = hs[:, hd * ML_DV:(hd + 1) * ML_DV]
        parts.append(hh * lax.rsqrt(jnp.mean(hh * hh, axis=-1, keepdims=True) + EPS))
    hn = jnp.concatenate(parts, axis=-1) * mln_ref[...]
    y_ml = hn * jax.nn.sigmoid(o_ref[0])
    yo_hy = [lax.dot_general(yhy_ref[0, 0, pl.ds(j, HY_WIDTH, stride=TIME_ROWS), :].astype(BF16),
                             wout_ref[:HY_WIDTH, :], (((0,), (0,)), ((), ())), preferred_element_type=F32)
             for j in range(TIME_ROWS)]
    yo = (jnp.concatenate(yo_hy, axis=0)
          + jnp.dot(y_ml.astype(BF16), wout_ref[HY_WIDTH:, :], preferred_element_type=F32))
    xf = x_ref[0] + _pe_tile(er_ref[...], ec_ref[...])
    x1 = xf + g1_ref[0] * _rms(yo, post1_ref[...])
    x1_ref[0] = x1
    h2f = _rms(x1, pre2_ref[...]) * (1.0 + sc2_ref[0]) + sh2_ref[0]
    _store_row_tiles(h2_ref, h2f)
    h2 = h2f.astype(BF16)
    logits = jnp.dot(h2, wr_ref[...], preferred_element_type=F32)
    ex = jnp.exp(logits - jnp.max(logits, axis=-1, keepdims=True))
    aff_ref[0] = ex / jnp.sum(ex, axis=-1, keepdims=True)
    logits_t = lax.dot_general(wrt_ref[...], h2, (((1,), (1,)), ((), ())), preferred_element_type=F32)
    ext = jnp.exp(logits_t - jnp.max(logits_t, axis=0, keepdims=True))
    afft_ref[0] = ext / jnp.sum(ext, axis=0, keepdims=True)


def _out_proj(y_hy, hsum, o, x, tabs, mod3, ml_norm, post1, pre2, w_out, w_router, tm):
    B, L, D = x.shape
    er, ec = tabs
    E = w_router.shape[1]
    assert tm == TIME_ROWS * LANES
    full = lambda a: pl.BlockSpec(a.shape, lambda b, i: (0,) * a.ndim)
    tok = lambda w: pl.BlockSpec((1, tm, w), lambda b, i: (b, i, 0))
    wout = w_out.astype(BF16)
    wr = w_router.astype(BF16)
    wrt = w_router.T.astype(BF16)
    return pl.pallas_call(
        _out_proj_kernel,
        grid=(B, L // tm),
        in_specs=[pl.BlockSpec((1, 1, HY_WIDTH * TIME_ROWS, LANES), lambda b, i: (b, i, 0, 0)),
                  tok(ML_WIDTH), tok(ML_WIDTH), tok(D),
                  pl.BlockSpec((tm // GRID_W, D // 2), lambda b, i: (i, 0)), full(ec),
                  _mod_spec(D, 2, 0, True), _mod_spec(D, 3, 0, True), _mod_spec(D, 4, 0, True),
                  full(ml_norm), full(post1), full(pre2),
                  full(wout), full(wr), full(wrt)],
        out_specs=[tok(D), pl.BlockSpec((1, tm * (D // LANES), LANES), lambda b, i: (b, i, 0)), tok(E),
                   pl.BlockSpec((1, E, tm), lambda b, i: (b, 0, i))],
        out_shape=[jax.ShapeDtypeStruct((B, L, D), F32),
                   jax.ShapeDtypeStruct((B, L * (D // LANES), LANES), F32),
                   jax.ShapeDtypeStruct((B, L, E), F32), jax.ShapeDtypeStruct((B, E, L), F32)],
        compiler_params=_cparams(("parallel", "parallel")),
        name="out_proj_router",
    )(y_hy, hsum, o, x, er, ec, mod3, mod3, mod3, ml_norm, post1, pre2, wout, wr, wrt)


LANES = 128
ROW_GROUP = 16
SELECT_FAST_SLOTS = 32


def _store_row_tiles(ref, val):
    n, nt = val.shape[0], val.shape[1] // LANES
    for c in range(nt):
        ref[0, pl.ds(c, n, stride=nt), :] = val[:, c * LANES:(c + 1) * LANES]


def _select_kernel(afft_ref, gind_ref, sel_ref, off_ref, gmax_ref, *, cap):
    E, L = afft_ref.shape[1], afft_ref.shape[2]
    aff = afft_ref[0]
    iota = lax.broadcasted_iota(jnp.int32, (E, L), 1)
    count = lambda ind: jnp.sum(ind, axis=1, keepdims=True)
    count_ge = lambda th: count(jnp.where(aff >= th, 1.0, 0.0))
    pow2 = lambda j: pltpu.bitcast((j - 24) << 23, F32)

    def estep(_, c):
        lo, hi = c
        mid = (lo + hi) >> 1
        ok = count_ge(pow2(mid)) >= cap
        return jnp.where(ok, mid, lo), jnp.where(ok, hi, mid)

    jlo, jhi = lax.fori_loop(0, 7, estep, (jnp.full((E, 1), 24, jnp.int32),
                                           jnp.full((E, 1), 152, jnp.int32)))

    def vstep(_, c):
        lo, hi = c
        mid = lo + (hi - lo) * 0.5
        ok = count_ge(mid) >= cap
        return jnp.where(ok, mid, lo), jnp.where(ok, hi, mid)

    lo, hi = lax.fori_loop(0, 40, vstep, (pow2(jlo), pow2(jhi)))
    gt = jnp.where(aff >= hi, 1.0, 0.0)
    eq = jnp.where(aff >= lo, 1.0, 0.0) - gt
    need = cap - count(gt)

    def istep(_, c):
        lo, hi = c
        mid = (lo + hi) >> 1
        ok = count(jnp.where(iota <= mid, eq, 0.0)) >= need
        return jnp.where(ok, lo, mid), jnp.where(ok, mid, hi)

    _, last = lax.fori_loop(0, L.bit_length() - 1, istep,
                            (jnp.full((E, 1), -1, jnp.int32), jnp.full((E, 1), L - 1, jnp.int32)))
    sel = gt + jnp.where(iota <= last, eq, 0.0)
    sel_ref[0] = sel
    cnt = jnp.dot(sel.astype(BF16), gind_ref[...], preferred_element_type=F32)
    G = cnt.shape[1]
    earlier = (lax.broadcasted_iota(jnp.int32, (G, G), 0) < lax.broadcasted_iota(jnp.int32, (G, G), 1))
    off_ref[0] = jnp.dot(cnt.astype(BF16), earlier.astype(BF16), preferred_element_type=F32).astype(jnp.int32)
    gmax_ref[0] = jnp.max(cnt, axis=0, keepdims=True).astype(jnp.int32)


def _compact_kernel(off_ref, gmax_ref, sel_ref, idx_ref):
    E, L = sel_ref.shape[1], sel_ref.shape[2]
    T = LANES
    G = L // T
    idx_ref[...] = jnp.zeros_like(idx_ref)
    r_i = lax.broadcasted_iota(jnp.int32, (T, T), 0)
    c_i = lax.broadcasted_iota(jnp.int32, (T, T), 1)
    before = (r_i < c_i).astype(BF16)
    slot_f = r_i.astype(F32)
    lane_f = lax.broadcasted_iota(jnp.int32, (1, T), 1).astype(F32)

    def group(g, carry):
        s = sel_ref[0, :, pl.ds(pl.multiple_of(g * T, T), T)]
        rank = jnp.dot(s.astype(BF16), before, preferred_element_type=F32)
        tok = (lane_f + jnp.asarray(g * T, F32)) * s

        def emit(n_slots):
            for e in range(E):
                hit = rank[e:e + 1, :] == slot_f[:n_slots]
                ids = jnp.sum(jnp.where(hit, tok[e:e + 1, :], 0.0), axis=1, keepdims=True)
                idx_ref[0, e, pl.ds(off_ref[0, 0, e * G + g], n_slots), :] = ids.astype(jnp.int32)

        few = gmax_ref[0, 0, g] <= SELECT_FAST_SLOTS
        pl.when(few)(lambda: emit(SELECT_FAST_SLOTS))
        pl.when(jnp.logical_not(few))(lambda: emit(T))
        return carry

    lax.fori_loop(0, G, group, 0)


def _route_select(afft, cap):
    B, E, L = afft.shape
    G = L // LANES
    gind = (jnp.arange(L)[:, None] // LANES == jnp.arange(G)[None, :]).astype(BF16)
    sel, off, gmax = pl.pallas_call(
        functools.partial(_select_kernel, cap=cap),
        grid=(B,),
        in_specs=[pl.BlockSpec((1, E, L), lambda b: (b, 0, 0)), pl.BlockSpec((L, G), lambda b: (0, 0))],
        out_specs=[pl.BlockSpec((1, E, L), lambda b: (b, 0, 0)), pl.BlockSpec((1, E, G), lambda b: (b, 0, 0)),
                   pl.BlockSpec((1, 1, G), lambda b: (b, 0, 0))],
        out_shape=[jax.ShapeDtypeStruct((B, E, L), F32), jax.ShapeDtypeStruct((B, E, G), jnp.int32),
                   jax.ShapeDtypeStruct((B, 1, G), jnp.int32)],
        compiler_params=_cparams(("parallel",)),
        name="route_select",
    )(afft, gind)
    smem = lambda n: pl.BlockSpec((1, 1, n), lambda b: (b, 0, 0), memory_space=pltpu.SMEM)
    idx = pl.pallas_call(
        _compact_kernel,
        grid=(B,),
        in_specs=[smem(E * G), smem(G), pl.BlockSpec((1, E, L), lambda b: (b, 0, 0))],
        out_specs=pl.BlockSpec((1, E, cap + LANES, 1), lambda b: (b, 0, 0, 0)),
        out_shape=jax.ShapeDtypeStruct((B, E, cap + LANES, 1), jnp.int32),
        compiler_params=_cparams(("parallel",)),
        name="route_compact",
    )(off.reshape(B, 1, E * G), gmax, sel)
    return idx[:, :, :cap, 0]


def _tile_rows(i, nt):
    return pl.ds(pl.multiple_of(i * nt, nt), nt)


def _gather_kernel(idx_ref, h_ref, aff_ref, xs_ref, gs_ref, buf_ref):
    cap, D = xs_ref.shape[1], xs_ref.shape[2]
    nt = D // LANES

    def body(s, c):
        base = s * ROW_GROUP
        rows = [idx_ref[0, 0, base + k] for k in range(ROW_GROUP)]
        vals = [h_ref[0, _tile_rows(r, nt), :] for r in rows]
        gates = [aff_ref[0, pl.ds(r, 1), :] for r in rows]
        for k in range(ROW_GROUP):
            buf_ref[_tile_rows(base + k, nt), :] = vals[k]
            gs_ref[0, pl.ds(base + k, 1), :] = gates[k]
        return c

    lax.fori_loop(0, cap // ROW_GROUP, body, 0)
    for c in range(nt):
        xs_ref[0, :, c * LANES:(c + 1) * LANES] = buf_ref[pl.ds(c, cap, stride=nt), :].astype(BF16)


def _route_gather(idx, h2t, aff, D):
    B, E, cap = idx.shape
    nt = D // LANES
    L = h2t.shape[1] // nt
    return pl.pallas_call(
        _gather_kernel,
        grid=(B, E),
        in_specs=[pl.BlockSpec((1, 1, cap), lambda b, e: (b * E + e, 0, 0), memory_space=pltpu.SMEM),
                  pl.BlockSpec((1, L * nt, LANES), lambda b, e: (b, 0, 0), pipeline_mode=pl.Buffered(1)),
                  pl.BlockSpec((1, L, E), lambda b, e: (b, 0, 0))],
        out_specs=[pl.BlockSpec((1, cap, D), lambda b, e: (e, b, 0)),
                   pl.BlockSpec((1, cap, E), lambda b, e: (e, b, 0))],
        out_shape=[jax.ShapeDtypeStruct((E, B * cap, D), BF16),
                   jax.ShapeDtypeStruct((E, B * cap, E), F32)],
        scratch_shapes=[pltpu.VMEM((cap * nt, LANES), F32)],
        compiler_params=_cparams(("arbitrary", "arbitrary")),
        name="route_gather",
    )(idx.reshape(B * E, 1, cap), h2t, aff)


def _combine_kernel(idx_ref, ye_ref, x1_ref, g2_ref, post2_ref, o_ref, y_ref, buf_ref, *, n_experts):
    j = pl.program_id(1)
    cap, D = ye_ref.shape[1], ye_ref.shape[2]
    nt = D // LANES
    tm = x1_ref.shape[1]

    @pl.when(j == 0)
    def _():
        y_ref[...] = jnp.zeros_like(y_ref)

    @pl.when(j < n_experts)
    def _():
        for c in range(nt):
            buf_ref[pl.ds(c, cap, stride=nt), :] = ye_ref[0, :, c * LANES:(c + 1) * LANES]

        def body(s, c):
            base = s * ROW_GROUP
            rows = [idx_ref[0, 0, base + k] for k in range(ROW_GROUP)]
            vals = [y_ref[_tile_rows(rows[k], nt), :] + buf_ref[_tile_rows(base + k, nt), :]
                    for k in range(ROW_GROUP)]
            for k in range(ROW_GROUP):
                y_ref[_tile_rows(rows[k], nt), :] = vals[k]
            return c

        lax.fori_loop(0, cap // ROW_GROUP, body, 0)

    @pl.when(j >= n_experts)
    def _():
        base = pl.multiple_of((j - n_experts) * (tm * nt), tm * nt)
        cols = [y_ref[pl.ds(base + c, tm, stride=nt), :] for c in range(nt)]
        ssq = sum(jnp.sum(y * y, axis=-1, keepdims=True) for y in cols)
        rstd = lax.rsqrt(ssq * (1.0 / D) + EPS)
        for c in range(nt):
            cs = slice(c * LANES, (c + 1) * LANES)
            o_ref[0, :, cs] = x1_ref[0, :, cs] + g2_ref[0, :, cs] * (cols[c] * rstd * post2_ref[:, cs])


def _route_combine(idx, ye, x1, mod3, post2, tm):
    B, E, cap = idx.shape
    _, L, D = x1.shape
    nt = D // LANES
    assert ye.shape[0] == E + 1
    tile = lambda j: jnp.maximum(j - E, 0)
    expert = lambda j: jnp.minimum(j, E - 1)
    return pl.pallas_call(
        functools.partial(_combine_kernel, n_experts=E),
        grid=(B, E + L // tm),
        in_specs=[pl.BlockSpec((1, 1, cap), lambda b, j: (b * E + expert(j), 0, 0), memory_space=pltpu.SMEM),
                  pl.BlockSpec((1, cap, D), lambda b, j: (expert(j) + 1, b, 0)),
                  pl.BlockSpec((1, tm, D), lambda b, j: (b, tile(j), 0)),
                  _mod_spec(D, 5, 0, True),
                  pl.BlockSpec((1, D), lambda b, j: (0, 0))],
        out_specs=pl.BlockSpec((1, tm, D), lambda b, j: (b, tile(j), 0)),
        out_shape=jax.ShapeDtypeStruct((B, L, D), F32),
        scratch_shapes=[pltpu.VMEM((L * nt, LANES), F32), pltpu.VMEM((cap * nt, LANES), F32)],
        compiler_params=_cparams(("arbitrary", "arbitrary")),
        name="route_combine_final",
    )(idx.reshape(B * E, 1, cap), ye, x1, mod3, post2)


def _expert_kernel(xs_ref, g_ref, wg_ref, wu_ref, wd_ref, ye_ref, wgb_ref, wub_ref, wdb_ref):
    s = pl.program_id(0)
    f = pl.program_id(1)
    ne = pl.num_programs(0) - 1
    tf = wg_ref.shape[2]
    slot = lax.rem(s, 2)

    @pl.when(s < ne)
    def _():
        cols = pl.ds(pl.multiple_of(f * tf, tf), tf)
        wgb_ref[slot, :, cols] = wg_ref[0].astype(BF16)
        wub_ref[slot, :, cols] = wu_ref[0].astype(BF16)
        wdb_ref[slot, cols, :] = wd_ref[0].astype(BF16)

    @pl.when(s == 0)
    def _():
        ye_ref[...] = jnp.zeros_like(ye_ref)

    @pl.when(s > 0)
    def _():
        e = s - 1
        prev = 1 - slot
        mt = xs_ref.shape[1] // FFN_M_SPLIT
        for mi in range(FFN_M_SPLIT):
            rs = slice(mi * mt, (mi + 1) * mt)
            xs = xs_ref[0, rs, :]
            a = jnp.dot(xs, wgb_ref[prev], preferred_element_type=F32)
            u = jnp.dot(xs, wub_ref[prev], preferred_element_type=F32)
            hmid = (a * jax.nn.sigmoid(a) * u).astype(BF16)
            out = jnp.dot(hmid, wdb_ref[prev], preferred_element_type=F32)
            gs = g_ref[0, rs, :]
            esel = lax.broadcasted_iota(jnp.int32, gs.shape, 1)
            ye_ref[0, rs, :] = out * jnp.sum(jnp.where(esel == e, gs, 0.0), axis=1, keepdims=True)


FFN_M_SPLIT = 2


def _expert_ffn(xs, gs, w_gate, w_up, w_down, tf):
    E, M, D = xs.shape
    F = w_gate.shape[2]
    nf = F // tf
    mt = M // nf
    behind = lambda s: jnp.maximum(s - 1, 0)
    ahead = lambda s: jnp.minimum(s, E - 1)
    return pl.pallas_call(
        _expert_kernel,
        grid=(E + 1, nf),
        in_specs=[pl.BlockSpec((1, mt, D), lambda s, f: (behind(s), f, 0)),
                  pl.BlockSpec((1, mt, E), lambda s, f: (behind(s), f, 0)),
                  pl.BlockSpec((1, D, tf), lambda s, f: (ahead(s), 0, f)),
                  pl.BlockSpec((1, D, tf), lambda s, f: (ahead(s), 0, f)),
                  pl.BlockSpec((1, tf, D), lambda s, f: (ahead(s), f, 0))],
        out_specs=pl.BlockSpec((1, mt, D), lambda s, f: (s, f, 0)),
        out_shape=jax.ShapeDtypeStruct((E + 1, M, D), F32),
        scratch_shapes=[pltpu.VMEM((2, D, F), BF16), pltpu.VMEM((2, D, F), BF16), pltpu.VMEM((2, F, D), BF16)],
        compiler_params=_cparams(("arbitrary", "arbitrary")),
        name="expert_ffn",
    )(xs, gs, w_gate, w_up, w_down)


def kernel(x, c, ctx, c_ctx, w_mod, b_mod, pre_norm1, post_norm1, pre_norm2, post_norm2, w_in, conv_w, conv_b, filt_w1, filt_b1, filt_w2, filt_b2, filt_w3, filt_b3, filt_w4, filt_freq, hyena_bias, ml_gate_b, ml_norm, w_out, w_router, w_exp_gate, w_exp_up, w_exp_down):
    B, L, D = x.shape
    depth = w_mod.shape[0]
    assert depth == 1, "single-layer block"
    li = 0
    tabs = _pe_tables(L // GRID_W, GRID_W, D)

    cc = jnp.concatenate([c, c_ctx[None], jnp.zeros((8 - B - 1, D), F32)], axis=0)
    mod3 = _modulation(cc, w_mod[li], b_mod[li])[:, None, :]

    pre1 = pre_norm1[li][None, :]
    w_proj = _in_proj_weights(w_in[li])
    zhy, q, kt, v, o, gatet = _in_proj(x, tabs, mod3, 0, pre1, w_proj, IN_PROJ_TILE, True)
    qc, ktc, vc, _, gatetc = _in_proj(ctx, None, mod3, B, pre1, w_proj, ctx.shape[1], False)

    kcirc = _hyena_filter(L, filt_w1[li], filt_b1[li], filt_w2[li], filt_b2[li], filt_w3[li],
                          filt_b3[li], filt_w4[li], filt_freq[li])
    y_hy = _hyena_conv(zhy, kcirc, conv_w[li], conv_b[li], hyena_bias[li])
    hsum = _mlstm(q, kt, v, gatet, qc, ktc, vc, gatetc, ml_gate_b[li])

    x1, h2t, aff, afft = _out_proj(y_hy, hsum, o, x, tabs, mod3, ml_norm[li][None, :],
                                   post_norm1[li][None, :], pre_norm2[li][None, :], w_out[li],
                                   w_router[li], IN_PROJ_TILE)

    cap = CAP_FACTOR * L // N_EXPERTS
    idx = _route_select(afft, cap)
    xs, gs = _route_gather(idx, h2t, aff, D)
    ye = _expert_ffn(xs, gs, w_exp_gate[li], w_exp_up[li], w_exp_down[li], 512)
    return _route_combine(idx, ye, x1, mod3, post_norm2[li][None, :], TOKEN_TILE)
```

```python
import functools
import math

import jax
import jax.numpy as jnp
from jax import lax
from jax.experimental import pallas as pl
from jax.experimental.pallas import tpu as pltpu

GRID_W = 64
HY_WIDTH = 512
ML_HEADS = 4
ML_DK = 64
ML_DV = 128
ML_WIDTH = ML_HEADS * ML_DV
HY_COLS = 3 * HY_WIDTH
QK_COLS = ML_HEADS * ML_DK
N_GATES = 4 * ML_HEADS
FILTER_EMB = 33
DECAY_TARGET = 1e-2
FAST_DECAY_PCT = 0.3
SLOW_DECAY_PCT = 1.5
CHUNK = 128
GATE_CAP = 15.0
N_EXPERTS = 16
CAP_FACTOR = 2
EPS = 1e-6

F32 = jnp.float32
BF16 = jnp.bfloat16

TOKEN_TILE = 512
TIME_ROWS = 8
IN_PROJ_TILE = 1024
VMEM_LIMIT = 56 * 1024 * 1024


def _cparams(sem):
    return pltpu.CompilerParams(dimension_semantics=sem, vmem_limit_bytes=VMEM_LIMIT)


def _rms(xf, g):
    return xf * lax.rsqrt(jnp.mean(xf * xf, axis=-1, keepdims=True) + EPS) * g


def _bdot(a, b):
    return jnp.dot(a.astype(BF16), b.astype(BF16), preferred_element_type=F32)


def _pe_tables_kernel(omega_ref, er_ref, ec_ref):
    quarter = omega_ref.shape[1]
    om = omega_ref[...]
    for ref in (er_ref, ec_ref):
        n = ref.shape[0]
        pos = lax.broadcasted_iota(jnp.int32, (n, quarter), 0).astype(F32)
        ang = pos * om
        ref[:, :quarter] = jnp.sin(ang)
        ref[:, quarter:] = jnp.cos(ang)


def _pe_tables(rows, cols, dim):
    quarter = dim // 4
    omega = (1.0 / (10000.0 ** (jnp.arange(quarter, dtype=F32) / quarter)))[None, :]
    return pl.pallas_call(
        _pe_tables_kernel,
        out_shape=(jax.ShapeDtypeStruct((rows, dim // 2), F32),
                   jax.ShapeDtypeStruct((cols, dim // 2), F32)),
        name="pe_tables",
    )(omega)


def _pe_tile(er_blk, ec):
    nr, half = er_blk.shape
    row_part = jnp.broadcast_to(er_blk[:, None, :], (nr, GRID_W, half)).reshape(nr * GRID_W, half)
    col_part = jnp.broadcast_to(ec[None, :, :], (nr, GRID_W, half)).reshape(nr * GRID_W, half)
    return jnp.concatenate([row_part, col_part], axis=-1)


def _mod_kernel(c_ref, w_ref, b_ref, o_ref):
    c = c_ref[...]
    s = c * jax.nn.sigmoid(c)
    o_ref[...] = _bdot(s, w_ref[...]) + b_ref[...]


def _modulation(cc, w_mod, b_mod):
    rows, d = cc.shape
    n = w_mod.shape[1]
    tn = 512
    return pl.pallas_call(
        _mod_kernel,
        grid=(n // tn,),
        in_specs=[pl.BlockSpec((rows, d), lambda j: (0, 0)),
                  pl.BlockSpec((d, tn), lambda j: (0, j)),
                  pl.BlockSpec((1, tn), lambda j: (0, j))],
        out_specs=pl.BlockSpec((rows, tn), lambda j: (0, j)),
        out_shape=jax.ShapeDtypeStruct((rows, n), F32),
        compiler_params=_cparams(("arbitrary",)),
        name="modulation",
    )(cc, w_mod, b_mod[None, :])


def _in_proj_kernel(*refs, with_hyena):
    if with_hyena:
        (x_ref, er_ref, ec_ref, sh_ref, sc_ref, g_ref, wn_ref, wt_ref,
         zt_ref, q_ref, kt_ref, v_ref, o_ref, gatet_ref) = refs
        xf = x_ref[0] + _pe_tile(er_ref[...], ec_ref[...])
    else:
        (x_ref, sh_ref, sc_ref, g_ref, wn_ref, wt_ref,
         q_ref, kt_ref, v_ref, o_ref, gatet_ref) = refs
        xf = x_ref[0]
    h = _rms(xf, g_ref[...]) * (1.0 + sc_ref[0]) + sh_ref[0]
    hb = h.astype(BF16)
    z = jnp.dot(hb, wn_ref[...], preferred_element_type=F32)
    qs = z[:, :QK_COLS] * (ML_DK ** -0.5)
    for hd in range(ML_HEADS):
        q_ref[0, hd] = qs[:, hd * ML_DK:(hd + 1) * ML_DK].astype(BF16)
    v_ref[0] = z[:, QK_COLS:QK_COLS + ML_WIDTH].astype(BF16)
    o_ref[0] = z[:, QK_COLS + ML_WIDTH:]
    zt = lax.dot_general(wt_ref[...], hb, (((1,), (1,)), ((), ())), preferred_element_type=F32)
    off = 0
    if with_hyena:
        for j in range(zt.shape[1] // LANES):
            zt_ref[0, 0, pl.ds(j, HY_COLS, stride=TIME_ROWS), :] = zt[:HY_COLS, j * LANES:(j + 1) * LANES]
        off = HY_COLS
    for hd in range(ML_HEADS):
        kt_ref[0, hd] = zt[off + hd * ML_DK:off + (hd + 1) * ML_DK, :].astype(BF16)
    gatet_ref[0] = zt[off + QK_COLS:, :]


def _mod_spec(D, chunk, row0, per_sample):
    return pl.BlockSpec((1, 1, D), lambda b, i: ((b if per_sample else 0) + row0, 0, chunk))


def _in_proj_weights(w_in):
    w_q = w_in[:, HY_COLS:HY_COLS + QK_COLS]
    w_k = w_in[:, HY_COLS + QK_COLS:HY_COLS + 2 * QK_COLS]
    w_vo = w_in[:, HY_COLS + 2 * QK_COLS:HY_COLS + 2 * QK_COLS + 2 * ML_WIDTH]
    w_g = w_in[:, HY_COLS + 2 * QK_COLS + 2 * ML_WIDTH:]
    wn = jnp.concatenate([w_q, w_vo], axis=1).astype(BF16)
    wt = jnp.concatenate([w_in[:, :HY_COLS], w_k, w_g], axis=1).T.astype(BF16)
    return wn, wt


def _in_proj(x, tabs, mod3, mod_row0, g, weights, tm, with_hyena):
    B, L, D = x.shape
    wn, wt = weights
    if not with_hyena:
        wt = wt[HY_COLS:]
    full = lambda a: pl.BlockSpec(a.shape, lambda b, i: (0,) * a.ndim)
    in_specs = [pl.BlockSpec((1, tm, D), lambda b, i: (b, i, 0))]
    args = [x]
    if with_hyena:
        er, ec = tabs
        in_specs += [pl.BlockSpec((tm // GRID_W, D // 2), lambda b, i: (i, 0)), full(ec)]
        args += [er, ec]
    in_specs += [_mod_spec(D, 0, mod_row0, with_hyena), _mod_spec(D, 1, mod_row0, with_hyena),
                 full(g), full(wn), full(wt)]
    args += [mod3, mod3, g, wn, wt]
    out_shape, out_specs = [], []
    if with_hyena:
        assert tm == TIME_ROWS * LANES
        out_shape.append(jax.ShapeDtypeStruct((B, L // tm, HY_COLS * TIME_ROWS, LANES), F32))
        out_specs.append(pl.BlockSpec((1, 1, HY_COLS * TIME_ROWS, LANES), lambda b, i: (b, i, 0, 0)))
    out_shape += [jax.ShapeDtypeStruct((B, ML_HEADS, L, ML_DK), BF16),
                  jax.ShapeDtypeStruct((B, ML_HEADS, ML_DK, L), BF16),
                  jax.ShapeDtypeStruct((B, L, ML_WIDTH), BF16),
                  jax.ShapeDtypeStruct((B, L, ML_WIDTH), F32),
                  jax.ShapeDtypeStruct((B, N_GATES, L), F32)]
    out_specs += [pl.BlockSpec((1, ML_HEADS, tm, ML_DK), lambda b, i: (b, 0, i, 0)),
                  pl.BlockSpec((1, ML_HEADS, ML_DK, tm), lambda b, i: (b, 0, 0, i)),
                  pl.BlockSpec((1, tm, ML_WIDTH), lambda b, i: (b, i, 0)),
                  pl.BlockSpec((1, tm, ML_WIDTH), lambda b, i: (b, i, 0)),
                  pl.BlockSpec((1, N_GATES, tm), lambda b, i: (b, 0, i))]
    return pl.pallas_call(
        functools.partial(_in_proj_kernel, with_hyena=with_hyena),
        grid=(B, L // tm),
        in_specs=in_specs,
        out_specs=out_specs,
        out_shape=out_shape,
        compiler_params=_cparams(("parallel", "parallel")),
        name="in_proj_hy" if with_hyena else "in_proj_ctx",
    )(*args)


FILT_TILE = 1024
FILT_CBLK = 128


def _filter_kernel(w1a_ref, w1b_ref, w1c_ref, b1_ref, w2_ref, b2_ref, w3_ref, b3_ref, fr_ref,
                   w4f_ref, w4b_ref, dl_ref, mir_ref, k_ref, hf_ref, *, L):
    bands = (FILTER_EMB - 1) // 2

    @pl.when(pl.program_id(0) == 0)
    def _():
        fk = (1e-4 + lax.broadcasted_iota(jnp.int32, (bands, 1), 0).astype(F32)
              * ((bands - 1 - 1e-4) / (bands - 1)))
        fr = fr_ref[...]
        for j in range(L // FILT_TILE):
            pos = (lax.broadcasted_iota(jnp.int32, (1, FILT_TILE), 1) + j * FILT_TILE).astype(F32)
            tl = pos * (1.0 / (L - 1))
            ang = fk * (pos * (2.0 * math.pi / L))
            pre = (w1a_ref[...].astype(F32) * tl.astype(BF16).astype(F32)
                   + _bdot(w1b_ref[...], jnp.cos(ang)) + _bdot(w1c_ref[...], -jnp.sin(ang)))
            h = jnp.sin(fr * (pre + b1_ref[...]))
            h = jnp.sin(fr * (_bdot(w2_ref[...], h) + b2_ref[...]))
            h = jnp.sin(fr * (_bdot(w3_ref[...], h) + b3_ref[...]))
            hf_ref[:, j * FILT_TILE:(j + 1) * FILT_TILE] = h.astype(BF16)

    pos = lax.broadcasted_iota(jnp.int32, (1, L), 1).astype(F32)
    decay = jnp.exp(-(pos * (1.0 / (L - 1))) * dl_ref[...])
    hf = hf_ref[...]
    k_ref[:, :L] = jnp.dot(w4f_ref[...], hf, preferred_element_type=F32) * decay
    gb = (jnp.dot(w4b_ref[...], hf, preferred_element_type=F32) * decay).astype(BF16)
    nblk = L // LANES
    for j in range(nblk):
        src = gb[:, (nblk - 1 - j) * LANES:(nblk - j) * LANES]
        nxt = gb[:, (nblk - j) * LANES:(nblk - j + 1) * LANES] if j > 0 else jnp.zeros_like(src)
        k_ref[:, L + j * LANES:L + (j + 1) * LANES] = jnp.dot(
            jnp.concatenate([src, nxt], axis=1), mir_ref[...], preferred_element_type=F32)


def _hyena_filter(L, w1, b1, w2, b2, w3, b3, w4, freq):
    hid = w2.shape[0]
    bands = (FILTER_EMB - 1) // 2
    col = lambda a: a[:, None]
    w1t = w1.T.astype(BF16)
    min_decay = math.log(DECAY_TARGET) / SLOW_DECAY_PCT
    max_decay = math.log(DECAY_TARGET) / FAST_DECAY_PCT
    dl = jnp.abs(jnp.linspace(min_decay, max_decay, HY_WIDTH, dtype=F32))[:, None]
    w4t = w4.T.astype(BF16)
    assert DFT_PASSES == 1, "the mirrored taps are kept at the bf16 precision a single-pass DFT reads"
    lane = jnp.arange(LANES)
    mir = jnp.concatenate([(lane[:, None] + lane[None, :] == LANES),
                           (lane[:, None] == 0) & (lane[None, :] == 0)], axis=0).astype(BF16)
    full = lambda a: pl.BlockSpec(a.shape, lambda i: (0,) * a.ndim)
    args = [w1t[:, 0:1], w1t[:, 1:1 + bands], w1t[:, 1 + bands:], col(b1), w2.T.astype(BF16), col(b2),
            w3.T.astype(BF16), col(b3), col(freq)]
    return pl.pallas_call(
        functools.partial(_filter_kernel, L=L),
        grid=(HY_WIDTH // FILT_CBLK,),
        in_specs=[full(a) for a in args] + [
            pl.BlockSpec((FILT_CBLK, hid), lambda i: (i, 0)),
            pl.BlockSpec((FILT_CBLK, hid), lambda i: (HY_WIDTH // FILT_CBLK + i, 0)),
            pl.BlockSpec((FILT_CBLK, 1), lambda i: (i, 0)), full(mir)],
        out_specs=pl.BlockSpec((FILT_CBLK, 2 * L), lambda i: (i, 0)),
        out_shape=jax.ShapeDtypeStruct((HY_WIDTH, 2 * L), F32),
        scratch_shapes=[pltpu.VMEM((hid, L), BF16)],
        compiler_params=_cparams(("arbitrary",)),
        name="hyena_filter",
    )(*args, w4t, w4t, dl, mir)


FFT_N = 128
HY_CBLK = 32
HY_GROUP = 8
HY_UNROLL = 4
DFT_PASSES = 1


def _dft_constants(n1_data):
    import numpy as np
    n = FFT_N
    k = np.arange(n)
    ang = -2.0 * np.pi * ((k[:, None] * k[None, :]) % n) / n
    fre, fim = np.cos(ang), np.sin(ang)
    m = n1_data
    fa_d = np.block([[fre[:, :m], -fim[:, :m]], [fim[:, :m], fre[:, :m]]])
    fa_f = np.concatenate([fre, fim], axis=0)
    fb = np.block([[fre, fim], [-fim, fre]])
    fbi = np.block([[fre, -fim], [fim, fre]])
    fc = np.block([[fre[:m, :], fim[:m, :]], [-fim[:m, :], fre[:m, :]]]) / (n * n)
    tang = -2.0 * np.pi * (k[:, None] * k[None, :]) / (n * n)
    tw = np.stack([np.cos(tang), np.sin(tang)])

    def hilo(a):
        a32 = jnp.asarray(a, F32)
        hi = a32.astype(BF16)
        lo = (a32 - hi.astype(F32)).astype(BF16)
        return jnp.stack([hi, lo])

    return hilo(fa_d), hilo(fa_f), hilo(fb), hilo(fbi), hilo(fc), jnp.asarray(tw, F32)


def _mm_const_lhs(c_ref, d):
    dh = d.astype(BF16)
    acc = jnp.dot(c_ref[0], dh, preferred_element_type=F32)
    if DFT_PASSES == 3:
        dl = (d - dh.astype(F32)).astype(BF16)
        acc = acc + (jnp.dot(c_ref[0], dl, preferred_element_type=F32)
                     + jnp.dot(c_ref[1], dh, preferred_element_type=F32))
    return acc


def _mm_const_rhs(d, c_ref):
    dh = d.astype(BF16)
    acc = jnp.dot(dh, c_ref[0], preferred_element_type=F32)
    if DFT_PASSES == 3:
        dl = (d - dh.astype(F32)).astype(BF16)
        acc = acc + (jnp.dot(dl, c_ref[0], preferred_element_type=F32)
                     + jnp.dot(dh, c_ref[1], preferred_element_type=F32))
    return acc


def _cmul(are, aim, bre, bim):
    return are * bre - aim * bim, are * bim + aim * bre


def _hyena_conv_kernel(x0_ref, x1_ref, v_ref, kc_ref, w0_ref, w1_ref, wv_ref, b0_ref, b1_ref, bv_ref,
                       hb_ref, fad_ref, faf_ref, fb_ref, fbi_ref, fc_ref, tw_ref,
                       o_ref, u_ref, s_ref, ks_ref):
    n = FFT_N
    cb = kc_ref.shape[0]
    nb, nt = x0_ref.shape[0], x0_ref.shape[1]
    m = nt * TIME_ROWS

    def chan(ref, b, c):
        return ref[b, :, pl.ds(pl.multiple_of(c * TIME_ROWS, TIME_ROWS), TIME_ROWS), :].reshape(m, n)

    sub = lax.broadcasted_iota(jnp.int32, (m, n), 0)
    lane = lax.broadcasted_iota(jnp.int32, (m, n), 1)
    tre, tim = tw_ref[0], tw_ref[1]

    def sconv(z, w_ref, b_ref, c):
        a = pltpu.roll(z, 1, axis=1)
        prev = jnp.where(lane == 0, jnp.where(sub == 0, 0.0, pltpu.roll(a, 1, axis=0)), a)
        a2 = pltpu.roll(z, n - 1, axis=1)
        nxt = jnp.where(lane == n - 1, jnp.where(sub == m - 1, 0.0, pltpu.roll(a2, m - 1, axis=0)), a2)
        return prev * w_ref[0, c] + z * w_ref[1, c] + nxt * w_ref[2, c] + b_ref[c]

    def spectrum_rows(res):
        outs = []
        for h in range(2):
            are, aim = _cmul(res[:n, h * n:(h + 1) * n], res[n:, h * n:(h + 1) * n], tre, tim)
            outs.append(jnp.concatenate([are, aim], axis=1))
        return outs

    def fwd_pair(p, carry):
        c0 = 2 * p
        us = []
        for c in (c0, c0 + 1):
            ub = []
            for b in range(nb):
                x1c = sconv(chan(x1_ref, b, c), w1_ref, b1_ref, c)
                vc = sconv(chan(v_ref, b, c), wv_ref, bv_ref, c)
                u = x1c * vc
                u_ref[b, c] = u
                ub.append(u)
            us.append(ub)
        wd = jnp.concatenate([jnp.concatenate([us[0][b], us[1][b]], axis=1) for b in range(nb)], axis=0)
        sa, sb = spectrum_rows(_mm_const_lhs(fad_ref, wd))
        s_ref[c0] = sa.astype(s_ref.dtype)
        s_ref[c0 + 1] = sb.astype(s_ref.dtype)
        wk = jnp.concatenate([kc_ref[c0], kc_ref[c0 + 1]], axis=1)
        ka, kb = spectrum_rows(_mm_const_lhs(faf_ref, wk))
        ks_ref[c0] = ka.astype(ks_ref.dtype)
        ks_ref[c0 + 1] = kb.astype(ks_ref.dtype)
        return carry

    lax.fori_loop(0, cb // 2, fwd_pair, 0, unroll=HY_UNROLL)

    def mid_group(g, carry):
        gs = pl.ds(pl.multiple_of(g * HY_GROUP, HY_GROUP), HY_GROUP)
        x = _mm_const_rhs(s_ref[gs].reshape(HY_GROUP * n, 2 * n), fb_ref)
        k = _mm_const_rhs(ks_ref[gs].reshape(HY_GROUP * n, 2 * n), fb_ref)
        yre, yim = _cmul(x[:, :n], x[:, n:], k[:, :n], k[:, n:])
        vv = _mm_const_rhs(jnp.concatenate([yre, yim], axis=1), fbi_ref).reshape(HY_GROUP, n, 2 * n)
        vre, vim = _cmul(vv[:, :, :n], vv[:, :, n:], tre[None], -tim[None])
        s_ref[gs] = jnp.concatenate([vre, vim], axis=2).astype(s_ref.dtype)
        return carry

    lax.fori_loop(0, cb // HY_GROUP, mid_group, 0, unroll=HY_UNROLL)

    def inv_pair(p, carry):
        c0 = 2 * p
        sa, sb = s_ref[c0], s_ref[c0 + 1]
        wd = jnp.concatenate([jnp.concatenate([sa[:, :n], sb[:, :n]], axis=1),
                              jnp.concatenate([sa[:, n:], sb[:, n:]], axis=1)], axis=0)
        res = _mm_const_lhs(fc_ref, wd)
        for h, c in enumerate((c0, c0 + 1)):
            for b in range(nb):
                y = res[b * m:(b + 1) * m, h * n:(h + 1) * n]
                x0c = sconv(chan(x0_ref, b, c), w0_ref, b0_ref, c)
                out = x0c * (y + hb_ref[c] * u_ref[b, c])
                o_ref[b, :, pl.ds(pl.multiple_of(c * TIME_ROWS, TIME_ROWS), TIME_ROWS), :] = out.reshape(
                    nt, TIME_ROWS, n)
        return carry

    lax.fori_loop(0, cb // 2, inv_pair, 0, unroll=HY_UNROLL)


def _hyena_conv(zt, kcirc, conv_w, conv_b, hy_bias):
    B, nt, _, n = zt.shape
    m = nt * TIME_ROWS
    L = m * n
    C = HY_WIDTH
    assert B == 2 and n == FFT_N and 2 * L == n * n, "complex packing of two samples over a 128 x 128 point transform"
    z4 = zt
    rows = HY_CBLK * TIME_ROWS
    spec_dt = BF16 if DFT_PASSES == 1 else F32
    k3 = kcirc.reshape(C, n, n)
    cw = conv_w.reshape(3, 3 * C, 1, 1)
    cbias = conv_b.reshape(3 * C, 1, 1)
    hb = hy_bias.reshape(C, 1, 1)
    consts = _dft_constants(m)
    nblk = C // HY_CBLK
    zspec = lambda part: pl.BlockSpec((B, nt, rows, n), lambda i: (0, 0, part * nblk + i, 0))
    wspec = lambda part: pl.BlockSpec((3, HY_CBLK, 1, 1), lambda i: (0, part * nblk + i, 0, 0))
    bspec = lambda part: pl.BlockSpec((HY_CBLK, 1, 1), lambda i: (part * nblk + i, 0, 0))
    full = lambda a: pl.BlockSpec(a.shape, lambda i: (0,) * a.ndim)
    y = pl.pallas_call(
        _hyena_conv_kernel,
        grid=(nblk,),
        in_specs=[zspec(0), zspec(1), zspec(2), pl.BlockSpec((HY_CBLK, n, n), lambda i: (i, 0, 0)),
                  wspec(0), wspec(1), wspec(2), bspec(0), bspec(1), bspec(2), bspec(0)]
                 + [full(a) for a in consts],
        out_specs=pl.BlockSpec((B, nt, rows, n), lambda i: (0, 0, i, 0)),
        out_shape=jax.ShapeDtypeStruct((B, nt, C * TIME_ROWS, n), F32),
        scratch_shapes=[pltpu.VMEM((B, HY_CBLK, m, n), F32), pltpu.VMEM((HY_CBLK, n, 2 * n), spec_dt),
                        pltpu.VMEM((HY_CBLK, n, 2 * n), spec_dt)],
        compiler_params=_cparams(("parallel",)),
        name="hyena_conv",
    )(z4, z4, z4, k3, cw, cw, cw, cbias, cbias, cbias, hb, *consts)
    return y


def _split3(a):
    hi = a.astype(BF16)
    r1 = a - hi.astype(F32)
    mid = r1.astype(BF16)
    lo = (r1 - mid.astype(F32)).astype(BF16)
    return hi, mid, lo


def _exact_dot_right(a, tri_bf):
    hi, mid, lo = _split3(a)
    d = lambda p: jnp.dot(p, tri_bf, preferred_element_type=F32)
    return (d(lo) + d(mid)) + d(hi)


def _soft_gates(g):
    g = GATE_CAP * jnp.tanh(g * (1.0 / GATE_CAP))
    logsig = jnp.minimum(g, 0.0) - jnp.log1p(jnp.exp(-jnp.abs(g)))
    return g, logsig


def _gate_prep_kernel(gt_ref, gbt_ref, rows_ref, cols_ref):
    T = CHUNK
    n = gt_ref.shape[2]
    up_bf = (lax.broadcasted_iota(jnp.int32, (T, T), 1) >= lax.broadcasted_iota(jnp.int32, (T, T), 0)).astype(BF16)
    out8 = lax.broadcasted_iota(jnp.int32, (8, 1), 0)
    cap, ls = _soft_gates(gt_ref[0] + gbt_ref[...])
    H = ML_HEADS
    head = lambda hd: jnp.where(out8 == 0, cap[hd:hd + 1], jnp.where(out8 == 1, ls[H + hd:H + hd + 1],
                                jnp.where(out8 == 2, cap[2 * H + hd:2 * H + hd + 1],
                                          jnp.where(out8 == 3, ls[3 * H + hd:3 * H + hd + 1], 0.0))))
    base = jnp.concatenate([head(hd) for hd in range(H)], axis=0)
    kind = jnp.concatenate([out8] * H, axis=0)
    for j in range(n // T):
        blk = base[:, j * T:(j + 1) * T]
        run = _exact_dot_right(blk, up_bf)
        suf = run[:, T - 1:T] - run + blk
        rows = jnp.where(kind == 1, run, jnp.where(kind == 3, suf, blk))
        cols = rows.T
        for hd in range(H):
            rows_ref[0, hd, :, j * T:(j + 1) * T] = rows[8 * hd:8 * (hd + 1)]
            cols_ref[0, hd, j * T:(j + 1) * T, :] = cols[:, 8 * hd:8 * (hd + 1)]


def _gate_prep(gatet, gate_b, tile):
    B, G, L = gatet.shape
    H = ML_HEADS
    return pl.pallas_call(
        _gate_prep_kernel,
        grid=(B, L // tile),
        in_specs=[pl.BlockSpec((1, G, tile), lambda b, i: (b, 0, i)),
                  pl.BlockSpec((G, 1), lambda b, i: (0, 0))],
        out_specs=[pl.BlockSpec((1, H, 8, tile), lambda b, i: (b, 0, 0, i)),
                   pl.BlockSpec((1, H, tile, 8), lambda b, i: (b, 0, i, 0))],
        out_shape=[jax.ShapeDtypeStruct((B, H, 8, L), F32), jax.ShapeDtypeStruct((B, H, L, 8), F32)],
        compiler_params=_cparams(("parallel", "parallel")),
        name="mlstm_gate_prep",
    )(gatet, gate_b[:, None])


def _mlstm_kernel(q_ref, kt_ref, v_ref, r_ref, bt_ref, qc_ref, ktc_ref, vc_ref, rc_ref, btc_ref,
                  h_ref, cf_ref, cb_ref):
    T = CHUNK
    L = q_ref.shape[2]
    Lc = qc_ref.shape[2]
    nc, ncc = L // T, Lc // T
    row = lax.broadcasted_iota(jnp.int32, (T, T), 0)
    col = lax.broadcasted_iota(jnp.int32, (T, T), 1)
    lo_mask = col <= row
    up_mask = col >= row
    ones_blk = jnp.ones((T, ML_DV), BF16)

    cf_ref[...] = jnp.zeros_like(cf_ref)
    cb_ref[...] = jnp.zeros_like(cb_ref)

    def chunk_step(q, kt, v, rows, cols, c_ref, backward):
        i_r = rows[2:3] if backward else rows[0:1]
        b_r = rows[3:4] if backward else rows[1:2]
        b_c = jnp.broadcast_to(cols[:, 3:4] if backward else cols[:, 1:2], (T, T))
        b_end = b_r[:, 0:1] if backward else b_r[:, T - 1:T]
        mask = up_mask if backward else lo_mask
        w_intra = jnp.exp(jnp.where(mask, b_c - b_r + i_r, -jnp.inf) - GATE_CAP)
        s = jnp.dot(q, kt, preferred_element_type=F32) * w_intra
        qe = (q.astype(F32) * jnp.exp(b_c[:, :ML_DK])).astype(BF16)
        v_aug = jnp.concatenate([v, ones_blk], axis=1)
        c_aug = c_ref[...]
        res = jnp.dot(jnp.concatenate([s.astype(BF16), qe], axis=1),
                      jnp.concatenate([v_aug, c_aug.astype(BF16)], axis=0),
                      preferred_element_type=F32)
        h = res[:, :ML_DV] / jnp.maximum(jnp.abs(res[:, ML_DV:]), math.exp(-GATE_CAP))
        kw = (kt.astype(F32) * jnp.exp(b_end - b_r + i_r - GATE_CAP)).astype(BF16)
        c_ref[...] = jnp.exp(b_end) * c_aug + jnp.dot(kw, v_aug, preferred_element_type=F32)
        return h

    for j in range(ncc):
        for backward in (False, True):
            jj = (ncc - 1 - j) if backward else j
            cs = slice(jj * T, (jj + 1) * T)
            chunk_step(qc_ref[0, 0, cs, :], ktc_ref[0, 0, :, cs], vc_ref[0, cs, :], rc_ref[0, 0, :, cs],
                       btc_ref[0, 0, cs, :], cb_ref if backward else cf_ref, backward)

    def latent_pair(j, accumulate):
        for backward in (False, True):
            jj = (nc - 1 - j) if backward else j
            rs = pl.ds(pl.multiple_of(jj * T, T), T)
            h = chunk_step(q_ref[0, 0, rs, :], kt_ref[0, 0, :, rs], v_ref[0, rs, :], r_ref[0, 0, :, rs],
                           bt_ref[0, 0, rs, :], cb_ref if backward else cf_ref, backward)
            if accumulate:
                h_ref[0, rs, :] = h_ref[0, rs, :] + h
            else:
                h_ref[0, rs, :] = h

    def first_half(j, carry):
        latent_pair(j, False)
        return carry

    def second_half(j, carry):
        latent_pair(j, True)
        return carry

    lax.fori_loop(0, nc // 2, first_half, 0, unroll=MLSTM_UNROLL)
    lax.fori_loop(nc // 2, nc, second_half, 0, unroll=MLSTM_UNROLL)


MLSTM_UNROLL = 4


def _mlstm(q, kt, v, gatet, qc, ktc, vc, gatetc, gate_b):
    B, H, L, dk = q.shape
    Lc = qc.shape[2]
    rows, cols = _gate_prep(gatet, gate_b, min(L, 1024))
    rows_c, cols_c = _gate_prep(gatetc, gate_b, Lc)
    seq = lambda n: [pl.BlockSpec((1, 1, n, dk), lambda b, h: (b, h, 0, 0)),
                     pl.BlockSpec((1, 1, dk, n), lambda b, h: (b, h, 0, 0)),
                     pl.BlockSpec((1, n, ML_DV), lambda b, h: (b, 0, h)),
                     pl.BlockSpec((1, 1, 8, n), lambda b, h: (b, h, 0, 0)),
                     pl.BlockSpec((1, 1, n, 8), lambda b, h: (b, h, 0, 0))]
    return pl.pallas_call(
        _mlstm_kernel,
        grid=(B, H),
        in_specs=seq(L) + seq(Lc),
        out_specs=pl.BlockSpec((1, L, ML_DV), lambda b, h: (b, 0, h)),
        out_shape=jax.ShapeDtypeStruct((B, L, ML_WIDTH), F32),
        scratch_shapes=[pltpu.VMEM((dk, 2 * ML_DV), F32), pltpu.VMEM((dk, 2 * ML_DV), F32)],
        compiler_params=_cparams(("parallel", "parallel")),
        name="mlstm_scan",
    )(q, kt, v, rows, cols, qc, ktc, vc, rows_c, cols_c)


def _out_proj_kernel(yhy_ref, hs_ref, o_ref, x_ref, er_ref, ec_ref, g1_ref, sh2_ref, sc2_ref,
                     mln_ref, post1_ref, pre2_ref, wout_ref, wr_ref, wrt_ref,
                     x1_ref, h2_ref, aff_ref, afft_ref):
    hs = hs_ref[0]
    parts = []
    for hd in range(ML_HEADS):
        hh = hs[:, hd * ML_DV:(hd + 1) * ML_DV]
        parts.append(hh * lax.rsqrt(jnp.mean(hh * hh, axis=-1, keepdims=True) + EPS))
    hn = jnp.concatenate(parts, axis=-1) * mln_ref[...]
    y_ml = hn * jax.nn.sigmoid(o_ref[0])
    yo_hy = [lax.dot_general(yhy_ref[0, 0, pl.ds(j, HY_WIDTH, stride=TIME_ROWS), :].astype(BF16),
                             wout_ref[:HY_WIDTH, :], (((0,), (0,)), ((), ())), preferred_element_type=F32)
             for j in range(TIME_ROWS)]
    yo = (jnp.concatenate(yo_hy, axis=0)
          + jnp.dot(y_ml.astype(BF16), wout_ref[HY_WIDTH:, :], preferred_element_type=F32))
    xf = x_ref[0] + _pe_tile(er_ref[...], ec_ref[...])
    x1 = xf + g1_ref[0] * _rms(yo, post1_ref[...])
    x1_ref[0] = x1
    h2f = _rms(x1, pre2_ref[...]) * (1.0 + sc2_ref[0]) + sh2_ref[0]
    _store_row_tiles(h2_ref, h2f)
    h2 = h2f.astype(BF16)
    logits = jnp.dot(h2, wr_ref[...], preferred_element_type=F32)
    ex = jnp.exp(logits - jnp.max(logits, axis=-1, keepdims=True))
    aff_ref[0] = ex / jnp.sum(ex, axis=-1, keepdims=True)
    logits_t = lax.dot_general(wrt_ref[...], h2, (((1,), (1,)), ((), ())), preferred_element_type=F32)
    ext = jnp.exp(logits_t - jnp.max(logits_t, axis=0, keepdims=True))
    afft_ref[0] = ext / jnp.sum(ext, axis=0, keepdims=True)


def _out_proj(y_hy, hsum, o, x, tabs, mod3, ml_norm, post1, pre2, w_out, w_router, tm):
    B, L, D = x.shape
    er, ec = tabs
    E = w_router.shape[1]
    assert tm == TIME_ROWS * LANES
    full = lambda a: pl.BlockSpec(a.shape, lambda b, i: (0,) * a.ndim)
    tok = lambda w: pl.BlockSpec((1, tm, w), lambda b, i: (b, i, 0))
    wout = w_out.astype(BF16)
    wr = w_router.astype(BF16)
    wrt = w_router.T.astype(BF16)
    return pl.pallas_call(
        _out_proj_kernel,
        grid=(B, L // tm),
        in_specs=[pl.BlockSpec((1, 1, HY_WIDTH * TIME_ROWS, LANES), lambda b, i: (b, i, 0, 0)),
                  tok(ML_WIDTH), tok(ML_WIDTH), tok(D),
                  pl.BlockSpec((tm // GRID_W, D // 2), lambda b, i: (i, 0)), full(ec),
                  _mod_spec(D, 2, 0, True), _mod_spec(D, 3, 0, True), _mod_spec(D, 4, 0, True),
                  full(ml_norm), full(post1), full(pre2),
                  full(wout), full(wr), full(wrt)],
        out_specs=[tok(D), pl.BlockSpec((1, tm * (D // LANES), LANES), lambda b, i: (b, i, 0)), tok(E),
                   pl.BlockSpec((1, E, tm), lambda b, i: (b, 0, i))],
        out_shape=[jax.ShapeDtypeStruct((B, L, D), F32),
                   jax.ShapeDtypeStruct((B, L * (D // LANES), LANES), F32),
                   jax.ShapeDtypeStruct((B, L, E), F32), jax.ShapeDtypeStruct((B, E, L), F32)],
        compiler_params=_cparams(("parallel", "parallel")),
        name="out_proj_router",
    )(y_hy, hsum, o, x, er, ec, mod3, mod3, mod3, ml_norm, post1, pre2, wout, wr, wrt)


LANES = 128
ROW_GROUP = 16
SELECT_FAST_SLOTS = 32


def _store_row_tiles(ref, val):
    n, nt = val.shape[0], val.shape[1] // LANES
    for c in range(nt):
        ref[0, pl.ds(c, n, stride=nt), :] = val[:, c * LANES:(c + 1) * LANES]


def _select_kernel(afft_ref, gind_ref, sel_ref, off_ref, gmax_ref, *, cap):
    E, L = afft_ref.shape[1], afft_ref.shape[2]
    aff = afft_ref[0]
    iota = lax.broadcasted_iota(jnp.int32, (E, L), 1)
    count = lambda ind: jnp.sum(ind, axis=1, keepdims=True)
    count_ge = lambda th: count(jnp.where(aff >= th, 1.0, 0.0))
    pow2 = lambda j: pltpu.bitcast((j - 24) << 23, F32)

    def estep(_, c):
        lo, hi = c
        mid = (lo + hi) >> 1
        ok = count_ge(pow2(mid)) >= cap
        return jnp.where(ok, mid, lo), jnp.where(ok, hi, mid)

    jlo, jhi = lax.fori_loop(0, 7, estep, (jnp.full((E, 1), 24, jnp.int32),
                                           jnp.full((E, 1), 152, jnp.int32)))

    def vstep(_, c):
        lo, hi = c
        mid = lo + (hi - lo) * 0.5
        ok = count_ge(mid) >= cap
        return jnp.where(ok, mid, lo), jnp.where(ok, hi, mid)

    lo, hi = lax.fori_loop(0, 40, vstep, (pow2(jlo), pow2(jhi)))
    gt = jnp.where(aff >= hi, 1.0, 0.0)
    eq = jnp.where(aff >= lo, 1.0, 0.0) - gt
    need = cap - count(gt)

    def istep(_, c):
        lo, hi = c
        mid = (lo + hi) >> 1
        ok = count(jnp.where(iota <= mid, eq, 0.0)) >= need
        return jnp.where(ok, lo, mid), jnp.where(ok, mid, hi)

    _, last = lax.fori_loop(0, L.bit_length() - 1, istep,
                            (jnp.full((E, 1), -1, jnp.int32), jnp.full((E, 1), L - 1, jnp.int32)))
    sel = gt + jnp.where(iota <= last, eq, 0.0)
    sel_ref[0] = sel
    cnt = jnp.dot(sel.astype(BF16), gind_ref[...], preferred_element_type=F32)
    G = cnt.shape[1]
    earlier = (lax.broadcasted_iota(jnp.int32, (G, G), 0) < lax.broadcasted_iota(jnp.int32, (G, G), 1))
    off_ref[0] = jnp.dot(cnt.astype(BF16), earlier.astype(BF16), preferred_element_type=F32).astype(jnp.int32)
    gmax_ref[0] = jnp.max(cnt, axis=0, keepdims=True).astype(jnp.int32)


def _compact_kernel(off_ref, gmax_ref, sel_ref, idx_ref, rank_ref):
    E, L = sel_ref.shape[1], sel_ref.shape[2]
    T = LANES
    G = L // T
    idx_ref[...] = jnp.zeros_like(idx_ref)
    r_i = lax.broadcasted_iota(jnp.int32, (T, T), 0)
    c_i = lax.broadcasted_iota(jnp.int32, (T, T), 1)
    before = (r_i < c_i).astype(BF16)
    slot_f = r_i.astype(F32)
    lane_f = lax.broadcasted_iota(jnp.int32, (1, T), 1).astype(F32)

    for g in range(G):
        rank_ref[:, g * T:(g + 1) * T] = jnp.dot(sel_ref[0, :, g * T:(g + 1) * T].astype(BF16), before,
                                                 preferred_element_type=F32)

    def group(g, carry):
        cols = pl.ds(pl.multiple_of(g * T, T), T)
        s = sel_ref[0, :, cols]
        rank = rank_ref[:, cols]
        tok = (lane_f + jnp.asarray(g * T, F32)) * s

        def emit(n_slots):
            for e in range(E):
                hit = rank[e:e + 1, :] == slot_f[:n_slots]
                ids = jnp.sum(jnp.where(hit, tok[e:e + 1, :], 0.0), axis=1, keepdims=True)
                idx_ref[0, e, pl.ds(off_ref[0, 0, e * G + g], n_slots), :] = ids.astype(jnp.int32)

        few = gmax_ref[0, 0, g] <= SELECT_FAST_SLOTS
        pl.when(few)(lambda: emit(SELECT_FAST_SLOTS))
        pl.when(jnp.logical_not(few))(lambda: emit(T))
        return carry

    lax.fori_loop(0, G, group, 0, unroll=2)


def _route_select(afft, cap):
    B, E, L = afft.shape
    G = L // LANES
    gind = (jnp.arange(L)[:, None] // LANES == jnp.arange(G)[None, :]).astype(BF16)
    sel, off, gmax = pl.pallas_call(
        functools.partial(_select_kernel, cap=cap),
        grid=(B,),
        in_specs=[pl.BlockSpec((1, E, L), lambda b: (b, 0, 0)), pl.BlockSpec((L, G), lambda b: (0, 0))],
        out_specs=[pl.BlockSpec((1, E, L), lambda b: (b, 0, 0)), pl.BlockSpec((1, E, G), lambda b: (b, 0, 0)),
                   pl.BlockSpec((1, 1, G), lambda b: (b, 0, 0))],
        out_shape=[jax.ShapeDtypeStruct((B, E, L), F32), jax.ShapeDtypeStruct((B, E, G), jnp.int32),
                   jax.ShapeDtypeStruct((B, 1, G), jnp.int32)],
        compiler_params=_cparams(("parallel",)),
        name="route_select",
    )(afft, gind)
    smem = lambda n: pl.BlockSpec((1, 1, n), lambda b: (b, 0, 0), memory_space=pltpu.SMEM)
    idx = pl.pallas_call(
        _compact_kernel,
        grid=(B,),
        in_specs=[smem(E * G), smem(G), pl.BlockSpec((1, E, L), lambda b: (b, 0, 0))],
        out_specs=pl.BlockSpec((1, E, cap + LANES, 1), lambda b: (b, 0, 0, 0)),
        out_shape=jax.ShapeDtypeStruct((B, E, cap + LANES, 1), jnp.int32),
        scratch_shapes=[pltpu.VMEM((E, L), F32)],
        compiler_params=_cparams(("parallel",)),
        name="route_compact",
    )(off.reshape(B, 1, E * G), gmax, sel)
    return idx[:, :, :cap, 0]


def _tile_rows(i, nt):
    return pl.ds(pl.multiple_of(i * nt, nt), nt)


def _gather_kernel(idx_ref, h_ref, aff_ref, xs_ref, gs_ref, buf_ref):
    cap, D = xs_ref.shape[1], xs_ref.shape[2]
    nt = D // LANES

    def body(s, c):
        base = s * ROW_GROUP
        rows = [idx_ref[0, 0, base + k] for k in range(ROW_GROUP)]
        vals = [h_ref[0, _tile_rows(r, nt), :] for r in rows]
        gates = [aff_ref[0, pl.ds(r, 1), :] for r in rows]
        for k in range(ROW_GROUP):
            buf_ref[_tile_rows(base + k, nt), :] = vals[k]
            gs_ref[0, pl.ds(base + k, 1), :] = gates[k]
        return c

    lax.fori_loop(0, cap // ROW_GROUP, body, 0)
    for c in range(nt):
        xs_ref[0, :, c * LANES:(c + 1) * LANES] = buf_ref[pl.ds(c, cap, stride=nt), :].astype(BF16)


def _route_gather(idx, h2t, aff, D):
    B, E, cap = idx.shape
    nt = D // LANES
    L = h2t.shape[1] // nt
    return pl.pallas_call(
        _gather_kernel,
        grid=(B, E),
        in_specs=[pl.BlockSpec((1, 1, cap), lambda b, e: (b * E + e, 0, 0), memory_space=pltpu.SMEM),
                  pl.BlockSpec((1, L * nt, LANES), lambda b, e: (b, 0, 0), pipeline_mode=pl.Buffered(1)),
                  pl.BlockSpec((1, L, E), lambda b, e: (b, 0, 0))],
        out_specs=[pl.BlockSpec((1, cap, D), lambda b, e: (e, b, 0)),
                   pl.BlockSpec((1, cap, E), lambda b, e: (e, b, 0))],
        out_shape=[jax.ShapeDtypeStruct((E, B * cap, D), BF16),
                   jax.ShapeDtypeStruct((E, B * cap, E), F32)],
        scratch_shapes=[pltpu.VMEM((cap * nt, LANES), F32)],
        compiler_params=_cparams(("arbitrary", "arbitrary")),
        name="route_gather",
    )(idx.reshape(B * E, 1, cap), h2t, aff)


def _combine_kernel(idx_ref, ye_ref, x1_ref, g2_ref, post2_ref, o_ref, y_ref, buf_ref, *, n_experts):
    j = pl.program_id(1)
    cap, D = ye_ref.shape[1], ye_ref.shape[2]
    nt = D // LANES
    tm = x1_ref.shape[1]

    @pl.when(j == 0)
    def _():
        y_ref[...] = jnp.zeros_like(y_ref)

    @pl.when(j < n_experts)
    def _():
        for c in range(nt):
            buf_ref[pl.ds(c, cap, stride=nt), :] = ye_ref[0, :, c * LANES:(c + 1) * LANES]

        def body(s, c):
            base = s * ROW_GROUP
            rows = [idx_ref[0, 0, base + k] for k in range(ROW_GROUP)]
            vals = [y_ref[_tile_rows(rows[k], nt), :] + buf_ref[_tile_rows(base + k, nt), :]
                    for k in range(ROW_GROUP)]
            for k in range(ROW_GROUP):
                y_ref[_tile_rows(rows[k], nt), :] = vals[k]
            return c

        lax.fori_loop(0, cap // ROW_GROUP, body, 0)

    @pl.when(j >= n_experts)
    def _():
        base = pl.multiple_of((j - n_experts) * (tm * nt), tm * nt)
        cols = [y_ref[pl.ds(base + c, tm, stride=nt), :] for c in range(nt)]
        ssq = sum(jnp.sum(y * y, axis=-1, keepdims=True) for y in cols)
        rstd = lax.rsqrt(ssq * (1.0 / D) + EPS)
        for c in range(nt):
            cs = slice(c * LANES, (c + 1) * LANES)
            o_ref[0, :, cs] = x1_ref[0, :, cs] + g2_ref[0, :, cs] * (cols[c] * rstd * post2_ref[:, cs])


def _route_combine(idx, ye, x1, mod3, post2, tm):
    B, E, cap = idx.shape
    _, L, D = x1.shape
    nt = D // LANES
    assert ye.shape[0] == E + 1
    tile = lambda j: jnp.maximum(j - E, 0)
    expert = lambda j: jnp.minimum(j, E - 1)
    return pl.pallas_call(
        functools.partial(_combine_kernel, n_experts=E),
        grid=(B, E + L // tm),
        in_specs=[pl.BlockSpec((1, 1, cap), lambda b, j: (b * E + expert(j), 0, 0), memory_space=pltpu.SMEM),
                  pl.BlockSpec((1, cap, D), lambda b, j: (expert(j) + 1, b, 0)),
                  pl.BlockSpec((1, tm, D), lambda b, j: (b, tile(j), 0)),
                  _mod_spec(D, 5, 0, True),
                  pl.BlockSpec((1, D), lambda b, j: (0, 0))],
        out_specs=pl.BlockSpec((1, tm, D), lambda b, j: (b, tile(j), 0)),
        out_shape=jax.ShapeDtypeStruct((B, L, D), F32),
        scratch_shapes=[pltpu.VMEM((L * nt, LANES), F32), pltpu.VMEM((cap * nt, LANES), F32)],
        compiler_params=_cparams(("arbitrary", "arbitrary")),
        name="route_combine_final",
    )(idx.reshape(B * E, 1, cap), ye, x1, mod3, post2)


def _expert_kernel(xs_ref, g_ref, wg_ref, wu_ref, wd_ref, ye_ref, wgb_ref, wub_ref, wdb_ref):
    s = pl.program_id(0)
    f = pl.program_id(1)
    ne = pl.num_programs(0) - 1
    tf = wg_ref.shape[2]
    slot = lax.rem(s, 2)

    @pl.when(s < ne)
    def _():
        cols = pl.ds(pl.multiple_of(f * tf, tf), tf)
        wgb_ref[slot, :, cols] = wg_ref[0].astype(BF16)
        wub_ref[slot, :, cols] = wu_ref[0].astype(BF16)
        wdb_ref[slot, cols, :] = wd_ref[0].astype(BF16)

    @pl.when(s == 0)
    def _():
        ye_ref[...] = jnp.zeros_like(ye_ref)

    @pl.when(s > 0)
    def _():
        e = s - 1
        prev = 1 - slot
        mt = xs_ref.shape[1] // FFN_M_SPLIT
        for mi in range(FFN_M_SPLIT):
            rs = slice(mi * mt, (mi + 1) * mt)
            xs = xs_ref[0, rs, :]
            a = jnp.dot(xs, wgb_ref[prev], preferred_element_type=F32)
            u = jnp.dot(xs, wub_ref[prev], preferred_element_type=F32)
            hmid = (a * jax.nn.sigmoid(a) * u).astype(BF16)
            out = jnp.dot(hmid, wdb_ref[prev], preferred_element_type=F32)
            gs = g_ref[0, rs, :]
            esel = lax.broadcasted_iota(jnp.int32, gs.shape, 1)
            ye_ref[0, rs, :] = out * jnp.sum(jnp.where(esel == e, gs, 0.0), axis=1, keepdims=True)


FFN_M_SPLIT = 2


def _expert_ffn(xs, gs, w_gate, w_up, w_down, tf):
    E, M, D = xs.shape
    F = w_gate.shape[2]
    nf = F // tf
    mt = M // nf
    behind = lambda s: jnp.maximum(s - 1, 0)
    ahead = lambda s: jnp.minimum(s, E - 1)
    return pl.pallas_call(
        _expert_kernel,
        grid=(E + 1, nf),
        in_specs=[pl.BlockSpec((1, mt, D), lambda s, f: (behind(s), f, 0)),
                  pl.BlockSpec((1, mt, E), lambda s, f: (behind(s), f, 0)),
                  pl.BlockSpec((1, D, tf), lambda s, f: (ahead(s), 0, f)),
                  pl.BlockSpec((1, D, tf), lambda s, f: (ahead(s), 0, f)),
                  pl.BlockSpec((1, tf, D), lambda s, f: (ahead(s), f, 0))],
        out_specs=pl.BlockSpec((1, mt, D), lambda s, f: (s, f, 0)),
        out_shape=jax.ShapeDtypeStruct((E + 1, M, D), F32),
        scratch_shapes=[pltpu.VMEM((2, D, F), BF16), pltpu.VMEM((2, D, F), BF16), pltpu.VMEM((2, F, D), BF16)],
        compiler_params=_cparams(("arbitrary", "arbitrary")),
        name="expert_ffn",
    )(xs, gs, w_gate, w_up, w_down)


def kernel(x, c, ctx, c_ctx, w_mod, b_mod, pre_norm1, post_norm1, pre_norm2, post_norm2, w_in, conv_w, conv_b, filt_w1, filt_b1, filt_w2, filt_b2, filt_w3, filt_b3, filt_w4, filt_freq, hyena_bias, ml_gate_b, ml_norm, w_out, w_router, w_exp_gate, w_exp_up, w_exp_down):
    B, L, D = x.shape
    depth = w_mod.shape[0]
    assert depth == 1, "single-layer block"
    li = 0
    tabs = _pe_tables(L // GRID_W, GRID_W, D)

    cc = jnp.concatenate([c, c_ctx[None], jnp.zeros((8 - B - 1, D), F32)], axis=0)
    mod3 = _modulation(cc, w_mod[li], b_mod[li])[:, None, :]

    pre1 = pre_norm1[li][None, :]
    w_proj = _in_proj_weights(w_in[li])
    zhy, q, kt, v, o, gatet = _in_proj(x, tabs, mod3, 0, pre1, w_proj, IN_PROJ_TILE, True)
    qc, ktc, vc, _, gatetc = _in_proj(ctx, None, mod3, B, pre1, w_proj, ctx.shape[1], False)

    kcirc = _hyena_filter(L, filt_w1[li], filt_b1[li], filt_w2[li], filt_b2[li], filt_w3[li],
                          filt_b3[li], filt_w4[li], filt_freq[li])
    y_hy = _hyena_conv(zhy, kcirc, conv_w[li], conv_b[li], hyena_bias[li])
    hsum = _mlstm(q, kt, v, gatet, qc, ktc, vc, gatetc, ml_gate_b[li])

    x1, h2t, aff, afft = _out_proj(y_hy, hsum, o, x, tabs, mod3, ml_norm[li][None, :],
                                   post_norm1[li][None, :], pre_norm2[li][None, :], w_out[li],
                                   w_router[li], IN_PROJ_TILE)

    cap = CAP_FACTOR * L // N_EXPERTS
    idx = _route_select(afft, cap)
    xs, gs = _route_gather(idx, h2t, aff, D)
    ye = _expert_ffn(xs, gs, w_exp_gate[li], w_exp_up[li], w_exp_down[li], 512)
    return _route_combine(idx, ye, x1, mod3, post_norm2[li][None, :], TOKEN_TILE)
```

```python
import functools
import math

import jax
import jax.numpy as jnp
from jax import lax
from jax.experimental import pallas as pl
from jax.experimental.pallas import tpu as pltpu

GRID_W = 64
HY_WIDTH = 512
ML_HEADS = 4
ML_DK = 64
ML_DV = 128
ML_WIDTH = ML_HEADS * ML_DV
HY_COLS = 3 * HY_WIDTH
QK_COLS = ML_HEADS * ML_DK
N_GATES = 4 * ML_HEADS
FILTER_EMB = 33
DECAY_TARGET = 1e-2
FAST_DECAY_PCT = 0.3
SLOW_DECAY_PCT = 1.5
CHUNK = 128
GATE_CAP = 15.0
N_EXPERTS = 16
CAP_FACTOR = 2
EPS = 1e-6

F32 = jnp.float32
BF16 = jnp.bfloat16

TOKEN_TILE = 512
TIME_ROWS = 8
IN_PROJ_TILE = 1024
VMEM_LIMIT = 56 * 1024 * 1024


def _cparams(sem):
    return pltpu.CompilerParams(dimension_semantics=sem, vmem_limit_bytes=VMEM_LIMIT)


def _rms(xf, g):
    return xf * lax.rsqrt(jnp.mean(xf * xf, axis=-1, keepdims=True) + EPS) * g


def _bdot(a, b):
    return jnp.dot(a.astype(BF16), b.astype(BF16), preferred_element_type=F32)


def _pe_tables_kernel(omega_ref, er_ref, ec_ref):
    quarter = omega_ref.shape[1]
    om = omega_ref[...]
    for ref in (er_ref, ec_ref):
        n = ref.shape[0]
        pos = lax.broadcasted_iota(jnp.int32, (n, quarter), 0).astype(F32)
        ang = pos * om
        ref[:, :quarter] = jnp.sin(ang)
        ref[:, quarter:] = jnp.cos(ang)


def _pe_tables(rows, cols, dim):
    quarter = dim // 4
    omega = (1.0 / (10000.0 ** (jnp.arange(quarter, dtype=F32) / quarter)))[None, :]
    return pl.pallas_call(
        _pe_tables_kernel,
        out_shape=(jax.ShapeDtypeStruct((rows, dim // 2), F32),
                   jax.ShapeDtypeStruct((cols, dim // 2), F32)),
        name="pe_tables",
    )(omega)


def _pe_tile(er_blk, ec):
    nr, half = er_blk.shape
    row_part = jnp.broadcast_to(er_blk[:, None, :], (nr, GRID_W, half)).reshape(nr * GRID_W, half)
    col_part = jnp.broadcast_to(ec[None, :, :], (nr, GRID_W, half)).reshape(nr * GRID_W, half)
    return jnp.concatenate([row_part, col_part], axis=-1)


def _mod_kernel(c_ref, w_ref, b_ref, o_ref):
    c = c_ref[...]
    s = c * jax.nn.sigmoid(c)
    o_ref[...] = _bdot(s, w_ref[...]) + b_ref[...]


def _modulation(cc, w_mod, b_mod):
    rows, d = cc.shape
    n = w_mod.shape[1]
    tn = 512
    return pl.pallas_call(
        _mod_kernel,
        grid=(n // tn,),
        in_specs=[pl.BlockSpec((rows, d), lambda j: (0, 0)),
                  pl.BlockSpec((d, tn), lambda j: (0, j)),
                  pl.BlockSpec((1, tn), lambda j: (0, j))],
        out_specs=pl.BlockSpec((rows, tn), lambda j: (0, j)),
        out_shape=jax.ShapeDtypeStruct((rows, n), F32),
        compiler_params=_cparams(("arbitrary",)),
        name="modulation",
    )(cc, w_mod, b_mod[None, :])


def _in_proj_kernel(*refs, with_hyena):
    if with_hyena:
        (x_ref, er_ref, ec_ref, sh_ref, sc_ref, g_ref, wn_ref, wt_ref,
         zt_ref, q_ref, kt_ref, v_ref, o_ref, gatet_ref) = refs
        xf = x_ref[0] + _pe_tile(er_ref[...], ec_ref[...])
    else:
        (x_ref, sh_ref, sc_ref, g_ref, wn_ref, wt_ref,
         q_ref, kt_ref, v_ref, o_ref, gatet_ref) = refs
        xf = x_ref[0]
    h = _rms(xf, g_ref[...]) * (1.0 + sc_ref[0]) + sh_ref[0]
    hb = h.astype(BF16)
    z = jnp.dot(hb, wn_ref[...], preferred_element_type=F32)
    qs = z[:, :QK_COLS] * (ML_DK ** -0.5)
    for hd in range(ML_HEADS):
        q_ref[0, hd] = qs[:, hd * ML_DK:(hd + 1) * ML_DK].astype(BF16)
    v_ref[0] = z[:, QK_COLS:QK_COLS + ML_WIDTH].astype(BF16)
    o_ref[0] = z[:, QK_COLS + ML_WIDTH:]
    zt = lax.dot_general(wt_ref[...], hb, (((1,), (1,)), ((), ())), preferred_element_type=F32)
    off = 0
    if with_hyena:
        for j in range(zt.shape[1] // LANES):
            zt_ref[0, 0, pl.ds(j, HY_COLS, stride=TIME_ROWS), :] = zt[:HY_COLS, j * LANES:(j + 1) * LANES]
        off = HY_COLS
    for hd in range(ML_HEADS):
        kt_ref[0, hd] = zt[off + hd * ML_DK:off + (hd + 1) * ML_DK, :].astype(BF16)
    gatet_ref[0] = zt[off + QK_COLS:, :]


def _mod_spec(D, chunk, row0, per_sample):
    return pl.BlockSpec((1, 1, D), lambda b, i: ((b if per_sample else 0) + row0, 0, chunk))


def _in_proj_weights(w_in):
    w_q = w_in[:, HY_COLS:HY_COLS + QK_COLS]
    w_k = w_in[:, HY_COLS + QK_COLS:HY_COLS + 2 * QK_COLS]
    w_vo = w_in[:, HY_COLS + 2 * QK_COLS:HY_COLS + 2 * QK_COLS + 2 * ML_WIDTH]
    w_g = w_in[:, HY_COLS + 2 * QK_COLS + 2 * ML_WIDTH:]
    wn = jnp.concatenate([w_q, w_vo], axis=1).astype(BF16)
    wt = jnp.concatenate([w_in[:, :HY_COLS], w_k, w_g], axis=1).T.astype(BF16)
    return wn, wt


def _in_proj(x, tabs, mod3, mod_row0, g, weights, tm, with_hyena):
    B, L, D = x.shape
    wn, wt = weights
    if not with_hyena:
        wt = wt[HY_COLS:]
    full = lambda a: pl.BlockSpec(a.shape, lambda b, i: (0,) * a.ndim)
    in_specs = [pl.BlockSpec((1, tm, D), lambda b, i: (b, i, 0))]
    args = [x]
    if with_hyena:
        er, ec = tabs
        in_specs += [pl.BlockSpec((tm // GRID_W, D // 2), lambda b, i: (i, 0)), full(ec)]
        args += [er, ec]
    in_specs += [_mod_spec(D, 0, mod_row0, with_hyena), _mod_spec(D, 1, mod_row0, with_hyena),
                 full(g), full(wn), full(wt)]
    args += [mod3, mod3, g, wn, wt]
    out_shape, out_specs = [], []
    if with_hyena:
        assert tm == TIME_ROWS * LANES
        out_shape.append(jax.ShapeDtypeStruct((B, L // tm, HY_COLS * TIME_ROWS, LANES), F32))
        out_specs.append(pl.BlockSpec((1, 1, HY_COLS * TIME_ROWS, LANES), lambda b, i: (b, i, 0, 0)))
    out_shape += [jax.ShapeDtypeStruct((B, ML_HEADS, L, ML_DK), BF16),
                  jax.ShapeDtypeStruct((B, ML_HEADS, ML_DK, L), BF16),
                  jax.ShapeDtypeStruct((B, L, ML_WIDTH), BF16),
                  jax.ShapeDtypeStruct((B, L, ML_WIDTH), F32),
                  jax.ShapeDtypeStruct((B, N_GATES, L), F32)]
    out_specs += [pl.BlockSpec((1, ML_HEADS, tm, ML_DK), lambda b, i: (b, 0, i, 0)),
                  pl.BlockSpec((1, ML_HEADS, ML_DK, tm), lambda b, i: (b, 0, 0, i)),
                  pl.BlockSpec((1, tm, ML_WIDTH), lambda b, i: (b, i, 0)),
                  pl.BlockSpec((1, tm, ML_WIDTH), lambda b, i: (b, i, 0)),
                  pl.BlockSpec((1, N_GATES, tm), lambda b, i: (b, 0, i))]
    return pl.pallas_call(
        functools.partial(_in_proj_kernel, with_hyena=with_hyena),
        grid=(B, L // tm),
        in_specs=in_specs,
        out_specs=out_specs,
        out_shape=out_shape,
        compiler_params=_cparams(("parallel", "parallel")),
        name="in_proj_hy" if with_hyena else "in_proj_ctx",
    )(*args)


FILT_TILE = 1024
FILT_CBLK = 128


def _filter_kernel(w1a_ref, w1b_ref, w1c_ref, b1_ref, w2_ref, b2_ref, w3_ref, b3_ref, fr_ref,
                   w4f_ref, w4b_ref, dl_ref, mir_ref, k_ref, hf_ref, *, L):
    bands = (FILTER_EMB - 1) // 2

    @pl.when(pl.program_id(0) == 0)
    def _():
        fk = (1e-4 + lax.broadcasted_iota(jnp.int32, (bands, 1), 0).astype(F32)
              * ((bands - 1 - 1e-4) / (bands - 1)))
        fr = fr_ref[...]
        for j in range(L // FILT_TILE):
            pos = (lax.broadcasted_iota(jnp.int32, (1, FILT_TILE), 1) + j * FILT_TILE).astype(F32)
            tl = pos * (1.0 / (L - 1))
            ang = fk * (pos * (2.0 * math.pi / L))
            pre = (w1a_ref[...].astype(F32) * tl.astype(BF16).astype(F32)
                   + _bdot(w1b_ref[...], jnp.cos(ang)) + _bdot(w1c_ref[...], -jnp.sin(ang)))
            h = jnp.sin(fr * (pre + b1_ref[...]))
            h = jnp.sin(fr * (_bdot(w2_ref[...], h) + b2_ref[...]))
            h = jnp.sin(fr * (_bdot(w3_ref[...], h) + b3_ref[...]))
            hf_ref[:, j * FILT_TILE:(j + 1) * FILT_TILE] = h.astype(BF16)

    pos = lax.broadcasted_iota(jnp.int32, (1, L), 1).astype(F32)
    decay = jnp.exp(-(pos * (1.0 / (L - 1))) * dl_ref[...])
    hf = hf_ref[...]
    k_ref[:, :L] = jnp.dot(w4f_ref[...], hf, preferred_element_type=F32) * decay
    gb = (jnp.dot(w4b_ref[...], hf, preferred_element_type=F32) * decay).astype(BF16)
    nblk = L // LANES
    for j in range(nblk):
        src = gb[:, (nblk - 1 - j) * LANES:(nblk - j) * LANES]
        nxt = gb[:, (nblk - j) * LANES:(nblk - j + 1) * LANES] if j > 0 else jnp.zeros_like(src)
        k_ref[:, L + j * LANES:L + (j + 1) * LANES] = jnp.dot(
            jnp.concatenate([src, nxt], axis=1), mir_ref[...], preferred_element_type=F32)


def _hyena_filter(L, w1, b1, w2, b2, w3, b3, w4, freq):
    hid = w2.shape[0]
    bands = (FILTER_EMB - 1) // 2
    col = lambda a: a[:, None]
    w1t = w1.T.astype(BF16)
    min_decay = math.log(DECAY_TARGET) / SLOW_DECAY_PCT
    max_decay = math.log(DECAY_TARGET) / FAST_DECAY_PCT
    dl = jnp.abs(jnp.linspace(min_decay, max_decay, HY_WIDTH, dtype=F32))[:, None]
    w4t = w4.T.astype(BF16)
    assert DFT_PASSES == 1, "the mirrored taps are kept at the bf16 precision a single-pass DFT reads"
    lane = jnp.arange(LANES)
    mir = jnp.concatenate([(lane[:, None] + lane[None, :] == LANES),
                           (lane[:, None] == 0) & (lane[None, :] == 0)], axis=0).astype(BF16)
    full = lambda a: pl.BlockSpec(a.shape, lambda i: (0,) * a.ndim)
    args = [w1t[:, 0:1], w1t[:, 1:1 + bands], w1t[:, 1 + bands:], col(b1), w2.T.astype(BF16), col(b2),
            w3.T.astype(BF16), col(b3), col(freq)]
    return pl.pallas_call(
        functools.partial(_filter_kernel, L=L),
        grid=(HY_WIDTH // FILT_CBLK,),
        in_specs=[full(a) for a in args] + [
            pl.BlockSpec((FILT_CBLK, hid), lambda i: (i, 0)),
            pl.BlockSpec((FILT_CBLK, hid), lambda i: (HY_WIDTH // FILT_CBLK + i, 0)),
            pl.BlockSpec((FILT_CBLK, 1), lambda i: (i, 0)), full(mir)],
        out_specs=pl.BlockSpec((FILT_CBLK, 2 * L), lambda i: (i, 0)),
        out_shape=jax.ShapeDtypeStruct((HY_WIDTH, 2 * L), F32),
        scratch_shapes=[pltpu.VMEM((hid, L), BF16)],
        compiler_params=_cparams(("arbitrary",)),
        name="hyena_filter",
    )(*args, w4t, w4t, dl, mir)


FFT_N = 128
HY_CBLK = 32
HY_GROUP = 8
HY_UNROLL = 4
DFT_PASSES = 1


def _dft_constants(n1_data):
    import numpy as np
    n = FFT_N
    k = np.arange(n)
    ang = -2.0 * np.pi * ((k[:, None] * k[None, :]) % n) / n
    fre, fim = np.cos(ang), np.sin(ang)
    m = n1_data
    fa_d = np.block([[fre[:, :m], -fim[:, :m]], [fim[:, :m], fre[:, :m]]])
    fa_f = np.concatenate([fre, fim], axis=0)
    fb = np.block([[fre, fim], [-fim, fre]])
    fbi = np.block([[fre, -fim], [fim, fre]])
    fc = np.block([[fre[:m, :], fim[:m, :]], [-fim[:m, :], fre[:m, :]]]) / (n * n)
    tang = -2.0 * np.pi * (k[:, None] * k[None, :]) / (n * n)
    tw = np.stack([np.cos(tang), np.sin(tang)])

    def hilo(a):
        a32 = jnp.asarray(a, F32)
        hi = a32.astype(BF16)
        lo = (a32 - hi.astype(F32)).astype(BF16)
        return jnp.stack([hi, lo])

    return hilo(fa_d), hilo(fa_f), hilo(fb), hilo(fbi), hilo(fc), jnp.asarray(tw, F32)


def _mm_const_lhs(c_ref, d):
    dh = d.astype(BF16)
    acc = jnp.dot(c_ref[0], dh, preferred_element_type=F32)
    if DFT_PASSES == 3:
        dl = (d - dh.astype(F32)).astype(BF16)
        acc = acc + (jnp.dot(c_ref[0], dl, preferred_element_type=F32)
                     + jnp.dot(c_ref[1], dh, preferred_element_type=F32))
    return acc


def _mm_const_rhs(d, c_ref):
    dh = d.astype(BF16)
    acc = jnp.dot(dh, c_ref[0], preferred_element_type=F32)
    if DFT_PASSES == 3:
        dl = (d - dh.astype(F32)).astype(BF16)
        acc = acc + (jnp.dot(dl, c_ref[0], preferred_element_type=F32)
                     + jnp.dot(dh, c_ref[1], preferred_element_type=F32))
    return acc


def _cmul(are, aim, bre, bim):
    return are * bre - aim * bim, are * bim + aim * bre


def _hyena_conv_kernel(x0_ref, x1_ref, v_ref, kc_ref, w0_ref, w1_ref, wv_ref, b0_ref, b1_ref, bv_ref,
                       hb_ref, fad_ref, faf_ref, fb_ref, fbi_ref, fc_ref, tw_ref,
                       o_ref, u_ref, s_ref, ks_ref):
    n = FFT_N
    cb = kc_ref.shape[0]
    nb, nt = x0_ref.shape[0], x0_ref.shape[1]
    m = nt * TIME_ROWS

    def chan(ref, b, c):
        return ref[b, :, pl.ds(pl.multiple_of(c * TIME_ROWS, TIME_ROWS), TIME_ROWS), :].reshape(m, n)

    sub = lax.broadcasted_iota(jnp.int32, (m, n), 0)
    lane = lax.broadcasted_iota(jnp.int32, (m, n), 1)
    tre, tim = tw_ref[0], tw_ref[1]

    def sconv(z, w_ref, b_ref, c):
        a = pltpu.roll(z, 1, axis=1)
        prev = jnp.where(lane == 0, jnp.where(sub == 0, 0.0, pltpu.roll(a, 1, axis=0)), a)
        a2 = pltpu.roll(z, n - 1, axis=1)
        nxt = jnp.where(lane == n - 1, jnp.where(sub == m - 1, 0.0, pltpu.roll(a2, m - 1, axis=0)), a2)
        return prev * w_ref[0, c] + z * w_ref[1, c] + nxt * w_ref[2, c] + b_ref[c]

    def spectrum_rows(res):
        outs = []
        for h in range(2):
            are, aim = _cmul(res[:n, h * n:(h + 1) * n], res[n:, h * n:(h + 1) * n], tre, tim)
            outs.append(jnp.concatenate([are, aim], axis=1))
        return outs

    def fwd_pair(p, carry):
        c0 = 2 * p
        us = []
        for c in (c0, c0 + 1):
            ub = []
            for b in range(nb):
                x1c = sconv(chan(x1_ref, b, c), w1_ref, b1_ref, c)
                vc = sconv(chan(v_ref, b, c), wv_ref, bv_ref, c)
                u = x1c * vc
                u_ref[b, c] = u
                ub.append(u)
            us.append(ub)
        wd = jnp.concatenate([jnp.concatenate([us[0][b], us[1][b]], axis=1) for b in range(nb)], axis=0)
        sa, sb = spectrum_rows(_mm_const_lhs(fad_ref, wd))
        s_ref[c0] = sa.astype(s_ref.dtype)
        s_ref[c0 + 1] = sb.astype(s_ref.dtype)
        wk = jnp.concatenate([kc_ref[c0], kc_ref[c0 + 1]], axis=1)
        ka, kb = spectrum_rows(_mm_const_lhs(faf_ref, wk))
        ks_ref[c0] = ka.astype(ks_ref.dtype)
        ks_ref[c0 + 1] = kb.astype(ks_ref.dtype)
        return carry

    lax.fori_loop(0, cb // 2, fwd_pair, 0, unroll=HY_UNROLL)

    def mid_group(g, carry):
        gs = pl.ds(pl.multiple_of(g * HY_GROUP, HY_GROUP), HY_GROUP)
        x = _mm_const_rhs(s_ref[gs].reshape(HY_GROUP * n, 2 * n), fb_ref)
        k = _mm_const_rhs(ks_ref[gs].reshape(HY_GROUP * n, 2 * n), fb_ref)
        yre, yim = _cmul(x[:, :n], x[:, n:], k[:, :n], k[:, n:])
        vv = _mm_const_rhs(jnp.concatenate([yre, yim], axis=1), fbi_ref).reshape(HY_GROUP, n, 2 * n)
        vre, vim = _cmul(vv[:, :, :n], vv[:, :, n:], tre[None], -tim[None])
        s_ref[gs] = jnp.concatenate([vre, vim], axis=2).astype(s_ref.dtype)
        return carry

    lax.fori_loop(0, cb // HY_GROUP, mid_group, 0, unroll=HY_UNROLL)

    def inv_pair(p, carry):
        c0 = 2 * p
        sa, sb = s_ref[c0], s_ref[c0 + 1]
        wd = jnp.concatenate([jnp.concatenate([sa[:, :n], sb[:, :n]], axis=1),
                              jnp.concatenate([sa[:, n:], sb[:, n:]], axis=1)], axis=0)
        res = _mm_const_lhs(fc_ref, wd)
        for h, c in enumerate((c0, c0 + 1)):
            for b in range(nb):
                y = res[b * m:(b + 1) * m, h * n:(h + 1) * n]
                x0c = sconv(chan(x0_ref, b, c), w0_ref, b0_ref, c)
                out = x0c * (y + hb_ref[c] * u_ref[b, c])
                o_ref[b, :, pl.ds(pl.multiple_of(c * TIME_ROWS, TIME_ROWS), TIME_ROWS), :] = out.reshape(
                    nt, TIME_ROWS, n)
        return carry

    lax.fori_loop(0, cb // 2, inv_pair, 0, unroll=HY_UNROLL)


def _hyena_conv(zt, kcirc, conv_w, conv_b, hy_bias):
    B, nt, _, n = zt.shape
    m = nt * TIME_ROWS
    L = m * n
    C = HY_WIDTH
    assert B == 2 and n == FFT_N and 2 * L == n * n, "complex packing of two samples over a 128 x 128 point transform"
    z4 = zt
    rows = HY_CBLK * TIME_ROWS
    spec_dt = BF16 if DFT_PASSES == 1 else F32
    k3 = kcirc.reshape(C, n, n)
    cw = conv_w.reshape(3, 3 * C, 1, 1)
    cbias = conv_b.reshape(3 * C, 1, 1)
    hb = hy_bias.reshape(C, 1, 1)
    consts = _dft_constants(m)
    nblk = C // HY_CBLK
    zspec = lambda part: pl.BlockSpec((B, nt, rows, n), lambda i: (0, 0, part * nblk + i, 0))
    wspec = lambda part: pl.BlockSpec((3, HY_CBLK, 1, 1), lambda i: (0, part * nblk + i, 0, 0))
    bspec = lambda part: pl.BlockSpec((HY_CBLK, 1, 1), lambda i: (part * nblk + i, 0, 0))
    full = lambda a: pl.BlockSpec(a.shape, lambda i: (0,) * a.ndim)
    y = pl.pallas_call(
        _hyena_conv_kernel,
        grid=(nblk,),
        in_specs=[zspec(0), zspec(1), zspec(2), pl.BlockSpec((HY_CBLK, n, n), lambda i: (i, 0, 0)),
                  wspec(0), wspec(1), wspec(2), bspec(0), bspec(1), bspec(2), bspec(0)]
                 + [full(a) for a in consts],
        out_specs=pl.BlockSpec((B, nt, rows, n), lambda i: (0, 0, i, 0)),
        out_shape=jax.ShapeDtypeStruct((B, nt, C * TIME_ROWS, n), F32),
        scratch_shapes=[pltpu.VMEM((B, HY_CBLK, m, n), F32), pltpu.VMEM((HY_CBLK, n, 2 * n), spec_dt),
                        pltpu.VMEM((HY_CBLK, n, 2 * n), spec_dt)],
        compiler_params=_cparams(("parallel",)),
        name="hyena_conv",
    )(z4, z4, z4, k3, cw, cw, cw, cbias, cbias, cbias, hb, *consts)
    return y


def _split3(a):
    hi = a.astype(BF16)
    r1 = a - hi.astype(F32)
    mid = r1.astype(BF16)
    lo = (r1 - mid.astype(F32)).astype(BF16)
    return hi, mid, lo


def _exact_dot_right(a, tri_bf):
    hi, mid, lo = _split3(a)
    d = lambda p: jnp.dot(p, tri_bf, preferred_element_type=F32)
    return (d(lo) + d(mid)) + d(hi)


def _soft_gates(g):
    g = GATE_CAP * jnp.tanh(g * (1.0 / GATE_CAP))
    logsig = jnp.minimum(g, 0.0) - jnp.log1p(jnp.exp(-jnp.abs(g)))
    return g, logsig


def _gate_prep_kernel(gt_ref, gbt_ref, rows_ref, cols_ref):
    T = CHUNK
    n = gt_ref.shape[2]
    up_bf = (lax.broadcasted_iota(jnp.int32, (T, T), 1) >= lax.broadcasted_iota(jnp.int32, (T, T), 0)).astype(BF16)
    out8 = lax.broadcasted_iota(jnp.int32, (8, 1), 0)
    cap, ls = _soft_gates(gt_ref[0] + gbt_ref[...])
    H = ML_HEADS
    head = lambda hd: jnp.where(out8 == 0, cap[hd:hd + 1], jnp.where(out8 == 1, ls[H + hd:H + hd + 1],
                                jnp.where(out8 == 2, cap[2 * H + hd:2 * H + hd + 1],
                                          jnp.where(out8 == 3, ls[3 * H + hd:3 * H + hd + 1], 0.0))))
    base = jnp.concatenate([head(hd) for hd in range(H)], axis=0)
    kind = jnp.concatenate([out8] * H, axis=0)
    for j in range(n // T):
        blk = base[:, j * T:(j + 1) * T]
        run = _exact_dot_right(blk, up_bf)
        suf = run[:, T - 1:T] - run + blk
        rows = jnp.where(kind == 1, run, jnp.where(kind == 3, suf, blk))
        cols = rows.T
        for hd in range(H):
            rows_ref[0, hd, :, j * T:(j + 1) * T] = rows[8 * hd:8 * (hd + 1)]
            cols_ref[0, hd, j * T:(j + 1) * T, :] = cols[:, 8 * hd:8 * (hd + 1)]


def _gate_prep(gatet, gate_b, tile):
    B, G, L = gatet.shape
    H = ML_HEADS
    return pl.pallas_call(
        _gate_prep_kernel,
        grid=(B, L // tile),
        in_specs=[pl.BlockSpec((1, G, tile), lambda b, i: (b, 0, i)),
                  pl.BlockSpec((G, 1), lambda b, i: (0, 0))],
        out_specs=[pl.BlockSpec((1, H, 8, tile), lambda b, i: (b, 0, 0, i)),
                   pl.BlockSpec((1, H, tile, 8), lambda b, i: (b, 0, i, 0))],
        out_shape=[jax.ShapeDtypeStruct((B, H, 8, L), F32), jax.ShapeDtypeStruct((B, H, L, 8), F32)],
        compiler_params=_cparams(("parallel", "parallel")),
        name="mlstm_gate_prep",
    )(gatet, gate_b[:, None])


def _mlstm_kernel(q_ref, kt_ref, v_ref, r_ref, bt_ref, qc_ref, ktc_ref, vc_ref, rc_ref, btc_ref,
                  h_ref, cf_ref, cb_ref):
    T = CHUNK
    L = q_ref.shape[2]
    Lc = qc_ref.shape[2]
    nc, ncc = L // T, Lc // T
    row = lax.broadcasted_iota(jnp.int32, (T, T), 0)
    col = lax.broadcasted_iota(jnp.int32, (T, T), 1)
    lo_mask = col <= row
    up_mask = col >= row
    ones_blk = jnp.ones((T, ML_DV), BF16)

    cf_ref[...] = jnp.zeros_like(cf_ref)
    cb_ref[...] = jnp.zeros_like(cb_ref)

    def chunk_step(q, kt, v, rows, cols, c_ref, backward):
        i_r = rows[2:3] if backward else rows[0:1]
        b_r = rows[3:4] if backward else rows[1:2]
        b_c = jnp.broadcast_to(cols[:, 3:4] if backward else cols[:, 1:2], (T, T))
        b_end = b_r[:, 0:1] if backward else b_r[:, T - 1:T]
        mask = up_mask if backward else lo_mask
        w_intra = jnp.exp(jnp.where(mask, b_c - b_r + i_r, -jnp.inf) - GATE_CAP)
        s = jnp.dot(q, kt, preferred_element_type=F32) * w_intra
        qe = (q.astype(F32) * jnp.exp(b_c[:, :ML_DK])).astype(BF16)
        v_aug = jnp.concatenate([v, ones_blk], axis=1)
        c_aug = c_ref[...]
        res = jnp.dot(jnp.concatenate([s.astype(BF16), qe], axis=1),
                      jnp.concatenate([v_aug, c_aug.astype(BF16)], axis=0),
                      preferred_element_type=F32)
        h = res[:, :ML_DV] / jnp.maximum(jnp.abs(res[:, ML_DV:]), math.exp(-GATE_CAP))
        kw = (kt.astype(F32) * jnp.exp(b_end - b_r + i_r - GATE_CAP)).astype(BF16)
        c_ref[...] = jnp.exp(b_end) * c_aug + jnp.dot(kw, v_aug, preferred_element_type=F32)
        return h

    for j in range(ncc):
        for backward in (False, True):
            jj = (ncc - 1 - j) if backward else j
            cs = slice(jj * T, (jj + 1) * T)
            chunk_step(qc_ref[0, 0, cs, :], ktc_ref[0, 0, :, cs], vc_ref[0, cs, :], rc_ref[0, 0, :, cs],
                       btc_ref[0, 0, cs, :], cb_ref if backward else cf_ref, backward)

    def latent_pair(j, accumulate):
        for backward in (False, True):
            jj = (nc - 1 - j) if backward else j
            rs = pl.ds(pl.multiple_of(jj * T, T), T)
            h = chunk_step(q_ref[0, 0, rs, :], kt_ref[0, 0, :, rs], v_ref[0, rs, :], r_ref[0, 0, :, rs],
                           bt_ref[0, 0, rs, :], cb_ref if backward else cf_ref, backward)
            if accumulate:
                h_ref[0, rs, :] = h_ref[0, rs, :] + h
            else:
                h_ref[0, rs, :] = h

    def first_half(j, carry):
        latent_pair(j, False)
        return carry

    def second_half(j, carry):
        latent_pair(j, True)
        return carry

    lax.fori_loop(0, nc // 2, first_half, 0, unroll=MLSTM_UNROLL)
    lax.fori_loop(nc // 2, nc, second_half, 0, unroll=MLSTM_UNROLL)


MLSTM_UNROLL = 8


def _mlstm(q, kt, v, gatet, qc, ktc, vc, gatetc, gate_b):
    B, H, L, dk = q.shape
    Lc = qc.shape[2]
    rows, cols = _gate_prep(gatet, gate_b, min(L, 1024))
    rows_c, cols_c = _gate_prep(gatetc, gate_b, Lc)
    seq = lambda n: [pl.BlockSpec((1, 1, n, dk), lambda b, h: (b, h, 0, 0)),
                     pl.BlockSpec((1, 1, dk, n), lambda b, h: (b, h, 0, 0)),
                     pl.BlockSpec((1, n, ML_DV), lambda b, h: (b, 0, h)),
                     pl.BlockSpec((1, 1, 8, n), lambda b, h: (b, h, 0, 0)),
                     pl.BlockSpec((1, 1, n, 8), lambda b, h: (b, h, 0, 0))]
    return pl.pallas_call(
        _mlstm_kernel,
        grid=(B, H),
        in_specs=seq(L) + seq(Lc),
        out_specs=pl.BlockSpec((1, L, ML_DV), lambda b, h: (b, 0, h)),
        out_shape=jax.ShapeDtypeStruct((B, L, ML_WIDTH), F32),
        scratch_shapes=[pltpu.VMEM((dk, 2 * ML_DV), F32), pltpu.VMEM((dk, 2 * ML_DV), F32)],
        compiler_params=_cparams(("parallel", "parallel")),
        name="mlstm_scan",
    )(q, kt, v, rows, cols, qc, ktc, vc, rows_c, cols_c)


def _out_proj_kernel(yhy_ref, hs_ref, o_ref, x_ref, er_ref, ec_ref, g1_ref, sh2_ref, sc2_ref,
                     mln_ref, post1_ref, pre2_ref, wout_ref, wr_ref, wrt_ref,
                     x1_ref, h2_ref, aff_ref, afft_ref):
    hs = hs_ref[0]
    parts = []
    for hd in range(ML_HEADS):
        hh = hs[:, hd * ML_DV:(hd + 1) * ML_DV]
        parts.append(hh * lax.rsqrt(jnp.mean(hh * hh, axis=-1, keepdims=True) + EPS))
    hn = jnp.concatenate(parts, axis=-1) * mln_ref[...]
    y_ml = hn * jax.nn.sigmoid(o_ref[0])
    yo_hy = [lax.dot_general(yhy_ref[0, 0, pl.ds(j, HY_WIDTH, stride=TIME_ROWS), :].astype(BF16),
                             wout_ref[:HY_WIDTH, :], (((0,), (0,)), ((), ())), preferred_element_type=F32)
             for j in range(TIME_ROWS)]
    yo = (jnp.concatenate(yo_hy, axis=0)
          + jnp.dot(y_ml.astype(BF16), wout_ref[HY_WIDTH:, :], preferred_element_type=F32))
    xf = x_ref[0] + _pe_tile(er_ref[...], ec_ref[...])
    x1 = xf + g1_ref[0] * _rms(yo, post1_ref[...])
    x1_ref[0] = x1
    h2f = _rms(x1, pre2_ref[...]) * (1.0 + sc2_ref[0]) + sh2_ref[0]
    _store_row_tiles(h2_ref, h2f)
    h2 = h2f.astype(BF16)
    logits = jnp.dot(h2, wr_ref[...], preferred_element_type=F32)
    ex = jnp.exp(logits - jnp.max(logits, axis=-1, keepdims=True))
    aff_ref[0] = ex / jnp.sum(ex, axis=-1, keepdims=True)
    logits_t = lax.dot_general(wrt_ref[...], h2, (((1,), (1,)), ((), ())), preferred_element_type=F32)
    ext = jnp.exp(logits_t - jnp.max(logits_t, axis=0, keepdims=True))
    afft_ref[0] = ext / jnp.sum(ext, axis=0, keepdims=True)


def _out_proj(y_hy, hsum, o, x, tabs, mod3, ml_norm, post1, pre2, w_out, w_router, tm):
    B, L, D = x.shape
    er, ec = tabs
    E = w_router.shape[1]
    assert tm == TIME_ROWS * LANES
    full = lambda a: pl.BlockSpec(a.shape, lambda b, i: (0,) * a.ndim)
    tok = lambda w: pl.BlockSpec((1, tm, w), lambda b, i: (b, i, 0))
    wout = w_out.astype(BF16)
    wr = w_router.astype(BF16)
    wrt = w_router.T.astype(BF16)
    return pl.pallas_call(
        _out_proj_kernel,
        grid=(B, L // tm),
        in_specs=[pl.BlockSpec((1, 1, HY_WIDTH * TIME_ROWS, LANES), lambda b, i: (b, i, 0, 0)),
                  tok(ML_WIDTH), tok(ML_WIDTH), tok(D),
                  pl.BlockSpec((tm // GRID_W, D // 2), lambda b, i: (i, 0)), full(ec),
                  _mod_spec(D, 2, 0, True), _mod_spec(D, 3, 0, True), _mod_spec(D, 4, 0, True),
                  full(ml_norm), full(post1), full(pre2),
                  full(wout), full(wr), full(wrt)],
        out_specs=[tok(D), pl.BlockSpec((1, tm * (D // LANES), LANES), lambda b, i: (b, i, 0)), tok(E),
                   pl.BlockSpec((1, E, tm), lambda b, i: (b, 0, i))],
        out_shape=[jax.ShapeDtypeStruct((B, L, D), F32),
                   jax.ShapeDtypeStruct((B, L * (D // LANES), LANES), F32),
                   jax.ShapeDtypeStruct((B, L, E), F32), jax.ShapeDtypeStruct((B, E, L), F32)],
        compiler_params=_cparams(("parallel", "parallel")),
        name="out_proj_router",
    )(y_hy, hsum, o, x, er, ec, mod3, mod3, mod3, ml_norm, post1, pre2, wout, wr, wrt)


LANES = 128
ROW_GROUP = 16
SELECT_FAST_SLOTS = 32


def _store_row_tiles(ref, val):
    n, nt = val.shape[0], val.shape[1] // LANES
    for c in range(nt):
        ref[0, pl.ds(c, n, stride=nt), :] = val[:, c * LANES:(c + 1) * LANES]


def _select_kernel(afft_ref, gind_ref, sel_ref, off_ref, gmax_ref, *, cap):
    E, L = afft_ref.shape[1], afft_ref.shape[2]
    aff = afft_ref[0]
    iota = lax.broadcasted_iota(jnp.int32, (E, L), 1)
    count = lambda ind: jnp.sum(ind, axis=1, keepdims=True)
    count_ge = lambda th: count(jnp.where(aff >= th, 1.0, 0.0))
    pow2 = lambda j: pltpu.bitcast((j - 24) << 23, F32)

    def estep(_, c):
        lo, hi = c
        mid = (lo + hi) >> 1
        ok = count_ge(pow2(mid)) >= cap
        return jnp.where(ok, mid, lo), jnp.where(ok, hi, mid)

    jlo, jhi = lax.fori_loop(0, 7, estep, (jnp.full((E, 1), 24, jnp.int32),
                                           jnp.full((E, 1), 152, jnp.int32)))

    def vstep(_, c):
        lo, hi = c
        mid = lo + (hi - lo) * 0.5
        ok = count_ge(mid) >= cap
        return jnp.where(ok, mid, lo), jnp.where(ok, hi, mid)

    lo, hi = lax.fori_loop(0, 40, vstep, (pow2(jlo), pow2(jhi)))
    gt = jnp.where(aff >= hi, 1.0, 0.0)
    eq = jnp.where(aff >= lo, 1.0, 0.0) - gt
    need = cap - count(gt)

    def istep(_, c):
        lo, hi = c
        mid = (lo + hi) >> 1
        ok = count(jnp.where(iota <= mid, eq, 0.0)) >= need
        return jnp.where(ok, lo, mid), jnp.where(ok, mid, hi)

    _, last = lax.fori_loop(0, L.bit_length() - 1, istep,
                            (jnp.full((E, 1), -1, jnp.int32), jnp.full((E, 1), L - 1, jnp.int32)))
    sel = gt + jnp.where(iota <= last, eq, 0.0)
    sel_ref[0] = sel
    cnt = jnp.dot(sel.astype(BF16), gind_ref[...], preferred_element_type=F32)
    G = cnt.shape[1]
    earlier = (lax.broadcasted_iota(jnp.int32, (G, G), 0) < lax.broadcasted_iota(jnp.int32, (G, G), 1))
    off_ref[0] = jnp.dot(cnt.astype(BF16), earlier.astype(BF16), preferred_element_type=F32).astype(jnp.int32)
    gmax_ref[0] = jnp.max(cnt, axis=0, keepdims=True).astype(jnp.int32)


def _compact_kernel(off_ref, gmax_ref, sel_ref, idx_ref, rank_ref):
    E, L = sel_ref.shape[1], sel_ref.shape[2]
    T = LANES
    G = L // T
    idx_ref[...] = jnp.zeros_like(idx_ref)
    r_i = lax.broadcasted_iota(jnp.int32, (T, T), 0)
    c_i = lax.broadcasted_iota(jnp.int32, (T, T), 1)
    before = (r_i < c_i).astype(BF16)
    slot_f = r_i.astype(F32)
    lane_f = lax.broadcasted_iota(jnp.int32, (1, T), 1).astype(F32)

    for g in range(G):
        rank_ref[:, g * T:(g + 1) * T] = jnp.dot(sel_ref[0, :, g * T:(g + 1) * T].astype(BF16), before,
                                                 preferred_element_type=F32)

    def group(g, carry):
        cols = pl.ds(pl.multiple_of(g * T, T), T)
        s = sel_ref[0, :, cols]
        rank = rank_ref[:, cols]
        tok = (lane_f + jnp.asarray(g * T, F32)) * s

        def emit(n_slots):
            for e in range(E):
                hit = rank[e:e + 1, :] == slot_f[:n_slots]
                ids = jnp.sum(jnp.where(hit, tok[e:e + 1, :], 0.0), axis=1, keepdims=True)
                idx_ref[0, e, pl.ds(off_ref[0, 0, e * G + g], n_slots), :] = ids.astype(jnp.int32)

        few = gmax_ref[0, 0, g] <= SELECT_FAST_SLOTS
        pl.when(few)(lambda: emit(SELECT_FAST_SLOTS))
        pl.when(jnp.logical_not(few))(lambda: emit(T))
        return carry

    lax.fori_loop(0, G, group, 0, unroll=2)


def _route_select(afft, cap):
    B, E, L = afft.shape
    G = L // LANES
    gind = (jnp.arange(L)[:, None] // LANES == jnp.arange(G)[None, :]).astype(BF16)
    sel, off, gmax = pl.pallas_call(
        functools.partial(_select_kernel, cap=cap),
        grid=(B,),
        in_specs=[pl.BlockSpec((1, E, L), lambda b: (b, 0, 0)), pl.BlockSpec((L, G), lambda b: (0, 0))],
        out_specs=[pl.BlockSpec((1, E, L), lambda b: (b, 0, 0)), pl.BlockSpec((1, E, G), lambda b: (b, 0, 0)),
                   pl.BlockSpec((1, 1, G), lambda b: (b, 0, 0))],
        out_shape=[jax.ShapeDtypeStruct((B, E, L), F32), jax.ShapeDtypeStruct((B, E, G), jnp.int32),
                   jax.ShapeDtypeStruct((B, 1, G), jnp.int32)],
        compiler_params=_cparams(("parallel",)),
        name="route_select",
    )(afft, gind)
    smem = lambda n: pl.BlockSpec((1, 1, n), lambda b: (b, 0, 0), memory_space=pltpu.SMEM)
    idx = pl.pallas_call(
        _compact_kernel,
        grid=(B,),
        in_specs=[smem(E * G), smem(G), pl.BlockSpec((1, E, L), lambda b: (b, 0, 0))],
        out_specs=pl.BlockSpec((1, E, cap + LANES, 1), lambda b: (b, 0, 0, 0)),
        out_shape=jax.ShapeDtypeStruct((B, E, cap + LANES, 1), jnp.int32),
        scratch_shapes=[pltpu.VMEM((E, L), F32)],
        compiler_params=_cparams(("parallel",)),
        name="route_compact",
    )(off.reshape(B, 1, E * G), gmax, sel)
    return idx[:, :, :cap, 0]


def _tile_rows(i, nt):
    return pl.ds(pl.multiple_of(i * nt, nt), nt)


def _gather_kernel(idx_ref, h_ref, aff_ref, xs_ref, gs_ref, buf_ref):
    cap, D = xs_ref.shape[1], xs_ref.shape[2]
    nt = D // LANES

    def body(s, c):
        base = s * ROW_GROUP
        rows = [idx_ref[0, 0, base + k] for k in range(ROW_GROUP)]
        vals = [h_ref[0, _tile_rows(r, nt), :] for r in rows]
        gates = [aff_ref[0, pl.ds(r, 1), :] for r in rows]
        for k in range(ROW_GROUP):
            buf_ref[_tile_rows(base + k, nt), :] = vals[k]
            gs_ref[0, pl.ds(base + k, 1), :] = gates[k]
        return c

    lax.fori_loop(0, cap // ROW_GROUP, body, 0)
    for c in range(nt):
        xs_ref[0, :, c * LANES:(c + 1) * LANES] = buf_ref[pl.ds(c, cap, stride=nt), :].astype(BF16)


def _route_gather(idx, h2t, aff, D):
    B, E, cap = idx.shape
    nt = D // LANES
    L = h2t.shape[1] // nt
    return pl.pallas_call(
        _gather_kernel,
        grid=(B, E),
        in_specs=[pl.BlockSpec((1, 1, cap), lambda b, e: (b * E + e, 0, 0), memory_space=pltpu.SMEM),
                  pl.BlockSpec((1, L * nt, LANES), lambda b, e: (b, 0, 0), pipeline_mode=pl.Buffered(1)),
                  pl.BlockSpec((1, L, E), lambda b, e: (b, 0, 0))],
        out_specs=[pl.BlockSpec((1, cap, D), lambda b, e: (e, b, 0)),
                   pl.BlockSpec((1, cap, E), lambda b, e: (e, b, 0))],
        out_shape=[jax.ShapeDtypeStruct((E, B * cap, D), BF16),
                   jax.ShapeDtypeStruct((E, B * cap, E), F32)],
        scratch_shapes=[pltpu.VMEM((cap * nt, LANES), F32)],
        compiler_params=_cparams(("arbitrary", "arbitrary")),
        name="route_gather",
    )(idx.reshape(B * E, 1, cap), h2t, aff)


def _combine_kernel(idx_ref, ye_ref, x1_ref, g2_ref, post2_ref, o_ref, y_ref, buf_ref, *, n_experts):
    j = pl.program_id(1)
    cap, D = ye_ref.shape[1], ye_ref.shape[2]
    nt = D // LANES
    tm = x1_ref.shape[1]

    @pl.when(j == 0)
    def _():
        y_ref[...] = jnp.zeros_like(y_ref)

    @pl.when(j < n_experts)
    def _():
        for c in range(nt):
            buf_ref[pl.ds(c, cap, stride=nt), :] = ye_ref[0, :, c * LANES:(c + 1) * LANES]

        def body(s, c):
            base = s * ROW_GROUP
            rows = [idx_ref[0, 0, base + k] for k in range(ROW_GROUP)]
            vals = [y_ref[_tile_rows(rows[k], nt), :] + buf_ref[_tile_rows(base + k, nt), :]
                    for k in range(ROW_GROUP)]
            for k in range(ROW_GROUP):
                y_ref[_tile_rows(rows[k], nt), :] = vals[k]
            return c

        lax.fori_loop(0, cap // ROW_GROUP, body, 0)

    @pl.when(j >= n_experts)
    def _():
        base = pl.multiple_of((j - n_experts) * (tm * nt), tm * nt)
        cols = [y_ref[pl.ds(base + c, tm, stride=nt), :] for c in range(nt)]
        ssq = sum(jnp.sum(y * y, axis=-1, keepdims=True) for y in cols)
        rstd = lax.rsqrt(ssq * (1.0 / D) + EPS)
        for c in range(nt):
            cs = slice(c * LANES, (c + 1) * LANES)
            o_ref[0, :, cs] = x1_ref[0, :, cs] + g2_ref[0, :, cs] * (cols[c] * rstd * post2_ref[:, cs])


def _route_combine(idx, ye, x1, mod3, post2, tm):
    B, E, cap = idx.shape
    _, L, D = x1.shape
    nt = D // LANES
    assert ye.shape[0] == E + 1
    tile = lambda j: jnp.maximum(j - E, 0)
    expert = lambda j: jnp.minimum(j, E - 1)
    return pl.pallas_call(
        functools.partial(_combine_kernel, n_experts=E),
        grid=(B, E + L // tm),
        in_specs=[pl.BlockSpec((1, 1, cap), lambda b, j: (b * E + expert(j), 0, 0), memory_space=pltpu.SMEM),
                  pl.BlockSpec((1, cap, D), lambda b, j: (expert(j) + 1, b, 0)),
                  pl.BlockSpec((1, tm, D), lambda b, j: (b, tile(j), 0)),
                  _mod_spec(D, 5, 0, True),
                  pl.BlockSpec((1, D), lambda b, j: (0, 0))],
        out_specs=pl.BlockSpec((1, tm, D), lambda b, j: (b, tile(j), 0)),
        out_shape=jax.ShapeDtypeStruct((B, L, D), F32),
        scratch_shapes=[pltpu.VMEM((L * nt, LANES), F32), pltpu.VMEM((cap * nt, LANES), F32)],
        compiler_params=_cparams(("arbitrary", "arbitrary")),
        name="route_combine_final",
    )(idx.reshape(B * E, 1, cap), ye, x1, mod3, post2)


def _expert_kernel(xs_ref, g_ref, wg_ref, wu_ref, wd_ref, ye_ref, wgb_ref, wub_ref, wdb_ref):
    s = pl.program_id(0)
    f = pl.program_id(1)
    ne = pl.num_programs(0) - 1
    tf = wg_ref.shape[2]
    slot = lax.rem(s, 2)

    @pl.when(s < ne)
    def _():
        cols = pl.ds(pl.multiple_of(f * tf, tf), tf)
        wgb_ref[slot, :, cols] = wg_ref[0].astype(BF16)
        wub_ref[slot, :, cols] = wu_ref[0].astype(BF16)
        wdb_ref[slot, cols, :] = wd_ref[0].astype(BF16)

    @pl.when(s == 0)
    def _():
        ye_ref[...] = jnp.zeros_like(ye_ref)

    @pl.when(s > 0)
    def _():
        e = s - 1
        prev = 1 - slot
        mt = xs_ref.shape[1] // FFN_M_SPLIT
        for mi in range(FFN_M_SPLIT):
            rs = slice(mi * mt, (mi + 1) * mt)
            xs = xs_ref[0, rs, :]
            a = jnp.dot(xs, wgb_ref[prev], preferred_element_type=F32)
            u = jnp.dot(xs, wub_ref[prev], preferred_element_type=F32)
            hmid = (a * jax.nn.sigmoid(a) * u).astype(BF16)
            out = jnp.dot(hmid, wdb_ref[prev], preferred_element_type=F32)
            gs = g_ref[0, rs, :]
            esel = lax.broadcasted_iota(jnp.int32, gs.shape, 1)
            ye_ref[0, rs, :] = out * jnp.sum(jnp.where(esel == e, gs, 0.0), axis=1, keepdims=True)


FFN_M_SPLIT = 2


def _expert_ffn(xs, gs, w_gate, w_up, w_down, tf):
    E, M, D = xs.shape
    F = w_gate.shape[2]
    nf = F // tf
    mt = M // nf
    behind = lambda s: jnp.maximum(s - 1, 0)
    ahead = lambda s: jnp.minimum(s, E - 1)
    return pl.pallas_call(
        _expert_kernel,
        grid=(E + 1, nf),
        in_specs=[pl.BlockSpec((1, mt, D), lambda s, f: (behind(s), f, 0)),
                  pl.BlockSpec((1, mt, E), lambda s, f: (behind(s), f, 0)),
                  pl.BlockSpec((1, D, tf), lambda s, f: (ahead(s), 0, f)),
                  pl.BlockSpec((1, D, tf), lambda s, f: (ahead(s), 0, f)),
                  pl.BlockSpec((1, tf, D), lambda s, f: (ahead(s), f, 0))],
        out_specs=pl.BlockSpec((1, mt, D), lambda s, f: (s, f, 0)),
        out_shape=jax.ShapeDtypeStruct((E + 1, M, D), F32),
        scratch_shapes=[pltpu.VMEM((2, D, F), BF16), pltpu.VMEM((2, D, F), BF16), pltpu.VMEM((2, F, D), BF16)],
        compiler_params=_cparams(("arbitrary", "arbitrary")),
        name="expert_ffn",
    )(xs, gs, w_gate, w_up, w_down)


def kernel(x, c, ctx, c_ctx, w_mod, b_mod, pre_norm1, post_norm1, pre_norm2, post_norm2, w_in, conv_w, conv_b, filt_w1, filt_b1, filt_w2, filt_b2, filt_w3, filt_b3, filt_w4, filt_freq, hyena_bias, ml_gate_b, ml_norm, w_out, w_router, w_exp_gate, w_exp_up, w_exp_down):
    B, L, D = x.shape
    depth = w_mod.shape[0]
    assert depth == 1, "single-layer block"
    li = 0
    tabs = _pe_tables(L // GRID_W, GRID_W, D)

    cc = jnp.concatenate([c, c_ctx[None], jnp.zeros((8 - B - 1, D), F32)], axis=0)
    mod3 = _modulation(cc, w_mod[li], b_mod[li])[:, None, :]

    pre1 = pre_norm1[li][None, :]
    w_proj = _in_proj_weights(w_in[li])
    zhy, q, kt, v, o, gatet = _in_proj(x, tabs, mod3, 0, pre1, w_proj, IN_PROJ_TILE, True)
    qc, ktc, vc, _, gatetc = _in_proj(ctx, None, mod3, B, pre1, w_proj, ctx.shape[1], False)

    kcirc = _hyena_filter(L, filt_w1[li], filt_b1[li], filt_w2[li], filt_b2[li], filt_w3[li],
                          filt_b3[li], filt_w4[li], filt_freq[li])
    y_hy = _hyena_conv(zhy, kcirc, conv_w[li], conv_b[li], hyena_bias[li])
    hsum = _mlstm(q, kt, v, gatet, qc, ktc, vc, gatetc, ml_gate_b[li])

    x1, h2t, aff, afft = _out_proj(y_hy, hsum, o, x, tabs, mod3, ml_norm[li][None, :],
                                   post_norm1[li][None, :], pre_norm2[li][None, :], w_out[li],
                                   w_router[li], IN_PROJ_TILE)

    cap = CAP_FACTOR * L // N_EXPERTS
    idx = _route_select(afft, cap)
    xs, gs = _route_gather(idx, h2t, aff, D)
    ye = _expert_ffn(xs, gs, w_exp_gate[li], w_exp_up[li], w_exp_down[li], 512)
    return _route_combine(idx, ye, x1, mod3, post_norm2[li][None, :], TOKEN_TILE)
```

```python
import functools
import math

import jax
import jax.numpy as jnp
from jax import lax
from jax.experimental import pallas as pl
from jax.experimental.pallas import tpu as pltpu

GRID_W = 64
HY_WIDTH = 512
ML_HEADS = 4
ML_DK = 64
ML_DV = 128
ML_WIDTH = ML_HEADS * ML_DV
HY_COLS = 3 * HY_WIDTH
QK_COLS = ML_HEADS * ML_DK
N_GATES = 4 * ML_HEADS
FILTER_EMB = 33
DECAY_TARGET = 1e-2
FAST_DECAY_PCT = 0.3
SLOW_DECAY_PCT = 1.5
CHUNK = 128
GATE_CAP = 15.0
N_EXPERTS = 16
CAP_FACTOR = 2
EPS = 1e-6

F32 = jnp.float32
BF16 = jnp.bfloat16

TOKEN_TILE = 512
TIME_ROWS = 8
IN_PROJ_TILE = 1024
VMEM_LIMIT = 56 * 1024 * 1024


def _cparams(sem):
    return pltpu.CompilerParams(dimension_semantics=sem, vmem_limit_bytes=VMEM_LIMIT)


def _rms(xf, g):
    return xf * lax.rsqrt(jnp.mean(xf * xf, axis=-1, keepdims=True) + EPS) * g


def _bdot(a, b):
    return jnp.dot(a.astype(BF16), b.astype(BF16), preferred_element_type=F32)


def _pe_tables_kernel(omega_ref, er_ref, ec_ref):
    quarter = omega_ref.shape[1]
    om = omega_ref[...]
    for ref in (er_ref, ec_ref):
        n = ref.shape[0]
        pos = lax.broadcasted_iota(jnp.int32, (n, quarter), 0).astype(F32)
        ang = pos * om
        ref[:, :quarter] = jnp.sin(ang)
        ref[:, quarter:] = jnp.cos(ang)


def _pe_tables(rows, cols, dim):
    quarter = dim // 4
    omega = (1.0 / (10000.0 ** (jnp.arange(quarter, dtype=F32) / quarter)))[None, :]
    return pl.pallas_call(
        _pe_tables_kernel,
        out_shape=(jax.ShapeDtypeStruct((rows, dim // 2), F32),
                   jax.ShapeDtypeStruct((cols, dim // 2), F32)),
        name="pe_tables",
    )(omega)


def _pe_tile(er_blk, ec):
    nr, half = er_blk.shape
    row_part = jnp.broadcast_to(er_blk[:, None, :], (nr, GRID_W, half)).reshape(nr * GRID_W, half)
    col_part = jnp.broadcast_to(ec[None, :, :], (nr, GRID_W, half)).reshape(nr * GRID_W, half)
    return jnp.concatenate([row_part, col_part], axis=-1)


def _mod_kernel(c_ref, w_ref, b_ref, o_ref):
    c = c_ref[...]
    s = c * jax.nn.sigmoid(c)
    o_ref[...] = _bdot(s, w_ref[...]) + b_ref[...]


def _modulation(cc, w_mod, b_mod):
    rows, d = cc.shape
    n = w_mod.shape[1]
    tn = 512
    return pl.pallas_call(
        _mod_kernel,
        grid=(n // tn,),
        in_specs=[pl.BlockSpec((rows, d), lambda j: (0, 0)),
                  pl.BlockSpec((d, tn), lambda j: (0, j)),
                  pl.BlockSpec((1, tn), lambda j: (0, j))],
        out_specs=pl.BlockSpec((rows, tn), lambda j: (0, j)),
        out_shape=jax.ShapeDtypeStruct((rows, n), F32),
        compiler_params=_cparams(("arbitrary",)),
        name="modulation",
    )(cc, w_mod, b_mod[None, :])


def _in_proj_kernel(*refs, with_hyena):
    if with_hyena:
        (x_ref, er_ref, ec_ref, sh_ref, sc_ref, g_ref, wn_ref, wt_ref,
         zt_ref, q_ref, kt_ref, v_ref, o_ref, gatet_ref) = refs
        xf = x_ref[0] + _pe_tile(er_ref[...], ec_ref[...])
    else:
        (x_ref, sh_ref, sc_ref, g_ref, wn_ref, wt_ref,
         q_ref, kt_ref, v_ref, o_ref, gatet_ref) = refs
        xf = x_ref[0]
    h = _rms(xf, g_ref[...]) * (1.0 + sc_ref[0]) + sh_ref[0]
    hb = h.astype(BF16)
    z = jnp.dot(hb, wn_ref[...], preferred_element_type=F32)
    qs = z[:, :QK_COLS] * (ML_DK ** -0.5)
    for hd in range(ML_HEADS):
        q_ref[0, hd] = qs[:, hd * ML_DK:(hd + 1) * ML_DK].astype(BF16)
    v_ref[0] = z[:, QK_COLS:QK_COLS + ML_WIDTH].astype(BF16)
    o_ref[0] = z[:, QK_COLS + ML_WIDTH:]
    zt = lax.dot_general(wt_ref[...], hb, (((1,), (1,)), ((), ())), preferred_element_type=F32)
    off = 0
    if with_hyena:
        for j in range(zt.shape[1] // LANES):
            zt_ref[0, 0, pl.ds(j, HY_COLS, stride=TIME_ROWS), :] = zt[:HY_COLS, j * LANES:(j + 1) * LANES]
        off = HY_COLS
    for hd in range(ML_HEADS):
        kt_ref[0, hd] = zt[off + hd * ML_DK:off + (hd + 1) * ML_DK, :].astype(BF16)
    gatet_ref[0] = zt[off + QK_COLS:, :]


def _mod_spec(D, chunk, row0, per_sample):
    return pl.BlockSpec((1, 1, D), lambda b, i: ((b if per_sample else 0) + row0, 0, chunk))


def _in_proj_weights(w_in):
    w_q = w_in[:, HY_COLS:HY_COLS + QK_COLS]
    w_k = w_in[:, HY_COLS + QK_COLS:HY_COLS + 2 * QK_COLS]
    w_vo = w_in[:, HY_COLS + 2 * QK_COLS:HY_COLS + 2 * QK_COLS + 2 * ML_WIDTH]
    w_g = w_in[:, HY_COLS + 2 * QK_COLS + 2 * ML_WIDTH:]
    wn = jnp.concatenate([w_q, w_vo], axis=1).astype(BF16)
    wt = jnp.concatenate([w_in[:, :HY_COLS], w_k, w_g], axis=1).T.astype(BF16)
    return wn, wt


def _in_proj(x, tabs, mod3, mod_row0, g, weights, tm, with_hyena):
    B, L, D = x.shape
    wn, wt = weights
    if not with_hyena:
        wt = wt[HY_COLS:]
    full = lambda a: pl.BlockSpec(a.shape, lambda b, i: (0,) * a.ndim)
    in_specs = [pl.BlockSpec((1, tm, D), lambda b, i: (b, i, 0))]
    args = [x]
    if with_hyena:
        er, ec = tabs
        in_specs += [pl.BlockSpec((tm // GRID_W, D // 2), lambda b, i: (i, 0)), full(ec)]
        args += [er, ec]
    in_specs += [_mod_spec(D, 0, mod_row0, with_hyena), _mod_spec(D, 1, mod_row0, with_hyena),
                 full(g), full(wn), full(wt)]
    args += [mod3, mod3, g, wn, wt]
    out_shape, out_specs = [], []
    if with_hyena:
        assert tm == TIME_ROWS * LANES
        out_shape.append(jax.ShapeDtypeStruct((B, L // tm, HY_COLS * TIME_ROWS, LANES), F32))
        out_specs.append(pl.BlockSpec((1, 1, HY_COLS * TIME_ROWS, LANES), lambda b, i: (b, i, 0, 0)))
    out_shape += [jax.ShapeDtypeStruct((B, ML_HEADS, L, ML_DK), BF16),
                  jax.ShapeDtypeStruct((B, ML_HEADS, ML_DK, L), BF16),
                  jax.ShapeDtypeStruct((B, L, ML_WIDTH), BF16),
                  jax.ShapeDtypeStruct((B, L, ML_WIDTH), F32),
                  jax.ShapeDtypeStruct((B, N_GATES, L), F32)]
    out_specs += [pl.BlockSpec((1, ML_HEADS, tm, ML_DK), lambda b, i: (b, 0, i, 0)),
                  pl.BlockSpec((1, ML_HEADS, ML_DK, tm), lambda b, i: (b, 0, 0, i)),
                  pl.BlockSpec((1, tm, ML_WIDTH), lambda b, i: (b, i, 0)),
                  pl.BlockSpec((1, tm, ML_WIDTH), lambda b, i: (b, i, 0)),
                  pl.BlockSpec((1, N_GATES, tm), lambda b, i: (b, 0, i))]
    return pl.pallas_call(
        functools.partial(_in_proj_kernel, with_hyena=with_hyena),
        grid=(B, L // tm),
        in_specs=in_specs,
        out_specs=out_specs,
        out_shape=out_shape,
        compiler_params=_cparams(("parallel", "parallel")),
        name="in_proj_hy" if with_hyena else "in_proj_ctx",
    )(*args)


FILT_TILE = 1024
FILT_CBLK = 128


def _filter_kernel(w1a_ref, w1b_ref, w1c_ref, b1_ref, w2_ref, b2_ref, w3_ref, b3_ref, fr_ref,
                   w4f_ref, w4b_ref, dl_ref, mir_ref, k_ref, hf_ref, *, L):
    bands = (FILTER_EMB - 1) // 2

    @pl.when(pl.program_id(0) == 0)
    def _():
        fk = (1e-4 + lax.broadcasted_iota(jnp.int32, (bands, 1), 0).astype(F32)
              * ((bands - 1 - 1e-4) / (bands - 1)))
        fr = fr_ref[...]
        for j in range(L // FILT_TILE):
            pos = (lax.broadcasted_iota(jnp.int32, (1, FILT_TILE), 1) + j * FILT_TILE).astype(F32)
            tl = pos * (1.0 / (L - 1))
            ang = fk * (pos * (2.0 * math.pi / L))
            pre = (w1a_ref[...].astype(F32) * tl.astype(BF16).astype(F32)
                   + _bdot(w1b_ref[...], jnp.cos(ang)) + _bdot(w1c_ref[...], -jnp.sin(ang)))
            h = jnp.sin(fr * (pre + b1_ref[...]))
            h = jnp.sin(fr * (_bdot(w2_ref[...], h) + b2_ref[...]))
            h = jnp.sin(fr * (_bdot(w3_ref[...], h) + b3_ref[...]))
            hf_ref[:, j * FILT_TILE:(j + 1) * FILT_TILE] = h.astype(BF16)

    pos = lax.broadcasted_iota(jnp.int32, (1, L), 1).astype(F32)
    decay = jnp.exp(-(pos * (1.0 / (L - 1))) * dl_ref[...])
    hf = hf_ref[...]
    k_ref[:, :L] = jnp.dot(w4f_ref[...], hf, preferred_element_type=F32) * decay
    gb = (jnp.dot(w4b_ref[...], hf, preferred_element_type=F32) * decay).astype(BF16)
    nblk = L // LANES
    for j in range(nblk):
        src = gb[:, (nblk - 1 - j) * LANES:(nblk - j) * LANES]
        nxt = gb[:, (nblk - j) * LANES:(nblk - j + 1) * LANES] if j > 0 else jnp.zeros_like(src)
        k_ref[:, L + j * LANES:L + (j + 1) * LANES] = jnp.dot(
            jnp.concatenate([src, nxt], axis=1), mir_ref[...], preferred_element_type=F32)


def _hyena_filter(L, w1, b1, w2, b2, w3, b3, w4, freq):
    hid = w2.shape[0]
    bands = (FILTER_EMB - 1) // 2
    col = lambda a: a[:, None]
    w1t = w1.T.astype(BF16)
    min_decay = math.log(DECAY_TARGET) / SLOW_DECAY_PCT
    max_decay = math.log(DECAY_TARGET) / FAST_DECAY_PCT
    dl = jnp.abs(jnp.linspace(min_decay, max_decay, HY_WIDTH, dtype=F32))[:, None]
    w4t = w4.T.astype(BF16)
    assert DFT_PASSES == 1, "the mirrored taps are kept at the bf16 precision a single-pass DFT reads"
    lane = jnp.arange(LANES)
    mir = jnp.concatenate([(lane[:, None] + lane[None, :] == LANES),
                           (lane[:, None] == 0) & (lane[None, :] == 0)], axis=0).astype(BF16)
    full = lambda a: pl.BlockSpec(a.shape, lambda i: (0,) * a.ndim)
    args = [w1t[:, 0:1], w1t[:, 1:1 + bands], w1t[:, 1 + bands:], col(b1), w2.T.astype(BF16), col(b2),
            w3.T.astype(BF16), col(b3), col(freq)]
    return pl.pallas_call(
        functools.partial(_filter_kernel, L=L),
        grid=(HY_WIDTH // FILT_CBLK,),
        in_specs=[full(a) for a in args] + [
            pl.BlockSpec((FILT_CBLK, hid), lambda i: (i, 0)),
            pl.BlockSpec((FILT_CBLK, hid), lambda i: (HY_WIDTH // FILT_CBLK + i, 0)),
            pl.BlockSpec((FILT_CBLK, 1), lambda i: (i, 0)), full(mir)],
        out_specs=pl.BlockSpec((FILT_CBLK, 2 * L), lambda i: (i, 0)),
        out_shape=jax.ShapeDtypeStruct((HY_WIDTH, 2 * L), F32),
        scratch_shapes=[pltpu.VMEM((hid, L), BF16)],
        compiler_params=_cparams(("arbitrary",)),
        name="hyena_filter",
    )(*args, w4t, w4t, dl, mir)


FFT_N = 128
HY_CBLK = 32
HY_GROUP = 8
HY_UNROLL = 8
DFT_PASSES = 1


def _dft_constants(n1_data):
    import numpy as np
    n = FFT_N
    k = np.arange(n)
    ang = -2.0 * np.pi * ((k[:, None] * k[None, :]) % n) / n
    fre, fim = np.cos(ang), np.sin(ang)
    m = n1_data
    fa_d = np.block([[fre[:, :m], -fim[:, :m]], [fim[:, :m], fre[:, :m]]])
    fa_f = np.concatenate([fre, fim], axis=0)
    fb = np.block([[fre, fim], [-fim, fre]])
    fbi = np.block([[fre, -fim], [fim, fre]])
    fc = np.block([[fre[:m, :], fim[:m, :]], [-fim[:m, :], fre[:m, :]]]) / (n * n)
    tang = -2.0 * np.pi * (k[:, None] * k[None, :]) / (n * n)
    tw = np.stack([np.cos(tang), np.sin(tang)])

    def hilo(a):
        a32 = jnp.asarray(a, F32)
        hi = a32.astype(BF16)
        lo = (a32 - hi.astype(F32)).astype(BF16)
        return jnp.stack([hi, lo])

    return hilo(fa_d), hilo(fa_f), hilo(fb), hilo(fbi), hilo(fc), jnp.asarray(tw, F32)


def _mm_const_lhs(c_ref, d):
    dh = d.astype(BF16)
    acc = jnp.dot(c_ref[0], dh, preferred_element_type=F32)
    if DFT_PASSES == 3:
        dl = (d - dh.astype(F32)).astype(BF16)
        acc = acc + (jnp.dot(c_ref[0], dl, preferred_element_type=F32)
                     + jnp.dot(c_ref[1], dh, preferred_element_type=F32))
    return acc


def _mm_const_rhs(d, c_ref):
    dh = d.astype(BF16)
    acc = jnp.dot(dh, c_ref[0], preferred_element_type=F32)
    if DFT_PASSES == 3:
        dl = (d - dh.astype(F32)).astype(BF16)
        acc = acc + (jnp.dot(dl, c_ref[0], preferred_element_type=F32)
                     + jnp.dot(dh, c_ref[1], preferred_element_type=F32))
    return acc


def _cmul(are, aim, bre, bim):
    return are * bre - aim * bim, are * bim + aim * bre


def _hyena_conv_kernel(x0_ref, x1_ref, v_ref, kc_ref, w0_ref, w1_ref, wv_ref, b0_ref, b1_ref, bv_ref,
                       hb_ref, fad_ref, faf_ref, fb_ref, fbi_ref, fc_ref, tw_ref,
                       o_ref, u_ref, s_ref, ks_ref):
    n = FFT_N
    cb = kc_ref.shape[0]
    nb, nt = x0_ref.shape[0], x0_ref.shape[1]
    m = nt * TIME_ROWS

    def chan(ref, b, c):
        return ref[b, :, pl.ds(pl.multiple_of(c * TIME_ROWS, TIME_ROWS), TIME_ROWS), :].reshape(m, n)

    sub = lax.broadcasted_iota(jnp.int32, (m, n), 0)
    lane = lax.broadcasted_iota(jnp.int32, (m, n), 1)
    tre, tim = tw_ref[0], tw_ref[1]

    def sconv(z, w_ref, b_ref, c):
        a = pltpu.roll(z, 1, axis=1)
        prev = jnp.where(lane == 0, jnp.where(sub == 0, 0.0, pltpu.roll(a, 1, axis=0)), a)
        a2 = pltpu.roll(z, n - 1, axis=1)
        nxt = jnp.where(lane == n - 1, jnp.where(sub == m - 1, 0.0, pltpu.roll(a2, m - 1, axis=0)), a2)
        return prev * w_ref[0, c] + z * w_ref[1, c] + nxt * w_ref[2, c] + b_ref[c]

    def spectrum_rows(res):
        outs = []
        for h in range(2):
            are, aim = _cmul(res[:n, h * n:(h + 1) * n], res[n:, h * n:(h + 1) * n], tre, tim)
            outs.append(jnp.concatenate([are, aim], axis=1))
        return outs

    def fwd_pair(p, carry):
        c0 = 2 * p
        us = []
        for c in (c0, c0 + 1):
            ub = []
            for b in range(nb):
                x1c = sconv(chan(x1_ref, b, c), w1_ref, b1_ref, c)
                vc = sconv(chan(v_ref, b, c), wv_ref, bv_ref, c)
                u = x1c * vc
                u_ref[b, c] = u
                ub.append(u)
            us.append(ub)
        wd = jnp.concatenate([jnp.concatenate([us[0][b], us[1][b]], axis=1) for b in range(nb)], axis=0)
        sa, sb = spectrum_rows(_mm_const_lhs(fad_ref, wd))
        s_ref[c0] = sa.astype(s_ref.dtype)
        s_ref[c0 + 1] = sb.astype(s_ref.dtype)
        wk = jnp.concatenate([kc_ref[c0], kc_ref[c0 + 1]], axis=1)
        ka, kb = spectrum_rows(_mm_const_lhs(faf_ref, wk))
        ks_ref[c0] = ka.astype(ks_ref.dtype)
        ks_ref[c0 + 1] = kb.astype(ks_ref.dtype)
        return carry

    lax.fori_loop(0, cb // 2, fwd_pair, 0, unroll=HY_UNROLL)

    def mid_group(g, carry):
        gs = pl.ds(pl.multiple_of(g * HY_GROUP, HY_GROUP), HY_GROUP)
        x = _mm_const_rhs(s_ref[gs].reshape(HY_GROUP * n, 2 * n), fb_ref)
        k = _mm_const_rhs(ks_ref[gs].reshape(HY_GROUP * n, 2 * n), fb_ref)
        yre, yim = _cmul(x[:, :n], x[:, n:], k[:, :n], k[:, n:])
        vv = _mm_const_rhs(jnp.concatenate([yre, yim], axis=1), fbi_ref).reshape(HY_GROUP, n, 2 * n)
        vre, vim = _cmul(vv[:, :, :n], vv[:, :, n:], tre[None], -tim[None])
        s_ref[gs] = jnp.concatenate([vre, vim], axis=2).astype(s_ref.dtype)
        return carry

    lax.fori_loop(0, cb // HY_GROUP, mid_group, 0, unroll=HY_UNROLL)

    def inv_pair(p, carry):
        c0 = 2 * p
        sa, sb = s_ref[c0], s_ref[c0 + 1]
        wd = jnp.concatenate([jnp.concatenate([sa[:, :n], sb[:, :n]], axis=1),
                              jnp.concatenate([sa[:, n:], sb[:, n:]], axis=1)], axis=0)
        res = _mm_const_lhs(fc_ref, wd)
        for h, c in enumerate((c0, c0 + 1)):
            for b in range(nb):
                y = res[b * m:(b + 1) * m, h * n:(h + 1) * n]
                x0c = sconv(chan(x0_ref, b, c), w0_ref, b0_ref, c)
                out = x0c * (y + hb_ref[c] * u_ref[b, c])
                o_ref[b, :, pl.ds(pl.multiple_of(c * TIME_ROWS, TIME_ROWS), TIME_ROWS), :] = out.reshape(
                    nt, TIME_ROWS, n)
        return carry

    lax.fori_loop(0, cb // 2, inv_pair, 0, unroll=HY_UNROLL)


def _hyena_conv(zt, kcirc, conv_w, conv_b, hy_bias):
    B, nt, _, n = zt.shape
    m = nt * TIME_ROWS
    L = m * n
    C = HY_WIDTH
    assert B == 2 and n == FFT_N and 2 * L == n * n, "complex packing of two samples over a 128 x 128 point transform"
    z4 = zt
    rows = HY_CBLK * TIME_ROWS
    spec_dt = BF16 if DFT_PASSES == 1 else F32
    k3 = kcirc.reshape(C, n, n)
    cw = conv_w.reshape(3, 3 * C, 1, 1)
    cbias = conv_b.reshape(3 * C, 1, 1)
    hb = hy_bias.reshape(C, 1, 1)
    consts = _dft_constants(m)
    nblk = C // HY_CBLK
    zspec = lambda part: pl.BlockSpec((B, nt, rows, n), lambda i: (0, 0, part * nblk + i, 0))
    wspec = lambda part: pl.BlockSpec((3, HY_CBLK, 1, 1), lambda i: (0, part * nblk + i, 0, 0))
    bspec = lambda part: pl.BlockSpec((HY_CBLK, 1, 1), lambda i: (part * nblk + i, 0, 0))
    full = lambda a: pl.BlockSpec(a.shape, lambda i: (0,) * a.ndim)
    y = pl.pallas_call(
        _hyena_conv_kernel,
        grid=(nblk,),
        in_specs=[zspec(0), zspec(1), zspec(2), pl.BlockSpec((HY_CBLK, n, n), lambda i: (i, 0, 0)),
                  wspec(0), wspec(1), wspec(2), bspec(0), bspec(1), bspec(2), bspec(0)]
                 + [full(a) for a in consts],
        out_specs=pl.BlockSpec((B, nt, rows, n), lambda i: (0, 0, i, 0)),
        out_shape=jax.ShapeDtypeStruct((B, nt, C * TIME_ROWS, n), F32),
        scratch_shapes=[pltpu.VMEM((B, HY_CBLK, m, n), F32), pltpu.VMEM((HY_CBLK, n, 2 * n), spec_dt),
                        pltpu.VMEM((HY_CBLK, n, 2 * n), spec_dt)],
        compiler_params=_cparams(("parallel",)),
        name="hyena_conv",
    )(z4, z4, z4, k3, cw, cw, cw, cbias, cbias, cbias, hb, *consts)
    return y


def _split3(a):
    hi = a.astype(BF16)
    r1 = a - hi.astype(F32)
    mid = r1.astype(BF16)
    lo = (r1 - mid.astype(F32)).astype(BF16)
    return hi, mid, lo


def _exact_dot_right(a, tri_bf):
    hi, mid, lo = _split3(a)
    d = lambda p: jnp.dot(p, tri_bf, preferred_element_type=F32)
    return (d(lo) + d(mid)) + d(hi)


def _soft_gates(g):
    g = GATE_CAP * jnp.tanh(g * (1.0 / GATE_CAP))
    logsig = jnp.minimum(g, 0.0) - jnp.log1p(jnp.exp(-jnp.abs(g)))
    return g, logsig


def _gate_prep_kernel(gt_ref, gbt_ref, rows_ref, cols_ref):
    T = CHUNK
    n = gt_ref.shape[2]
    up_bf = (lax.broadcasted_iota(jnp.int32, (T, T), 1) >= lax.broadcasted_iota(jnp.int32, (T, T), 0)).astype(BF16)
    out8 = lax.broadcasted_iota(jnp.int32, (8, 1), 0)
    cap, ls = _soft_gates(gt_ref[0] + gbt_ref[...])
    H = ML_HEADS
    head = lambda hd: jnp.where(out8 == 0, cap[hd:hd + 1], jnp.where(out8 == 1, ls[H + hd:H + hd + 1],
                                jnp.where(out8 == 2, cap[2 * H + hd:2 * H + hd + 1],
                                          jnp.where(out8 == 3, ls[3 * H + hd:3 * H + hd + 1], 0.0))))
    base = jnp.concatenate([head(hd) for hd in range(H)], axis=0)
    kind = jnp.concatenate([out8] * H, axis=0)
    for j in range(n // T):
        blk = base[:, j * T:(j + 1) * T]
        run = _exact_dot_right(blk, up_bf)
        suf = run[:, T - 1:T] - run + blk
        rows = jnp.where(kind == 1, run, jnp.where(kind == 3, suf, blk))
        cols = rows.T
        for hd in range(H):
            rows_ref[0, hd, :, j * T:(j + 1) * T] = rows[8 * hd:8 * (hd + 1)]
            cols_ref[0, hd, j * T:(j + 1) * T, :] = cols[:, 8 * hd:8 * (hd + 1)]


def _gate_prep(gatet, gate_b, tile):
    B, G, L = gatet.shape
    H = ML_HEADS
    return pl.pallas_call(
        _gate_prep_kernel,
        grid=(B, L // tile),
        in_specs=[pl.BlockSpec((1, G, tile), lambda b, i: (b, 0, i)),
                  pl.BlockSpec((G, 1), lambda b, i: (0, 0))],
        out_specs=[pl.BlockSpec((1, H, 8, tile), lambda b, i: (b, 0, 0, i)),
                   pl.BlockSpec((1, H, tile, 8), lambda b, i: (b, 0, i, 0))],
        out_shape=[jax.ShapeDtypeStruct((B, H, 8, L), F32), jax.ShapeDtypeStruct((B, H, L, 8), F32)],
        compiler_params=_cparams(("parallel", "parallel")),
        name="mlstm_gate_prep",
    )(gatet, gate_b[:, None])


def _mlstm_kernel(q_ref, kt_ref, v_ref, r_ref, bt_ref, qc_ref, ktc_ref, vc_ref, rc_ref, btc_ref,
                  h_ref, cf_ref, cb_ref):
    T = CHUNK
    L = q_ref.shape[2]
    Lc = qc_ref.shape[2]
    nc, ncc = L // T, Lc // T
    row = lax.broadcasted_iota(jnp.int32, (T, T), 0)
    col = lax.broadcasted_iota(jnp.int32, (T, T), 1)
    lo_mask = col <= row
    up_mask = col >= row
    ones_blk = jnp.ones((T, ML_DV), BF16)

    cf_ref[...] = jnp.zeros_like(cf_ref)
    cb_ref[...] = jnp.zeros_like(cb_ref)

    def chunk_step(q, kt, v, rows, cols, c_ref, backward):
        i_r = rows[2:3] if backward else rows[0:1]
        b_r = rows[3:4] if backward else rows[1:2]
        b_c = jnp.broadcast_to(cols[:, 3:4] if backward else cols[:, 1:2], (T, T))
        b_end = b_r[:, 0:1] if backward else b_r[:, T - 1:T]
        mask = up_mask if backward else lo_mask
        w_intra = jnp.exp(jnp.where(mask, b_c - b_r + i_r, -jnp.inf) - GATE_CAP)
        s = jnp.dot(q, kt, preferred_element_type=F32) * w_intra
        qe = (q.astype(F32) * jnp.exp(b_c[:, :ML_DK])).astype(BF16)
        v_aug = jnp.concatenate([v, ones_blk], axis=1)
        c_aug = c_ref[...]
        res = jnp.dot(jnp.concatenate([s.astype(BF16), qe], axis=1),
                      jnp.concatenate([v_aug, c_aug.astype(BF16)], axis=0),
                      preferred_element_type=F32)
        h = res[:, :ML_DV] / jnp.maximum(jnp.abs(res[:, ML_DV:]), math.exp(-GATE_CAP))
        kw = (kt.astype(F32) * jnp.exp(b_end - b_r + i_r - GATE_CAP)).astype(BF16)
        c_ref[...] = jnp.exp(b_end) * c_aug + jnp.dot(kw, v_aug, preferred_element_type=F32)
        return h

    for j in range(ncc):
        for backward in (False, True):
            jj = (ncc - 1 - j) if backward else j
            cs = slice(jj * T, (jj + 1) * T)
            chunk_step(qc_ref[0, 0, cs, :], ktc_ref[0, 0, :, cs], vc_ref[0, cs, :], rc_ref[0, 0, :, cs],
                       btc_ref[0, 0, cs, :], cb_ref if backward else cf_ref, backward)

    def latent_pair(j, accumulate):
        for backward in (False, True):
            jj = (nc - 1 - j) if backward else j
            rs = pl.ds(pl.multiple_of(jj * T, T), T)
            h = chunk_step(q_ref[0, 0, rs, :], kt_ref[0, 0, :, rs], v_ref[0, rs, :], r_ref[0, 0, :, rs],
                           bt_ref[0, 0, rs, :], cb_ref if backward else cf_ref, backward)
            if accumulate:
                h_ref[0, rs, :] = h_ref[0, rs, :] + h
            else:
                h_ref[0, rs, :] = h

    def first_half(j, carry):
        latent_pair(j, False)
        return carry

    def second_half(j, carry):
        latent_pair(j, True)
        return carry

    lax.fori_loop(0, nc // 2, first_half, 0, unroll=MLSTM_UNROLL)
    lax.fori_loop(nc // 2, nc, second_half, 0, unroll=MLSTM_UNROLL)


MLSTM_UNROLL = 8


def _mlstm(q, kt, v, gatet, qc, ktc, vc, gatetc, gate_b):
    B, H, L, dk = q.shape
    Lc = qc.shape[2]
    rows, cols = _gate_prep(gatet, gate_b, min(L, 1024))
    rows_c, cols_c = _gate_prep(gatetc, gate_b, Lc)
    seq = lambda n: [pl.BlockSpec((1, 1, n, dk), lambda b, h: (b, h, 0, 0)),
                     pl.BlockSpec((1, 1, dk, n), lambda b, h: (b, h, 0, 0)),
                     pl.BlockSpec((1, n, ML_DV), lambda b, h: (b, 0, h)),
                     pl.BlockSpec((1, 1, 8, n), lambda b, h: (b, h, 0, 0)),
                     pl.BlockSpec((1, 1, n, 8), lambda b, h: (b, h, 0, 0))]
    return pl.pallas_call(
        _mlstm_kernel,
        grid=(B, H),
        in_specs=seq(L) + seq(Lc),
        out_specs=pl.BlockSpec((1, L, ML_DV), lambda b, h: (b, 0, h)),
        out_shape=jax.ShapeDtypeStruct((B, L, ML_WIDTH), F32),
        scratch_shapes=[pltpu.VMEM((dk, 2 * ML_DV), F32), pltpu.VMEM((dk, 2 * ML_DV), F32)],
        compiler_params=_cparams(("parallel", "parallel")),
        name="mlstm_scan",
    )(q, kt, v, rows, cols, qc, ktc, vc, rows_c, cols_c)


def _out_proj_kernel(yhy_ref, hs_ref, o_ref, x_ref, er_ref, ec_ref, g1_ref, sh2_ref, sc2_ref,
                     mln_ref, post1_ref, pre2_ref, wout_ref, wr_ref, wrt_ref,
                     x1_ref, h2_ref, aff_ref, afft_ref):
    hs = hs_ref[0]
    parts = []
    for hd in range(ML_HEADS):
        hh = hs[:, hd * ML_DV:(hd + 1) * ML_DV]
        parts.append(hh * lax.rsqrt(jnp.mean(hh * hh, axis=-1, keepdims=True) + EPS))
    hn = jnp.concatenate(parts, axis=-1) * mln_ref[...]
    y_ml = hn * jax.nn.sigmoid(o_ref[0])
    yo_hy = [lax.dot_general(yhy_ref[0, 0, pl.ds(j, HY_WIDTH, stride=TIME_ROWS), :].astype(BF16),
                             wout_ref[:HY_WIDTH, :], (((0,), (0,)), ((), ())), preferred_element_type=F32)
             for j in range(TIME_ROWS)]
    yo = (jnp.concatenate(yo_hy, axis=0)
          + jnp.dot(y_ml.astype(BF16), wout_ref[HY_WIDTH:, :], preferred_element_type=F32))
    xf = x_ref[0] + _pe_tile(er_ref[...], ec_ref[...])
    x1 = xf + g1_ref[0] * _rms(yo, post1_ref[...])
    x1_ref[0] = x1
    h2f = _rms(x1, pre2_ref[...]) * (1.0 + sc2_ref[0]) + sh2_ref[0]
    _store_row_tiles(h2_ref, h2f)
    h2 = h2f.astype(BF16)
    logits = jnp.dot(h2, wr_ref[...], preferred_element_type=F32)
    ex = jnp.exp(logits - jnp.max(logits, axis=-1, keepdims=True))
    aff_ref[0] = ex / jnp.sum(ex, axis=-1, keepdims=True)
    logits_t = lax.dot_general(wrt_ref[...], h2, (((1,), (1,)), ((), ())), preferred_element_type=F32)
    ext = jnp.exp(logits_t - jnp.max(logits_t, axis=0, keepdims=True))
    afft_ref[0] = ext / jnp.sum(ext, axis=0, keepdims=True)


def _out_proj(y_hy, hsum, o, x, tabs, mod3, ml_norm, post1, pre2, w_out, w_router, tm):
    B, L, D = x.shape
    er, ec = tabs
    E = w_router.shape[1]
    assert tm == TIME_ROWS * LANES
    full = lambda a: pl.BlockSpec(a.shape, lambda b, i: (0,) * a.ndim)
    tok = lambda w: pl.BlockSpec((1, tm, w), lambda b, i: (b, i, 0))
    wout = w_out.astype(BF16)
    wr = w_router.astype(BF16)
    wrt = w_router.T.astype(BF16)
    return pl.pallas_call(
        _out_proj_kernel,
        grid=(B, L // tm),
        in_specs=[pl.BlockSpec((1, 1, HY_WIDTH * TIME_ROWS, LANES), lambda b, i: (b, i, 0, 0)),
                  tok(ML_WIDTH), tok(ML_WIDTH), tok(D),
                  pl.BlockSpec((tm // GRID_W, D // 2), lambda b, i: (i, 0)), full(ec),
                  _mod_spec(D, 2, 0, True), _mod_spec(D, 3, 0, True), _mod_spec(D, 4, 0, True),
                  full(ml_norm), full(post1), full(pre2),
                  full(wout), full(wr), full(wrt)],
        out_specs=[tok(D), pl.BlockSpec((1, tm * (D // LANES), LANES), lambda b, i: (b, i, 0)), tok(E),
                   pl.BlockSpec((1, E, tm), lambda b, i: (b, 0, i))],
        out_shape=[jax.ShapeDtypeStruct((B, L, D), F32),
                   jax.ShapeDtypeStruct((B, L * (D // LANES), LANES), F32),
                   jax.ShapeDtypeStruct((B, L, E), F32), jax.ShapeDtypeStruct((B, E, L), F32)],
        compiler_params=_cparams(("parallel", "parallel")),
        name="out_proj_router",
    )(y_hy, hsum, o, x, er, ec, mod3, mod3, mod3, ml_norm, post1, pre2, wout, wr, wrt)


LANES = 128
ROW_GROUP = 16
SELECT_FAST_SLOTS = 32


def _store_row_tiles(ref, val):
    n, nt = val.shape[0], val.shape[1] // LANES
    for c in range(nt):
        ref[0, pl.ds(c, n, stride=nt), :] = val[:, c * LANES:(c + 1) * LANES]


def _select_kernel(afft_ref, gind_ref, sel_ref, off_ref, gmax_ref, *, cap):
    E, L = afft_ref.shape[1], afft_ref.shape[2]
    aff = afft_ref[0]
    iota = lax.broadcasted_iota(jnp.int32, (E, L), 1)
    count = lambda ind: jnp.sum(ind, axis=1, keepdims=True)
    count_ge = lambda th: count(jnp.where(aff >= th, 1.0, 0.0))
    pow2 = lambda j: pltpu.bitcast((j - 24) << 23, F32)

    def estep(_, c):
        lo, hi = c
        mid = (lo + hi) >> 1
        ok = count_ge(pow2(mid)) >= cap
        return jnp.where(ok, mid, lo), jnp.where(ok, hi, mid)

    jlo, jhi = lax.fori_loop(0, 7, estep, (jnp.full((E, 1), 24, jnp.int32),
                                           jnp.full((E, 1), 152, jnp.int32)))

    def vstep(_, c):
        lo, hi = c
        mid = lo + (hi - lo) * 0.5
        ok = count_ge(mid) >= cap
        return jnp.where(ok, mid, lo), jnp.where(ok, hi, mid)

    lo, hi = lax.fori_loop(0, 40, vstep, (pow2(jlo), pow2(jhi)))
    gt = jnp.where(aff >= hi, 1.0, 0.0)
    eq = jnp.where(aff >= lo, 1.0, 0.0) - gt
    need = cap - count(gt)

    def istep(_, c):
        lo, hi = c
        mid = (lo + hi) >> 1
        ok = count(jnp.where(iota <= mid, eq, 0.0)) >= need
        return jnp.where(ok, lo, mid), jnp.where(ok, mid, hi)

    _, last = lax.fori_loop(0, L.bit_length() - 1, istep,
                            (jnp.full((E, 1), -1, jnp.int32), jnp.full((E, 1), L - 1, jnp.int32)))
    sel = gt + jnp.where(iota <= last, eq, 0.0)
    sel_ref[0] = sel
    cnt = jnp.dot(sel.astype(BF16), gind_ref[...], preferred_element_type=F32)
    G = cnt.shape[1]
    earlier = (lax.broadcasted_iota(jnp.int32, (G, G), 0) < lax.broadcasted_iota(jnp.int32, (G, G), 1))
    off_ref[0] = jnp.dot(cnt.astype(BF16), earlier.astype(BF16), preferred_element_type=F32).astype(jnp.int32)
    gmax_ref[0] = jnp.max(cnt, axis=0, keepdims=True).astype(jnp.int32)


def _compact_kernel(off_ref, gmax_ref, sel_ref, idx_ref, rank_ref):
    E, L = sel_ref.shape[1], sel_ref.shape[2]
    T = LANES
    G = L // T
    idx_ref[...] = jnp.zeros_like(idx_ref)
    r_i = lax.broadcasted_iota(jnp.int32, (T, T), 0)
    c_i = lax.broadcasted_iota(jnp.int32, (T, T), 1)
    before = (r_i < c_i).astype(BF16)
    slot_f = r_i.astype(F32)
    lane_f = lax.broadcasted_iota(jnp.int32, (1, T), 1).astype(F32)

    for g in range(G):
        rank_ref[:, g * T:(g + 1) * T] = jnp.dot(sel_ref[0, :, g * T:(g + 1) * T].astype(BF16), before,
                                                 preferred_element_type=F32)

    def group(g, carry):
        cols = pl.ds(pl.multiple_of(g * T, T), T)
        s = sel_ref[0, :, cols]
        rank = rank_ref[:, cols]
        tok = (lane_f + jnp.asarray(g * T, F32)) * s

        def emit(n_slots):
            for e in range(E):
                hit = rank[e:e + 1, :] == slot_f[:n_slots]
                ids = jnp.sum(jnp.where(hit, tok[e:e + 1, :], 0.0), axis=1, keepdims=True)
                idx_ref[0, e, pl.ds(off_ref[0, 0, e * G + g], n_slots), :] = ids.astype(jnp.int32)

        few = gmax_ref[0, 0, g] <= SELECT_FAST_SLOTS
        pl.when(few)(lambda: emit(SELECT_FAST_SLOTS))
        pl.when(jnp.logical_not(few))(lambda: emit(T))
        return carry

    lax.fori_loop(0, G, group, 0, unroll=2)


def _route_select(afft, cap):
    B, E, L = afft.shape
    G = L // LANES
    gind = (jnp.arange(L)[:, None] // LANES == jnp.arange(G)[None, :]).astype(BF16)
    sel, off, gmax = pl.pallas_call(
        functools.partial(_select_kernel, cap=cap),
        grid=(B,),
        in_specs=[pl.BlockSpec((1, E, L), lambda b: (b, 0, 0)), pl.BlockSpec((L, G), lambda b: (0, 0))],
        out_specs=[pl.BlockSpec((1, E, L), lambda b: (b, 0, 0)), pl.BlockSpec((1, E, G), lambda b: (b, 0, 0)),
                   pl.BlockSpec((1, 1, G), lambda b: (b, 0, 0))],
        out_shape=[jax.ShapeDtypeStruct((B, E, L), F32), jax.ShapeDtypeStruct((B, E, G), jnp.int32),
                   jax.ShapeDtypeStruct((B, 1, G), jnp.int32)],
        compiler_params=_cparams(("parallel",)),
        name="route_select",
    )(afft, gind)
    smem = lambda n: pl.BlockSpec((1, 1, n), lambda b: (b, 0, 0), memory_space=pltpu.SMEM)
    idx = pl.pallas_call(
        _compact_kernel,
        grid=(B,),
        in_specs=[smem(E * G), smem(G), pl.BlockSpec((1, E, L), lambda b: (b, 0, 0))],
        out_specs=pl.BlockSpec((1, E, cap + LANES, 1), lambda b: (b, 0, 0, 0)),
        out_shape=jax.ShapeDtypeStruct((B, E, cap + LANES, 1), jnp.int32),
        scratch_shapes=[pltpu.VMEM((E, L), F32)],
        compiler_params=_cparams(("parallel",)),
        name="route_compact",
    )(off.reshape(B, 1, E * G), gmax, sel)
    return idx[:, :, :cap, 0]


def _tile_rows(i, nt):
    return pl.ds(pl.multiple_of(i * nt, nt), nt)


def _gather_kernel(idx_ref, h_ref, aff_ref, xs_ref, gs_ref, buf_ref):
    cap, D = xs_ref.shape[1], xs_ref.shape[2]
    nt = D // LANES

    def body(s, c):
        base = s * ROW_GROUP
        rows = [idx_ref[0, 0, base + k] for k in range(ROW_GROUP)]
        vals = [h_ref[0, _tile_rows(r, nt), :] for r in rows]
        gates = [aff_ref[0, pl.ds(r, 1), :] for r in rows]
        for k in range(ROW_GROUP):
            buf_ref[_tile_rows(base + k, nt), :] = vals[k]
            gs_ref[0, pl.ds(base + k, 1), :] = gates[k]
        return c

    lax.fori_loop(0, cap // ROW_GROUP, body, 0)
    for c in range(nt):
        xs_ref[0, :, c * LANES:(c + 1) * LANES] = buf_ref[pl.ds(c, cap, stride=nt), :].astype(BF16)


def _route_gather(idx, h2t, aff, D):
    B, E, cap = idx.shape
    nt = D // LANES
    L = h2t.shape[1] // nt
    return pl.pallas_call(
        _gather_kernel,
        grid=(B, E),
        in_specs=[pl.BlockSpec((1, 1, cap), lambda b, e: (b * E + e, 0, 0), memory_space=pltpu.SMEM),
                  pl.BlockSpec((1, L * nt, LANES), lambda b, e: (b, 0, 0), pipeline_mode=pl.Buffered(1)),
                  pl.BlockSpec((1, L, E), lambda b, e: (b, 0, 0))],
        out_specs=[pl.BlockSpec((1, cap, D), lambda b, e: (e, b, 0)),
                   pl.BlockSpec((1, cap, E), lambda b, e: (e, b, 0))],
        out_shape=[jax.ShapeDtypeStruct((E, B * cap, D), BF16),
                   jax.ShapeDtypeStruct((E, B * cap, E), F32)],
        scratch_shapes=[pltpu.VMEM((cap * nt, LANES), F32)],
        compiler_params=_cparams(("arbitrary", "arbitrary")),
        name="route_gather",
    )(idx.reshape(B * E, 1, cap), h2t, aff)


def _combine_kernel(idx_ref, ye_ref, x1_ref, g2_ref, post2_ref, o_ref, y_ref, buf_ref, *, n_experts):
    j = pl.program_id(1)
    cap, D = ye_ref.shape[1], ye_ref.shape[2]
    nt = D // LANES
    tm = x1_ref.shape[1]

    @pl.when(j == 0)
    def _():
        y_ref[...] = jnp.zeros_like(y_ref)

    @pl.when(j < n_experts)
    def _():
        for c in range(nt):
            buf_ref[pl.ds(c, cap, stride=nt), :] = ye_ref[0, :, c * LANES:(c + 1) * LANES]

        def body(s, c):
            base = s * ROW_GROUP
            rows = [idx_ref[0, 0, base + k] for k in range(ROW_GROUP)]
            vals = [y_ref[_tile_rows(rows[k], nt), :] + buf_ref[_tile_rows(base + k, nt), :]
                    for k in range(ROW_GROUP)]
            for k in range(ROW_GROUP):
                y_ref[_tile_rows(rows[k], nt), :] = vals[k]
            return c

        lax.fori_loop(0, cap // ROW_GROUP, body, 0)

    @pl.when(j >= n_experts)
    def _():
        base = pl.multiple_of((j - n_experts) * (tm * nt), tm * nt)
        cols = [y_ref[pl.ds(base + c, tm, stride=nt), :] for c in range(nt)]
        ssq = sum(jnp.sum(y * y, axis=-1, keepdims=True) for y in cols)
        rstd = lax.rsqrt(ssq * (1.0 / D) + EPS)
        for c in range(nt):
            cs = slice(c * LANES, (c + 1) * LANES)
            o_ref[0, :, cs] = x1_ref[0, :, cs] + g2_ref[0, :, cs] * (cols[c] * rstd * post2_ref[:, cs])


def _route_combine(idx, ye, x1, mod3, post2, tm):
    B, E, cap = idx.shape
    _, L, D = x1.shape
    nt = D // LANES
    assert ye.shape[0] == E + 1
    tile = lambda j: jnp.maximum(j - E, 0)
    expert = lambda j: jnp.minimum(j, E - 1)
    return pl.pallas_call(
        functools.partial(_combine_kernel, n_experts=E),
        grid=(B, E + L // tm),
        in_specs=[pl.BlockSpec((1, 1, cap), lambda b, j: (b * E + expert(j), 0, 0), memory_space=pltpu.SMEM),
                  pl.BlockSpec((1, cap, D), lambda b, j: (expert(j) + 1, b, 0)),
                  pl.BlockSpec((1, tm, D), lambda b, j: (b, tile(j), 0)),
                  _mod_spec(D, 5, 0, True),
                  pl.BlockSpec((1, D), lambda b, j: (0, 0))],
        out_specs=pl.BlockSpec((1, tm, D), lambda b, j: (b, tile(j), 0)),
        out_shape=jax.ShapeDtypeStruct((B, L, D), F32),
        scratch_shapes=[pltpu.VMEM((L * nt, LANES), F32), pltpu.VMEM((cap * nt, LANES), F32)],
        compiler_params=_cparams(("arbitrary", "arbitrary")),
        name="route_combine_final",
    )(idx.reshape(B * E, 1, cap), ye, x1, mod3, post2)


def _expert_kernel(xs_ref, g_ref, wg_ref, wu_ref, wd_ref, ye_ref, wgb_ref, wub_ref, wdb_ref):
    s = pl.program_id(0)
    f = pl.program_id(1)
    ne = pl.num_programs(0) - 1
    tf = wg_ref.shape[2]
    slot = lax.rem(s, 2)

    @pl.when(s < ne)
    def _():
        cols = pl.ds(pl.multiple_of(f * tf, tf), tf)
        wgb_ref[slot, :, cols] = wg_ref[0].astype(BF16)
        wub_ref[slot, :, cols] = wu_ref[0].astype(BF16)
        wdb_ref[slot, cols, :] = wd_ref[0].astype(BF16)

    @pl.when(s == 0)
    def _():
        ye_ref[...] = jnp.zeros_like(ye_ref)

    @pl.when(s > 0)
    def _():
        e = s - 1
        prev = 1 - slot
        mt = xs_ref.shape[1] // FFN_M_SPLIT
        for mi in range(FFN_M_SPLIT):
            rs = slice(mi * mt, (mi + 1) * mt)
            xs = xs_ref[0, rs, :]
            a = jnp.dot(xs, wgb_ref[prev], preferred_element_type=F32)
            u = jnp.dot(xs, wub_ref[prev], preferred_element_type=F32)
            hmid = (a * jax.nn.sigmoid(a) * u).astype(BF16)
            out = jnp.dot(hmid, wdb_ref[prev], preferred_element_type=F32)
            gs = g_ref[0, rs, :]
            esel = lax.broadcasted_iota(jnp.int32, gs.shape, 1)
            ye_ref[0, rs, :] = out * jnp.sum(jnp.where(esel == e, gs, 0.0), axis=1, keepdims=True)


FFN_M_SPLIT = 2


def _expert_ffn(xs, gs, w_gate, w_up, w_down, tf):
    E, M, D = xs.shape
    F = w_gate.shape[2]
    nf = F // tf
    mt = M // nf
    behind = lambda s: jnp.maximum(s - 1, 0)
    ahead = lambda s: jnp.minimum(s, E - 1)
    return pl.pallas_call(
        _expert_kernel,
        grid=(E + 1, nf),
        in_specs=[pl.BlockSpec((1, mt, D), lambda s, f: (behind(s), f, 0)),
                  pl.BlockSpec((1, mt, E), lambda s, f: (behind(s), f, 0)),
                  pl.BlockSpec((1, D, tf), lambda s, f: (ahead(s), 0, f)),
                  pl.BlockSpec((1, D, tf), lambda s, f: (ahead(s), 0, f)),
                  pl.BlockSpec((1, tf, D), lambda s, f: (ahead(s), f, 0))],
        out_specs=pl.BlockSpec((1, mt, D), lambda s, f: (s, f, 0)),
        out_shape=jax.ShapeDtypeStruct((E + 1, M, D), F32),
        scratch_shapes=[pltpu.VMEM((2, D, F), BF16), pltpu.VMEM((2, D, F), BF16), pltpu.VMEM((2, F, D), BF16)],
        compiler_params=_cparams(("arbitrary", "arbitrary")),
        name="expert_ffn",
    )(xs, gs, w_gate, w_up, w_down)


def kernel(x, c, ctx, c_ctx, w_mod, b_mod, pre_norm1, post_norm1, pre_norm2, post_norm2, w_in, conv_w, conv_b, filt_w1, filt_b1, filt_w2, filt_b2, filt_w3, filt_b3, filt_w4, filt_freq, hyena_bias, ml_gate_b, ml_norm, w_out, w_router, w_exp_gate, w_exp_up, w_exp_down):
    B, L, D = x.shape
    depth = w_mod.shape[0]
    assert depth == 1, "single-layer block"
    li = 0
    tabs = _pe_tables(L // GRID_W, GRID_W, D)

    cc = jnp.concatenate([c, c_ctx[None], jnp.zeros((8 - B - 1, D), F32)], axis=0)
    mod3 = _modulation(cc, w_mod[li], b_mod[li])[:, None, :]

    pre1 = pre_norm1[li][None, :]
    w_proj = _in_proj_weights(w_in[li])
    zhy, q, kt, v, o, gatet = _in_proj(x, tabs, mod3, 0, pre1, w_proj, IN_PROJ_TILE, True)
    qc, ktc, vc, _, gatetc = _in_proj(ctx, None, mod3, B, pre1, w_proj, ctx.shape[1], False)

    kcirc = _hyena_filter(L, filt_w1[li], filt_b1[li], filt_w2[li], filt_b2[li], filt_w3[li],
                          filt_b3[li], filt_w4[li], filt_freq[li])
    y_hy = _hyena_conv(zhy, kcirc, conv_w[li], conv_b[li], hyena_bias[li])
    hsum = _mlstm(q, kt, v, gatet, qc, ktc, vc, gatetc, ml_gate_b[li])

    x1, h2t, aff, afft = _out_proj(y_hy, hsum, o, x, tabs, mod3, ml_norm[li][None, :],
                                   post_norm1[li][None, :], pre_norm2[li][None, :], w_out[li],
                                   w_router[li], IN_PROJ_TILE)

    cap = CAP_FACTOR * L // N_EXPERTS
    idx = _route_select(afft, cap)
    xs, gs = _route_gather(idx, h2t, aff, D)
    ye = _expert_ffn(xs, gs, w_exp_gate[li], w_exp_up[li], w_exp_down[li], 512)
    return _route_combine(idx, ye, x1, mod3, post_norm2[li][None, :], TOKEN_TILE)
```
